```python
import math
import jax
import jax.numpy as jnp
from jax import lax
import numpy as np


D_MODEL = 1024
BATCH = 32
SEQ = 256
DEPTH = 1
DEC_BATCH = 4
DEC_SEQ = 4096
PAST_LEN = 256

GRID_W = 64
HY_W = D_MODEL // 2
ML_W = D_MODEL - HY_W
ML_HEADS = 4
ML_DH = ML_W // ML_HEADS
N_DIR = 2
CHUNK = 64
SHORT_K = 3
HY_ORDER = 2
FILT_BANDS = 8
FILT_EMB = 1 + 2 * FILT_BANDS
FILT_HIDDEN = 64
DECAY_TARGET = 1e-2
FAST_DECAY_PCT = 0.3
SLOW_DECAY_PCT = 1.5
N_EXPERTS = 32
TOP_K = 4
D_FF_EXPERT = D_MODEL
SWIGLU_LIMIT = 7.0
SWIGLU_ALPHA = 1.702
MOE_BLOCK = 128
N_GATES = N_DIR * 2 * ML_HEADS
D_IN = 3 * HY_W + 2 * ML_W + ML_W + ML_W + N_GATES
EPS = 1e-6

kernel_name = 'hyena_mlstm_moe_prefix_diffusion_step'


def _rmsnorm(x, g):
    xf = x.astype(jnp.float32)
    y = xf * lax.rsqrt(jnp.mean(xf * xf, axis=-1, keepdims=True) + EPS)
    return (y * g.astype(jnp.float32)).astype(x.dtype)


def _ada(cond, w, b):
    return jax.nn.silu(cond.astype(jnp.float32)) @ w.astype(jnp.float32) + b.astype(jnp.float32)


def _short_conv(u, w, b, grid_w):
    B, L, C = u.shape
    if grid_w is None:
        g = u[:, None]
    else:
        rows = L // grid_w
        g = u.reshape(B, rows, grid_w, C)
    p = jnp.pad(g, ((0, 0), (0, 0), (1, 1), (0, 0)))
    y = p[:, :, :-2] * w[0] + p[:, :, 1:-1] * w[1] + p[:, :, 2:] * w[2] + b
    return y.reshape(B, L, C)


def _implicit_filter(L, w1, b1, w2, b2, w3, freq):
    f32 = jnp.float32
    t = jnp.arange(L, dtype=f32)
    t01 = t / max(L - 1, 1)
    bands = jnp.linspace(1e-4, FILT_BANDS - 1, FILT_BANDS, dtype=f32)
    ang = (2.0 * math.pi / L) * t[:, None] * bands[None, :]
    feats = jnp.concatenate([t01[:, None], jnp.cos(ang), -jnp.sin(ang)], axis=-1)
    fr = freq.astype(f32)
    h = jnp.sin(fr * (feats @ w1.astype(f32) + b1.astype(f32)))
    h = jnp.sin(fr * (h @ w2.astype(f32) + b2.astype(f32)))
    h = (h @ w3.astype(f32)).reshape(L, HY_ORDER, N_DIR, HY_W)
    max_decay = math.log(DECAY_TARGET) / FAST_DECAY_PCT
    min_decay = math.log(DECAY_TARGET) / SLOW_DECAY_PCT
    deltas = jnp.abs(jnp.linspace(min_decay, max_decay, HY_W, dtype=f32))
    window = jnp.exp(-t01[:, None] * deltas[None, :])
    h = h * window[:, None, None, :]
    return jnp.transpose(h, (1, 2, 0, 3))


def _long_conv(z, hf, hb):
    B, L, C = z.shape
    n = 2 * L
    hc = jnp.concatenate([hf, jnp.zeros((1, C), hf.dtype), hb[:0:-1]], axis=0)
    Z = jnp.fft.rfft(z, n=n, axis=1)
    H = jnp.fft.rfft(hc, n=n, axis=0)
    return jnp.fft.irfft(Z * H[None], n=n, axis=1)[:, :L]


def _hyena(u_raw, conv_w, conv_b, f_w1, f_b1, f_w2, f_b2, f_w3, f_freq, hy_bias, grid_w):
    B, L, _ = u_raw.shape
    u = _short_conv(u_raw, conv_w, conv_b, grid_w).astype(jnp.float32)
    v, x1, x2 = jnp.split(u, 3, axis=-1)
    filt = _implicit_filter(L, f_w1, f_b1, f_w2, f_b2, f_w3, f_freq)
    bias = hy_bias.astype(jnp.float32)
    z = v
    for o, gate in enumerate((x1, x2)):
        z = gate * (_long_conv(z, filt[o, 0], filt[o, 1]) + bias[o] * z)
    return z.astype(u_raw.dtype)


def _mlstm_chunked(q, k, v, ig, lf, C0, n0, m0):
    B, H, L, DH = q.shape
    nc = L // CHUNK

    def chunks(a):
        return jnp.moveaxis(a.reshape(a.shape[:2] + (nc, CHUNK) + a.shape[3:]), 2, 0)

    mask = jnp.tril(jnp.ones((CHUNK, CHUNK), dtype=bool))

    def step(carry, xs_):
        C, n, m = carry
        qc, kc, vc, ic, fc = xs_
        b = jnp.cumsum(fc, axis=-1)
        Dm = jnp.where(mask, b[..., :, None] - b[..., None, :] + ic[..., None, :], -jnp.inf)
        inter = b + m[..., None]
        mj = jnp.maximum(jnp.max(Dm, axis=-1), inter)
        w_int = jnp.exp(inter - mj)
        S = jnp.einsum('bhqd,bhsd->bhqs', qc, kc) * jnp.exp(Dm - mj[..., None])
        num = w_int[..., None] * jnp.einsum('bhqd,bhde->bhqe', qc, C) + jnp.einsum('bhqs,bhse->bhqe', S, vc)
        den = w_int * jnp.einsum('bhqd,bhd->bhq', qc, n) + jnp.sum(S, axis=-1)
        h = num / jnp.maximum(jnp.abs(den), jnp.exp(-mj))[..., None]
        bq = b[..., -1]
        gl = bq[..., None] - b + ic
        m_new = jnp.maximum(bq + m, jnp.max(gl, axis=-1))
        a = jnp.exp(bq + m - m_new)
        wk = jnp.exp(gl - m_new[..., None])[..., None] * kc
        C_new = a[..., None, None] * C + jnp.einsum('bhsd,bhse->bhde', wk, vc)
        n_new = a[..., None] * n + jnp.sum(wk, axis=2)
        return (C_new, n_new, m_new), h

    (C, n, m), h = lax.scan(step, (C0, n0, m0), (chunks(q), chunks(k), chunks(v), chunks(ig), chunks(lf)))
    h = jnp.moveaxis(h, 0, 2).reshape(B, H, L, DH)
    return h, (C, n, m)


def _mlstm(qk_raw, v_raw, o_pre, g_pre, conv_w, conv_b, gate_b, norm_g, st0, grid_w):
    B, L, _ = v_raw.shape
    f32 = jnp.float32
    qk = jax.nn.silu(_short_conv(qk_raw, conv_w, conv_b, grid_w)).astype(f32)

    def heads(a):
        return a.reshape(B, L, ML_HEADS, ML_DH).transpose(0, 2, 1, 3)

    q = heads(qk[..., :ML_W])
    k = heads(qk[..., ML_W:]) * (ML_DH ** -0.5)
    v = heads(v_raw.astype(f32))
    g = g_pre.astype(f32).reshape(B, L, N_DIR, 2, ML_HEADS) + gate_b.astype(f32)
    ig = g[:, :, :, 0].transpose(2, 0, 3, 1)
    lf = jax.nn.log_sigmoid(g[:, :, :, 1]).transpose(2, 0, 3, 1)
    C0, n0, m0 = (s.astype(f32) for s in st0)
    h_f, (Cf, nf, mf) = _mlstm_chunked(q, k, v, ig[0], lf[0], C0[:, 0], n0[:, 0], m0[:, 0])
    fl = lambda a: jnp.flip(a, axis=2)
    h_b, (Cb, nb, mb) = _mlstm_chunked(fl(q), fl(k), fl(v), jnp.flip(ig[1], -1), jnp.flip(lf[1], -1),
                                       C0[:, 1], n0[:, 1], m0[:, 1])
    hs = h_f + fl(h_b)
    hs = hs * lax.rsqrt(jnp.mean(hs * hs, axis=-1, keepdims=True) + EPS)
    hs = hs.transpose(0, 2, 1, 3).reshape(B, L, ML_W) * norm_g.astype(f32)
    out = (jax.nn.sigmoid(o_pre.astype(f32)) * hs).astype(v_raw.dtype)
    st = (jnp.stack([Cf, Cb], axis=1), jnp.stack([nf, nb], axis=1), jnp.stack([mf, mb], axis=1))
    return out, st


def _moe(h, r_w, r_b, w_gu, b_gu, w_d, b_d):
    T, D = h.shape
    logits = (h @ r_w + r_b).astype(jnp.float32)
    top_v, top_i = lax.top_k(logits, TOP_K)
    gates = jax.nn.softmax(top_v, axis=-1)
    flat_e = top_i.reshape(-1)
    flat_g = gates.reshape(-1)
    flat_tok = jnp.repeat(jnp.arange(T, dtype=jnp.int32), TOP_K)
    order = jnp.argsort(flat_e)
    se = flat_e[order]
    counts = jnp.bincount(flat_e, length=N_EXPERTS)
    padded = (counts + MOE_BLOCK - 1) // MOE_BLOCK * MOE_BLOCK
    pad_end = jnp.cumsum(padded)
    dest = (pad_end - padded)[se] + jnp.arange(T * TOP_K) - (jnp.cumsum(counts) - counts)[se]
    n_blocks = -(-(T * TOP_K) // MOE_BLOCK) + N_EXPERTS
    R = n_blocks * MOE_BLOCK
    row_tok = jnp.full((R,), T, dtype=jnp.int32).at[dest].set(flat_tok[order])
    row_g = jnp.zeros((R,), jnp.float32).at[dest].set(flat_g[order])
    block_e = jnp.minimum(jnp.searchsorted(pad_end, jnp.arange(n_blocks) * MOE_BLOCK, side='right'), N_EXPERTS - 1)
    xs = jnp.concatenate([h, jnp.zeros((1, D), h.dtype)], axis=0)[row_tok].reshape(n_blocks, MOE_BLOCK, D)

    def expert_block(args):
        xb, e = args
        gu = xb @ w_gu[e] + b_gu[e]
        gate = jnp.minimum(gu[:, ::2], SWIGLU_LIMIT)
        lin = jnp.clip(gu[:, 1::2], -SWIGLU_LIMIT, SWIGLU_LIMIT)
        act = (lin + 1) * gate * jax.nn.sigmoid(SWIGLU_ALPHA * gate)
        return act @ w_d[e] + b_d[e]

    ys = lax.map(expert_block, (xs, block_e))
    out = jnp.zeros((T + 1, D), jnp.float32).at[row_tok].add(ys.reshape(R, D).astype(jnp.float32) * row_g[:, None])
    return out[:T].astype(h.dtype)


def _layer(x, mod, st0, grid_w, lw):
    (n1, w_in, hy_cw, hy_cb, f_w1, f_b1, f_w2, f_b2, f_w3, f_freq, hy_b,
     ml_cw, ml_cb, ml_gb, ml_ng, w_out, n2, r_w, r_b, e_wgu, e_bgu, e_wd, e_bd) = lw
    sh1, sc1, g1, sh2, sc2, g2 = [mod[:, None, i] for i in range(6)]
    h = _rmsnorm(x, n1) * (1 + sc1) + sh1
    z = h @ w_in
    o1 = 3 * HY_W
    o2 = o1 + 2 * ML_W
    o3 = o2 + ML_W
    o4 = o3 + ML_W
    y_hy = _hyena(z[..., :o1], hy_cw, hy_cb, f_w1, f_b1, f_w2, f_b2, f_w3, f_freq, hy_b, grid_w)
    y_ml, st = _mlstm(z[..., o1:o2], z[..., o2:o3], z[..., o3:o4], z[..., o4:],
                      ml_cw, ml_cb, ml_gb, ml_ng, st0, grid_w)
    x = x + g1 * (jnp.concatenate([y_hy, y_ml], axis=-1) @ w_out)
    h2 = _rmsnorm(x, n2) * (1 + sc2) + sh2
    B, L, D = x.shape
    x = x + g2 * _moe(h2.reshape(B * L, D), r_w, r_b, e_wgu, e_bgu, e_wd, e_bd).reshape(B, L, D)
    return x, st


def setup_inputs(seed: int = 0) -> dict:
    key = jax.random.key(seed)
    ks = iter(jax.random.split(key, 48))

    def nrm(shape, scale):
        return scale * jax.random.normal(next(ks), shape, jnp.float32)

    ig_b = nrm((DEPTH, N_DIR, 1, ML_HEADS), 0.1)
    fg_b = jnp.linspace(3.0, 6.0, ML_HEADS, dtype=jnp.float32) + nrm((DEPTH, N_DIR, 1, ML_HEADS), 0.1)
    return {
        'x_prompt': nrm((BATCH, SEQ, D_MODEL), 1.0),
        'x_sample': nrm((DEC_BATCH, DEC_SEQ, D_MODEL), 1.0),
        'state_mlstm_C': nrm((DEC_BATCH, DEPTH, N_DIR, ML_HEADS, ML_DH, ML_DH), 0.05),
        'state_mlstm_n': nrm((DEC_BATCH, DEPTH, N_DIR, ML_HEADS, ML_DH), 0.05),
        'state_mlstm_m': nrm((DEC_BATCH, DEPTH, N_DIR, ML_HEADS), 1.0),
        'c': nrm((DEC_BATCH, D_MODEL), 1.0),
        'c_ctx': nrm((D_MODEL,), 1.0),
        'ada_w': nrm((DEPTH, D_MODEL, 6 * D_MODEL), 0.5 * D_MODEL ** -0.5),
        'ada_b': nrm((DEPTH, 6 * D_MODEL), 0.1),
        'norm1_g': 1.0 + nrm((DEPTH, D_MODEL), 0.02),
        'w_in': nrm((DEPTH, D_MODEL, D_IN), D_MODEL ** -0.5),
        'hy_conv_w': nrm((DEPTH, SHORT_K, 3 * HY_W), 0.5),
        'hy_conv_b': nrm((DEPTH, 3 * HY_W), 0.02),
        'filt_w1': nrm((DEPTH, FILT_EMB, FILT_HIDDEN), FILT_EMB ** -0.5),
        'filt_b1': nrm((DEPTH, FILT_HIDDEN), 0.1),
        'filt_w2': nrm((DEPTH, FILT_HIDDEN, FILT_HIDDEN), FILT_HIDDEN ** -0.5),
        'filt_b2': nrm((DEPTH, FILT_HIDDEN), 0.1),
        'filt_w3': nrm((DEPTH, FILT_HIDDEN, HY_ORDER * N_DIR * HY_W), 0.005),
        'filt_freq': 1.0 + nrm((DEPTH, FILT_HIDDEN), 0.01),
        'hy_bias': nrm((DEPTH, HY_ORDER, HY_W), 0.3),
        'ml_conv_w': nrm((DEPTH, SHORT_K, 2 * ML_W), 0.5),
        'ml_conv_b': nrm((DEPTH, 2 * ML_W), 0.02),
        'ml_gate_b': jnp.concatenate([ig_b, fg_b], axis=2),
        'ml_norm_g': 1.0 + nrm((DEPTH, ML_W), 0.02),
        'w_out': nrm((DEPTH, HY_W + ML_W, D_MODEL), (HY_W + ML_W) ** -0.5),
        'norm2_g': 1.0 + nrm((DEPTH, D_MODEL), 0.02),
        'router_w': nrm((DEPTH, D_MODEL, N_EXPERTS), D_MODEL ** -0.5),
        'router_b': nrm((DEPTH, N_EXPERTS), 0.01),
        'moe_w_gu': nrm((DEPTH, N_EXPERTS, D_MODEL, 2 * D_FF_EXPERT), D_MODEL ** -0.5),
        'moe_b_gu': nrm((DEPTH, N_EXPERTS, 2 * D_FF_EXPERT), 0.01),
        'moe_w_down': nrm((DEPTH, N_EXPERTS, D_FF_EXPERT, D_MODEL), D_FF_EXPERT ** -0.5),
        'moe_b_down': nrm((DEPTH, N_EXPERTS, D_MODEL), 0.01),
        'final_g': 1.0 + nrm((D_MODEL,), 0.02),
    }


def reference(x_prompt, x_sample, state_mlstm_C, state_mlstm_n, state_mlstm_m, c, c_ctx,
              ada_w, ada_b, norm1_g, w_in, hy_conv_w, hy_conv_b, filt_w1, filt_b1, filt_w2, filt_b2,
              filt_w3, filt_freq, hy_bias, ml_conv_w, ml_conv_b, ml_gate_b, ml_norm_g, w_out, norm2_g,
              router_w, router_b, moe_w_gu, moe_b_gu, moe_w_down, moe_b_down, final_g):
    f32 = jnp.float32
    xp = x_prompt
    xs = x_sample
    bp = xp.shape[0]
    zero_state = (jnp.zeros((bp, N_DIR, ML_HEADS, ML_DH, ML_DH), f32),
                  jnp.zeros((bp, N_DIR, ML_HEADS, ML_DH), f32),
                  jnp.zeros((bp, N_DIR, ML_HEADS), f32))
    new_C, new_n, new_m = [], [], []
    for l in range(DEPTH):
        lw = (norm1_g[l], w_in[l], hy_conv_w[l], hy_conv_b[l], filt_w1[l], filt_b1[l], filt_w2[l],
              filt_b2[l], filt_w3[l], filt_freq[l], hy_bias[l], ml_conv_w[l], ml_conv_b[l],
              ml_gate_b[l], ml_norm_g[l], w_out[l], norm2_g[l], router_w[l], router_b[l],
              moe_w_gu[l], moe_b_gu[l], moe_w_down[l], moe_b_down[l])
        mod_ctx = _ada(c_ctx, ada_w[l], ada_b[l]).reshape(1, 6, D_MODEL).astype(xp.dtype)
        mod_lat = _ada(c, ada_w[l], ada_b[l]).reshape(-1, 6, D_MODEL).astype(xs.dtype)
        xp, st = _layer(xp, mod_ctx, zero_state, None, lw)
        new_C.append(st[0])
        new_n.append(st[1])
        new_m.append(st[2])
        cache = (state_mlstm_C[:, l], state_mlstm_n[:, l], state_mlstm_m[:, l])
        xs, _ = _layer(xs, mod_lat, cache, GRID_W, lw)
    y_prompt = _rmsnorm(xp, final_g)
    y_sample = _rmsnorm(xs, final_g)
    new_mlstm_C = jnp.stack(new_C, axis=1)
    new_mlstm_n = jnp.stack(new_n, axis=1)
    new_mlstm_m = jnp.stack(new_m, axis=1)
    return (y_prompt, y_sample, new_mlstm_C, new_mlstm_n, new_mlstm_m)
```

```python
import functools
import math

import numpy as np
import jax
import jax.numpy as jnp
from jax import lax
from jax.experimental import pallas as pl
from jax.experimental.pallas import tpu as pltpu

F32 = jnp.float32
BF16 = jnp.bfloat16
HIGHEST = lax.Precision.HIGHEST
EPS = 1e-6

LANES = 128
SUBLANES = 8
VMEM_LIMIT_BYTES = 56 * 1024 * 1024

GRID_W = 64
ML_HEADS = 4
N_DIR = 2
HY_ORDER = 2
FILT_BANDS = 8
DECAY_TARGET = 1e-2
FAST_DECAY_PCT = 0.3
SLOW_DECAY_PCT = 1.5
TOP_K = 4
SWIGLU_LIMIT = 7.0
SWIGLU_ALPHA = 1.702

FFT_N2 = 128
MLSTM_CHUNK = 128
MOE_ROWS = 512
ROW_TILE = 512
TOK_TILE = 256


def _cparams(*sem):
    return pltpu.CompilerParams(dimension_semantics=sem, vmem_limit_bytes=VMEM_LIMIT_BYTES)


def _lane_tile(c, cap):
    return max(t for t in range(LANES, min(c, cap) + 1, LANES) if c % t == 0)


def _bdot(a, b):
    return jnp.dot(a.astype(BF16), b.astype(BF16), preferred_element_type=F32)


def _hdot(a, b):
    return jnp.dot(a, b, precision=HIGHEST, preferred_element_type=F32)


def _ada_kernel(c_ref, w_ref, b_ref, o_ref):
    c = c_ref[...]
    o_ref[...] = _hdot(c * jax.nn.sigmoid(c), w_ref[...]) + b_ref[...]


def _ada(cond, w, b):
    r, d = cond.shape
    n = w.shape[1]
    tn = _lane_tile(n, 1024)
    return pl.pallas_call(
        _ada_kernel,
        grid=(n // tn,),
        in_specs=[pl.BlockSpec((r, d), lambda j: (0, 0)),
                  pl.BlockSpec((d, tn), lambda j: (0, j)),
                  pl.BlockSpec((1, tn), lambda j: (0, j))],
        out_specs=pl.BlockSpec((r, tn), lambda j: (0, j)),
        out_shape=jax.ShapeDtypeStruct((r, n), F32),
        compiler_params=_cparams("arbitrary"),
        name="ada",
    )(cond, w, b.reshape(1, n))


def _mod_index_map(n_ctx_tiles, tiles_per_lat):
    def index_map(i):
        return (jnp.where(i < n_ctx_tiles, 0, 1 + (i - n_ctx_tiles) // tiles_per_lat), 0, 0)
    return index_map


def _inproj_kernel(x_ref, mod_ref, g_ref, w_ref, wg_ref, *out_refs, offsets):
    x = x_ref[...]
    h = x * lax.rsqrt(jnp.mean(x * x, axis=-1, keepdims=True) + EPS) * g_ref[...]
    h = h * (1.0 + mod_ref[0, 1:2, :]) + mod_ref[0, 0:1, :]
    hb = h.astype(BF16)
    for o_ref, (lo, hi) in zip(out_refs[:-1], offsets):
        o_ref[...] = jnp.dot(hb, w_ref[:, lo:hi], preferred_element_type=F32)
    h_lo = (h - hb.astype(F32)).astype(BF16)
    g = jnp.dot(hb, wg_ref[...], preferred_element_type=F32)
    out_refs[-1][...] = (g[:, :LANES] + g[:, LANES:]
                         + jnp.dot(h_lo, wg_ref[:, :LANES], preferred_element_type=F32))


def _inproj(x, mods, norm_g, w_main, w_gate, seg_widths, n_ctx_tiles, tiles_per_lat, tm):
    t, d = x.shape
    offsets, lo = [], 0
    for wd in seg_widths:
        offsets.append((lo, lo + wd))
        lo += wd
    wg_hi = w_gate.astype(BF16)
    wg = jnp.concatenate([wg_hi, (w_gate - wg_hi.astype(F32)).astype(BF16)], axis=1)
    widths = tuple(seg_widths) + (LANES,)
    return pl.pallas_call(
        functools.partial(_inproj_kernel, offsets=tuple(offsets)),
        grid=(t // tm,),
        in_specs=[pl.BlockSpec((tm, d), lambda i: (i, 0)),
                  pl.BlockSpec((1, 6, d), _mod_index_map(n_ctx_tiles, tiles_per_lat)),
                  pl.BlockSpec((1, d), lambda i: (0, 0)),
                  pl.BlockSpec(w_main.shape, lambda i: (0, 0)),
                  pl.BlockSpec(wg.shape, lambda i: (0, 0))],
        out_specs=[pl.BlockSpec((tm, wd), lambda i: (i, 0)) for wd in widths],
        out_shape=[jax.ShapeDtypeStruct((t, wd), F32) for wd in widths],
        compiler_params=_cparams("arbitrary"),
        name="inproj",
    )(x, mods, norm_g.reshape(1, d), w_main, wg)


def _short_conv_kernel(x_ref, w_ref, b_ref, o_ref, *, row_w, silu):
    x = x_ref[0]
    l = x.shape[0]
    pos = lax.broadcasted_iota(jnp.int32, x.shape, 0) % row_w
    prev = jnp.where(pos == 0, 0.0, pltpu.roll(x, 1, 0))
    nxt = jnp.where(pos == row_w - 1, 0.0, pltpu.roll(x, l - 1, 0))
    y = prev * w_ref[0:1, :] + x * w_ref[1:2, :] + nxt * w_ref[2:3, :] + b_ref[...]
    if silu:
        y = y * jax.nn.sigmoid(y)
    o_ref[0] = y


def _short_conv(x, w, b, row_w, silu):
    bsz, l, c = x.shape
    ct = _lane_tile(c, 256)
    return pl.pallas_call(
        functools.partial(_short_conv_kernel, row_w=row_w, silu=silu),
        grid=(bsz, c // ct),
        in_specs=[pl.BlockSpec((1, l, ct), lambda i, j: (i, 0, j)),
                  pl.BlockSpec((3, ct), lambda i, j: (0, j)),
                  pl.BlockSpec((1, ct), lambda i, j: (0, j))],
        out_specs=pl.BlockSpec((1, l, ct), lambda i, j: (i, 0, j)),
        out_shape=jax.ShapeDtypeStruct(x.shape, F32),
        compiler_params=_cparams("arbitrary", "arbitrary"),
        name="short_conv",
    )(x, w, b.reshape(1, c))


def _dft_direct_tables(l):
    n = 2 * l
    k = np.arange(n)[:, None].astype(np.float64)
    t = np.arange(n)[None, :].astype(np.float64)
    ang = 2.0 * np.pi * ((k * t) % n) / n
    cm, sm = np.cos(ang), np.sin(ang)
    fwd = np.block([[cm[:, :l], sm[:, :l]], [-sm[:, :l], cm[:, :l]]])
    filt = np.concatenate([cm, -sm], axis=0)
    return fwd, filt


def _dft_two_level_tables(l, n2):
    n = 2 * l
    n1 = n // n2
    k1 = np.arange(n1)[:, None].astype(np.float64)
    a = np.arange(n1)[None, :].astype(np.float64)
    ang1 = 2.0 * np.pi * ((k1 * a) % n1) / n1
    c1, s1 = np.cos(ang1), np.sin(ang1)
    h = n1 // 2
    m1 = np.block([[c1[:, :h], s1[:, :h]], [-s1[:, :h], c1[:, :h]]])
    m1f = np.concatenate([c1, -s1], axis=0)
    kk = (np.arange(n1)[:, None, None] + n1 * np.arange(n2)[None, :, None]).astype(np.float64)
    b = np.arange(n2)[None, None, :].astype(np.float64)
    ang = 2.0 * np.pi * ((kk * b) % n) / n
    cg, sg = np.cos(ang), np.sin(ang)
    gt = np.concatenate([np.concatenate([cg, sg], axis=2),
                         np.concatenate([-sg, cg], axis=2)], axis=1)
    return m1, m1f, gt


def _filter_hidden(n0, rows, l, bandv_ref, w1_ref, b1_ref, w2_ref, b2_ref, freq_ref):
    n = n0 + lax.broadcasted_iota(jnp.int32, (rows, 1), 0)
    t = jnp.where(n < l, n, 2 * l - n).astype(F32)
    t01 = t / float(max(l - 1, 1))
    lane = lax.broadcasted_iota(jnp.int32, (rows, LANES), 1)
    ang = (2.0 * math.pi / l) * t * bandv_ref[...]
    feats = jnp.where(lane == 0, t01,
                      jnp.where(lane <= FILT_BANDS, jnp.cos(ang),
                                jnp.where(lane <= 2 * FILT_BANDS, -jnp.sin(ang), 0.0)))
    fr = freq_ref[...]
    h = jnp.sin(fr * (_hdot(feats, w1_ref[...]) + b1_ref[...]))
    h = jnp.sin(fr * (_hdot(h, w2_ref[...]) + b2_ref[...]))
    return h, t01, n


def _filter_rows(n0, rows, l, mlp_refs, w3_ref, delta_ref):
    h, t01, n = _filter_hidden(n0, rows, l, *mlp_refs)
    hf = _bdot(h, w3_ref[0, 0])
    hb = _bdot(h, w3_ref[0, 1])
    window = jnp.exp(-t01 * delta_ref[...])
    return jnp.where(n < l, hf, jnp.where(n > l, hb, 0.0)) * window


def _filter_direct_kernel(bandv_ref, w1_ref, b1_ref, w2_ref, b2_ref, freq_ref, w3_ref, delta_ref,
                          ff_ref, h_ref, *, l):
    mlp_refs = (bandv_ref, w1_ref, b1_ref, w2_ref, b2_ref, freq_ref)
    hc = _filter_rows(0, 2 * l, l, mlp_refs, w3_ref, delta_ref)
    h_ref[0] = _bdot(ff_ref[...], hc) * (1.0 / (2 * l))


def _filter_two_level_kernel(bandv_ref, w1_ref, b1_ref, w2_ref, b2_ref, freq_ref, w3_ref, delta_ref,
                             m1f_ref, gt_ref, h_ref, hc_buf, a_buf, *, l, n2, rows):
    mlp_refs = (bandv_ref, w1_ref, b1_ref, w2_ref, b2_ref, freq_ref)
    n = 2 * l
    n1 = n // n2

    def fill(i, carry):
        r0 = pl.multiple_of(i * rows, rows)
        hc_buf[pl.ds(r0, rows), :] = _filter_rows(r0, rows, l, mlp_refs, w3_ref, delta_ref)
        return carry
    lax.fori_loop(0, n // rows, fill, 0)

    def step1(b, carry):
        col = hc_buf[pl.ds(b, n1, stride=n2), :]
        a = _bdot(m1f_ref[...], col)
        a_buf[pl.ds(b, n1, stride=2 * n2), :] = a[:n1]
        a_buf[pl.ds(n2 + b, n1, stride=2 * n2), :] = a[n1:]
        return carry
    lax.fori_loop(0, n2, step1, 0)

    def step2(k1, carry):
        r0 = pl.multiple_of(k1 * 2 * n2, 2 * n2)
        h_ref[0, k1] = (_bdot(gt_ref[k1], a_buf[pl.ds(r0, 2 * n2), :]) * (1.0 / n)).astype(h_ref.dtype)
        return carry
    lax.fori_loop(0, n1, step2, 0)


def _filter_inputs(l, hy_w, f_w1, f_b1, f_w2, f_b2, f_w3, f_freq):
    hid = f_w1.shape[1]
    emb = f_w1.shape[0]
    bands = jnp.linspace(1e-4, FILT_BANDS - 1, FILT_BANDS, dtype=F32)
    bandv = jnp.zeros((1, LANES), F32).at[0, 1:1 + FILT_BANDS].set(bands)
    bandv = bandv.at[0, 1 + FILT_BANDS:1 + 2 * FILT_BANDS].set(bands)
    w1p = jnp.zeros((LANES, hid), F32).at[:emb].set(f_w1)
    w3 = f_w3.reshape(hid, HY_ORDER, N_DIR, hy_w).transpose(1, 2, 0, 3)
    max_decay = math.log(DECAY_TARGET) / FAST_DECAY_PCT
    min_decay = math.log(DECAY_TARGET) / SLOW_DECAY_PCT
    deltas = jnp.abs(jnp.linspace(min_decay, max_decay, hy_w, dtype=F32)).reshape(1, hy_w)
    args = (bandv, w1p, f_b1.reshape(1, hid), f_w2, f_b2.reshape(1, hid), f_freq.reshape(1, hid), w3, deltas)
    return args, hid


def _filter_specs(hid, ct, order_axis_only):
    if order_axis_only:
        c0 = lambda o: (0, 0)
        w3m = lambda o: (o, 0, 0, 0)
        dm = lambda o: (0, 0)
    else:
        c0 = lambda o, j: (0, 0)
        w3m = lambda o, j: (o, 0, 0, j)
        dm = lambda o, j: (0, j)
    return [pl.BlockSpec((1, LANES), c0), pl.BlockSpec((LANES, hid), c0), pl.BlockSpec((1, hid), c0),
            pl.BlockSpec((hid, hid), c0), pl.BlockSpec((1, hid), c0), pl.BlockSpec((1, hid), c0),
            pl.BlockSpec((1, N_DIR, hid, ct), w3m), pl.BlockSpec((1, ct), dm)]


def _filter_spectrum_direct(l, hy_w, filt_params, ff):
    args, hid = _filter_inputs(l, hy_w, *filt_params)
    n = 2 * l
    return pl.pallas_call(
        functools.partial(_filter_direct_kernel, l=l),
        grid=(HY_ORDER,),
        in_specs=_filter_specs(hid, hy_w, True) + [pl.BlockSpec((2 * n, n), lambda o: (0, 0))],
        out_specs=pl.BlockSpec((1, 2 * n, hy_w), lambda o: (o, 0, 0)),
        out_shape=jax.ShapeDtypeStruct((HY_ORDER, 2 * n, hy_w), F32),
        compiler_params=_cparams("arbitrary"),
        name="filter_direct",
    )(*args, ff)


def _filter_spectrum_two_level(l, hy_w, filt_params, m1f, gt, ct):
    args, hid = _filter_inputs(l, hy_w, *filt_params)
    n = 2 * l
    n2 = FFT_N2
    n1 = n // n2
    return pl.pallas_call(
        functools.partial(_filter_two_level_kernel, l=l, n2=n2, rows=min(n, 512)),
        grid=(HY_ORDER, hy_w // ct),
        in_specs=_filter_specs(hid, ct, False) + [
            pl.BlockSpec((2 * n1, n1), lambda o, j: (0, 0)),
            pl.BlockSpec((n1, 2 * n2, 2 * n2), lambda o, j: (0, 0, 0))],
        out_specs=pl.BlockSpec((1, n1, 2 * n2, ct), lambda o, j: (o, 0, 0, j)),
        out_shape=jax.ShapeDtypeStruct((HY_ORDER, n1, 2 * n2, hy_w), BF16),
        scratch_shapes=[pltpu.VMEM((n, ct), F32), pltpu.VMEM((n1 * 2 * n2, ct), F32)],
        compiler_params=_cparams("arbitrary", "arbitrary"),
        name="filter_two_level",
    )(*args, m1f, gt)


def _complex_mul(x, h, half):
    xr, xi = x[:half], x[half:]
    hr, hi = h[:half], h[half:]
    return jnp.concatenate([xr * hr - xi * hi, xr * hi + xi * hr], axis=0)


def _conv_direct_kernel(z_ref, gate_ref, bias_ref, h_ref, fwd_ref, inv_ref, o_ref):
    l = z_ref.shape[1]
    z = jnp.concatenate([z_ref[0], z_ref[1]], axis=0)
    x = _bdot(fwd_ref[...], z)
    y = _bdot(inv_ref[...], _complex_mul(x, h_ref[0], 2 * l))
    bias = bias_ref[0]
    o_ref[0] = gate_ref[0] * (y[:l] + bias * z_ref[0])
    o_ref[1] = gate_ref[1] * (y[l:] + bias * z_ref[1])


def _conv_two_level_kernel(z_ref, gate_ref, bias_ref, h_ref, m1_ref, m1i_ref, gt_ref, o_ref,
                           a_buf, *, n2):
    l = z_ref.shape[1]
    n1 = 2 * l // n2
    hn = n1 // 2

    def step1(b, carry):
        za = z_ref[0, pl.ds(b, hn, stride=n2), :]
        zb = z_ref[1, pl.ds(b, hn, stride=n2), :]
        a = _bdot(m1_ref[...], jnp.concatenate([za, zb], axis=0))
        a_buf[pl.ds(b, n1, stride=2 * n2), :] = a[:n1]
        a_buf[pl.ds(n2 + b, n1, stride=2 * n2), :] = a[n1:]
        return carry
    lax.fori_loop(0, n2, step1, 0)

    def step2(k1, carry):
        r0 = pl.multiple_of(k1 * 2 * n2, 2 * n2)
        x = _bdot(gt_ref[k1], a_buf[pl.ds(r0, 2 * n2), :])
        y = _complex_mul(x, h_ref[0, k1].astype(F32), n2).astype(BF16)
        a_buf[pl.ds(r0, 2 * n2), :] = lax.dot_general(gt_ref[k1], y, (((0,), (0,)), ((), ())),
                                                      preferred_element_type=F32)
        return carry
    lax.fori_loop(0, n1, step2, 0)

    def step3(b, carry):
        br = a_buf[pl.ds(b, n1, stride=2 * n2), :]
        bi = a_buf[pl.ds(n2 + b, n1, stride=2 * n2), :]
        y = _bdot(m1i_ref[...], jnp.concatenate([br, bi], axis=0))
        o_ref[0, pl.ds(b, hn, stride=n2), :] = y[:hn]
        o_ref[1, pl.ds(b, hn, stride=n2), :] = y[hn:]
        return carry
    lax.fori_loop(0, n2, step3, 0)

    bias = bias_ref[0]
    for s in range(2):
        o_ref[s] = gate_ref[s] * (o_ref[s] + bias * z_ref[s])


def _long_conv_gated(u, z, z_col, gate_col, spectrum, order, bias, tables, ct):
    bsz, l, _ = u.shape
    c = spectrum.shape[-1]
    nct = c // ct
    zspec = pl.BlockSpec((2, l, ct), lambda i, j: (i, 0, z_col * nct + j))
    gspec = pl.BlockSpec((2, l, ct), lambda i, j: (i, 0, gate_col * nct + j))
    bspec = pl.BlockSpec((1, 1, ct), lambda i, j: (order, 0, j))
    ospec = pl.BlockSpec((2, l, ct), lambda i, j: (i, 0, j))
    out_shape = jax.ShapeDtypeStruct((bsz, l, c), F32)
    bias3 = bias.reshape(HY_ORDER, 1, c)
    if len(tables) == 2:
        fwd, inv = tables
        n = 2 * l
        return pl.pallas_call(
            _conv_direct_kernel,
            grid=(bsz // 2, nct),
            in_specs=[zspec, gspec, bspec,
                      pl.BlockSpec((1, 2 * n, ct), lambda i, j: (order, 0, j)),
                      pl.BlockSpec(fwd.shape, lambda i, j: (0, 0)),
                      pl.BlockSpec(inv.shape, lambda i, j: (0, 0))],
            out_specs=ospec, out_shape=out_shape,
            compiler_params=_cparams("arbitrary", "arbitrary"),
            name="long_conv_direct",
        )(z, u, bias3, spectrum, fwd, inv)
    m1, m1i, gt = tables
    n2 = FFT_N2
    n1 = 2 * l // n2
    const2 = lambda i, j: (0, 0)
    const3 = lambda i, j: (0, 0, 0)
    return pl.pallas_call(
        functools.partial(_conv_two_level_kernel, n2=n2),
        grid=(nct, bsz // 2),
        in_specs=[pl.BlockSpec((2, l, ct), lambda j, i: (i, 0, z_col * nct + j)),
                  pl.BlockSpec((2, l, ct), lambda j, i: (i, 0, gate_col * nct + j)),
                  pl.BlockSpec((1, 1, ct), lambda j, i: (order, 0, j)),
                  pl.BlockSpec((1, n1, 2 * n2, ct), lambda j, i: (order, 0, 0, j)),
                  pl.BlockSpec(m1.shape, const2), pl.BlockSpec(m1i.shape, const2),
                  pl.BlockSpec(gt.shape, const3)],
        out_specs=pl.BlockSpec((2, l, ct), lambda j, i: (i, 0, j)),
        out_shape=out_shape,
        scratch_shapes=[pltpu.VMEM((n1 * 2 * n2, ct), F32)],
        compiler_params=_cparams("arbitrary", "arbitrary"),
        name="long_conv_two_level",
    )(z, u, bias3, spectrum, m1, m1i, gt)


def _hyena(z_hy, conv_w, conv_b, filt_params, hy_bias, row_w):
    bsz, l, c3 = z_hy.shape
    c = c3 // 3
    u = _short_conv(z_hy, conv_w, conv_b, row_w, silu=False)
    if 2 * l // FFT_N2 <= 4:
        fwd, filt = _dft_direct_tables(l)
        tables = (jnp.asarray(fwd, BF16), jnp.asarray(fwd.T, BF16))
        spectrum = _filter_spectrum_direct(l, c, filt_params, jnp.asarray(filt, BF16))
        ct = c
    else:
        m1, m1f, gt = _dft_two_level_tables(l, FFT_N2)
        tables = (jnp.asarray(m1, BF16), jnp.asarray(m1.T, BF16), jnp.asarray(gt, BF16))
        ct = LANES
        spectrum = _filter_spectrum_two_level(l, c, filt_params, jnp.asarray(m1f, BF16), tables[2], ct)
    z1 = _long_conv_gated(u, u, 0, 1, spectrum, 0, hy_bias, tables, ct)
    return _long_conv_gated(u, z1, 0, 2, spectrum, 1, hy_bias, tables, ct)


def _log_sigmoid(x):
    return jnp.minimum(x, 0.0) - jnp.log1p(jnp.exp(-jnp.abs(x)))


def _mlstm_chunk(q, k, v, icol, irow, bcol, brow, btot, mask, c, n, m):
    qb, kb, vb = q.astype(BF16), k.astype(BF16), v.astype(BF16)
    dm = jnp.where(mask, bcol - brow + irow, -jnp.inf)
    inter = bcol + m
    mj = jnp.maximum(jnp.max(dm, axis=-1, keepdims=True), inter)
    w_int = jnp.exp(inter - mj)
    s = lax.dot_general(qb, kb, (((1,), (1,)), ((), ())), preferred_element_type=F32) * jnp.exp(dm - mj)
    num = w_int * _bdot(qb, c) + _bdot(s, vb)
    den = w_int * jnp.sum(q * n, axis=-1, keepdims=True) + jnp.sum(s, axis=-1, keepdims=True)
    h = num / jnp.maximum(jnp.abs(den), jnp.exp(-mj))
    m_new = jnp.maximum(btot + m, jnp.max(btot - brow + irow, axis=-1, keepdims=True))
    a = jnp.exp(btot + m - m_new)
    wk = jnp.exp(btot - bcol + icol - m_new) * k
    c_new = a * c + _bdot(wk.T, vb)
    n_new = a * n + jnp.sum(wk, axis=0, keepdims=True)
    return h, c_new, n_new, m_new


def _mlstm_kernel(q_ref, k_ref, v_ref, o_ref, g_ref, gb_ref, ng_ref, c0_ref, n0_ref, m0_ref,
                  y_ref, c_ref, n_ref, m_ref, hf_buf, hb_buf, *, chunk):
    l, dh = q_ref.shape[1], q_ref.shape[2]
    nc = l // chunk
    scale = dh ** -0.5
    row = lax.broadcasted_iota(jnp.int32, (chunk, chunk), 0)
    col = lax.broadcasted_iota(jnp.int32, (chunk, chunk), 1)
    lower, upper = col <= row, col >= row
    tri_l, tri_u = lower.astype(F32), upper.astype(F32)
    lane = lax.broadcasted_iota(jnp.int32, (chunk, LANES), 1)

    def gates(r0):
        g = g_ref[0, 0, pl.ds(r0, chunk), :] + gb_ref[0]
        val = jnp.where((lane == 1) | (lane == 3), _log_sigmoid(g), g)
        w = jnp.where(lane == 1, _hdot(tri_l, val), jnp.where(lane == 3, _hdot(tri_u, val), val))
        return w, w.T

    def body(j, carry):
        cf, nf, mf, cb, nb, mb = carry
        rf = pl.multiple_of(j * chunk, chunk)
        w, wt = gates(rf)
        hf, cf, nf, mf = _mlstm_chunk(
            q_ref[0, pl.ds(rf, chunk), :], k_ref[0, pl.ds(rf, chunk), :] * scale, v_ref[0, pl.ds(rf, chunk), :],
            w[:, 0:1], wt[0:1, :], w[:, 1:2], wt[1:2, :], wt[1:2, chunk - 1:chunk], lower, cf, nf, mf)
        hf_buf[pl.ds(rf, chunk), :] = hf
        rb = pl.multiple_of((nc - 1 - j) * chunk, chunk)
        w, wt = gates(rb)
        hb, cb, nb, mb = _mlstm_chunk(
            q_ref[0, pl.ds(rb, chunk), :], k_ref[0, pl.ds(rb, chunk), :] * scale, v_ref[0, pl.ds(rb, chunk), :],
            w[:, 2:3], wt[2:3, :], w[:, 3:4], wt[3:4, :], wt[3:4, 0:1], upper, cb, nb, mb)
        hb_buf[pl.ds(rb, chunk), :] = hb
        return cf, nf, mf, cb, nb, mb

    init = (c0_ref[0, 0, 0], n0_ref[0, 0, 0:1, :], m0_ref[0, 0, 0:1, 0:1],
            c0_ref[0, 1, 0], n0_ref[0, 0, 1:2, :], m0_ref[0, 0, 1:2, 0:1])
    cf, nf, mf, cb, nb, mb = lax.fori_loop(0, nc, body, init)
    c_ref[0, 0, 0] = cf
    c_ref[0, 1, 0] = cb
    n_ref[0, 0, 0:1, :] = nf
    n_ref[0, 0, 1:2, :] = nb
    m_ref[0, 0, 0:1, :] = jnp.broadcast_to(mf, (1, LANES))
    m_ref[0, 0, 1:2, :] = jnp.broadcast_to(mb, (1, LANES))

    def finish(j, carry):
        r0 = pl.multiple_of(j * chunk, chunk)
        hs = hf_buf[pl.ds(r0, chunk), :] + hb_buf[pl.ds(r0, chunk), :]
        hs = hs * lax.rsqrt(jnp.mean(hs * hs, axis=-1, keepdims=True) + EPS) * ng_ref[...]
        y_ref[0, pl.ds(r0, chunk), :] = jax.nn.sigmoid(o_ref[0, pl.ds(r0, chunk), :]) * hs
        return carry
    lax.fori_loop(0, nc, finish, 0)


def _mlstm(qk, z_v, z_o, z_g, gate_b, norm_g, c0, n0, m0):
    bsz, l, ml = z_v.shape
    heads = gate_b.shape[-1]
    dh = ml // heads
    chunk = min(MLSTM_CHUNK, l)
    ng = N_DIR * 2 * heads
    gh = z_g[..., :ng].reshape(bsz, l, N_DIR * 2, heads).transpose(0, 3, 1, 2)
    gh = jnp.pad(gh, ((0, 0), (0, 0), (0, 0), (0, LANES - N_DIR * 2)))
    gb = jnp.pad(gate_b.reshape(N_DIR * 2, heads).T, ((0, 0), (0, LANES - N_DIR * 2))).reshape(heads, 1, LANES)
    n0h = n0.transpose(0, 2, 1, 3)
    m0h = jnp.broadcast_to(m0.transpose(0, 2, 1)[..., None], (bsz, heads, N_DIR, LANES))
    seq = lambda col0: pl.BlockSpec((1, l, dh), lambda b, h: (b, 0, col0 + h))
    cspec = pl.BlockSpec((1, N_DIR, 1, dh, dh), lambda b, h: (b, 0, h, 0, 0))
    sspec = pl.BlockSpec((1, 1, N_DIR, dh), lambda b, h: (b, h, 0, 0))
    mspec = pl.BlockSpec((1, 1, N_DIR, LANES), lambda b, h: (b, h, 0, 0))
    y, c, n, m = pl.pallas_call(
        functools.partial(_mlstm_kernel, chunk=chunk),
        grid=(bsz, heads),
        in_specs=[seq(0), seq(heads), seq(0), seq(0),
                  pl.BlockSpec((1, 1, l, LANES), lambda b, h: (b, h, 0, 0)),
                  pl.BlockSpec((1, 1, LANES), lambda b, h: (h, 0, 0)),
                  pl.BlockSpec((1, dh), lambda b, h: (0, h)),
                  cspec, sspec, mspec],
        out_specs=[seq(0), cspec, sspec, mspec],
        out_shape=[jax.ShapeDtypeStruct((bsz, l, ml), F32),
                   jax.ShapeDtypeStruct((bsz, N_DIR, heads, dh, dh), F32),
                   jax.ShapeDtypeStruct((bsz, heads, N_DIR, dh), F32),
                   jax.ShapeDtypeStruct((bsz, heads, N_DIR, LANES), F32)],
        scratch_shapes=[pltpu.VMEM((l, dh), F32), pltpu.VMEM((l, dh), F32)],
        compiler_params=_cparams("arbitrary", "arbitrary"),
        name="mlstm",
    )(qk, qk, z_v, z_o, gh, gb, norm_g.reshape(1, ml), c0, n0h, m0h)
    return y, (c, n.transpose(0, 2, 1, 3), m[..., 0].transpose(0, 2, 1))


def _lane_pack(cols, lane):
    out = jnp.zeros(lane.shape, cols[0].dtype)
    for j, colv in enumerate(cols):
        out = jnp.where(lane == j, colv, out)
    return out


def _post_kernel(yhy_ref, yml_ref, x_ref, mod_ref, wo_ref, n2g_ref, rw_ref, rb_ref,
                 x1_ref, h2_ref, ti_ref, tg_ref, rk_ref, cnt_ref, carry, *, n_exp, top_k):
    @pl.when(pl.program_id(0) == 0)
    def _():
        carry[...] = jnp.zeros_like(carry)

    hy_w = yhy_ref.shape[1]
    tm = x_ref.shape[0]
    proj = _bdot(yhy_ref[...], wo_ref[:hy_w, :]) + _bdot(yml_ref[...], wo_ref[hy_w:, :])
    x1 = x_ref[...] + mod_ref[0, 2:3, :] * proj
    x1_ref[...] = x1
    h2 = x1 * lax.rsqrt(jnp.mean(x1 * x1, axis=-1, keepdims=True) + EPS) * n2g_ref[...]
    h2 = h2 * (1.0 + mod_ref[0, 4:5, :]) + mod_ref[0, 3:4, :]
    h2_ref[...] = h2

    lane = lax.broadcasted_iota(jnp.int32, (tm, LANES), 1)
    work = jnp.where(lane < n_exp, _hdot(h2, rw_ref[...]) + rb_ref[...], -jnp.inf)
    vals, idxs, hots = [], [], []
    for _ in range(top_k):
        mx = jnp.max(work, axis=-1, keepdims=True)
        idx = jnp.min(jnp.where(work == mx, lane, LANES), axis=-1, keepdims=True)
        hot = lane == idx
        vals.append(mx)
        idxs.append(idx)
        hots.append(hot)
        work = jnp.where(hot, -jnp.inf, work)
    exps = [jnp.exp(v - vals[0]) for v in vals]
    tot = functools.reduce(lambda a, b: a + b, exps)
    ti_ref[...] = _lane_pack(idxs, lane)
    tg_ref[...] = _lane_pack([e / tot for e in exps], lane)

    hot_sum = functools.reduce(lambda a, b: a + b, [h.astype(F32) for h in hots])
    row = lax.broadcasted_iota(jnp.int32, (tm, tm), 0)
    col = lax.broadcasted_iota(jnp.int32, (tm, tm), 1)
    before = _bdot((col < row).astype(F32), hot_sum) + carry[...]
    ranks = [jnp.sum(jnp.where(h, before, 0.0), axis=-1, keepdims=True).astype(jnp.int32) for h in hots]
    rk_ref[...] = _lane_pack(ranks, lane)
    carry[...] = carry[...] + jnp.sum(hot_sum, axis=0, keepdims=True)
    cnt_ref[...] = jnp.broadcast_to(carry[...], cnt_ref.shape)


def _post(y_hy, y_ml, x, mods, w_out, norm2_g, r_w, r_b, n_ctx_tiles, tiles_per_lat, tm):
    t, d = x.shape
    n_exp = r_w.shape[1]
    rwp = jnp.pad(r_w, ((0, 0), (0, LANES - n_exp)))
    rbp = jnp.pad(r_b, (0, LANES - n_exp)).reshape(1, LANES)
    row = lambda wd: pl.BlockSpec((tm, wd), lambda i: (i, 0))
    const = lambda shape: pl.BlockSpec(shape, lambda i: (0, 0))
    return pl.pallas_call(
        functools.partial(_post_kernel, n_exp=n_exp, top_k=TOP_K),
        grid=(t // tm,),
        in_specs=[row(y_hy.shape[1]), row(y_ml.shape[1]), row(d),
                  pl.BlockSpec((1, 6, d), _mod_index_map(n_ctx_tiles, tiles_per_lat)),
                  const(w_out.shape), const((1, d)), const((d, LANES)), const((1, LANES))],
        out_specs=[row(d), row(d), row(LANES), row(LANES), row(LANES), const((SUBLANES, LANES))],
        out_shape=[jax.ShapeDtypeStruct((t, d), F32), jax.ShapeDtypeStruct((t, d), F32),
                   jax.ShapeDtypeStruct((t, LANES), jnp.int32), jax.ShapeDtypeStruct((t, LANES), F32),
                   jax.ShapeDtypeStruct((t, LANES), jnp.int32), jax.ShapeDtypeStruct((SUBLANES, LANES), F32)],
        scratch_shapes=[pltpu.VMEM((1, LANES), F32)],
        compiler_params=_cparams("arbitrary"),
        name="post",
    )(y_hy, y_ml, x, mods, w_out.astype(BF16), norm2_g.reshape(1, d), rwp, rbp)


def _dest_kernel(ti_ref, rk_ref, ps_ref, d_ref, *, top_k):
    lane = lax.broadcasted_iota(jnp.int32, ti_ref.shape, 1)
    ti, rk = ti_ref[...], rk_ref[...]
    cols = []
    for j in range(top_k):
        start = jnp.sum(jnp.where(lane == ti[:, j:j + 1], ps_ref[...], 0.0), axis=-1, keepdims=True)
        cols.append(start.astype(jnp.int32) + rk[:, j:j + 1])
    d_ref[...] = _lane_pack(cols, lane)


def _dest_rows(ti, rk, pad_start, tm):
    t = ti.shape[0]
    row = pl.BlockSpec((tm, LANES), lambda i: (i, 0))
    return pl.pallas_call(
        functools.partial(_dest_kernel, top_k=TOP_K),
        grid=(t // tm,),
        in_specs=[row, row, pl.BlockSpec((1, LANES), lambda i: (0, 0))],
        out_specs=row,
        out_shape=jax.ShapeDtypeStruct((t, LANES), jnp.int32),
        compiler_params=_cparams("arbitrary"),
        name="dest_rows",
    )(ti, rk, pad_start)


def _dispatch_kernel(dest_ref, h_ref, zero_ref, xs_ref, sem, *, top_k):
    del zero_ref
    tt = h_ref.shape[0]

    def issue(r, carry):
        pltpu.make_async_copy(h_ref.at[pl.ds(r // top_k, 1)], xs_ref.at[pl.ds(dest_ref[0, 0, r], 1)], sem).start()
        return carry
    lax.fori_loop(0, tt * top_k, issue, 0)
    for _ in range(top_k):
        pltpu.make_async_copy(h_ref, xs_ref.at[pl.ds(0, tt)], sem).wait()


def _dispatch(dest, h2, n_rows, tt):
    t, d = h2.shape
    zeros = jnp.zeros((n_rows, d), F32)
    return pl.pallas_call(
        functools.partial(_dispatch_kernel, top_k=TOP_K),
        grid=(t // tt,),
        in_specs=[pl.BlockSpec((1, 1, tt * TOP_K), lambda i: (i, 0, 0), memory_space=pltpu.SMEM),
                  pl.BlockSpec((tt, d), lambda i: (i, 0)),
                  pl.BlockSpec(memory_space=pl.ANY)],
        out_specs=pl.BlockSpec(memory_space=pl.ANY),
        out_shape=jax.ShapeDtypeStruct((n_rows, d), F32),
        scratch_shapes=[pltpu.SemaphoreType.DMA(())],
        input_output_aliases={2: 0},
        compiler_params=_cparams("arbitrary"),
        name="dispatch",
    )(dest, h2, zeros)


def _ffn_kernel(be_ref, nu_ref, xs_ref, wg_ref, wl_ref, bg_ref, bl_ref, wd_ref, bd_ref, ys_ref, *, ft):
    del be_ref
    f = wg_ref.shape[2]

    @pl.when(pl.program_id(0) < nu_ref[0])
    def _():
        x = xs_ref[...].astype(BF16)
        for j, f0 in enumerate(range(0, f, ft)):
            g = jnp.dot(x, wg_ref[0, :, f0:f0 + ft], preferred_element_type=F32) + bg_ref[0, :, f0:f0 + ft]
            lin = jnp.dot(x, wl_ref[0, :, f0:f0 + ft], preferred_element_type=F32) + bl_ref[0, :, f0:f0 + ft]
            gate = jnp.minimum(g, SWIGLU_LIMIT)
            lin = jnp.clip(lin, -SWIGLU_LIMIT, SWIGLU_LIMIT)
            act = (lin + 1.0) * gate * jax.nn.sigmoid(SWIGLU_ALPHA * gate)
            part = jnp.dot(act.astype(BF16), wd_ref[0, f0:f0 + ft, :], preferred_element_type=F32)
            if j == 0:
                ys_ref[...] = part + bd_ref[0]
            else:
                ys_ref[...] += part

    @pl.when(pl.program_id(0) >= nu_ref[0])
    def _():
        ys_ref[...] = jnp.zeros_like(ys_ref)


def _ffn(block_e, n_used, xs, w_g, w_l, b_g, b_l, w_d, b_d, rows):
    n_rows, d = xs.shape
    n_exp, _, f = w_g.shape
    live = lambda i, nu: jnp.minimum(i, nu[0] - 1)
    wmap = lambda i, be, nu: (be[live(i, nu)], 0, 0)
    return pl.pallas_call(
        functools.partial(_ffn_kernel, ft=min(f, 512)),
        grid_spec=pltpu.PrefetchScalarGridSpec(
            num_scalar_prefetch=2,
            grid=(n_rows // rows,),
            in_specs=[pl.BlockSpec((rows, d), lambda i, be, nu: (live(i, nu), 0)),
                      pl.BlockSpec((1, d, f), wmap), pl.BlockSpec((1, d, f), wmap),
                      pl.BlockSpec((1, 1, f), wmap), pl.BlockSpec((1, 1, f), wmap),
                      pl.BlockSpec((1, f, d), wmap), pl.BlockSpec((1, 1, d), wmap)],
            out_specs=pl.BlockSpec((rows, d), lambda i, be, nu: (i, 0))),
        out_shape=jax.ShapeDtypeStruct((n_rows, d), F32),
        compiler_params=_cparams("arbitrary"),
        name="expert_ffn",
    )(block_e, n_used, xs, w_g, w_l, b_g.reshape(n_exp, 1, f), b_l.reshape(n_exp, 1, f),
      w_d, b_d.reshape(n_exp, 1, d))


def _combine_kernel(dest_ref, tg_ref, x1_ref, mod_ref, fg_ref, ys_ref, o_ref, ybuf, sem, *, top_k, final_norm):
    tt = x1_ref.shape[0]

    def issue(r, carry):
        pltpu.make_async_copy(ys_ref.at[pl.ds(dest_ref[0, 0, r], 1)],
                              ybuf.at[r % top_k, pl.ds(r // top_k, 1)], sem).start()
        return carry
    lax.fori_loop(0, tt * top_k, issue, 0)
    for j in range(top_k):
        pltpu.make_async_copy(ys_ref.at[pl.ds(0, tt)], ybuf.at[j], sem).wait()
    tg = tg_ref[...]
    moe = tg[:, 0:1] * ybuf[0]
    for j in range(1, top_k):
        moe = moe + tg[:, j:j + 1] * ybuf[j]
    x2 = x1_ref[...] + mod_ref[0, 5:6, :] * moe
    if final_norm:
        x2 = x2 * lax.rsqrt(jnp.mean(x2 * x2, axis=-1, keepdims=True) + EPS) * fg_ref[...]
    o_ref[...] = x2


def _combine(dest, tg, x1, mods, final_g, ys, n_ctx_tiles, tiles_per_lat, tt, final_norm):
    t, d = x1.shape
    return pl.pallas_call(
        functools.partial(_combine_kernel, top_k=TOP_K, final_norm=final_norm),
        grid=(t // tt,),
        in_specs=[pl.BlockSpec((1, 1, tt * TOP_K), lambda i: (i, 0, 0), memory_space=pltpu.SMEM),
                  pl.BlockSpec((tt, LANES), lambda i: (i, 0)),
                  pl.BlockSpec((tt, d), lambda i: (i, 0)),
                  pl.BlockSpec((1, 6, d), _mod_index_map(n_ctx_tiles, tiles_per_lat)),
                  pl.BlockSpec((1, d), lambda i: (0, 0)),
                  pl.BlockSpec(memory_space=pl.ANY)],
        out_specs=pl.BlockSpec((tt, d), lambda i: (i, 0)),
        out_shape=jax.ShapeDtypeStruct((t, d), F32),
        scratch_shapes=[pltpu.VMEM((TOP_K, tt, d), F32), pltpu.SemaphoreType.DMA(())],
        compiler_params=_cparams("arbitrary"),
        name="combine",
    )(dest, tg, x1, mods, final_g.reshape(1, d), ys)


def _moe_plan(counts, rows, n_blocks):
    n_exp = counts.shape[0]
    padded = (counts + rows - 1) // rows * rows
    pad_end = jnp.cumsum(padded)
    block_e = jnp.minimum(jnp.searchsorted(pad_end, jnp.arange(n_blocks, dtype=jnp.int32) * rows, side="right"),
                          n_exp - 1).astype(jnp.int32)
    n_used = (pad_end[-1:] // rows).astype(jnp.int32)
    pad_start = jnp.pad((pad_end - padded).astype(F32), (0, LANES - n_exp)).reshape(1, LANES)
    return pad_start, block_e, n_used


def _sequence_mixers(z_hy, z_qk, z_v, z_o, z_g, lw, state, row_w):
    (hy_cw, hy_cb, filt_params, hy_b, ml_cw, ml_cb, ml_gb, ml_ng) = lw
    y_hy = _hyena(z_hy, hy_cw, hy_cb, filt_params, hy_b, row_w)
    qk = _short_conv(z_qk, ml_cw, ml_cb, row_w, silu=True)
    y_ml, st = _mlstm(qk, z_v, z_o, z_g, ml_gb, ml_ng, *state)
    return y_hy, y_ml, st


def kernel(x_prompt, x_sample, state_mlstm_C, state_mlstm_n, state_mlstm_m, c, c_ctx, ada_w, ada_b, norm1_g,
           w_in, hy_conv_w, hy_conv_b, filt_w1, filt_b1, filt_w2, filt_b2, filt_w3, filt_freq, hy_bias,
           ml_conv_w, ml_conv_b, ml_gate_b, ml_norm_g, w_out, norm2_g, router_w, router_b, moe_w_gu,
           moe_b_gu, moe_w_down, moe_b_down, final_g):
    bp, lp, d = x_prompt.shape
    bs, ls, _ = x_sample.shape
    depth = ada_w.shape[0]
    heads = ml_gate_b.shape[-1]
    hy_w = hy_bias.shape[-1]
    ml_w = ml_norm_g.shape[-1]
    dh = ml_w // heads
    n_exp = router_w.shape[-1]
    t_ctx, t_lat = bp * lp, bs * ls
    t = t_ctx + t_lat
    tm = min(ROW_TILE, lp)
    tt = min(TOK_TILE, lp)
    ng = N_DIR * 2 * heads
    seg_widths = (3 * hy_w, 2 * ml_w, ml_w, ml_w)
    n_main = 3 * hy_w + 4 * ml_w
    n_blocks = -(-(t * TOP_K) // MOE_ROWS) + n_exp

    x = jnp.concatenate([x_prompt.reshape(t_ctx, d), x_sample.reshape(t_lat, d)], axis=0)
    cond = jnp.concatenate([c_ctx[None], c, jnp.zeros((SUBLANES - 1 - bs, d), F32)], axis=0)
    zero_state = (jnp.zeros((bp, N_DIR, heads, dh, dh), F32), jnp.zeros((bp, N_DIR, heads, dh), F32),
                  jnp.zeros((bp, N_DIR, heads), F32))
    new_c, new_n, new_m = [], [], []
    for l in range(depth):
        mods = _ada(cond, ada_w[l], ada_b[l]).reshape(SUBLANES, 6, d)
        w_gate = jnp.pad(w_in[l][:, n_main:], ((0, 0), (0, LANES - ng)))
        z = _inproj(x, mods, norm1_g[l], w_in[l][:, :n_main].astype(BF16), w_gate, seg_widths,
                    t_ctx // tm, ls // tm, tm)
        lw = (hy_conv_w[l], hy_conv_b[l],
              (filt_w1[l], filt_b1[l], filt_w2[l], filt_b2[l], filt_w3[l], filt_freq[l]), hy_bias[l],
              ml_conv_w[l], ml_conv_b[l], ml_gate_b[l], ml_norm_g[l])
        z_ctx = [a[:t_ctx].reshape(bp, lp, a.shape[1]) for a in z]
        z_lat = [a[t_ctx:].reshape(bs, ls, a.shape[1]) for a in z]
        cache = (state_mlstm_C[:, l], state_mlstm_n[:, l], state_mlstm_m[:, l])
        hy_c, ml_c, st = _sequence_mixers(*z_ctx, lw, zero_state, lp)
        hy_l, ml_l, _ = _sequence_mixers(*z_lat, lw, cache, GRID_W)
        new_c.append(st[0])
        new_n.append(st[1])
        new_m.append(st[2])
        y_hy = jnp.concatenate([hy_c.reshape(t_ctx, hy_w), hy_l.reshape(t_lat, hy_w)], axis=0)
        y_ml = jnp.concatenate([ml_c.reshape(t_ctx, ml_w), ml_l.reshape(t_lat, ml_w)], axis=0)
        x1, h2, ti, tg, rk, cnt = _post(y_hy, y_ml, x, mods, w_out[l], norm2_g[l], router_w[l], router_b[l],
                                        t_ctx // tm, ls // tm, tm)
        pad_start, block_e, n_used = _moe_plan(cnt[0, :n_exp].astype(jnp.int32), MOE_ROWS, n_blocks)
        dest = _dest_rows(ti, rk, pad_start, tm)[:, :TOP_K].reshape(t // tt, 1, tt * TOP_K)
        xs = _dispatch(dest, h2, n_blocks * MOE_ROWS, tt)
        w_g = moe_w_gu[l][..., 0::2].astype(BF16)
        w_l = moe_w_gu[l][..., 1::2].astype(BF16)
        ys = _ffn(block_e, n_used, xs, w_g, w_l, moe_b_gu[l][..., 0::2], moe_b_gu[l][..., 1::2],
                  moe_w_down[l].astype(BF16), moe_b_down[l], MOE_ROWS)
        x = _combine(dest, tg, x1, mods, final_g, ys, t_ctx // tt, ls // tt, tt, final_norm=l == depth - 1)
    y_prompt = x[:t_ctx].reshape(bp, lp, d)
    y_sample = x[t_ctx:].reshape(bs, ls, d)
    return (y_prompt, y_sample, jnp.stack(new_c, axis=1), jnp.stack(new_n, axis=1), jnp.stack(new_m, axis=1))
```

```python
import functools
import math

import numpy as np
import jax
import jax.numpy as jnp
from jax import lax
from jax.experimental import pallas as pl
from jax.experimental.pallas import tpu as pltpu

F32 = jnp.float32
BF16 = jnp.bfloat16
HIGHEST = lax.Precision.HIGHEST
EPS = 1e-6

LANES = 128
SUBLANES = 8
VMEM_LIMIT_BYTES = 56 * 1024 * 1024

GRID_W = 64
ML_HEADS = 4
N_DIR = 2
HY_ORDER = 2
FILT_BANDS = 8
DECAY_TARGET = 1e-2
FAST_DECAY_PCT = 0.3
SLOW_DECAY_PCT = 1.5
TOP_K = 4
SWIGLU_LIMIT = 7.0
SWIGLU_ALPHA = 1.702

FFT_N2 = 128
MLSTM_CHUNK = 128
MOE_ROWS = 512
ROW_TILE = 512
TOK_TILE = 256


def _cparams(*sem):
    return pltpu.CompilerParams(dimension_semantics=sem, vmem_limit_bytes=VMEM_LIMIT_BYTES)


def _lane_tile(c, cap):
    return max(t for t in range(LANES, min(c, cap) + 1, LANES) if c % t == 0)


def _bdot(a, b):
    return jnp.dot(a.astype(BF16), b.astype(BF16), preferred_element_type=F32)


def _hdot(a, b):
    return jnp.dot(a, b, precision=HIGHEST, preferred_element_type=F32)


def _ada_kernel(c_ref, w_ref, b_ref, o_ref):
    c = c_ref[...]
    o_ref[...] = _hdot(c * jax.nn.sigmoid(c), w_ref[...]) + b_ref[...]


def _ada(cond, w, b):
    r, d = cond.shape
    n = w.shape[1]
    tn = _lane_tile(n, 1024)
    return pl.pallas_call(
        _ada_kernel,
        grid=(n // tn,),
        in_specs=[pl.BlockSpec((r, d), lambda j: (0, 0)),
                  pl.BlockSpec((d, tn), lambda j: (0, j)),
                  pl.BlockSpec((1, tn), lambda j: (0, j))],
        out_specs=pl.BlockSpec((r, tn), lambda j: (0, j)),
        out_shape=jax.ShapeDtypeStruct((r, n), F32),
        compiler_params=_cparams("arbitrary"),
        name="ada",
    )(cond, w, b.reshape(1, n))


def _mod_index_map(n_ctx_tiles, tiles_per_lat):
    def index_map(i):
        return (jnp.where(i < n_ctx_tiles, 0, 1 + (i - n_ctx_tiles) // tiles_per_lat), 0, 0)
    return index_map


def _inproj_kernel(x_ref, mod_ref, g_ref, w_ref, wg_ref, *out_refs, offsets):
    x = x_ref[...]
    h = x * lax.rsqrt(jnp.mean(x * x, axis=-1, keepdims=True) + EPS) * g_ref[...]
    h = h * (1.0 + mod_ref[0, 1:2, :]) + mod_ref[0, 0:1, :]
    hb = h.astype(BF16)
    for o_ref, (lo, hi) in zip(out_refs[:-1], offsets):
        o_ref[...] = jnp.dot(hb, w_ref[:, lo:hi], preferred_element_type=F32)
    h_lo = (h - hb.astype(F32)).astype(BF16)
    g = jnp.dot(hb, wg_ref[...], preferred_element_type=F32)
    out_refs[-1][...] = (g[:, :LANES] + g[:, LANES:]
                         + jnp.dot(h_lo, wg_ref[:, :LANES], preferred_element_type=F32))


def _inproj(x, mods, norm_g, w_main, w_gate, seg_widths, n_ctx_tiles, tiles_per_lat, tm):
    t, d = x.shape
    offsets, lo = [], 0
    for wd in seg_widths:
        offsets.append((lo, lo + wd))
        lo += wd
    wg_hi = w_gate.astype(BF16)
    wg = jnp.concatenate([wg_hi, (w_gate - wg_hi.astype(F32)).astype(BF16)], axis=1)
    widths = tuple(seg_widths) + (LANES,)
    return pl.pallas_call(
        functools.partial(_inproj_kernel, offsets=tuple(offsets)),
        grid=(t // tm,),
        in_specs=[pl.BlockSpec((tm, d), lambda i: (i, 0)),
                  pl.BlockSpec((1, 6, d), _mod_index_map(n_ctx_tiles, tiles_per_lat)),
                  pl.BlockSpec((1, d), lambda i: (0, 0)),
                  pl.BlockSpec(w_main.shape, lambda i: (0, 0)),
                  pl.BlockSpec(wg.shape, lambda i: (0, 0))],
        out_specs=[pl.BlockSpec((tm, wd), lambda i: (i, 0)) for wd in widths],
        out_shape=[jax.ShapeDtypeStruct((t, wd), F32) for wd in widths],
        compiler_params=_cparams("arbitrary"),
        name="inproj",
    )(x, mods, norm_g.reshape(1, d), w_main, wg)


def _short_conv_kernel(x_ref, w_ref, b_ref, o_ref, *, row_w, silu):
    x = x_ref[0]
    l = x.shape[0]
    pos = lax.broadcasted_iota(jnp.int32, x.shape, 0) % row_w
    prev = jnp.where(pos == 0, 0.0, pltpu.roll(x, 1, 0))
    nxt = jnp.where(pos == row_w - 1, 0.0, pltpu.roll(x, l - 1, 0))
    y = prev * w_ref[0:1, :] + x * w_ref[1:2, :] + nxt * w_ref[2:3, :] + b_ref[...]
    if silu:
        y = y * jax.nn.sigmoid(y)
    o_ref[0] = y


def _short_conv(x, w, b, row_w, silu):
    bsz, l, c = x.shape
    ct = _lane_tile(c, 256)
    return pl.pallas_call(
        functools.partial(_short_conv_kernel, row_w=row_w, silu=silu),
        grid=(bsz, c // ct),
        in_specs=[pl.BlockSpec((1, l, ct), lambda i, j: (i, 0, j)),
                  pl.BlockSpec((3, ct), lambda i, j: (0, j)),
                  pl.BlockSpec((1, ct), lambda i, j: (0, j))],
        out_specs=pl.BlockSpec((1, l, ct), lambda i, j: (i, 0, j)),
        out_shape=jax.ShapeDtypeStruct(x.shape, F32),
        compiler_params=_cparams("arbitrary", "arbitrary"),
        name="short_conv",
    )(x, w, b.reshape(1, c))


def _dft_direct_tables(l):
    n = 2 * l
    k = np.arange(n)[:, None].astype(np.float64)
    t = np.arange(n)[None, :].astype(np.float64)
    ang = 2.0 * np.pi * ((k * t) % n) / n
    cm, sm = np.cos(ang), np.sin(ang)
    fwd = np.block([[cm[:, :l], sm[:, :l]], [-sm[:, :l], cm[:, :l]]])
    filt = np.concatenate([cm, -sm], axis=0)
    return fwd, filt


def _dft_two_level_tables(l, n2):
    n = 2 * l
    n1 = n // n2
    k1 = np.arange(n1)[:, None].astype(np.float64)
    a = np.arange(n1)[None, :].astype(np.float64)
    ang1 = 2.0 * np.pi * ((k1 * a) % n1) / n1
    c1, s1 = np.cos(ang1), np.sin(ang1)
    h = n1 // 2
    m1 = np.block([[c1[:, :h], s1[:, :h]], [-s1[:, :h], c1[:, :h]]])
    m1f = np.concatenate([c1, -s1], axis=0)
    kk = (np.arange(n1)[:, None, None] + n1 * np.arange(n2)[None, :, None]).astype(np.float64)
    b = np.arange(n2)[None, None, :].astype(np.float64)
    ang = 2.0 * np.pi * ((kk * b) % n) / n
    cg, sg = np.cos(ang), np.sin(ang)
    gt = np.concatenate([np.concatenate([cg, sg], axis=2),
                         np.concatenate([-sg, cg], axis=2)], axis=1)
    return m1, m1f, gt


def _circular_lag(n0, rows, l):
    n = n0 + lax.broadcasted_iota(jnp.int32, (rows, 1), 0)
    t = jnp.where(n < l, n, 2 * l - n).astype(F32)
    return n, t, t / float(max(l - 1, 1))


def _filter_hidden_kernel(bandv_ref, w1_ref, b1_ref, w2_ref, b2_ref, freq_ref, o_ref, *, l):
    rows = o_ref.shape[0]
    _, t, t01 = _circular_lag(pl.program_id(0) * rows, rows, l)
    lane = lax.broadcasted_iota(jnp.int32, (rows, LANES), 1)
    ang = (2.0 * math.pi / l) * t * bandv_ref[...]
    feats = jnp.where(lane == 0, t01,
                      jnp.where(lane <= FILT_BANDS, jnp.cos(ang),
                                jnp.where(lane <= 2 * FILT_BANDS, -jnp.sin(ang), 0.0)))
    fr = freq_ref[...]
    h = jnp.sin(fr * (_hdot(feats, w1_ref[...]) + b1_ref[...]))
    o_ref[...] = jnp.sin(fr * (_hdot(h, w2_ref[...]) + b2_ref[...]))


def _filter_hidden(l, f_w1, f_b1, f_w2, f_b2, f_freq):
    emb, hid = f_w1.shape
    n = 2 * l
    rows = min(n, 512)
    bands = jnp.linspace(1e-4, FILT_BANDS - 1, FILT_BANDS, dtype=F32)
    bandv = jnp.zeros((1, LANES), F32).at[0, 1:1 + FILT_BANDS].set(bands)
    bandv = bandv.at[0, 1 + FILT_BANDS:1 + 2 * FILT_BANDS].set(bands)
    w1p = jnp.zeros((LANES, hid), F32).at[:emb].set(f_w1)
    c0 = lambda i: (0, 0)
    return pl.pallas_call(
        functools.partial(_filter_hidden_kernel, l=l),
        grid=(n // rows,),
        in_specs=[pl.BlockSpec((1, LANES), c0), pl.BlockSpec((LANES, hid), c0), pl.BlockSpec((1, hid), c0),
                  pl.BlockSpec((hid, hid), c0), pl.BlockSpec((1, hid), c0), pl.BlockSpec((1, hid), c0)],
        out_specs=pl.BlockSpec((rows, hid), lambda i: (i, 0)),
        out_shape=jax.ShapeDtypeStruct((n, hid), F32),
        compiler_params=_cparams("arbitrary"),
        name="filter_hidden",
    )(bandv, w1p, f_b1.reshape(1, hid), f_w2, f_b2.reshape(1, hid), f_freq.reshape(1, hid))


def _filter_rows(n0, rows, l, hid_ref, w3_ref, delta_ref):
    n, _, t01 = _circular_lag(n0, rows, l)
    h = hid_ref[pl.ds(n0, rows), :]
    hf = _bdot(h, w3_ref[0, 0])
    hb = _bdot(h, w3_ref[0, 1])
    window = jnp.exp(-t01 * delta_ref[...])
    return jnp.where(n < l, hf, jnp.where(n > l, hb, 0.0)) * window


def _filter_direct_kernel(hid_ref, w3_ref, delta_ref, ff_ref, h_ref, *, l):
    hc = _filter_rows(0, 2 * l, l, hid_ref, w3_ref, delta_ref)
    h_ref[0] = _bdot(ff_ref[...], hc) * (1.0 / (2 * l))


def _filter_two_level_kernel(hid_ref, w3_ref, delta_ref, m1f_ref, gt_ref, h_ref, hc_buf, a_buf, *, l, n2, rows):
    n = 2 * l
    n1 = n // n2

    def fill(i, carry):
        r0 = pl.multiple_of(i * rows, rows)
        hc_buf[pl.ds(r0, rows), :] = _filter_rows(r0, rows, l, hid_ref, w3_ref, delta_ref)
        return carry
    lax.fori_loop(0, n // rows, fill, 0)

    def step1(b, carry):
        col = hc_buf[pl.ds(b, n1, stride=n2), :]
        a = _bdot(m1f_ref[...], col)
        a_buf[pl.ds(b, n1, stride=2 * n2), :] = a[:n1]
        a_buf[pl.ds(n2 + b, n1, stride=2 * n2), :] = a[n1:]
        return carry
    lax.fori_loop(0, n2, step1, 0)

    def step2(k1, carry):
        r0 = pl.multiple_of(k1 * 2 * n2, 2 * n2)
        h_ref[0, k1] = (_bdot(gt_ref[k1], a_buf[pl.ds(r0, 2 * n2), :]) * (1.0 / n)).astype(h_ref.dtype)
        return carry
    lax.fori_loop(0, n1, step2, 0)


def _filter_tail_inputs(hy_w, f_w3):
    hid = f_w3.shape[0]
    w3 = f_w3.reshape(hid, HY_ORDER, N_DIR, hy_w).transpose(1, 2, 0, 3)
    max_decay = math.log(DECAY_TARGET) / FAST_DECAY_PCT
    min_decay = math.log(DECAY_TARGET) / SLOW_DECAY_PCT
    deltas = jnp.abs(jnp.linspace(min_decay, max_decay, hy_w, dtype=F32)).reshape(1, hy_w)
    return w3, deltas


def _filter_spectrum_direct(l, hy_w, filt_params, ff):
    f_w1, f_b1, f_w2, f_b2, f_w3, f_freq = filt_params
    hidden = _filter_hidden(l, f_w1, f_b1, f_w2, f_b2, f_freq)
    w3, deltas = _filter_tail_inputs(hy_w, f_w3)
    hid = f_w3.shape[0]
    n = 2 * l
    return pl.pallas_call(
        functools.partial(_filter_direct_kernel, l=l),
        grid=(HY_ORDER,),
        in_specs=[pl.BlockSpec((n, hid), lambda o: (0, 0)),
                  pl.BlockSpec((1, N_DIR, hid, hy_w), lambda o: (o, 0, 0, 0)),
                  pl.BlockSpec((1, hy_w), lambda o: (0, 0)),
                  pl.BlockSpec((2 * n, n), lambda o: (0, 0))],
        out_specs=pl.BlockSpec((1, 2 * n, hy_w), lambda o: (o, 0, 0)),
        out_shape=jax.ShapeDtypeStruct((HY_ORDER, 2 * n, hy_w), F32),
        compiler_params=_cparams("arbitrary"),
        name="filter_direct",
    )(hidden, w3, deltas, ff)


def _filter_spectrum_two_level(l, hy_w, filt_params, m1f, gt, ct):
    f_w1, f_b1, f_w2, f_b2, f_w3, f_freq = filt_params
    hidden = _filter_hidden(l, f_w1, f_b1, f_w2, f_b2, f_freq)
    w3, deltas = _filter_tail_inputs(hy_w, f_w3)
    hid = f_w3.shape[0]
    n = 2 * l
    n2 = FFT_N2
    n1 = n // n2
    return pl.pallas_call(
        functools.partial(_filter_two_level_kernel, l=l, n2=n2, rows=min(n, 512)),
        grid=(HY_ORDER, hy_w // ct),
        in_specs=[pl.BlockSpec((n, hid), lambda o, j: (0, 0)),
                  pl.BlockSpec((1, N_DIR, hid, ct), lambda o, j: (o, 0, 0, j)),
                  pl.BlockSpec((1, ct), lambda o, j: (0, j)),
                  pl.BlockSpec((2 * n1, n1), lambda o, j: (0, 0)),
                  pl.BlockSpec((n1, 2 * n2, 2 * n2), lambda o, j: (0, 0, 0))],
        out_specs=pl.BlockSpec((1, n1, 2 * n2, ct), lambda o, j: (o, 0, 0, j)),
        out_shape=jax.ShapeDtypeStruct((HY_ORDER, n1, 2 * n2, hy_w), BF16),
        scratch_shapes=[pltpu.VMEM((n, ct), F32), pltpu.VMEM((n1 * 2 * n2, ct), F32)],
        compiler_params=_cparams("arbitrary", "arbitrary"),
        name="filter_two_level",
    )(hidden, w3, deltas, m1f, gt)


def _complex_mul(x, h, half):
    xr, xi = x[:half], x[half:]
    hr, hi = h[:half], h[half:]
    return jnp.concatenate([xr * hr - xi * hi, xr * hi + xi * hr], axis=0)


def _conv_direct_kernel(z_ref, gate_ref, bias_ref, h_ref, fwd_ref, inv_ref, o_ref):
    l = z_ref.shape[1]
    z = jnp.concatenate([z_ref[0], z_ref[1]], axis=0)
    x = _bdot(fwd_ref[...], z)
    y = _bdot(inv_ref[...], _complex_mul(x, h_ref[0], 2 * l))
    bias = bias_ref[0]
    o_ref[0] = gate_ref[0] * (y[:l] + bias * z_ref[0])
    o_ref[1] = gate_ref[1] * (y[l:] + bias * z_ref[1])


def _conv_two_level_kernel(z_ref, gate_ref, bias_ref, h_ref, m1_ref, m1i_ref, gt_ref, o_ref,
                           a_buf, *, n2):
    l = z_ref.shape[1]
    n1 = 2 * l // n2
    hn = n1 // 2

    def step1(b, carry):
        za = z_ref[0, pl.ds(b, hn, stride=n2), :]
        zb = z_ref[1, pl.ds(b, hn, stride=n2), :]
        a = _bdot(m1_ref[...], jnp.concatenate([za, zb], axis=0))
        a_buf[pl.ds(b, n1, stride=2 * n2), :] = a[:n1]
        a_buf[pl.ds(n2 + b, n1, stride=2 * n2), :] = a[n1:]
        return carry
    lax.fori_loop(0, n2, step1, 0)

    def step2(k1, carry):
        r0 = pl.multiple_of(k1 * 2 * n2, 2 * n2)
        x = _bdot(gt_ref[k1], a_buf[pl.ds(r0, 2 * n2), :])
        y = _complex_mul(x, h_ref[0, k1].astype(F32), n2).astype(BF16)
        a_buf[pl.ds(r0, 2 * n2), :] = lax.dot_general(gt_ref[k1], y, (((0,), (0,)), ((), ())),
                                                      preferred_element_type=F32)
        return carry
    lax.fori_loop(0, n1, step2, 0)

    def step3(b, carry):
        br = a_buf[pl.ds(b, n1, stride=2 * n2), :]
        bi = a_buf[pl.ds(n2 + b, n1, stride=2 * n2), :]
        y = _bdot(m1i_ref[...], jnp.concatenate([br, bi], axis=0))
        o_ref[0, pl.ds(b, hn, stride=n2), :] = y[:hn]
        o_ref[1, pl.ds(b, hn, stride=n2), :] = y[hn:]
        return carry
    lax.fori_loop(0, n2, step3, 0)

    bias = bias_ref[0]
    for s in range(2):
        o_ref[s] = gate_ref[s] * (o_ref[s] + bias * z_ref[s])


def _long_conv_gated(u, z, z_col, gate_col, spectrum, order, bias, tables, ct):
    bsz, l, _ = u.shape
    c = spectrum.shape[-1]
    nct = c // ct
    zspec = pl.BlockSpec((2, l, ct), lambda i, j: (i, 0, z_col * nct + j))
    gspec = pl.BlockSpec((2, l, ct), lambda i, j: (i, 0, gate_col * nct + j))
    bspec = pl.BlockSpec((1, 1, ct), lambda i, j: (order, 0, j))
    ospec = pl.BlockSpec((2, l, ct), lambda i, j: (i, 0, j))
    out_shape = jax.ShapeDtypeStruct((bsz, l, c), F32)
    bias3 = bias.reshape(HY_ORDER, 1, c)
    if len(tables) == 2:
        fwd, inv = tables
        n = 2 * l
        return pl.pallas_call(
            _conv_direct_kernel,
            grid=(bsz // 2, nct),
            in_specs=[zspec, gspec, bspec,
                      pl.BlockSpec((1, 2 * n, ct), lambda i, j: (order, 0, j)),
                      pl.BlockSpec(fwd.shape, lambda i, j: (0, 0)),
                      pl.BlockSpec(inv.shape, lambda i, j: (0, 0))],
            out_specs=ospec, out_shape=out_shape,
            compiler_params=_cparams("arbitrary", "arbitrary"),
            name="long_conv_direct",
        )(z, u, bias3, spectrum, fwd, inv)
    m1, m1i, gt = tables
    n2 = FFT_N2
    n1 = 2 * l // n2
    const2 = lambda i, j: (0, 0)
    const3 = lambda i, j: (0, 0, 0)
    return pl.pallas_call(
        functools.partial(_conv_two_level_kernel, n2=n2),
        grid=(nct, bsz // 2),
        in_specs=[pl.BlockSpec((2, l, ct), lambda j, i: (i, 0, z_col * nct + j)),
                  pl.BlockSpec((2, l, ct), lambda j, i: (i, 0, gate_col * nct + j)),
                  pl.BlockSpec((1, 1, ct), lambda j, i: (order, 0, j)),
                  pl.BlockSpec((1, n1, 2 * n2, ct), lambda j, i: (order, 0, 0, j)),
                  pl.BlockSpec(m1.shape, const2), pl.BlockSpec(m1i.shape, const2),
                  pl.BlockSpec(gt.shape, const3)],
        out_specs=pl.BlockSpec((2, l, ct), lambda j, i: (i, 0, j)),
        out_shape=out_shape,
        scratch_shapes=[pltpu.VMEM((n1 * 2 * n2, ct), F32)],
        compiler_params=_cparams("arbitrary", "arbitrary"),
        name="long_conv_two_level",
    )(z, u, bias3, spectrum, m1, m1i, gt)


def _hyena(z_hy, conv_w, conv_b, filt_params, hy_bias, row_w):
    bsz, l, c3 = z_hy.shape
    c = c3 // 3
    u = _short_conv(z_hy, conv_w, conv_b, row_w, silu=False)
    table = lambda a: jnp.asarray(a, F32).astype(BF16)
    if 2 * l // FFT_N2 <= 4:
        fwd, filt = _dft_direct_tables(l)
        tables = (table(fwd), table(fwd.T))
        spectrum = _filter_spectrum_direct(l, c, filt_params, table(filt))
        ct = c
    else:
        m1, m1f, gt = _dft_two_level_tables(l, FFT_N2)
        tables = (table(m1), table(m1.T), table(gt))
        ct = LANES
        spectrum = _filter_spectrum_two_level(l, c, filt_params, table(m1f), tables[2], ct)
    z1 = _long_conv_gated(u, u, 0, 1, spectrum, 0, hy_bias, tables, ct)
    return _long_conv_gated(u, z1, 0, 2, spectrum, 1, hy_bias, tables, ct)


def _log_sigmoid(x):
    return jnp.minimum(x, 0.0) - jnp.log1p(jnp.exp(-jnp.abs(x)))


def _mlstm_chunk(q, k, v, icol, irow, bcol, brow, btot, mask, c, n, m):
    qb, kb, vb = q.astype(BF16), k.astype(BF16), v.astype(BF16)
    dm = jnp.where(mask, bcol - brow + irow, -jnp.inf)
    inter = bcol + m
    mj = jnp.maximum(jnp.max(dm, axis=-1, keepdims=True), inter)
    w_int = jnp.exp(inter - mj)
    s = lax.dot_general(qb, kb, (((1,), (1,)), ((), ())), preferred_element_type=F32) * jnp.exp(dm - mj)
    num = w_int * _bdot(qb, c) + _bdot(s, vb)
    den = w_int * jnp.sum(q * n, axis=-1, keepdims=True) + jnp.sum(s, axis=-1, keepdims=True)
    h = num / jnp.maximum(jnp.abs(den), jnp.exp(-mj))
    m_new = jnp.maximum(btot + m, jnp.max(btot - brow + irow, axis=-1, keepdims=True))
    a = jnp.exp(btot + m - m_new)
    wk = jnp.exp(btot - bcol + icol - m_new) * k
    c_new = a * c + _bdot(wk.T, vb)
    n_new = a * n + jnp.sum(wk, axis=0, keepdims=True)
    return h, c_new, n_new, m_new


def _mlstm_kernel(q_ref, k_ref, v_ref, o_ref, g_ref, gb_ref, ng_ref, c0_ref, n0_ref, m0_ref,
                  y_ref, c_ref, n_ref, m_ref, hf_buf, hb_buf, *, chunk):
    l, dh = q_ref.shape[1], q_ref.shape[2]
    nc = l // chunk
    scale = dh ** -0.5
    row = lax.broadcasted_iota(jnp.int32, (chunk, chunk), 0)
    col = lax.broadcasted_iota(jnp.int32, (chunk, chunk), 1)
    lower, upper = col <= row, col >= row
    tri_l, tri_u = lower.astype(F32), upper.astype(F32)
    lane = lax.broadcasted_iota(jnp.int32, (chunk, LANES), 1)

    def gates(r0):
        g = g_ref[0, 0, pl.ds(r0, chunk), :] + gb_ref[0]
        val = jnp.where((lane == 1) | (lane == 3), _log_sigmoid(g), g)
        w = jnp.where(lane == 1, _hdot(tri_l, val), jnp.where(lane == 3, _hdot(tri_u, val), val))
        return w, w.T

    def body(j, carry):
        cf, nf, mf, cb, nb, mb = carry
        rf = pl.multiple_of(j * chunk, chunk)
        w, wt = gates(rf)
        hf, cf, nf, mf = _mlstm_chunk(
            q_ref[0, pl.ds(rf, chunk), :], k_ref[0, pl.ds(rf, chunk), :] * scale, v_ref[0, pl.ds(rf, chunk), :],
            w[:, 0:1], wt[0:1, :], w[:, 1:2], wt[1:2, :], wt[1:2, chunk - 1:chunk], lower, cf, nf, mf)
        hf_buf[pl.ds(rf, chunk), :] = hf
        rb = pl.multiple_of((nc - 1 - j) * chunk, chunk)
        w, wt = gates(rb)
        hb, cb, nb, mb = _mlstm_chunk(
            q_ref[0, pl.ds(rb, chunk), :], k_ref[0, pl.ds(rb, chunk), :] * scale, v_ref[0, pl.ds(rb, chunk), :],
            w[:, 2:3], wt[2:3, :], w[:, 3:4], wt[3:4, :], wt[3:4, 0:1], upper, cb, nb, mb)
        hb_buf[pl.ds(rb, chunk), :] = hb
        return cf, nf, mf, cb, nb, mb

    init = (c0_ref[0, 0, 0], n0_ref[0, 0, 0:1, :], m0_ref[0, 0, 0:1, 0:1],
            c0_ref[0, 1, 0], n0_ref[0, 0, 1:2, :], m0_ref[0, 0, 1:2, 0:1])
    cf, nf, mf, cb, nb, mb = lax.fori_loop(0, nc, body, init)
    c_ref[0, 0, 0] = cf
    c_ref[0, 1, 0] = cb
    n_ref[0, 0, 0:1, :] = nf
    n_ref[0, 0, 1:2, :] = nb
    m_ref[0, 0, 0:1, :] = jnp.broadcast_to(mf, (1, LANES))
    m_ref[0, 0, 1:2, :] = jnp.broadcast_to(mb, (1, LANES))

    def finish(j, carry):
        r0 = pl.multiple_of(j * chunk, chunk)
        hs = hf_buf[pl.ds(r0, chunk), :] + hb_buf[pl.ds(r0, chunk), :]
        hs = hs * lax.rsqrt(jnp.mean(hs * hs, axis=-1, keepdims=True) + EPS) * ng_ref[...]
        y_ref[0, pl.ds(r0, chunk), :] = jax.nn.sigmoid(o_ref[0, pl.ds(r0, chunk), :]) * hs
        return carry
    lax.fori_loop(0, nc, finish, 0)


def _mlstm(qk, z_v, z_o, z_g, gate_b, norm_g, c0, n0, m0):
    bsz, l, ml = z_v.shape
    heads = gate_b.shape[-1]
    dh = ml // heads
    chunk = min(MLSTM_CHUNK, l)
    ng = N_DIR * 2 * heads
    gh = z_g[..., :ng].reshape(bsz, l, N_DIR * 2, heads).transpose(0, 3, 1, 2)
    gh = jnp.pad(gh, ((0, 0), (0, 0), (0, 0), (0, LANES - N_DIR * 2)))
    gb = jnp.pad(gate_b.reshape(N_DIR * 2, heads).T, ((0, 0), (0, LANES - N_DIR * 2))).reshape(heads, 1, LANES)
    n0h = n0.transpose(0, 2, 1, 3)
    m0h = jnp.broadcast_to(m0.transpose(0, 2, 1)[..., None], (bsz, heads, N_DIR, LANES))
    seq = lambda col0: pl.BlockSpec((1, l, dh), lambda b, h: (b, 0, col0 + h))
    cspec = pl.BlockSpec((1, N_DIR, 1, dh, dh), lambda b, h: (b, 0, h, 0, 0))
    sspec = pl.BlockSpec((1, 1, N_DIR, dh), lambda b, h: (b, h, 0, 0))
    mspec = pl.BlockSpec((1, 1, N_DIR, LANES), lambda b, h: (b, h, 0, 0))
    y, c, n, m = pl.pallas_call(
        functools.partial(_mlstm_kernel, chunk=chunk),
        grid=(bsz, heads),
        in_specs=[seq(0), seq(heads), seq(0), seq(0),
                  pl.BlockSpec((1, 1, l, LANES), lambda b, h: (b, h, 0, 0)),
                  pl.BlockSpec((1, 1, LANES), lambda b, h: (h, 0, 0)),
                  pl.BlockSpec((1, dh), lambda b, h: (0, h)),
                  cspec, sspec, mspec],
        out_specs=[seq(0), cspec, sspec, mspec],
        out_shape=[jax.ShapeDtypeStruct((bsz, l, ml), F32),
                   jax.ShapeDtypeStruct((bsz, N_DIR, heads, dh, dh), F32),
                   jax.ShapeDtypeStruct((bsz, heads, N_DIR, dh), F32),
                   jax.ShapeDtypeStruct((bsz, heads, N_DIR, LANES), F32)],
        scratch_shapes=[pltpu.VMEM((l, dh), F32), pltpu.VMEM((l, dh), F32)],
        compiler_params=_cparams("arbitrary", "arbitrary"),
        name="mlstm",
    )(qk, qk, z_v, z_o, gh, gb, norm_g.reshape(1, ml), c0, n0h, m0h)
    return y, (c, n.transpose(0, 2, 1, 3), m[..., 0].transpose(0, 2, 1))


def _lane_pack(cols, lane):
    out = jnp.zeros(lane.shape, cols[0].dtype)
    for j, colv in enumerate(cols):
        out = jnp.where(lane == j, colv, out)
    return out


def _post_kernel(yhy_ref, yml_ref, x_ref, mod_ref, wo_ref, n2g_ref, rw_ref, rb_ref,
                 x1_ref, h2_ref, ti_ref, tg_ref, rk_ref, cnt_ref, carry, *, n_exp, top_k):
    @pl.when(pl.program_id(0) == 0)
    def _():
        carry[...] = jnp.zeros_like(carry)

    hy_w = yhy_ref.shape[1]
    tm = x_ref.shape[0]
    proj = _bdot(yhy_ref[...], wo_ref[:hy_w, :]) + _bdot(yml_ref[...], wo_ref[hy_w:, :])
    x1 = x_ref[...] + mod_ref[0, 2:3, :] * proj
    x1_ref[...] = x1
    h2 = x1 * lax.rsqrt(jnp.mean(x1 * x1, axis=-1, keepdims=True) + EPS) * n2g_ref[...]
    h2 = h2 * (1.0 + mod_ref[0, 4:5, :]) + mod_ref[0, 3:4, :]
    h2_ref[...] = h2

    lane = lax.broadcasted_iota(jnp.int32, (tm, LANES), 1)
    work = jnp.where(lane < n_exp, _hdot(h2, rw_ref[...]) + rb_ref[...], -jnp.inf)
    vals, idxs, hots = [], [], []
    for _ in range(top_k):
        mx = jnp.max(work, axis=-1, keepdims=True)
        idx = jnp.min(jnp.where(work == mx, lane, LANES), axis=-1, keepdims=True)
        hot = lane == idx
        vals.append(mx)
        idxs.append(idx)
        hots.append(hot)
        work = jnp.where(hot, -jnp.inf, work)
    exps = [jnp.exp(v - vals[0]) for v in vals]
    tot = functools.reduce(lambda a, b: a + b, exps)
    ti_ref[...] = _lane_pack(idxs, lane)
    tg_ref[...] = _lane_pack([e / tot for e in exps], lane)

    hot_sum = functools.reduce(lambda a, b: a + b, [h.astype(F32) for h in hots])
    row = lax.broadcasted_iota(jnp.int32, (tm, tm), 0)
    col = lax.broadcasted_iota(jnp.int32, (tm, tm), 1)
    before = _bdot((col < row).astype(F32), hot_sum) + carry[...]
    ranks = [jnp.sum(jnp.where(h, before, 0.0), axis=-1, keepdims=True).astype(jnp.int32) for h in hots]
    rk_ref[...] = _lane_pack(ranks, lane)
    carry[...] = carry[...] + jnp.sum(hot_sum, axis=0, keepdims=True)
    cnt_ref[...] = jnp.broadcast_to(carry[...], cnt_ref.shape)


def _post(y_hy, y_ml, x, mods, w_out, norm2_g, r_w, r_b, n_ctx_tiles, tiles_per_lat, tm):
    t, d = x.shape
    n_exp = r_w.shape[1]
    rwp = jnp.pad(r_w, ((0, 0), (0, LANES - n_exp)))
    rbp = jnp.pad(r_b, (0, LANES - n_exp)).reshape(1, LANES)
    row = lambda wd: pl.BlockSpec((tm, wd), lambda i: (i, 0))
    const = lambda shape: pl.BlockSpec(shape, lambda i: (0, 0))
    return pl.pallas_call(
        functools.partial(_post_kernel, n_exp=n_exp, top_k=TOP_K),
        grid=(t // tm,),
        in_specs=[row(y_hy.shape[1]), row(y_ml.shape[1]), row(d),
                  pl.BlockSpec((1, 6, d), _mod_index_map(n_ctx_tiles, tiles_per_lat)),
                  const(w_out.shape), const((1, d)), const((d, LANES)), const((1, LANES))],
        out_specs=[row(d), row(d), row(LANES), row(LANES), row(LANES), const((SUBLANES, LANES))],
        out_shape=[jax.ShapeDtypeStruct((t, d), F32), jax.ShapeDtypeStruct((t, d), F32),
                   jax.ShapeDtypeStruct((t, LANES), jnp.int32), jax.ShapeDtypeStruct((t, LANES), F32),
                   jax.ShapeDtypeStruct((t, LANES), jnp.int32), jax.ShapeDtypeStruct((SUBLANES, LANES), F32)],
        scratch_shapes=[pltpu.VMEM((1, LANES), F32)],
        compiler_params=_cparams("arbitrary"),
        name="post",
    )(y_hy, y_ml, x, mods, w_out.astype(BF16), norm2_g.reshape(1, d), rwp, rbp)


def _dest_kernel(ti_ref, rk_ref, ps_ref, d_ref, *, top_k):
    lane = lax.broadcasted_iota(jnp.int32, ti_ref.shape, 1)
    ti, rk = ti_ref[...], rk_ref[...]
    cols = []
    for j in range(top_k):
        start = jnp.sum(jnp.where(lane == ti[:, j:j + 1], ps_ref[...], 0.0), axis=-1, keepdims=True)
        cols.append(start.astype(jnp.int32) + rk[:, j:j + 1])
    d_ref[...] = _lane_pack(cols, lane)


def _dest_rows(ti, rk, pad_start, tm):
    t = ti.shape[0]
    row = pl.BlockSpec((tm, LANES), lambda i: (i, 0))
    return pl.pallas_call(
        functools.partial(_dest_kernel, top_k=TOP_K),
        grid=(t // tm,),
        in_specs=[row, row, pl.BlockSpec((1, LANES), lambda i: (0, 0))],
        out_specs=row,
        out_shape=jax.ShapeDtypeStruct((t, LANES), jnp.int32),
        compiler_params=_cparams("arbitrary"),
        name="dest_rows",
    )(ti, rk, pad_start)


def _dispatch_kernel(dest_ref, h_ref, zero_ref, xs_ref, sem, *, top_k):
    del zero_ref
    tt = h_ref.shape[0]

    def issue(r, carry):
        pltpu.make_async_copy(h_ref.at[pl.ds(r // top_k, 1)], xs_ref.at[pl.ds(dest_ref[0, 0, r], 1)], sem).start()
        return carry
    lax.fori_loop(0, tt * top_k, issue, 0)
    for _ in range(top_k):
        pltpu.make_async_copy(h_ref, xs_ref.at[pl.ds(0, tt)], sem).wait()


def _dispatch(dest, h2, n_rows, tt):
    t, d = h2.shape
    zeros = jnp.zeros((n_rows, d), F32)
    return pl.pallas_call(
        functools.partial(_dispatch_kernel, top_k=TOP_K),
        grid=(t // tt,),
        in_specs=[pl.BlockSpec((1, 1, tt * TOP_K), lambda i: (i, 0, 0), memory_space=pltpu.SMEM),
                  pl.BlockSpec((tt, d), lambda i: (i, 0)),
                  pl.BlockSpec(memory_space=pl.ANY)],
        out_specs=pl.BlockSpec(memory_space=pl.ANY),
        out_shape=jax.ShapeDtypeStruct((n_rows, d), F32),
        scratch_shapes=[pltpu.SemaphoreType.DMA(())],
        input_output_aliases={2: 0},
        compiler_params=_cparams("arbitrary"),
        name="dispatch",
    )(dest, h2, zeros)


def _deinterleave_table():
    p = np.zeros((2 * LANES, 2 * LANES), np.float32)
    j = np.arange(LANES)
    p[2 * j, j] = 1.0
    p[2 * j + 1, LANES + j] = 1.0
    return p


def _ffn_kernel(be_ref, nu_ref, xs_ref, wgu_ref, bg_ref, bl_ref, wd_ref, bd_ref, perm_ref, ys_ref,
                wg_buf, wl_buf, wd_buf, *, ft):
    i = pl.program_id(0)
    f = wg_buf.shape[1]
    live = i < nu_ref[0]

    @pl.when(live & ((i == 0) | (be_ref[i] != be_ref[jnp.maximum(i - 1, 0)])))
    def _():
        for j in range(f // LANES):
            blk = wgu_ref[0, :, 2 * LANES * j:2 * LANES * (j + 1)].astype(BF16)
            split = jnp.dot(blk, perm_ref[...], preferred_element_type=F32)
            wg_buf[:, LANES * j:LANES * (j + 1)] = split[:, :LANES].astype(BF16)
            wl_buf[:, LANES * j:LANES * (j + 1)] = split[:, LANES:].astype(BF16)
        wd_buf[...] = wd_ref[0].astype(BF16)

    @pl.when(live)
    def _():
        x = xs_ref[...].astype(BF16)
        for j, f0 in enumerate(range(0, f, ft)):
            g = jnp.dot(x, wg_buf[:, f0:f0 + ft], preferred_element_type=F32) + bg_ref[0, :, f0:f0 + ft]
            lin = jnp.dot(x, wl_buf[:, f0:f0 + ft], preferred_element_type=F32) + bl_ref[0, :, f0:f0 + ft]
            gate = jnp.minimum(g, SWIGLU_LIMIT)
            lin = jnp.clip(lin, -SWIGLU_LIMIT, SWIGLU_LIMIT)
            act = (lin + 1.0) * gate * jax.nn.sigmoid(SWIGLU_ALPHA * gate)
            part = jnp.dot(act.astype(BF16), wd_buf[f0:f0 + ft, :], preferred_element_type=F32)
            if j == 0:
                ys_ref[...] = part + bd_ref[0]
            else:
                ys_ref[...] += part

    @pl.when(jnp.logical_not(live))
    def _():
        ys_ref[...] = jnp.zeros_like(ys_ref)


def _ffn(block_e, n_used, xs, w_gu, b_gu, w_d, b_d, rows):
    n_rows, d = xs.shape
    n_exp, _, f2 = w_gu.shape
    f = f2 // 2
    live = lambda i, nu: jnp.minimum(i, nu[0] - 1)
    wmap = lambda i, be, nu: (be[live(i, nu)], 0, 0)
    perm = jnp.asarray(_deinterleave_table(), BF16)
    return pl.pallas_call(
        functools.partial(_ffn_kernel, ft=min(f, 512)),
        grid_spec=pltpu.PrefetchScalarGridSpec(
            num_scalar_prefetch=2,
            grid=(n_rows // rows,),
            in_specs=[pl.BlockSpec((rows, d), lambda i, be, nu: (live(i, nu), 0)),
                      pl.BlockSpec((1, d, f2), wmap),
                      pl.BlockSpec((1, 1, f), wmap), pl.BlockSpec((1, 1, f), wmap),
                      pl.BlockSpec((1, f, d), wmap), pl.BlockSpec((1, 1, d), wmap),
                      pl.BlockSpec(perm.shape, lambda i, be, nu: (0, 0))],
            out_specs=pl.BlockSpec((rows, d), lambda i, be, nu: (i, 0)),
            scratch_shapes=[pltpu.VMEM((d, f), BF16), pltpu.VMEM((d, f), BF16), pltpu.VMEM((f, d), BF16)]),
        out_shape=jax.ShapeDtypeStruct((n_rows, d), F32),
        compiler_params=_cparams("arbitrary"),
        name="expert_ffn",
    )(block_e, n_used, xs, w_gu, b_gu[:, 0::2].reshape(n_exp, 1, f), b_gu[:, 1::2].reshape(n_exp, 1, f),
      w_d, b_d.reshape(n_exp, 1, d), perm)


def _combine_kernel(dest_ref, tg_ref, x1_ref, mod_ref, fg_ref, ys_ref, o_ref, ybuf, sem, *, top_k, final_norm):
    tt = x1_ref.shape[0]

    def issue(r, carry):
        pltpu.make_async_copy(ys_ref.at[pl.ds(dest_ref[0, 0, r], 1)],
                              ybuf.at[r % top_k, pl.ds(r // top_k, 1)], sem).start()
        return carry
    lax.fori_loop(0, tt * top_k, issue, 0)
    for j in range(top_k):
        pltpu.make_async_copy(ys_ref.at[pl.ds(0, tt)], ybuf.at[j], sem).wait()
    tg = tg_ref[...]
    moe = tg[:, 0:1] * ybuf[0]
    for j in range(1, top_k):
        moe = moe + tg[:, j:j + 1] * ybuf[j]
    x2 = x1_ref[...] + mod_ref[0, 5:6, :] * moe
    if final_norm:
        x2 = x2 * lax.rsqrt(jnp.mean(x2 * x2, axis=-1, keepdims=True) + EPS) * fg_ref[...]
    o_ref[...] = x2


def _combine(dest, tg, x1, mods, final_g, ys, n_ctx_tiles, tiles_per_lat, tt, final_norm):
    t, d = x1.shape
    return pl.pallas_call(
        functools.partial(_combine_kernel, top_k=TOP_K, final_norm=final_norm),
        grid=(t // tt,),
        in_specs=[pl.BlockSpec((1, 1, tt * TOP_K), lambda i: (i, 0, 0), memory_space=pltpu.SMEM),
                  pl.BlockSpec((tt, LANES), lambda i: (i, 0)),
                  pl.BlockSpec((tt, d), lambda i: (i, 0)),
                  pl.BlockSpec((1, 6, d), _mod_index_map(n_ctx_tiles, tiles_per_lat)),
                  pl.BlockSpec((1, d), lambda i: (0, 0)),
                  pl.BlockSpec(memory_space=pl.ANY)],
        out_specs=pl.BlockSpec((tt, d), lambda i: (i, 0)),
        out_shape=jax.ShapeDtypeStruct((t, d), F32),
        scratch_shapes=[pltpu.VMEM((TOP_K, tt, d), F32), pltpu.SemaphoreType.DMA(())],
        compiler_params=_cparams("arbitrary"),
        name="combine",
    )(dest, tg, x1, mods, final_g.reshape(1, d), ys)


def _moe_plan(counts, rows, n_blocks):
    n_exp = counts.shape[0]
    padded = (counts + rows - 1) // rows * rows
    pad_end = jnp.cumsum(padded)
    block_row = jnp.arange(n_blocks, dtype=jnp.int32) * rows
    block_e = jnp.minimum(jnp.sum(pad_end[None, :] <= block_row[:, None], axis=1), n_exp - 1).astype(jnp.int32)
    n_used = (pad_end[-1:] // rows).astype(jnp.int32)
    pad_start = jnp.pad((pad_end - padded).astype(F32), (0, LANES - n_exp)).reshape(1, LANES)
    return pad_start, block_e, n_used


def _sequence_mixers(z_hy, z_qk, z_v, z_o, z_g, lw, state, row_w):
    (hy_cw, hy_cb, filt_params, hy_b, ml_cw, ml_cb, ml_gb, ml_ng) = lw
    y_hy = _hyena(z_hy, hy_cw, hy_cb, filt_params, hy_b, row_w)
    qk = _short_conv(z_qk, ml_cw, ml_cb, row_w, silu=True)
    y_ml, st = _mlstm(qk, z_v, z_o, z_g, ml_gb, ml_ng, *state)
    return y_hy, y_ml, st


def kernel(x_prompt, x_sample, state_mlstm_C, state_mlstm_n, state_mlstm_m, c, c_ctx, ada_w, ada_b, norm1_g,
           w_in, hy_conv_w, hy_conv_b, filt_w1, filt_b1, filt_w2, filt_b2, filt_w3, filt_freq, hy_bias,
           ml_conv_w, ml_conv_b, ml_gate_b, ml_norm_g, w_out, norm2_g, router_w, router_b, moe_w_gu,
           moe_b_gu, moe_w_down, moe_b_down, final_g):
    bp, lp, d = x_prompt.shape
    bs, ls, _ = x_sample.shape
    depth = ada_w.shape[0]
    heads = ml_gate_b.shape[-1]
    hy_w = hy_bias.shape[-1]
    ml_w = ml_norm_g.shape[-1]
    dh = ml_w // heads
    n_exp = router_w.shape[-1]
    t_ctx, t_lat = bp * lp, bs * ls
    t = t_ctx + t_lat
    tm = min(ROW_TILE, lp)
    tt = min(TOK_TILE, lp)
    ng = N_DIR * 2 * heads
    seg_widths = (3 * hy_w, 2 * ml_w, ml_w, ml_w)
    n_main = 3 * hy_w + 4 * ml_w
    n_blocks = -(-(t * TOP_K) // MOE_ROWS) + n_exp

    x = jnp.concatenate([x_prompt.reshape(t_ctx, d), x_sample.reshape(t_lat, d)], axis=0)
    cond = jnp.concatenate([c_ctx[None], c, jnp.zeros((SUBLANES - 1 - bs, d), F32)], axis=0)
    zero_state = (jnp.zeros((bp, N_DIR, heads, dh, dh), F32), jnp.zeros((bp, N_DIR, heads, dh), F32),
                  jnp.zeros((bp, N_DIR, heads), F32))
    new_c, new_n, new_m = [], [], []
    for l in range(depth):
        mods = _ada(cond, ada_w[l], ada_b[l]).reshape(SUBLANES, 6, d)
        w_gate = jnp.pad(w_in[l][:, n_main:], ((0, 0), (0, LANES - ng)))
        z = _inproj(x, mods, norm1_g[l], w_in[l][:, :n_main].astype(BF16), w_gate, seg_widths,
                    t_ctx // tm, ls // tm, tm)
        lw = (hy_conv_w[l], hy_conv_b[l],
              (filt_w1[l], filt_b1[l], filt_w2[l], filt_b2[l], filt_w3[l], filt_freq[l]), hy_bias[l],
              ml_conv_w[l], ml_conv_b[l], ml_gate_b[l], ml_norm_g[l])
        z_ctx = [a[:t_ctx].reshape(bp, lp, a.shape[1]) for a in z]
        z_lat = [a[t_ctx:].reshape(bs, ls, a.shape[1]) for a in z]
        cache = (state_mlstm_C[:, l], state_mlstm_n[:, l], state_mlstm_m[:, l])
        hy_c, ml_c, st = _sequence_mixers(*z_ctx, lw, zero_state, lp)
        hy_l, ml_l, _ = _sequence_mixers(*z_lat, lw, cache, GRID_W)
        new_c.append(st[0])
        new_n.append(st[1])
        new_m.append(st[2])
        y_hy = jnp.concatenate([hy_c.reshape(t_ctx, hy_w), hy_l.reshape(t_lat, hy_w)], axis=0)
        y_ml = jnp.concatenate([ml_c.reshape(t_ctx, ml_w), ml_l.reshape(t_lat, ml_w)], axis=0)
        x1, h2, ti, tg, rk, cnt = _post(y_hy, y_ml, x, mods, w_out[l], norm2_g[l], router_w[l], router_b[l],
                                        t_ctx // tm, ls // tm, tm)
        pad_start, block_e, n_used = _moe_plan(cnt[0, :n_exp].astype(jnp.int32), MOE_ROWS, n_blocks)
        dest = _dest_rows(ti, rk, pad_start, tm)[:, :TOP_K].reshape(t // tt, 1, tt * TOP_K)
        xs = _dispatch(dest, h2, n_blocks * MOE_ROWS, tt)
        ys = _ffn(block_e, n_used, xs, moe_w_gu[l], moe_b_gu[l], moe_w_down[l], moe_b_down[l], MOE_ROWS)
        x = _combine(dest, tg, x1, mods, final_g, ys, t_ctx // tt, ls // tt, tt, final_norm=l == depth - 1)
    y_prompt = x[:t_ctx].reshape(bp, lp, d)
    y_sample = x[t_ctx:].reshape(bs, ls, d)
    return (y_prompt, y_sample, jnp.stack(new_c, axis=1), jnp.stack(new_n, axis=1), jnp.stack(new_m, axis=1))
```

```python
import functools
import math

import numpy as np
import jax
import jax.numpy as jnp
from jax import lax
from jax.experimental import pallas as pl
from jax.experimental.pallas import tpu as pltpu

F32 = jnp.float32
BF16 = jnp.bfloat16
HIGHEST = lax.Precision.HIGHEST
EPS = 1e-6

LANES = 128
SUBLANES = 8
VMEM_LIMIT_BYTES = 56 * 1024 * 1024

GRID_W = 64
ML_HEADS = 4
N_DIR = 2
HY_ORDER = 2
FILT_BANDS = 8
DECAY_TARGET = 1e-2
FAST_DECAY_PCT = 0.3
SLOW_DECAY_PCT = 1.5
TOP_K = 4
SWIGLU_LIMIT = 7.0
SWIGLU_ALPHA = 1.702

FFT_N2 = 128
MLSTM_CHUNK = 128
MOE_ROWS = 512
ROW_TILE = 512
TOK_TILE = 256


def _cparams(*sem):
    return pltpu.CompilerParams(dimension_semantics=sem, vmem_limit_bytes=VMEM_LIMIT_BYTES)


def _lane_tile(c, cap):
    return max(t for t in range(LANES, min(c, cap) + 1, LANES) if c % t == 0)


def _bdot(a, b):
    return jnp.dot(a.astype(BF16), b.astype(BF16), preferred_element_type=F32)


def _hdot(a, b):
    return jnp.dot(a, b, precision=HIGHEST, preferred_element_type=F32)


def _ada_kernel(c_ref, w_ref, b_ref, o_ref):
    c = c_ref[...]
    o_ref[...] = _hdot(c * jax.nn.sigmoid(c), w_ref[...]) + b_ref[...]


def _ada(cond, w, b):
    r, d = cond.shape
    n = w.shape[1]
    tn = _lane_tile(n, 1024)
    return pl.pallas_call(
        _ada_kernel,
        grid=(n // tn,),
        in_specs=[pl.BlockSpec((r, d), lambda j: (0, 0)),
                  pl.BlockSpec((d, tn), lambda j: (0, j)),
                  pl.BlockSpec((1, tn), lambda j: (0, j))],
        out_specs=pl.BlockSpec((r, tn), lambda j: (0, j)),
        out_shape=jax.ShapeDtypeStruct((r, n), F32),
        compiler_params=_cparams("arbitrary"),
        name="ada",
    )(cond, w, b.reshape(1, n))


def _mod_index_map(n_ctx_tiles, tiles_per_lat):
    def index_map(i):
        return (jnp.where(i < n_ctx_tiles, 0, 1 + (i - n_ctx_tiles) // tiles_per_lat), 0, 0)
    return index_map


def _inproj_kernel(x_ref, mod_ref, g_ref, w_ref, wg_ref, *out_refs, offsets):
    x = x_ref[...]
    h = x * lax.rsqrt(jnp.mean(x * x, axis=-1, keepdims=True) + EPS) * g_ref[...]
    h = h * (1.0 + mod_ref[0, 1:2, :]) + mod_ref[0, 0:1, :]
    hb = h.astype(BF16)
    for o_ref, (lo, hi) in zip(out_refs[:-1], offsets):
        o_ref[...] = jnp.dot(hb, w_ref[:, lo:hi], preferred_element_type=F32)
    h_lo = (h - hb.astype(F32)).astype(BF16)
    g = jnp.dot(hb, wg_ref[...], preferred_element_type=F32)
    out_refs[-1][...] = (g[:, :LANES] + g[:, LANES:]
                         + jnp.dot(h_lo, wg_ref[:, :LANES], preferred_element_type=F32))


def _inproj(x, mods, norm_g, w_main, w_gate, seg_widths, n_ctx_tiles, tiles_per_lat, tm):
    t, d = x.shape
    offsets, lo = [], 0
    for wd in seg_widths:
        offsets.append((lo, lo + wd))
        lo += wd
    wg_hi = w_gate.astype(BF16)
    wg = jnp.concatenate([wg_hi, (w_gate - wg_hi.astype(F32)).astype(BF16)], axis=1)
    widths = tuple(seg_widths) + (LANES,)
    return pl.pallas_call(
        functools.partial(_inproj_kernel, offsets=tuple(offsets)),
        grid=(t // tm,),
        in_specs=[pl.BlockSpec((tm, d), lambda i: (i, 0)),
                  pl.BlockSpec((1, 6, d), _mod_index_map(n_ctx_tiles, tiles_per_lat)),
                  pl.BlockSpec((1, d), lambda i: (0, 0)),
                  pl.BlockSpec(w_main.shape, lambda i: (0, 0)),
                  pl.BlockSpec(wg.shape, lambda i: (0, 0))],
        out_specs=[pl.BlockSpec((tm, wd), lambda i: (i, 0)) for wd in widths],
        out_shape=[jax.ShapeDtypeStruct((t, wd), F32) for wd in widths],
        compiler_params=_cparams("arbitrary"),
        name="inproj",
    )(x, mods, norm_g.reshape(1, d), w_main, wg)


def _short_conv_kernel(x_ref, w_ref, b_ref, o_ref, *, row_w, silu):
    x = x_ref[0]
    l = x.shape[0]
    pos = lax.broadcasted_iota(jnp.int32, x.shape, 0) % row_w
    prev = jnp.where(pos == 0, 0.0, pltpu.roll(x, 1, 0))
    nxt = jnp.where(pos == row_w - 1, 0.0, pltpu.roll(x, l - 1, 0))
    y = prev * w_ref[0:1, :] + x * w_ref[1:2, :] + nxt * w_ref[2:3, :] + b_ref[...]
    if silu:
        y = y * jax.nn.sigmoid(y)
    o_ref[0] = y


def _short_conv(x, w, b, row_w, silu):
    bsz, l, c = x.shape
    ct = _lane_tile(c, 256)
    return pl.pallas_call(
        functools.partial(_short_conv_kernel, row_w=row_w, silu=silu),
        grid=(bsz, c // ct),
        in_specs=[pl.BlockSpec((1, l, ct), lambda i, j: (i, 0, j)),
                  pl.BlockSpec((3, ct), lambda i, j: (0, j)),
                  pl.BlockSpec((1, ct), lambda i, j: (0, j))],
        out_specs=pl.BlockSpec((1, l, ct), lambda i, j: (i, 0, j)),
        out_shape=jax.ShapeDtypeStruct(x.shape, F32),
        compiler_params=_cparams("arbitrary", "arbitrary"),
        name="short_conv",
    )(x, w, b.reshape(1, c))


def _dft_direct_tables(l):
    n = 2 * l
    k = np.arange(n)[:, None].astype(np.float64)
    t = np.arange(n)[None, :].astype(np.float64)
    ang = 2.0 * np.pi * ((k * t) % n) / n
    cm, sm = np.cos(ang), np.sin(ang)
    fwd = np.block([[cm[:, :l], sm[:, :l]], [-sm[:, :l], cm[:, :l]]])
    filt = np.concatenate([cm, -sm], axis=0)
    return fwd, filt


def _dft_two_level_tables(l, n2):
    n = 2 * l
    n1 = n // n2
    k1 = np.arange(n1)[:, None].astype(np.float64)
    a = np.arange(n1)[None, :].astype(np.float64)
    ang1 = 2.0 * np.pi * ((k1 * a) % n1) / n1
    c1, s1 = np.cos(ang1), np.sin(ang1)
    h = n1 // 2
    m1 = np.block([[c1[:, :h], s1[:, :h]], [-s1[:, :h], c1[:, :h]]])
    m1f = np.concatenate([c1, -s1], axis=0)
    kk = (np.arange(n1)[:, None, None] + n1 * np.arange(n2)[None, :, None]).astype(np.float64)
    b = np.arange(n2)[None, None, :].astype(np.float64)
    ang = 2.0 * np.pi * ((kk * b) % n) / n
    cg, sg = np.cos(ang), np.sin(ang)
    gt = np.concatenate([np.concatenate([cg, sg], axis=2),
                         np.concatenate([-sg, cg], axis=2)], axis=1)
    return m1, m1f, gt


def _circular_lag(n0, rows, l):
    n = n0 + lax.broadcasted_iota(jnp.int32, (rows, 1), 0)
    t = jnp.where(n < l, n, 2 * l - n).astype(F32)
    return n, t, t / float(max(l - 1, 1))


def _filter_hidden_kernel(bandv_ref, w1_ref, b1_ref, w2_ref, b2_ref, freq_ref, o_ref, *, l):
    rows = o_ref.shape[0]
    _, t, t01 = _circular_lag(pl.program_id(0) * rows, rows, l)
    lane = lax.broadcasted_iota(jnp.int32, (rows, LANES), 1)
    ang = (2.0 * math.pi / l) * t * bandv_ref[...]
    feats = jnp.where(lane == 0, t01,
                      jnp.where(lane <= FILT_BANDS, jnp.cos(ang),
                                jnp.where(lane <= 2 * FILT_BANDS, -jnp.sin(ang), 0.0)))
    fr = freq_ref[...]
    h = jnp.sin(fr * (_hdot(feats, w1_ref[...]) + b1_ref[...]))
    o_ref[...] = jnp.sin(fr * (_hdot(h, w2_ref[...]) + b2_ref[...]))


def _filter_hidden(l, f_w1, f_b1, f_w2, f_b2, f_freq):
    emb, hid = f_w1.shape
    n = 2 * l
    rows = min(n, 512)
    bands = jnp.linspace(1e-4, FILT_BANDS - 1, FILT_BANDS, dtype=F32)
    bandv = jnp.zeros((1, LANES), F32).at[0, 1:1 + FILT_BANDS].set(bands)
    bandv = bandv.at[0, 1 + FILT_BANDS:1 + 2 * FILT_BANDS].set(bands)
    w1p = jnp.zeros((LANES, hid), F32).at[:emb].set(f_w1)
    c0 = lambda i: (0, 0)
    return pl.pallas_call(
        functools.partial(_filter_hidden_kernel, l=l),
        grid=(n // rows,),
        in_specs=[pl.BlockSpec((1, LANES), c0), pl.BlockSpec((LANES, hid), c0), pl.BlockSpec((1, hid), c0),
                  pl.BlockSpec((hid, hid), c0), pl.BlockSpec((1, hid), c0), pl.BlockSpec((1, hid), c0)],
        out_specs=pl.BlockSpec((rows, hid), lambda i: (i, 0)),
        out_shape=jax.ShapeDtypeStruct((n, hid), F32),
        compiler_params=_cparams("arbitrary"),
        name="filter_hidden",
    )(bandv, w1p, f_b1.reshape(1, hid), f_w2, f_b2.reshape(1, hid), f_freq.reshape(1, hid))


def _filter_rows(n0, rows, l, hid_ref, w3_ref, delta_ref):
    n, _, t01 = _circular_lag(n0, rows, l)
    h = hid_ref[pl.ds(n0, rows), :]
    hf = _bdot(h, w3_ref[0, 0])
    hb = _bdot(h, w3_ref[0, 1])
    window = jnp.exp(-t01 * delta_ref[...])
    return jnp.where(n < l, hf, jnp.where(n > l, hb, 0.0)) * window


def _filter_direct_kernel(hid_ref, w3_ref, delta_ref, ff_ref, h_ref, *, l):
    hc = _filter_rows(0, 2 * l, l, hid_ref, w3_ref, delta_ref)
    h_ref[0] = _bdot(ff_ref[...], hc) * (1.0 / (2 * l))


def _filter_two_level_kernel(hid_ref, w3_ref, delta_ref, m1f_ref, gt_ref, h_ref, hc_buf, a_buf, *, l, n2, rows):
    n = 2 * l
    n1 = n // n2

    def fill(i, carry):
        r0 = pl.multiple_of(i * rows, rows)
        hc_buf[pl.ds(r0, rows), :] = _filter_rows(r0, rows, l, hid_ref, w3_ref, delta_ref)
        return carry
    lax.fori_loop(0, n // rows, fill, 0)

    def step1(b, carry):
        col = hc_buf[pl.ds(b, n1, stride=n2), :]
        a = _bdot(m1f_ref[...], col)
        a_buf[pl.ds(b, n1, stride=2 * n2), :] = a[:n1]
        a_buf[pl.ds(n2 + b, n1, stride=2 * n2), :] = a[n1:]
        return carry
    lax.fori_loop(0, n2, step1, 0)

    def step2(k1, carry):
        r0 = pl.multiple_of(k1 * 2 * n2, 2 * n2)
        h_ref[0, k1] = (_bdot(gt_ref[k1], a_buf[pl.ds(r0, 2 * n2), :]) * (1.0 / n)).astype(h_ref.dtype)
        return carry
    lax.fori_loop(0, n1, step2, 0)


def _filter_tail_inputs(hy_w, f_w3):
    hid = f_w3.shape[0]
    w3 = f_w3.reshape(hid, HY_ORDER, N_DIR, hy_w).transpose(1, 2, 0, 3)
    max_decay = math.log(DECAY_TARGET) / FAST_DECAY_PCT
    min_decay = math.log(DECAY_TARGET) / SLOW_DECAY_PCT
    deltas = jnp.abs(jnp.linspace(min_decay, max_decay, hy_w, dtype=F32)).reshape(1, hy_w)
    return w3, deltas


def _filter_spectrum_direct(l, hy_w, filt_params, ff):
    f_w1, f_b1, f_w2, f_b2, f_w3, f_freq = filt_params
    hidden = _filter_hidden(l, f_w1, f_b1, f_w2, f_b2, f_freq)
    w3, deltas = _filter_tail_inputs(hy_w, f_w3)
    hid = f_w3.shape[0]
    n = 2 * l
    return pl.pallas_call(
        functools.partial(_filter_direct_kernel, l=l),
        grid=(HY_ORDER,),
        in_specs=[pl.BlockSpec((n, hid), lambda o: (0, 0)),
                  pl.BlockSpec((1, N_DIR, hid, hy_w), lambda o: (o, 0, 0, 0)),
                  pl.BlockSpec((1, hy_w), lambda o: (0, 0)),
                  pl.BlockSpec((2 * n, n), lambda o: (0, 0))],
        out_specs=pl.BlockSpec((1, 2 * n, hy_w), lambda o: (o, 0, 0)),
        out_shape=jax.ShapeDtypeStruct((HY_ORDER, 2 * n, hy_w), F32),
        compiler_params=_cparams("arbitrary"),
        name="filter_direct",
    )(hidden, w3, deltas, ff)


def _filter_spectrum_two_level(l, hy_w, filt_params, m1f, gt, ct):
    f_w1, f_b1, f_w2, f_b2, f_w3, f_freq = filt_params
    hidden = _filter_hidden(l, f_w1, f_b1, f_w2, f_b2, f_freq)
    w3, deltas = _filter_tail_inputs(hy_w, f_w3)
    hid = f_w3.shape[0]
    n = 2 * l
    n2 = FFT_N2
    n1 = n // n2
    return pl.pallas_call(
        functools.partial(_filter_two_level_kernel, l=l, n2=n2, rows=min(n, 512)),
        grid=(HY_ORDER, hy_w // ct),
        in_specs=[pl.BlockSpec((n, hid), lambda o, j: (0, 0)),
                  pl.BlockSpec((1, N_DIR, hid, ct), lambda o, j: (o, 0, 0, j)),
                  pl.BlockSpec((1, ct), lambda o, j: (0, j)),
                  pl.BlockSpec((2 * n1, n1), lambda o, j: (0, 0)),
                  pl.BlockSpec((n1, 2 * n2, 2 * n2), lambda o, j: (0, 0, 0))],
        out_specs=pl.BlockSpec((1, n1, 2 * n2, ct), lambda o, j: (o, 0, 0, j)),
        out_shape=jax.ShapeDtypeStruct((HY_ORDER, n1, 2 * n2, hy_w), BF16),
        scratch_shapes=[pltpu.VMEM((n, ct), F32), pltpu.VMEM((n1 * 2 * n2, ct), F32)],
        compiler_params=_cparams("arbitrary", "arbitrary"),
        name="filter_two_level",
    )(hidden, w3, deltas, m1f, gt)


def _complex_mul(x, h, half):
    xr, xi = x[:half], x[half:]
    hr, hi = h[:half], h[half:]
    return jnp.concatenate([xr * hr - xi * hi, xr * hi + xi * hr], axis=0)


def _conv_direct_kernel(z_ref, gate_ref, bias_ref, h_ref, fwd_ref, inv_ref, o_ref):
    l = z_ref.shape[1]
    z = jnp.concatenate([z_ref[0], z_ref[1]], axis=0)
    x = _bdot(fwd_ref[...], z)
    y = _bdot(inv_ref[...], _complex_mul(x, h_ref[0], 2 * l))
    bias = bias_ref[0]
    o_ref[0] = gate_ref[0] * (y[:l] + bias * z_ref[0])
    o_ref[1] = gate_ref[1] * (y[l:] + bias * z_ref[1])


def _conv_two_level_kernel(z_ref, gate_ref, bias_ref, h_ref, m1_ref, m1i_ref, gt_ref, o_ref,
                           a_buf, *, n2):
    l = z_ref.shape[1]
    n1 = 2 * l // n2
    hn = n1 // 2

    def step1(b, carry):
        za = z_ref[0, pl.ds(b, hn, stride=n2), :]
        zb = z_ref[1, pl.ds(b, hn, stride=n2), :]
        a = _bdot(m1_ref[...], jnp.concatenate([za, zb], axis=0))
        a_buf[pl.ds(b, n1, stride=2 * n2), :] = a[:n1]
        a_buf[pl.ds(n2 + b, n1, stride=2 * n2), :] = a[n1:]
        return carry
    lax.fori_loop(0, n2, step1, 0)

    def step2(k1, carry):
        r0 = pl.multiple_of(k1 * 2 * n2, 2 * n2)
        x = _bdot(gt_ref[k1], a_buf[pl.ds(r0, 2 * n2), :])
        y = _complex_mul(x, h_ref[0, k1].astype(F32), n2).astype(BF16)
        a_buf[pl.ds(r0, 2 * n2), :] = lax.dot_general(gt_ref[k1], y, (((0,), (0,)), ((), ())),
                                                      preferred_element_type=F32)
        return carry
    lax.fori_loop(0, n1, step2, 0)

    def step3(b, carry):
        br = a_buf[pl.ds(b, n1, stride=2 * n2), :]
        bi = a_buf[pl.ds(n2 + b, n1, stride=2 * n2), :]
        y = _bdot(m1i_ref[...], jnp.concatenate([br, bi], axis=0))
        o_ref[0, pl.ds(b, hn, stride=n2), :] = y[:hn]
        o_ref[1, pl.ds(b, hn, stride=n2), :] = y[hn:]
        return carry
    lax.fori_loop(0, n2, step3, 0)

    bias = bias_ref[0]
    for s in range(2):
        o_ref[s] = gate_ref[s] * (o_ref[s] + bias * z_ref[s])


def _long_conv_gated(u, z, z_col, gate_col, spectrum, order, bias, tables, ct):
    bsz, l, _ = u.shape
    c = spectrum.shape[-1]
    nct = c // ct
    zspec = pl.BlockSpec((2, l, ct), lambda i, j: (i, 0, z_col * nct + j))
    gspec = pl.BlockSpec((2, l, ct), lambda i, j: (i, 0, gate_col * nct + j))
    bspec = pl.BlockSpec((1, 1, ct), lambda i, j: (order, 0, j))
    ospec = pl.BlockSpec((2, l, ct), lambda i, j: (i, 0, j))
    out_shape = jax.ShapeDtypeStruct((bsz, l, c), F32)
    bias3 = bias.reshape(HY_ORDER, 1, c)
    if len(tables) == 2:
        fwd, inv = tables
        n = 2 * l
        return pl.pallas_call(
            _conv_direct_kernel,
            grid=(bsz // 2, nct),
            in_specs=[zspec, gspec, bspec,
                      pl.BlockSpec((1, 2 * n, ct), lambda i, j: (order, 0, j)),
                      pl.BlockSpec(fwd.shape, lambda i, j: (0, 0)),
                      pl.BlockSpec(inv.shape, lambda i, j: (0, 0))],
            out_specs=ospec, out_shape=out_shape,
            compiler_params=_cparams("arbitrary", "arbitrary"),
            name="long_conv_direct",
        )(z, u, bias3, spectrum, fwd, inv)
    m1, m1i, gt = tables
    n2 = FFT_N2
    n1 = 2 * l // n2
    const2 = lambda i, j: (0, 0)
    const3 = lambda i, j: (0, 0, 0)
    return pl.pallas_call(
        functools.partial(_conv_two_level_kernel, n2=n2),
        grid=(nct, bsz // 2),
        in_specs=[pl.BlockSpec((2, l, ct), lambda j, i: (i, 0, z_col * nct + j)),
                  pl.BlockSpec((2, l, ct), lambda j, i: (i, 0, gate_col * nct + j)),
                  pl.BlockSpec((1, 1, ct), lambda j, i: (order, 0, j)),
                  pl.BlockSpec((1, n1, 2 * n2, ct), lambda j, i: (order, 0, 0, j)),
                  pl.BlockSpec(m1.shape, const2), pl.BlockSpec(m1i.shape, const2),
                  pl.BlockSpec(gt.shape, const3)],
        out_specs=pl.BlockSpec((2, l, ct), lambda j, i: (i, 0, j)),
        out_shape=out_shape,
        scratch_shapes=[pltpu.VMEM((n1 * 2 * n2, ct), F32)],
        compiler_params=_cparams("arbitrary", "arbitrary"),
        name="long_conv_two_level",
    )(z, u, bias3, spectrum, m1, m1i, gt)


def _hyena(z_hy, conv_w, conv_b, filt_params, hy_bias, row_w):
    bsz, l, c3 = z_hy.shape
    c = c3 // 3
    u = _short_conv(z_hy, conv_w, conv_b, row_w, silu=False)
    table = lambda a: jnp.asarray(a, F32).astype(BF16)
    if 2 * l // FFT_N2 <= 4:
        fwd, filt = _dft_direct_tables(l)
        tables = (table(fwd), table(fwd.T))
        spectrum = _filter_spectrum_direct(l, c, filt_params, table(filt))
        ct = c
    else:
        m1, m1f, gt = _dft_two_level_tables(l, FFT_N2)
        tables = (table(m1), table(m1.T), table(gt))
        ct = LANES
        spectrum = _filter_spectrum_two_level(l, c, filt_params, table(m1f), tables[2], ct)
    z1 = _long_conv_gated(u, u, 0, 1, spectrum, 0, hy_bias, tables, ct)
    return _long_conv_gated(u, z1, 0, 2, spectrum, 1, hy_bias, tables, ct)


def _log_sigmoid(x):
    return jnp.minimum(x, 0.0) - jnp.log1p(jnp.exp(-jnp.abs(x)))


def _mlstm_chunk(q, k, v, icol, irow, bcol, brow, btot, mask, c, n, m):
    qb, kb, vb = q.astype(BF16), k.astype(BF16), v.astype(BF16)
    dm = jnp.where(mask, bcol - brow + irow, -jnp.inf)
    inter = bcol + m
    mj = jnp.maximum(jnp.max(dm, axis=-1, keepdims=True), inter)
    w_int = jnp.exp(inter - mj)
    s = lax.dot_general(qb, kb, (((1,), (1,)), ((), ())), preferred_element_type=F32) * jnp.exp(dm - mj)
    num = w_int * _bdot(qb, c) + _bdot(s, vb)
    den = w_int * jnp.sum(q * n, axis=-1, keepdims=True) + jnp.sum(s, axis=-1, keepdims=True)
    h = num / jnp.maximum(jnp.abs(den), jnp.exp(-mj))
    m_new = jnp.maximum(btot + m, jnp.max(btot - brow + irow, axis=-1, keepdims=True))
    a = jnp.exp(btot + m - m_new)
    wk = jnp.exp(btot - bcol + icol - m_new) * k
    c_new = a * c + _bdot(wk.T, vb)
    n_new = a * n + jnp.sum(wk, axis=0, keepdims=True)
    return h, c_new, n_new, m_new


def _mlstm_kernel(q_ref, k_ref, v_ref, o_ref, g_ref, gb_ref, ng_ref, c0_ref, n0_ref, m0_ref,
                  y_ref, c_ref, n_ref, m_ref, hf_buf, hb_buf, *, chunk):
    l, dh = q_ref.shape[1], q_ref.shape[2]
    nc = l // chunk
    scale = dh ** -0.5
    row = lax.broadcasted_iota(jnp.int32, (chunk, chunk), 0)
    col = lax.broadcasted_iota(jnp.int32, (chunk, chunk), 1)
    lower, upper = col <= row, col >= row
    tri_l, tri_u = lower.astype(F32), upper.astype(F32)
    lane = lax.broadcasted_iota(jnp.int32, (chunk, LANES), 1)

    def gates(r0):
        g = g_ref[0, 0, pl.ds(r0, chunk), :] + gb_ref[0]
        val = jnp.where((lane == 1) | (lane == 3), _log_sigmoid(g), g)
        w = jnp.where(lane == 1, _hdot(tri_l, val), jnp.where(lane == 3, _hdot(tri_u, val), val))
        return w, w.T

    def body(j, carry):
        cf, nf, mf, cb, nb, mb = carry
        rf = pl.multiple_of(j * chunk, chunk)
        w, wt = gates(rf)
        hf, cf, nf, mf = _mlstm_chunk(
            q_ref[0, pl.ds(rf, chunk), :], k_ref[0, pl.ds(rf, chunk), :] * scale, v_ref[0, pl.ds(rf, chunk), :],
            w[:, 0:1], wt[0:1, :], w[:, 1:2], wt[1:2, :], wt[1:2, chunk - 1:chunk], lower, cf, nf, mf)
        hf_buf[pl.ds(rf, chunk), :] = hf
        rb = pl.multiple_of((nc - 1 - j) * chunk, chunk)
        w, wt = gates(rb)
        hb, cb, nb, mb = _mlstm_chunk(
            q_ref[0, pl.ds(rb, chunk), :], k_ref[0, pl.ds(rb, chunk), :] * scale, v_ref[0, pl.ds(rb, chunk), :],
            w[:, 2:3], wt[2:3, :], w[:, 3:4], wt[3:4, :], wt[3:4, 0:1], upper, cb, nb, mb)
        hb_buf[pl.ds(rb, chunk), :] = hb
        return cf, nf, mf, cb, nb, mb

    init = (c0_ref[0, 0, 0], n0_ref[0, 0, 0:1, :], m0_ref[0, 0, 0:1, 0:1],
            c0_ref[0, 1, 0], n0_ref[0, 0, 1:2, :], m0_ref[0, 0, 1:2, 0:1])
    cf, nf, mf, cb, nb, mb = lax.fori_loop(0, nc, body, init)
    c_ref[0, 0, 0] = cf
    c_ref[0, 1, 0] = cb
    n_ref[0, 0, 0:1, :] = nf
    n_ref[0, 0, 1:2, :] = nb
    m_ref[0, 0, 0:1, :] = jnp.broadcast_to(mf, (1, LANES))
    m_ref[0, 0, 1:2, :] = jnp.broadcast_to(mb, (1, LANES))

    def finish(j, carry):
        r0 = pl.multiple_of(j * chunk, chunk)
        hs = hf_buf[pl.ds(r0, chunk), :] + hb_buf[pl.ds(r0, chunk), :]
        hs = hs * lax.rsqrt(jnp.mean(hs * hs, axis=-1, keepdims=True) + EPS) * ng_ref[...]
        y_ref[0, pl.ds(r0, chunk), :] = jax.nn.sigmoid(o_ref[0, pl.ds(r0, chunk), :]) * hs
        return carry
    lax.fori_loop(0, nc, finish, 0)


def _mlstm(qk, z_v, z_o, z_g, gate_b, norm_g, c0, n0, m0):
    bsz, l, ml = z_v.shape
    heads = gate_b.shape[-1]
    dh = ml // heads
    chunk = min(MLSTM_CHUNK, l)
    ng = N_DIR * 2 * heads
    gh = z_g[..., :ng].reshape(bsz, l, N_DIR * 2, heads).transpose(0, 3, 1, 2)
    gh = jnp.pad(gh, ((0, 0), (0, 0), (0, 0), (0, LANES - N_DIR * 2)))
    gb = jnp.pad(gate_b.reshape(N_DIR * 2, heads).T, ((0, 0), (0, LANES - N_DIR * 2))).reshape(heads, 1, LANES)
    n0h = n0.transpose(0, 2, 1, 3)
    m0h = jnp.broadcast_to(m0.transpose(0, 2, 1)[..., None], (bsz, heads, N_DIR, LANES))
    seq = lambda col0: pl.BlockSpec((1, l, dh), lambda b, h: (b, 0, col0 + h))
    cspec = pl.BlockSpec((1, N_DIR, 1, dh, dh), lambda b, h: (b, 0, h, 0, 0))
    sspec = pl.BlockSpec((1, 1, N_DIR, dh), lambda b, h: (b, h, 0, 0))
    mspec = pl.BlockSpec((1, 1, N_DIR, LANES), lambda b, h: (b, h, 0, 0))
    y, c, n, m = pl.pallas_call(
        functools.partial(_mlstm_kernel, chunk=chunk),
        grid=(bsz, heads),
        in_specs=[seq(0), seq(heads), seq(0), seq(0),
                  pl.BlockSpec((1, 1, l, LANES), lambda b, h: (b, h, 0, 0)),
                  pl.BlockSpec((1, 1, LANES), lambda b, h: (h, 0, 0)),
                  pl.BlockSpec((1, dh), lambda b, h: (0, h)),
                  cspec, sspec, mspec],
        out_specs=[seq(0), cspec, sspec, mspec],
        out_shape=[jax.ShapeDtypeStruct((bsz, l, ml), F32),
                   jax.ShapeDtypeStruct((bsz, N_DIR, heads, dh, dh), F32),
                   jax.ShapeDtypeStruct((bsz, heads, N_DIR, dh), F32),
                   jax.ShapeDtypeStruct((bsz, heads, N_DIR, LANES), F32)],
        scratch_shapes=[pltpu.VMEM((l, dh), F32), pltpu.VMEM((l, dh), F32)],
        compiler_params=_cparams("arbitrary", "arbitrary"),
        name="mlstm",
    )(qk, qk, z_v, z_o, gh, gb, norm_g.reshape(1, ml), c0, n0h, m0h)
    return y, (c, n.transpose(0, 2, 1, 3), m[..., 0].transpose(0, 2, 1))


def _rows_to_tiles(tile_ref, x):
    r, d = x.shape
    s = d // LANES
    for k in range(s):
        tile_ref[pl.ds(k, r, stride=s), :] = x[:, LANES * k:LANES * (k + 1)]


def _tiles_to_rows(tile_ref, r):
    s = tile_ref.shape[0] // r
    return jnp.concatenate([tile_ref[pl.ds(k, r, stride=s), :] for k in range(s)], axis=1)


def _lane_pack(cols, lane):
    out = jnp.zeros(lane.shape, cols[0].dtype)
    for j, colv in enumerate(cols):
        out = jnp.where(lane == j, colv, out)
    return out


def _post_kernel(yhy_ref, yml_ref, x_ref, mod_ref, wo_ref, n2g_ref, rw_ref, rb_ref,
                 x1_ref, h2_ref, ti_ref, tg_ref, rk_ref, cnt_ref, carry, *, n_exp, top_k):
    @pl.when(pl.program_id(0) == 0)
    def _():
        carry[...] = jnp.zeros_like(carry)

    hy_w = yhy_ref.shape[1]
    tm = x_ref.shape[0]
    proj = _bdot(yhy_ref[...], wo_ref[:hy_w, :]) + _bdot(yml_ref[...], wo_ref[hy_w:, :])
    x1 = x_ref[...] + mod_ref[0, 2:3, :] * proj
    x1_ref[...] = x1
    h2 = x1 * lax.rsqrt(jnp.mean(x1 * x1, axis=-1, keepdims=True) + EPS) * n2g_ref[...]
    h2 = h2 * (1.0 + mod_ref[0, 4:5, :]) + mod_ref[0, 3:4, :]
    _rows_to_tiles(h2_ref, h2)

    lane = lax.broadcasted_iota(jnp.int32, (tm, LANES), 1)
    work = jnp.where(lane < n_exp, _hdot(h2, rw_ref[...]) + rb_ref[...], -jnp.inf)
    vals, idxs, hots = [], [], []
    for _ in range(top_k):
        mx = jnp.max(work, axis=-1, keepdims=True)
        idx = jnp.min(jnp.where(work == mx, lane, LANES), axis=-1, keepdims=True)
        hot = lane == idx
        vals.append(mx)
        idxs.append(idx)
        hots.append(hot)
        work = jnp.where(hot, -jnp.inf, work)
    exps = [jnp.exp(v - vals[0]) for v in vals]
    tot = functools.reduce(lambda a, b: a + b, exps)
    ti_ref[...] = _lane_pack(idxs, lane)
    tg_ref[...] = _lane_pack([e / tot for e in exps], lane)

    hot_sum = functools.reduce(lambda a, b: a + b, [h.astype(F32) for h in hots])
    row = lax.broadcasted_iota(jnp.int32, (tm, tm), 0)
    col = lax.broadcasted_iota(jnp.int32, (tm, tm), 1)
    before = _bdot((col < row).astype(F32), hot_sum) + carry[...]
    ranks = [jnp.sum(jnp.where(h, before, 0.0), axis=-1, keepdims=True).astype(jnp.int32) for h in hots]
    rk_ref[...] = _lane_pack(ranks, lane)
    carry[...] = carry[...] + jnp.sum(hot_sum, axis=0, keepdims=True)
    cnt_ref[...] = jnp.broadcast_to(carry[...], cnt_ref.shape)


def _post(y_hy, y_ml, x, mods, w_out, norm2_g, r_w, r_b, n_ctx_tiles, tiles_per_lat, tm):
    t, d = x.shape
    n_exp = r_w.shape[1]
    rwp = jnp.pad(r_w, ((0, 0), (0, LANES - n_exp)))
    rbp = jnp.pad(r_b, (0, LANES - n_exp)).reshape(1, LANES)
    row = lambda wd: pl.BlockSpec((tm, wd), lambda i: (i, 0))
    const = lambda shape: pl.BlockSpec(shape, lambda i: (0, 0))
    return pl.pallas_call(
        functools.partial(_post_kernel, n_exp=n_exp, top_k=TOP_K),
        grid=(t // tm,),
        in_specs=[row(y_hy.shape[1]), row(y_ml.shape[1]), row(d),
                  pl.BlockSpec((1, 6, d), _mod_index_map(n_ctx_tiles, tiles_per_lat)),
                  const(w_out.shape), const((1, d)), const((d, LANES)), const((1, LANES))],
        out_specs=[row(d), pl.BlockSpec((tm * d // LANES, LANES), lambda i: (i, 0)),
                   row(LANES), row(LANES), row(LANES), const((SUBLANES, LANES))],
        out_shape=[jax.ShapeDtypeStruct((t, d), F32), jax.ShapeDtypeStruct((t * d // LANES, LANES), F32),
                   jax.ShapeDtypeStruct((t, LANES), jnp.int32), jax.ShapeDtypeStruct((t, LANES), F32),
                   jax.ShapeDtypeStruct((t, LANES), jnp.int32), jax.ShapeDtypeStruct((SUBLANES, LANES), F32)],
        scratch_shapes=[pltpu.VMEM((1, LANES), F32)],
        compiler_params=_cparams("arbitrary"),
        name="post",
    )(y_hy, y_ml, x, mods, w_out.astype(BF16), norm2_g.reshape(1, d), rwp, rbp)


def _dest_kernel(ti_ref, rk_ref, ps_ref, d_ref, *, top_k):
    lane = lax.broadcasted_iota(jnp.int32, ti_ref.shape, 1)
    ti, rk = ti_ref[...], rk_ref[...]
    cols = []
    for j in range(top_k):
        start = jnp.sum(jnp.where(lane == ti[:, j:j + 1], ps_ref[...], 0.0), axis=-1, keepdims=True)
        cols.append(start.astype(jnp.int32) + rk[:, j:j + 1])
    d_ref[...] = _lane_pack(cols, lane)


def _dest_rows(ti, rk, pad_start, tm):
    t = ti.shape[0]
    row = pl.BlockSpec((tm, LANES), lambda i: (i, 0))
    return pl.pallas_call(
        functools.partial(_dest_kernel, top_k=TOP_K),
        grid=(t // tm,),
        in_specs=[row, row, pl.BlockSpec((1, LANES), lambda i: (0, 0))],
        out_specs=row,
        out_shape=jax.ShapeDtypeStruct((t, LANES), jnp.int32),
        compiler_params=_cparams("arbitrary"),
        name="dest_rows",
    )(ti, rk, pad_start)


def _dispatch_kernel(dest_ref, h_ref, zero_ref, xs_ref, sem, *, top_k):
    del zero_ref
    tt = dest_ref.shape[2] // top_k
    s = h_ref.shape[0] // tt

    def issue(r, carry):
        src = pl.multiple_of((r // top_k) * s, s)
        dst = pl.multiple_of(dest_ref[0, 0, r] * s, s)
        pltpu.make_async_copy(h_ref.at[pl.ds(src, s)], xs_ref.at[pl.ds(dst, s)], sem).start()
        return carry
    lax.fori_loop(0, tt * top_k, issue, 0)
    for _ in range(top_k):
        pltpu.make_async_copy(h_ref, xs_ref.at[pl.ds(0, tt * s)], sem).wait()


def _dispatch(dest, h2t, n_rows, tt):
    s = h2t.shape[0] * TOP_K // dest.size
    zeros = jnp.zeros((n_rows * s, LANES), F32)
    return pl.pallas_call(
        functools.partial(_dispatch_kernel, top_k=TOP_K),
        grid=(dest.shape[0],),
        in_specs=[pl.BlockSpec((1, 1, tt * TOP_K), lambda i: (i, 0, 0), memory_space=pltpu.SMEM),
                  pl.BlockSpec((tt * s, LANES), lambda i: (i, 0)),
                  pl.BlockSpec(memory_space=pl.ANY)],
        out_specs=pl.BlockSpec(memory_space=pl.ANY),
        out_shape=jax.ShapeDtypeStruct((n_rows * s, LANES), F32),
        scratch_shapes=[pltpu.SemaphoreType.DMA(())],
        input_output_aliases={2: 0},
        compiler_params=_cparams("arbitrary"),
        name="dispatch",
    )(dest, h2t, zeros)


def _deinterleave_table():
    p = np.zeros((2 * LANES, 2 * LANES), np.float32)
    j = np.arange(LANES)
    p[2 * j, j] = 1.0
    p[2 * j + 1, LANES + j] = 1.0
    return p


def _ffn_kernel(be_ref, nu_ref, xs_ref, wgu_ref, bg_ref, bl_ref, wd_ref, bd_ref, perm_ref, ys_ref,
                wg_buf, wl_buf, wd_buf, acc_buf, *, ft):
    i = pl.program_id(0)
    f = wg_buf.shape[1]
    rows = acc_buf.shape[0]
    live = i < nu_ref[0]

    @pl.when(live & ((i == 0) | (be_ref[i] != be_ref[jnp.maximum(i - 1, 0)])))
    def _():
        for j in range(f // LANES):
            blk = wgu_ref[0, :, 2 * LANES * j:2 * LANES * (j + 1)].astype(BF16)
            split = jnp.dot(blk, perm_ref[...], preferred_element_type=F32)
            wg_buf[:, LANES * j:LANES * (j + 1)] = split[:, :LANES].astype(BF16)
            wl_buf[:, LANES * j:LANES * (j + 1)] = split[:, LANES:].astype(BF16)
        wd_buf[...] = wd_ref[0].astype(BF16)

    @pl.when(live)
    def _():
        x = _tiles_to_rows(xs_ref, rows).astype(BF16)
        for j, f0 in enumerate(range(0, f, ft)):
            g = jnp.dot(x, wg_buf[:, f0:f0 + ft], preferred_element_type=F32) + bg_ref[0, :, f0:f0 + ft]
            lin = jnp.dot(x, wl_buf[:, f0:f0 + ft], preferred_element_type=F32) + bl_ref[0, :, f0:f0 + ft]
            gate = jnp.minimum(g, SWIGLU_LIMIT)
            lin = jnp.clip(lin, -SWIGLU_LIMIT, SWIGLU_LIMIT)
            act = (lin + 1.0) * gate * jax.nn.sigmoid(SWIGLU_ALPHA * gate)
            part = jnp.dot(act.astype(BF16), wd_buf[f0:f0 + ft, :], preferred_element_type=F32)
            if j == 0:
                acc_buf[...] = part + bd_ref[0]
            else:
                acc_buf[...] += part
        _rows_to_tiles(ys_ref, acc_buf[...])

    @pl.when(jnp.logical_not(live))
    def _():
        ys_ref[...] = jnp.zeros_like(ys_ref)


def _ffn(block_e, n_used, xs, w_gu, b_gu, w_d, b_d, rows):
    n_exp, d, f2 = w_gu.shape
    f = f2 // 2
    s = d // LANES
    n_rows = xs.shape[0] // s
    live = lambda i, nu: jnp.minimum(i, nu[0] - 1)
    wmap = lambda i, be, nu: (be[live(i, nu)], 0, 0)
    perm = jnp.asarray(_deinterleave_table(), BF16)
    return pl.pallas_call(
        functools.partial(_ffn_kernel, ft=min(f, 512)),
        grid_spec=pltpu.PrefetchScalarGridSpec(
            num_scalar_prefetch=2,
            grid=(n_rows // rows,),
            in_specs=[pl.BlockSpec((rows * s, LANES), lambda i, be, nu: (live(i, nu), 0)),
                      pl.BlockSpec((1, d, f2), wmap),
                      pl.BlockSpec((1, 1, f), wmap), pl.BlockSpec((1, 1, f), wmap),
                      pl.BlockSpec((1, f, d), wmap), pl.BlockSpec((1, 1, d), wmap),
                      pl.BlockSpec(perm.shape, lambda i, be, nu: (0, 0))],
            out_specs=pl.BlockSpec((rows * s, LANES), lambda i, be, nu: (i, 0)),
            scratch_shapes=[pltpu.VMEM((d, f), BF16), pltpu.VMEM((d, f), BF16), pltpu.VMEM((f, d), BF16),
                            pltpu.VMEM((rows, d), F32)]),
        out_shape=jax.ShapeDtypeStruct(xs.shape, F32),
        compiler_params=_cparams("arbitrary"),
        name="expert_ffn",
    )(block_e, n_used, xs, w_gu, b_gu[:, 0::2].reshape(n_exp, 1, f), b_gu[:, 1::2].reshape(n_exp, 1, f),
      w_d, b_d.reshape(n_exp, 1, d), perm)


def _combine_kernel(dest_ref, tg_ref, x1_ref, mod_ref, fg_ref, ys_ref, o_ref, ybuf, sem, *, top_k, final_norm):
    tt = x1_ref.shape[0]
    s = ybuf.shape[1] // tt

    def issue(r, carry):
        src = pl.multiple_of(dest_ref[0, 0, r] * s, s)
        dst = pl.multiple_of((r // top_k) * s, s)
        pltpu.make_async_copy(ys_ref.at[pl.ds(src, s)], ybuf.at[r % top_k, pl.ds(dst, s)], sem).start()
        return carry
    lax.fori_loop(0, tt * top_k, issue, 0)
    for j in range(top_k):
        pltpu.make_async_copy(ys_ref.at[pl.ds(0, tt * s)], ybuf.at[j], sem).wait()
    tg = tg_ref[...]
    moe = tg[:, 0:1] * _tiles_to_rows(ybuf.at[0], tt)
    for j in range(1, top_k):
        moe = moe + tg[:, j:j + 1] * _tiles_to_rows(ybuf.at[j], tt)
    x2 = x1_ref[...] + mod_ref[0, 5:6, :] * moe
    if final_norm:
        x2 = x2 * lax.rsqrt(jnp.mean(x2 * x2, axis=-1, keepdims=True) + EPS) * fg_ref[...]
    o_ref[...] = x2


def _combine(dest, tg, x1, mods, final_g, ys, n_ctx_tiles, tiles_per_lat, tt, final_norm):
    t, d = x1.shape
    return pl.pallas_call(
        functools.partial(_combine_kernel, top_k=TOP_K, final_norm=final_norm),
        grid=(t // tt,),
        in_specs=[pl.BlockSpec((1, 1, tt * TOP_K), lambda i: (i, 0, 0), memory_space=pltpu.SMEM),
                  pl.BlockSpec((tt, LANES), lambda i: (i, 0)),
                  pl.BlockSpec((tt, d), lambda i: (i, 0)),
                  pl.BlockSpec((1, 6, d), _mod_index_map(n_ctx_tiles, tiles_per_lat)),
                  pl.BlockSpec((1, d), lambda i: (0, 0)),
                  pl.BlockSpec(memory_space=pl.ANY)],
        out_specs=pl.BlockSpec((tt, d), lambda i: (i, 0)),
        out_shape=jax.ShapeDtypeStruct((t, d), F32),
        scratch_shapes=[pltpu.VMEM((TOP_K, tt * d // LANES, LANES), F32), pltpu.SemaphoreType.DMA(())],
        compiler_params=_cparams("arbitrary"),
        name="combine",
    )(dest, tg, x1, mods, final_g.reshape(1, d), ys)


def _moe_plan(counts, rows, n_blocks):
    n_exp = counts.shape[0]
    padded = (counts + rows - 1) // rows * rows
    pad_end = jnp.cumsum(padded)
    block_row = jnp.arange(n_blocks, dtype=jnp.int32) * rows
    block_e = jnp.minimum(jnp.sum(pad_end[None, :] <= block_row[:, None], axis=1), n_exp - 1).astype(jnp.int32)
    n_used = (pad_end[-1:] // rows).astype(jnp.int32)
    pad_start = jnp.pad((pad_end - padded).astype(F32), (0, LANES - n_exp)).reshape(1, LANES)
    return pad_start, block_e, n_used


def _sequence_mixers(z_hy, z_qk, z_v, z_o, z_g, lw, state, row_w):
    (hy_cw, hy_cb, filt_params, hy_b, ml_cw, ml_cb, ml_gb, ml_ng) = lw
    y_hy = _hyena(z_hy, hy_cw, hy_cb, filt_params, hy_b, row_w)
    qk = _short_conv(z_qk, ml_cw, ml_cb, row_w, silu=True)
    y_ml, st = _mlstm(qk, z_v, z_o, z_g, ml_gb, ml_ng, *state)
    return y_hy, y_ml, st


def kernel(x_prompt, x_sample, state_mlstm_C, state_mlstm_n, state_mlstm_m, c, c_ctx, ada_w, ada_b, norm1_g,
           w_in, hy_conv_w, hy_conv_b, filt_w1, filt_b1, filt_w2, filt_b2, filt_w3, filt_freq, hy_bias,
           ml_conv_w, ml_conv_b, ml_gate_b, ml_norm_g, w_out, norm2_g, router_w, router_b, moe_w_gu,
           moe_b_gu, moe_w_down, moe_b_down, final_g):
    bp, lp, d = x_prompt.shape
    bs, ls, _ = x_sample.shape
    depth = ada_w.shape[0]
    heads = ml_gate_b.shape[-1]
    hy_w = hy_bias.shape[-1]
    ml_w = ml_norm_g.shape[-1]
    dh = ml_w // heads
    n_exp = router_w.shape[-1]
    t_ctx, t_lat = bp * lp, bs * ls
    t = t_ctx + t_lat
    tm = min(ROW_TILE, lp)
    tt = min(TOK_TILE, lp)
    ng = N_DIR * 2 * heads
    seg_widths = (3 * hy_w, 2 * ml_w, ml_w, ml_w)
    n_main = 3 * hy_w + 4 * ml_w
    n_blocks = -(-(t * TOP_K) // MOE_ROWS) + n_exp

    x = jnp.concatenate([x_prompt.reshape(t_ctx, d), x_sample.reshape(t_lat, d)], axis=0)
    cond = jnp.concatenate([c_ctx[None], c, jnp.zeros((SUBLANES - 1 - bs, d), F32)], axis=0)
    zero_state = (jnp.zeros((bp, N_DIR, heads, dh, dh), F32), jnp.zeros((bp, N_DIR, heads, dh), F32),
                  jnp.zeros((bp, N_DIR, heads), F32))
    new_c, new_n, new_m = [], [], []
    for l in range(depth):
        mods = _ada(cond, ada_w[l], ada_b[l]).reshape(SUBLANES, 6, d)
        w_gate = jnp.pad(w_in[l][:, n_main:], ((0, 0), (0, LANES - ng)))
        z = _inproj(x, mods, norm1_g[l], w_in[l][:, :n_main].astype(BF16), w_gate, seg_widths,
                    t_ctx // tm, ls // tm, tm)
        lw = (hy_conv_w[l], hy_conv_b[l],
              (filt_w1[l], filt_b1[l], filt_w2[l], filt_b2[l], filt_w3[l], filt_freq[l]), hy_bias[l],
              ml_conv_w[l], ml_conv_b[l], ml_gate_b[l], ml_norm_g[l])
        z_ctx = [a[:t_ctx].reshape(bp, lp, a.shape[1]) for a in z]
        z_lat = [a[t_ctx:].reshape(bs, ls, a.shape[1]) for a in z]
        cache = (state_mlstm_C[:, l], state_mlstm_n[:, l], state_mlstm_m[:, l])
        hy_c, ml_c, st = _sequence_mixers(*z_ctx, lw, zero_state, lp)
        hy_l, ml_l, _ = _sequence_mixers(*z_lat, lw, cache, GRID_W)
        new_c.append(st[0])
        new_n.append(st[1])
        new_m.append(st[2])
        y_hy = jnp.concatenate([hy_c.reshape(t_ctx, hy_w), hy_l.reshape(t_lat, hy_w)], axis=0)
        y_ml = jnp.concatenate([ml_c.reshape(t_ctx, ml_w), ml_l.reshape(t_lat, ml_w)], axis=0)
        x1, h2, ti, tg, rk, cnt = _post(y_hy, y_ml, x, mods, w_out[l], norm2_g[l], router_w[l], router_b[l],
                                        t_ctx // tm, ls // tm, tm)
        pad_start, block_e, n_used = _moe_plan(cnt[0, :n_exp].astype(jnp.int32), MOE_ROWS, n_blocks)
        dest = _dest_rows(ti, rk, pad_start, tm)[:, :TOP_K].reshape(t // tt, 1, tt * TOP_K)
        xs = _dispatch(dest, h2, n_blocks * MOE_ROWS, tt)
        ys = _ffn(block_e, n_used, xs, moe_w_gu[l], moe_b_gu[l], moe_w_down[l], moe_b_down[l], MOE_ROWS)
        x = _combine(dest, tg, x1, mods, final_g, ys, t_ctx // tt, ls // tt, tt, final_norm=l == depth - 1)
    y_prompt = x[:t_ctx].reshape(bp, lp, d)
    y_sample = x[t_ctx:].reshape(bs, ls, d)
    return (y_prompt, y_sample, jnp.stack(new_c, axis=1), jnp.stack(new_n, axis=1), jnp.stack(new_m, axis=1))
```

```python
import functools
import math

import numpy as np
import jax
import jax.numpy as jnp
from jax import lax
from jax.experimental import pallas as pl
from jax.experimental.pallas import tpu as pltpu

F32 = jnp.float32
BF16 = jnp.bfloat16
HIGHEST = lax.Precision.HIGHEST
EPS = 1e-6

LANES = 128
SUBLANES = 8
VMEM_LIMIT_BYTES = 56 * 1024 * 1024

GRID_W = 64
ML_HEADS = 4
N_DIR = 2
HY_ORDER = 2
FILT_BANDS = 8
DECAY_TARGET = 1e-2
FAST_DECAY_PCT = 0.3
SLOW_DECAY_PCT = 1.5
TOP_K = 4
SWIGLU_LIMIT = 7.0
SWIGLU_ALPHA = 1.702

FFT_N2 = 128
MLSTM_CHUNK = 128
MOE_ROWS = 512
ROW_TILE = 512
TOK_TILE = 256


def _cparams(*sem):
    return pltpu.CompilerParams(dimension_semantics=sem, vmem_limit_bytes=VMEM_LIMIT_BYTES)


def _lane_tile(c, cap):
    return max(t for t in range(LANES, min(c, cap) + 1, LANES) if c % t == 0)


def _bdot(a, b):
    return jnp.dot(a.astype(BF16), b.astype(BF16), preferred_element_type=F32)


def _hdot(a, b):
    return jnp.dot(a, b, precision=HIGHEST, preferred_element_type=F32)


def _ada_kernel(c_ref, w_ref, b_ref, o_ref):
    c = c_ref[...]
    o_ref[...] = _hdot(c * jax.nn.sigmoid(c), w_ref[...]) + b_ref[...]


def _ada(cond, w, b):
    r, d = cond.shape
    n = w.shape[1]
    tn = _lane_tile(n, 1024)
    return pl.pallas_call(
        _ada_kernel,
        grid=(n // tn,),
        in_specs=[pl.BlockSpec((r, d), lambda j: (0, 0)),
                  pl.BlockSpec((d, tn), lambda j: (0, j)),
                  pl.BlockSpec((1, tn), lambda j: (0, j))],
        out_specs=pl.BlockSpec((r, tn), lambda j: (0, j)),
        out_shape=jax.ShapeDtypeStruct((r, n), F32),
        compiler_params=_cparams("arbitrary"),
        name="ada",
    )(cond, w, b.reshape(1, n))


def _mod_index_map(n_ctx_tiles, tiles_per_lat):
    def index_map(i):
        return (jnp.where(i < n_ctx_tiles, 0, 1 + (i - n_ctx_tiles) // tiles_per_lat), 0, 0)
    return index_map


def _inproj_kernel(x_ref, mod_ref, g_ref, w_ref, wg_ref, *out_refs, offsets):
    x = x_ref[...]
    h = x * lax.rsqrt(jnp.mean(x * x, axis=-1, keepdims=True) + EPS) * g_ref[...]
    h = h * (1.0 + mod_ref[0, 1:2, :]) + mod_ref[0, 0:1, :]
    hb = h.astype(BF16)
    for o_ref, (lo, hi) in zip(out_refs[:-1], offsets):
        o_ref[...] = jnp.dot(hb, w_ref[:, lo:hi], preferred_element_type=F32)
    h_lo = (h - hb.astype(F32)).astype(BF16)
    g = jnp.dot(hb, wg_ref[...], preferred_element_type=F32)
    out_refs[-1][...] = (g[:, :LANES] + g[:, LANES:]
                         + jnp.dot(h_lo, wg_ref[:, :LANES], preferred_element_type=F32))


def _inproj(x, mods, norm_g, w_main, w_gate, seg_widths, n_ctx_tiles, tiles_per_lat, tm):
    t, d = x.shape
    offsets, lo = [], 0
    for wd in seg_widths:
        offsets.append((lo, lo + wd))
        lo += wd
    wg_hi = w_gate.astype(BF16)
    wg = jnp.concatenate([wg_hi, (w_gate - wg_hi.astype(F32)).astype(BF16)], axis=1)
    widths = tuple(seg_widths) + (LANES,)
    return pl.pallas_call(
        functools.partial(_inproj_kernel, offsets=tuple(offsets)),
        grid=(t // tm,),
        in_specs=[pl.BlockSpec((tm, d), lambda i: (i, 0)),
                  pl.BlockSpec((1, 6, d), _mod_index_map(n_ctx_tiles, tiles_per_lat)),
                  pl.BlockSpec((1, d), lambda i: (0, 0)),
                  pl.BlockSpec(w_main.shape, lambda i: (0, 0)),
                  pl.BlockSpec(wg.shape, lambda i: (0, 0))],
        out_specs=[pl.BlockSpec((tm, wd), lambda i: (i, 0)) for wd in widths],
        out_shape=[jax.ShapeDtypeStruct((t, wd), F32) for wd in widths],
        compiler_params=_cparams("arbitrary"),
        name="inproj",
    )(x, mods, norm_g.reshape(1, d), w_main, wg)


def _short_conv_kernel(x_ref, w_ref, b_ref, o_ref, *, row_w, silu):
    x = x_ref[0]
    l = x.shape[0]
    pos = lax.broadcasted_iota(jnp.int32, x.shape, 0) % row_w
    prev = jnp.where(pos == 0, 0.0, pltpu.roll(x, 1, 0))
    nxt = jnp.where(pos == row_w - 1, 0.0, pltpu.roll(x, l - 1, 0))
    y = prev * w_ref[0:1, :] + x * w_ref[1:2, :] + nxt * w_ref[2:3, :] + b_ref[...]
    if silu:
        y = y * jax.nn.sigmoid(y)
    o_ref[0] = y


def _short_conv(x, w, b, row_w, silu):
    bsz, l, c = x.shape
    ct = _lane_tile(c, 256)
    return pl.pallas_call(
        functools.partial(_short_conv_kernel, row_w=row_w, silu=silu),
        grid=(bsz, c // ct),
        in_specs=[pl.BlockSpec((1, l, ct), lambda i, j: (i, 0, j)),
                  pl.BlockSpec((3, ct), lambda i, j: (0, j)),
                  pl.BlockSpec((1, ct), lambda i, j: (0, j))],
        out_specs=pl.BlockSpec((1, l, ct), lambda i, j: (i, 0, j)),
        out_shape=jax.ShapeDtypeStruct(x.shape, F32),
        compiler_params=_cparams("arbitrary", "arbitrary"),
        name="short_conv",
    )(x, w, b.reshape(1, c))


def _dft_direct_tables(l):
    n = 2 * l
    k = np.arange(n)[:, None].astype(np.float64)
    t = np.arange(n)[None, :].astype(np.float64)
    ang = 2.0 * np.pi * ((k * t) % n) / n
    cm, sm = np.cos(ang), np.sin(ang)
    fwd = np.block([[cm[:, :l], sm[:, :l]], [-sm[:, :l], cm[:, :l]]])
    filt = np.concatenate([cm, -sm], axis=0)
    return fwd, filt


def _dft_two_level_tables(l, n2):
    n = 2 * l
    n1 = n // n2
    k1 = np.arange(n1)[:, None].astype(np.float64)
    a = np.arange(n1)[None, :].astype(np.float64)
    ang1 = 2.0 * np.pi * ((k1 * a) % n1) / n1
    c1, s1 = np.cos(ang1), np.sin(ang1)
    h = n1 // 2
    m1 = np.block([[c1[:, :h], s1[:, :h]], [-s1[:, :h], c1[:, :h]]])
    m1f = np.concatenate([c1, -s1], axis=0)
    kk = (np.arange(n1)[:, None, None] + n1 * np.arange(n2)[None, :, None]).astype(np.float64)
    b = np.arange(n2)[None, None, :].astype(np.float64)
    ang = 2.0 * np.pi * ((kk * b) % n) / n
    cg, sg = np.cos(ang), np.sin(ang)
    gt = np.concatenate([np.concatenate([cg, sg], axis=2),
                         np.concatenate([-sg, cg], axis=2)], axis=1)
    return m1, m1f, gt


def _circular_lag(n0, rows, l):
    n = n0 + lax.broadcasted_iota(jnp.int32, (rows, 1), 0)
    t = jnp.where(n < l, n, 2 * l - n).astype(F32)
    return n, t, t / float(max(l - 1, 1))


def _filter_hidden_kernel(bandv_ref, w1_ref, b1_ref, w2_ref, b2_ref, freq_ref, o_ref, *, l):
    rows = o_ref.shape[0]
    _, t, t01 = _circular_lag(pl.program_id(0) * rows, rows, l)
    lane = lax.broadcasted_iota(jnp.int32, (rows, LANES), 1)
    ang = (2.0 * math.pi / l) * t * bandv_ref[...]
    feats = jnp.where(lane == 0, t01,
                      jnp.where(lane <= FILT_BANDS, jnp.cos(ang),
                                jnp.where(lane <= 2 * FILT_BANDS, -jnp.sin(ang), 0.0)))
    fr = freq_ref[...]
    h = jnp.sin(fr * (_hdot(feats, w1_ref[...]) + b1_ref[...]))
    o_ref[...] = jnp.sin(fr * (_hdot(h, w2_ref[...]) + b2_ref[...]))


def _filter_hidden(l, f_w1, f_b1, f_w2, f_b2, f_freq):
    emb, hid = f_w1.shape
    n = 2 * l
    rows = min(n, 512)
    bands = jnp.linspace(1e-4, FILT_BANDS - 1, FILT_BANDS, dtype=F32)
    bandv = jnp.zeros((1, LANES), F32).at[0, 1:1 + FILT_BANDS].set(bands)
    bandv = bandv.at[0, 1 + FILT_BANDS:1 + 2 * FILT_BANDS].set(bands)
    w1p = jnp.zeros((LANES, hid), F32).at[:emb].set(f_w1)
    c0 = lambda i: (0, 0)
    return pl.pallas_call(
        functools.partial(_filter_hidden_kernel, l=l),
        grid=(n // rows,),
        in_specs=[pl.BlockSpec((1, LANES), c0), pl.BlockSpec((LANES, hid), c0), pl.BlockSpec((1, hid), c0),
                  pl.BlockSpec((hid, hid), c0), pl.BlockSpec((1, hid), c0), pl.BlockSpec((1, hid), c0)],
        out_specs=pl.BlockSpec((rows, hid), lambda i: (i, 0)),
        out_shape=jax.ShapeDtypeStruct((n, hid), F32),
        compiler_params=_cparams("arbitrary"),
        name="filter_hidden",
    )(bandv, w1p, f_b1.reshape(1, hid), f_w2, f_b2.reshape(1, hid), f_freq.reshape(1, hid))


def _filter_rows(n0, rows, l, hid_ref, w3_ref, delta_ref):
    n, _, t01 = _circular_lag(n0, rows, l)
    h = hid_ref[pl.ds(n0, rows), :]
    hf = _bdot(h, w3_ref[0, 0])
    hb = _bdot(h, w3_ref[0, 1])
    window = jnp.exp(-t01 * delta_ref[...])
    return jnp.where(n < l, hf, jnp.where(n > l, hb, 0.0)) * window


def _filter_direct_kernel(hid_ref, w3_ref, delta_ref, ff_ref, h_ref, *, l):
    hc = _filter_rows(0, 2 * l, l, hid_ref, w3_ref, delta_ref)
    h_ref[0] = _bdot(ff_ref[...], hc) * (1.0 / (2 * l))


def _filter_two_level_kernel(hid_ref, w3_ref, delta_ref, m1f_ref, gt_ref, h_ref, hc_buf, a_buf, *, l, n2, rows):
    n = 2 * l
    n1 = n // n2

    def fill(i, carry):
        r0 = pl.multiple_of(i * rows, rows)
        hc_buf[pl.ds(r0, rows), :] = _filter_rows(r0, rows, l, hid_ref, w3_ref, delta_ref)
        return carry
    lax.fori_loop(0, n // rows, fill, 0)

    def step1(b, carry):
        col = hc_buf[pl.ds(b, n1, stride=n2), :]
        a = _bdot(m1f_ref[...], col)
        a_buf[pl.ds(b, n1, stride=2 * n2), :] = a[:n1]
        a_buf[pl.ds(n2 + b, n1, stride=2 * n2), :] = a[n1:]
        return carry
    lax.fori_loop(0, n2, step1, 0)

    def step2(k1, carry):
        r0 = pl.multiple_of(k1 * 2 * n2, 2 * n2)
        h_ref[0, k1] = (_bdot(gt_ref[k1], a_buf[pl.ds(r0, 2 * n2), :]) * (1.0 / n)).astype(h_ref.dtype)
        return carry
    lax.fori_loop(0, n1, step2, 0)


def _filter_tail_inputs(hy_w, f_w3):
    hid = f_w3.shape[0]
    w3 = f_w3.reshape(hid, HY_ORDER, N_DIR, hy_w).transpose(1, 2, 0, 3)
    max_decay = math.log(DECAY_TARGET) / FAST_DECAY_PCT
    min_decay = math.log(DECAY_TARGET) / SLOW_DECAY_PCT
    deltas = jnp.abs(jnp.linspace(min_decay, max_decay, hy_w, dtype=F32)).reshape(1, hy_w)
    return w3, deltas


def _filter_spectrum_direct(l, hy_w, filt_params, ff):
    f_w1, f_b1, f_w2, f_b2, f_w3, f_freq = filt_params
    hidden = _filter_hidden(l, f_w1, f_b1, f_w2, f_b2, f_freq)
    w3, deltas = _filter_tail_inputs(hy_w, f_w3)
    hid = f_w3.shape[0]
    n = 2 * l
    return pl.pallas_call(
        functools.partial(_filter_direct_kernel, l=l),
        grid=(HY_ORDER,),
        in_specs=[pl.BlockSpec((n, hid), lambda o: (0, 0)),
                  pl.BlockSpec((1, N_DIR, hid, hy_w), lambda o: (o, 0, 0, 0)),
                  pl.BlockSpec((1, hy_w), lambda o: (0, 0)),
                  pl.BlockSpec((2 * n, n), lambda o: (0, 0))],
        out_specs=pl.BlockSpec((1, 2 * n, hy_w), lambda o: (o, 0, 0)),
        out_shape=jax.ShapeDtypeStruct((HY_ORDER, 2 * n, hy_w), F32),
        compiler_params=_cparams("arbitrary"),
        name="filter_direct",
    )(hidden, w3, deltas, ff)


def _filter_spectrum_two_level(l, hy_w, filt_params, m1f, gt, ct):
    f_w1, f_b1, f_w2, f_b2, f_w3, f_freq = filt_params
    hidden = _filter_hidden(l, f_w1, f_b1, f_w2, f_b2, f_freq)
    w3, deltas = _filter_tail_inputs(hy_w, f_w3)
    hid = f_w3.shape[0]
    n = 2 * l
    n2 = FFT_N2
    n1 = n // n2
    return pl.pallas_call(
        functools.partial(_filter_two_level_kernel, l=l, n2=n2, rows=min(n, 512)),
        grid=(HY_ORDER, hy_w // ct),
        in_specs=[pl.BlockSpec((n, hid), lambda o, j: (0, 0)),
                  pl.BlockSpec((1, N_DIR, hid, ct), lambda o, j: (o, 0, 0, j)),
                  pl.BlockSpec((1, ct), lambda o, j: (0, j)),
                  pl.BlockSpec((2 * n1, n1), lambda o, j: (0, 0)),
                  pl.BlockSpec((n1, 2 * n2, 2 * n2), lambda o, j: (0, 0, 0))],
        out_specs=pl.BlockSpec((1, n1, 2 * n2, ct), lambda o, j: (o, 0, 0, j)),
        out_shape=jax.ShapeDtypeStruct((HY_ORDER, n1, 2 * n2, hy_w), BF16),
        scratch_shapes=[pltpu.VMEM((n, ct), F32), pltpu.VMEM((n1 * 2 * n2, ct), F32)],
        compiler_params=_cparams("arbitrary", "arbitrary"),
        name="filter_two_level",
    )(hidden, w3, deltas, m1f, gt)


def _complex_mul(x, h, half):
    xr, xi = x[:half], x[half:]
    hr, hi = h[:half], h[half:]
    return jnp.concatenate([xr * hr - xi * hi, xr * hi + xi * hr], axis=0)


def _conv_direct_kernel(z_ref, gate_ref, bias_ref, h_ref, fwd_ref, inv_ref, o_ref):
    l = z_ref.shape[1]
    z = jnp.concatenate([z_ref[0], z_ref[1]], axis=0)
    x = _bdot(fwd_ref[...], z)
    y = _bdot(inv_ref[...], _complex_mul(x, h_ref[0], 2 * l))
    bias = bias_ref[0]
    o_ref[0] = gate_ref[0] * (y[:l] + bias * z_ref[0])
    o_ref[1] = gate_ref[1] * (y[l:] + bias * z_ref[1])


def _conv_two_level_kernel(z_ref, gate_ref, bias_ref, h_ref, m1_ref, m1i_ref, gt_ref, o_ref,
                           a_buf, *, n2):
    l = z_ref.shape[1]
    n1 = 2 * l // n2
    hn = n1 // 2

    def step1(b, carry):
        za = z_ref[0, pl.ds(b, hn, stride=n2), :]
        zb = z_ref[1, pl.ds(b, hn, stride=n2), :]
        a = _bdot(m1_ref[...], jnp.concatenate([za, zb], axis=0))
        a_buf[pl.ds(b, n1, stride=2 * n2), :] = a[:n1]
        a_buf[pl.ds(n2 + b, n1, stride=2 * n2), :] = a[n1:]
        return carry
    lax.fori_loop(0, n2, step1, 0)

    def step2(k1, carry):
        r0 = pl.multiple_of(k1 * 2 * n2, 2 * n2)
        x = _bdot(gt_ref[k1], a_buf[pl.ds(r0, 2 * n2), :])
        y = _complex_mul(x, h_ref[0, k1].astype(F32), n2).astype(BF16)
        a_buf[pl.ds(r0, 2 * n2), :] = lax.dot_general(gt_ref[k1], y, (((0,), (0,)), ((), ())),
                                                      preferred_element_type=F32)
        return carry
    lax.fori_loop(0, n1, step2, 0)

    def step3(b, carry):
        br = a_buf[pl.ds(b, n1, stride=2 * n2), :]
        bi = a_buf[pl.ds(n2 + b, n1, stride=2 * n2), :]
        y = _bdot(m1i_ref[...], jnp.concatenate([br, bi], axis=0))
        o_ref[0, pl.ds(b, hn, stride=n2), :] = y[:hn]
        o_ref[1, pl.ds(b, hn, stride=n2), :] = y[hn:]
        return carry
    lax.fori_loop(0, n2, step3, 0)

    bias = bias_ref[0]
    for s in range(2):
        o_ref[s] = gate_ref[s] * (o_ref[s] + bias * z_ref[s])


def _long_conv_gated(u, z, z_col, gate_col, spectrum, order, bias, tables, ct):
    bsz, l, _ = u.shape
    c = spectrum.shape[-1]
    nct = c // ct
    zspec = pl.BlockSpec((2, l, ct), lambda i, j: (i, 0, z_col * nct + j))
    gspec = pl.BlockSpec((2, l, ct), lambda i, j: (i, 0, gate_col * nct + j))
    bspec = pl.BlockSpec((1, 1, ct), lambda i, j: (order, 0, j))
    ospec = pl.BlockSpec((2, l, ct), lambda i, j: (i, 0, j))
    out_shape = jax.ShapeDtypeStruct((bsz, l, c), F32)
    bias3 = bias.reshape(HY_ORDER, 1, c)
    if len(tables) == 2:
        fwd, inv = tables
        n = 2 * l
        return pl.pallas_call(
            _conv_direct_kernel,
            grid=(bsz // 2, nct),
            in_specs=[zspec, gspec, bspec,
                      pl.BlockSpec((1, 2 * n, ct), lambda i, j: (order, 0, j)),
                      pl.BlockSpec(fwd.shape, lambda i, j: (0, 0)),
                      pl.BlockSpec(inv.shape, lambda i, j: (0, 0))],
            out_specs=ospec, out_shape=out_shape,
            compiler_params=_cparams("arbitrary", "arbitrary"),
            name="long_conv_direct",
        )(z, u, bias3, spectrum, fwd, inv)
    m1, m1i, gt = tables
    n2 = FFT_N2
    n1 = 2 * l // n2
    const2 = lambda i, j: (0, 0)
    const3 = lambda i, j: (0, 0, 0)
    return pl.pallas_call(
        functools.partial(_conv_two_level_kernel, n2=n2),
        grid=(nct, bsz // 2),
        in_specs=[pl.BlockSpec((2, l, ct), lambda j, i: (i, 0, z_col * nct + j)),
                  pl.BlockSpec((2, l, ct), lambda j, i: (i, 0, gate_col * nct + j)),
                  pl.BlockSpec((1, 1, ct), lambda j, i: (order, 0, j)),
                  pl.BlockSpec((1, n1, 2 * n2, ct), lambda j, i: (order, 0, 0, j)),
                  pl.BlockSpec(m1.shape, const2), pl.BlockSpec(m1i.shape, const2),
                  pl.BlockSpec(gt.shape, const3)],
        out_specs=pl.BlockSpec((2, l, ct), lambda j, i: (i, 0, j)),
        out_shape=out_shape,
        scratch_shapes=[pltpu.VMEM((n1 * 2 * n2, ct), F32)],
        compiler_params=_cparams("arbitrary", "arbitrary"),
        name="long_conv_two_level",
    )(z, u, bias3, spectrum, m1, m1i, gt)


def _hyena(z_hy, conv_w, conv_b, filt_params, hy_bias, row_w):
    bsz, l, c3 = z_hy.shape
    c = c3 // 3
    u = _short_conv(z_hy, conv_w, conv_b, row_w, silu=False)
    table = lambda a: jnp.asarray(a, F32).astype(BF16)
    if 2 * l // FFT_N2 <= 4:
        fwd, filt = _dft_direct_tables(l)
        tables = (table(fwd), table(fwd.T))
        spectrum = _filter_spectrum_direct(l, c, filt_params, table(filt))
        ct = c
    else:
        m1, m1f, gt = _dft_two_level_tables(l, FFT_N2)
        tables = (table(m1), table(m1.T), table(gt))
        ct = LANES
        spectrum = _filter_spectrum_two_level(l, c, filt_params, table(m1f), tables[2], ct)
    z1 = _long_conv_gated(u, u, 0, 1, spectrum, 0, hy_bias, tables, ct)
    return _long_conv_gated(u, z1, 0, 2, spectrum, 1, hy_bias, tables, ct)


def _log_sigmoid(x):
    return jnp.minimum(x, 0.0) - jnp.log1p(jnp.exp(-jnp.abs(x)))


def _mlstm_chunk(q, k, v, icol, irow, bcol, brow, btot, mask, c, n, m):
    qb, kb, vb = q.astype(BF16), k.astype(BF16), v.astype(BF16)
    dm = jnp.where(mask, bcol - brow + irow, -jnp.inf)
    inter = bcol + m
    mj = jnp.maximum(jnp.max(dm, axis=-1, keepdims=True), inter)
    w_int = jnp.exp(inter - mj)
    s = lax.dot_general(qb, kb, (((1,), (1,)), ((), ())), preferred_element_type=F32) * jnp.exp(dm - mj)
    num = w_int * _bdot(qb, c) + _bdot(s, vb)
    den = w_int * jnp.sum(q * n, axis=-1, keepdims=True) + jnp.sum(s, axis=-1, keepdims=True)
    h = num / jnp.maximum(jnp.abs(den), jnp.exp(-mj))
    m_new = jnp.maximum(btot + m, jnp.max(btot - brow + irow, axis=-1, keepdims=True))
    a = jnp.exp(btot + m - m_new)
    wk = jnp.exp(btot - bcol + icol - m_new) * k
    c_new = a * c + _bdot(wk.T, vb)
    n_new = a * n + jnp.sum(wk, axis=0, keepdims=True)
    return h, c_new, n_new, m_new


def _mlstm_kernel(q_ref, k_ref, v_ref, o_ref, g_ref, gb_ref, ng_ref, c0_ref, n0_ref, m0_ref,
                  y_ref, c_ref, n_ref, m_ref, hf_buf, hb_buf, *, chunk):
    l, dh = q_ref.shape[1], q_ref.shape[2]
    nc = l // chunk
    scale = dh ** -0.5
    row = lax.broadcasted_iota(jnp.int32, (chunk, chunk), 0)
    col = lax.broadcasted_iota(jnp.int32, (chunk, chunk), 1)
    lower, upper = col <= row, col >= row
    tri_l, tri_u = lower.astype(F32), upper.astype(F32)
    lane = lax.broadcasted_iota(jnp.int32, (chunk, LANES), 1)

    def gates(r0):
        g = g_ref[0, 0, pl.ds(r0, chunk), :] + gb_ref[0]
        val = jnp.where((lane == 1) | (lane == 3), _log_sigmoid(g), g)
        w = jnp.where(lane == 1, _hdot(tri_l, val), jnp.where(lane == 3, _hdot(tri_u, val), val))
        return w, w.T

    def body(j, carry):
        cf, nf, mf, cb, nb, mb = carry
        rf = pl.multiple_of(j * chunk, chunk)
        w, wt = gates(rf)
        hf, cf, nf, mf = _mlstm_chunk(
            q_ref[0, pl.ds(rf, chunk), :], k_ref[0, pl.ds(rf, chunk), :] * scale, v_ref[0, pl.ds(rf, chunk), :],
            w[:, 0:1], wt[0:1, :], w[:, 1:2], wt[1:2, :], wt[1:2, chunk - 1:chunk], lower, cf, nf, mf)
        hf_buf[pl.ds(rf, chunk), :] = hf
        rb = pl.multiple_of((nc - 1 - j) * chunk, chunk)
        w, wt = gates(rb)
        hb, cb, nb, mb = _mlstm_chunk(
            q_ref[0, pl.ds(rb, chunk), :], k_ref[0, pl.ds(rb, chunk), :] * scale, v_ref[0, pl.ds(rb, chunk), :],
            w[:, 2:3], wt[2:3, :], w[:, 3:4], wt[3:4, :], wt[3:4, 0:1], upper, cb, nb, mb)
        hb_buf[pl.ds(rb, chunk), :] = hb
        return cf, nf, mf, cb, nb, mb

    init = (c0_ref[0, 0, 0], n0_ref[0, 0, 0:1, :], m0_ref[0, 0, 0:1, 0:1],
            c0_ref[0, 1, 0], n0_ref[0, 0, 1:2, :], m0_ref[0, 0, 1:2, 0:1])
    cf, nf, mf, cb, nb, mb = lax.fori_loop(0, nc, body, init)
    c_ref[0, 0, 0] = cf
    c_ref[0, 1, 0] = cb
    n_ref[0, 0, 0:1, :] = nf
    n_ref[0, 0, 1:2, :] = nb
    m_ref[0, 0, 0:1, :] = jnp.broadcast_to(mf, (1, LANES))
    m_ref[0, 0, 1:2, :] = jnp.broadcast_to(mb, (1, LANES))

    def finish(j, carry):
        r0 = pl.multiple_of(j * chunk, chunk)
        hs = hf_buf[pl.ds(r0, chunk), :] + hb_buf[pl.ds(r0, chunk), :]
        hs = hs * lax.rsqrt(jnp.mean(hs * hs, axis=-1, keepdims=True) + EPS) * ng_ref[...]
        y_ref[0, pl.ds(r0, chunk), :] = jax.nn.sigmoid(o_ref[0, pl.ds(r0, chunk), :]) * hs
        return carry
    lax.fori_loop(0, nc, finish, 0)


def _mlstm(qk, z_v, z_o, z_g, gate_b, norm_g, c0, n0, m0):
    bsz, l, ml = z_v.shape
    heads = gate_b.shape[-1]
    dh = ml // heads
    chunk = min(MLSTM_CHUNK, l)
    ng = N_DIR * 2 * heads
    gh = z_g[..., :ng].reshape(bsz, l, N_DIR * 2, heads).transpose(0, 3, 1, 2)
    gh = jnp.pad(gh, ((0, 0), (0, 0), (0, 0), (0, LANES - N_DIR * 2)))
    gb = jnp.pad(gate_b.reshape(N_DIR * 2, heads).T, ((0, 0), (0, LANES - N_DIR * 2))).reshape(heads, 1, LANES)
    n0h = n0.transpose(0, 2, 1, 3)
    m0h = jnp.broadcast_to(m0.transpose(0, 2, 1)[..., None], (bsz, heads, N_DIR, LANES))
    seq = lambda col0: pl.BlockSpec((1, l, dh), lambda b, h: (b, 0, col0 + h))
    cspec = pl.BlockSpec((1, N_DIR, 1, dh, dh), lambda b, h: (b, 0, h, 0, 0))
    sspec = pl.BlockSpec((1, 1, N_DIR, dh), lambda b, h: (b, h, 0, 0))
    mspec = pl.BlockSpec((1, 1, N_DIR, LANES), lambda b, h: (b, h, 0, 0))
    y, c, n, m = pl.pallas_call(
        functools.partial(_mlstm_kernel, chunk=chunk),
        grid=(bsz, heads),
        in_specs=[seq(0), seq(heads), seq(0), seq(0),
                  pl.BlockSpec((1, 1, l, LANES), lambda b, h: (b, h, 0, 0)),
                  pl.BlockSpec((1, 1, LANES), lambda b, h: (h, 0, 0)),
                  pl.BlockSpec((1, dh), lambda b, h: (0, h)),
                  cspec, sspec, mspec],
        out_specs=[seq(0), cspec, sspec, mspec],
        out_shape=[jax.ShapeDtypeStruct((bsz, l, ml), F32),
                   jax.ShapeDtypeStruct((bsz, N_DIR, heads, dh, dh), F32),
                   jax.ShapeDtypeStruct((bsz, heads, N_DIR, dh), F32),
                   jax.ShapeDtypeStruct((bsz, heads, N_DIR, LANES), F32)],
        scratch_shapes=[pltpu.VMEM((l, dh), F32), pltpu.VMEM((l, dh), F32)],
        compiler_params=_cparams("arbitrary", "arbitrary"),
        name="mlstm",
    )(qk, qk, z_v, z_o, gh, gb, norm_g.reshape(1, ml), c0, n0h, m0h)
    return y, (c, n.transpose(0, 2, 1, 3), m[..., 0].transpose(0, 2, 1))


def _rows_to_tiles(tile_ref, x):
    r, d = x.shape
    s = d // LANES
    for k in range(s):
        tile_ref[pl.ds(k, r, stride=s), :] = x[:, LANES * k:LANES * (k + 1)]


def _tiles_to_rows(tile_ref, r):
    s = tile_ref.shape[0] // r
    return jnp.concatenate([tile_ref[pl.ds(k, r, stride=s), :] for k in range(s)], axis=1)


def _lane_pack(cols, lane):
    out = jnp.zeros(lane.shape, cols[0].dtype)
    for j, colv in enumerate(cols):
        out = jnp.where(lane == j, colv, out)
    return out


def _post_kernel(yhy_ref, yml_ref, x_ref, mod_ref, wo_ref, n2g_ref, rw_ref, rb_ref,
                 x1_ref, h2_ref, ti_ref, tg_ref, rk_ref, cnt_ref, carry, *, n_exp, top_k):
    @pl.when(pl.program_id(0) == 0)
    def _():
        carry[...] = jnp.zeros_like(carry)

    hy_w = yhy_ref.shape[1]
    tm = x_ref.shape[0]
    proj = _bdot(yhy_ref[...], wo_ref[:hy_w, :]) + _bdot(yml_ref[...], wo_ref[hy_w:, :])
    x1 = x_ref[...] + mod_ref[0, 2:3, :] * proj
    x1_ref[...] = x1
    h2 = x1 * lax.rsqrt(jnp.mean(x1 * x1, axis=-1, keepdims=True) + EPS) * n2g_ref[...]
    h2 = h2 * (1.0 + mod_ref[0, 4:5, :]) + mod_ref[0, 3:4, :]
    _rows_to_tiles(h2_ref, h2)

    lane = lax.broadcasted_iota(jnp.int32, (tm, LANES), 1)
    work = jnp.where(lane < n_exp, _hdot(h2, rw_ref[...]) + rb_ref[...], -jnp.inf)
    vals, idxs, hots = [], [], []
    for _ in range(top_k):
        mx = jnp.max(work, axis=-1, keepdims=True)
        idx = jnp.min(jnp.where(work == mx, lane, LANES), axis=-1, keepdims=True)
        hot = lane == idx
        vals.append(mx)
        idxs.append(idx)
        hots.append(hot)
        work = jnp.where(hot, -jnp.inf, work)
    exps = [jnp.exp(v - vals[0]) for v in vals]
    tot = functools.reduce(lambda a, b: a + b, exps)
    ti_ref[...] = _lane_pack(idxs, lane)
    tg_ref[...] = _lane_pack([e / tot for e in exps], lane)

    hot_sum = functools.reduce(lambda a, b: a + b, [h.astype(F32) for h in hots])
    row = lax.broadcasted_iota(jnp.int32, (tm, tm), 0)
    col = lax.broadcasted_iota(jnp.int32, (tm, tm), 1)
    before = _bdot((col < row).astype(F32), hot_sum) + carry[...]
    ranks = [jnp.sum(jnp.where(h, before, 0.0), axis=-1, keepdims=True).astype(jnp.int32) for h in hots]
    rk_ref[...] = _lane_pack(ranks, lane)
    carry[...] = carry[...] + jnp.sum(hot_sum, axis=0, keepdims=True)
    cnt_ref[...] = jnp.broadcast_to(carry[...], cnt_ref.shape)


def _post(y_hy, y_ml, x, mods, w_out, norm2_g, r_w, r_b, n_ctx_tiles, tiles_per_lat, tm):
    t, d = x.shape
    n_exp = r_w.shape[1]
    rwp = jnp.pad(r_w, ((0, 0), (0, LANES - n_exp)))
    rbp = jnp.pad(r_b, (0, LANES - n_exp)).reshape(1, LANES)
    row = lambda wd: pl.BlockSpec((tm, wd), lambda i: (i, 0))
    const = lambda shape: pl.BlockSpec(shape, lambda i: (0, 0))
    return pl.pallas_call(
        functools.partial(_post_kernel, n_exp=n_exp, top_k=TOP_K),
        grid=(t // tm,),
        in_specs=[row(y_hy.shape[1]), row(y_ml.shape[1]), row(d),
                  pl.BlockSpec((1, 6, d), _mod_index_map(n_ctx_tiles, tiles_per_lat)),
                  const(w_out.shape), const((1, d)), const((d, LANES)), const((1, LANES))],
        out_specs=[row(d), pl.BlockSpec((tm * d // LANES, LANES), lambda i: (i, 0)),
                   row(LANES), row(LANES), row(LANES), const((SUBLANES, LANES))],
        out_shape=[jax.ShapeDtypeStruct((t, d), F32), jax.ShapeDtypeStruct((t * d // LANES, LANES), F32),
                   jax.ShapeDtypeStruct((t, LANES), jnp.int32), jax.ShapeDtypeStruct((t, LANES), F32),
                   jax.ShapeDtypeStruct((t, LANES), jnp.int32), jax.ShapeDtypeStruct((SUBLANES, LANES), F32)],
        scratch_shapes=[pltpu.VMEM((1, LANES), F32)],
        compiler_params=_cparams("arbitrary"),
        name="post",
    )(y_hy, y_ml, x, mods, w_out.astype(BF16), norm2_g.reshape(1, d), rwp, rbp)


def _dest_kernel(ti_ref, rk_ref, ps_ref, d_ref, *, top_k):
    lane = lax.broadcasted_iota(jnp.int32, ti_ref.shape, 1)
    ti, rk = ti_ref[...], rk_ref[...]
    cols = []
    for j in range(top_k):
        start = jnp.sum(jnp.where(lane == ti[:, j:j + 1], ps_ref[...], 0.0), axis=-1, keepdims=True)
        cols.append(start.astype(jnp.int32) + rk[:, j:j + 1])
    d_ref[...] = _lane_pack(cols, lane)


def _dest_rows(ti, rk, pad_start, tm):
    t = ti.shape[0]
    row = pl.BlockSpec((tm, LANES), lambda i: (i, 0))
    return pl.pallas_call(
        functools.partial(_dest_kernel, top_k=TOP_K),
        grid=(t // tm,),
        in_specs=[row, row, pl.BlockSpec((1, LANES), lambda i: (0, 0))],
        out_specs=row,
        out_shape=jax.ShapeDtypeStruct((t, LANES), jnp.int32),
        compiler_params=_cparams("arbitrary"),
        name="dest_rows",
    )(ti, rk, pad_start)


def _dispatch_kernel(dest_ref, h_ref, zero_ref, xs_ref, sem, *, top_k, s):
    del zero_ref
    i = pl.program_id(0)
    tt = dest_ref.shape[2] // top_k

    def issue(t, carry):
        src = pl.multiple_of((i * tt + t) * s, s)
        for j in range(top_k):
            dst = pl.multiple_of(dest_ref[0, 0, t * top_k + j] * s, s)
            pltpu.make_async_copy(h_ref.at[pl.ds(src, s)], xs_ref.at[pl.ds(dst, s)], sem).start()
        return carry
    lax.fori_loop(0, tt, issue, 0, unroll=2)

    def drain():
        for _ in range(top_k):
            pltpu.make_async_copy(h_ref.at[pl.ds(0, tt * s)], xs_ref.at[pl.ds(0, tt * s)], sem).wait()

    @pl.when(i > 0)
    def _():
        drain()

    @pl.when(i == pl.num_programs(0) - 1)
    def _():
        drain()


def _dispatch(dest, h2t, n_rows, tt):
    s = h2t.shape[0] * TOP_K // dest.size
    zeros = jnp.zeros((n_rows * s, LANES), F32)
    return pl.pallas_call(
        functools.partial(_dispatch_kernel, top_k=TOP_K, s=s),
        grid=(dest.shape[0],),
        in_specs=[pl.BlockSpec((1, 1, tt * TOP_K), lambda i: (i, 0, 0), memory_space=pltpu.SMEM),
                  pl.BlockSpec(memory_space=pl.ANY),
                  pl.BlockSpec(memory_space=pl.ANY)],
        out_specs=pl.BlockSpec(memory_space=pl.ANY),
        out_shape=jax.ShapeDtypeStruct((n_rows * s, LANES), F32),
        scratch_shapes=[pltpu.SemaphoreType.DMA(())],
        input_output_aliases={2: 0},
        compiler_params=_cparams("arbitrary"),
        name="dispatch",
    )(dest, h2t, zeros)


def _deinterleave_table():
    p = np.zeros((2 * LANES, 2 * LANES), np.float32)
    j = np.arange(LANES)
    p[2 * j, j] = 1.0
    p[2 * j + 1, LANES + j] = 1.0
    return p


def _ffn_kernel(be_ref, nu_ref, xs_ref, wgu_ref, bg_ref, bl_ref, wd_ref, bd_ref, perm_ref, ys_ref,
                wg_buf, wl_buf, wd_buf, acc_buf, *, ft):
    i = pl.program_id(0)
    f = wg_buf.shape[1]
    rows = acc_buf.shape[0]
    live = i < nu_ref[0]

    @pl.when(live & ((i == 0) | (be_ref[i] != be_ref[jnp.maximum(i - 1, 0)])))
    def _():
        for j in range(f // LANES):
            blk = wgu_ref[0, :, 2 * LANES * j:2 * LANES * (j + 1)].astype(BF16)
            split = jnp.dot(blk, perm_ref[...], preferred_element_type=F32)
            wg_buf[:, LANES * j:LANES * (j + 1)] = split[:, :LANES].astype(BF16)
            wl_buf[:, LANES * j:LANES * (j + 1)] = split[:, LANES:].astype(BF16)
        wd_buf[...] = wd_ref[0].astype(BF16)

    @pl.when(live)
    def _():
        x = _tiles_to_rows(xs_ref, rows).astype(BF16)
        for j, f0 in enumerate(range(0, f, ft)):
            g = jnp.dot(x, wg_buf[:, f0:f0 + ft], preferred_element_type=F32) + bg_ref[0, :, f0:f0 + ft]
            lin = jnp.dot(x, wl_buf[:, f0:f0 + ft], preferred_element_type=F32) + bl_ref[0, :, f0:f0 + ft]
            gate = jnp.minimum(g, SWIGLU_LIMIT)
            lin = jnp.clip(lin, -SWIGLU_LIMIT, SWIGLU_LIMIT)
            act = (lin + 1.0) * gate * jax.nn.sigmoid(SWIGLU_ALPHA * gate)
            part = jnp.dot(act.astype(BF16), wd_buf[f0:f0 + ft, :], preferred_element_type=F32)
            if j == 0:
                acc_buf[...] = part + bd_ref[0]
            else:
                acc_buf[...] += part
        _rows_to_tiles(ys_ref, acc_buf[...])

    @pl.when(jnp.logical_not(live))
    def _():
        ys_ref[...] = jnp.zeros_like(ys_ref)


def _ffn(block_e, n_used, xs, w_gu, b_gu, w_d, b_d, rows):
    n_exp, d, f2 = w_gu.shape
    f = f2 // 2
    s = d // LANES
    n_rows = xs.shape[0] // s
    live = lambda i, nu: jnp.minimum(i, nu[0] - 1)
    wmap = lambda i, be, nu: (be[live(i, nu)], 0, 0)
    perm = jnp.asarray(_deinterleave_table(), BF16)
    return pl.pallas_call(
        functools.partial(_ffn_kernel, ft=min(f, 512)),
        grid_spec=pltpu.PrefetchScalarGridSpec(
            num_scalar_prefetch=2,
            grid=(n_rows // rows,),
            in_specs=[pl.BlockSpec((rows * s, LANES), lambda i, be, nu: (live(i, nu), 0)),
                      pl.BlockSpec((1, d, f2), wmap),
                      pl.BlockSpec((1, 1, f), wmap), pl.BlockSpec((1, 1, f), wmap),
                      pl.BlockSpec((1, f, d), wmap), pl.BlockSpec((1, 1, d), wmap),
                      pl.BlockSpec(perm.shape, lambda i, be, nu: (0, 0))],
            out_specs=pl.BlockSpec((rows * s, LANES), lambda i, be, nu: (i, 0)),
            scratch_shapes=[pltpu.VMEM((d, f), BF16), pltpu.VMEM((d, f), BF16), pltpu.VMEM((f, d), BF16),
                            pltpu.VMEM((rows, d), F32)]),
        out_shape=jax.ShapeDtypeStruct(xs.shape, F32),
        compiler_params=_cparams("arbitrary"),
        name="expert_ffn",
    )(block_e, n_used, xs, w_gu, b_gu[:, 0::2].reshape(n_exp, 1, f), b_gu[:, 1::2].reshape(n_exp, 1, f),
      w_d, b_d.reshape(n_exp, 1, d), perm)


def _combine_kernel(dest_ref, next_ref, tg_ref, x1_ref, mod_ref, fg_ref, ys_ref, o_ref, ybuf, sem, *,
                    top_k, final_norm):
    i = pl.program_id(0)
    tt = x1_ref.shape[0]
    s = ybuf.shape[2] // tt
    slot = i % 2

    def gather(d_ref, to):
        def issue(t, carry):
            dst = pl.multiple_of(t * s, s)
            for j in range(top_k):
                src = pl.multiple_of(d_ref[0, 0, t * top_k + j] * s, s)
                pltpu.make_async_copy(ys_ref.at[pl.ds(src, s)], ybuf.at[to, j, pl.ds(dst, s)], sem.at[to]).start()
            return carry
        lax.fori_loop(0, tt, issue, 0, unroll=2)

    @pl.when(i == 0)
    def _():
        gather(dest_ref, 0)

    @pl.when(i + 1 < pl.num_programs(0))
    def _():
        gather(next_ref, 1 - slot)

    for j in range(top_k):
        pltpu.make_async_copy(ys_ref.at[pl.ds(0, tt * s)], ybuf.at[slot, j], sem.at[slot]).wait()
    tg = tg_ref[...]
    moe = tg[:, 0:1] * _tiles_to_rows(ybuf.at[slot, 0], tt)
    for j in range(1, top_k):
        moe = moe + tg[:, j:j + 1] * _tiles_to_rows(ybuf.at[slot, j], tt)
    x2 = x1_ref[...] + mod_ref[0, 5:6, :] * moe
    if final_norm:
        x2 = x2 * lax.rsqrt(jnp.mean(x2 * x2, axis=-1, keepdims=True) + EPS) * fg_ref[...]
    o_ref[...] = x2


def _combine(dest, tg, x1, mods, final_g, ys, n_ctx_tiles, tiles_per_lat, tt, final_norm):
    t, d = x1.shape
    steps = t // tt
    return pl.pallas_call(
        functools.partial(_combine_kernel, top_k=TOP_K, final_norm=final_norm),
        grid=(steps,),
        in_specs=[pl.BlockSpec((1, 1, tt * TOP_K), lambda i: (i, 0, 0), memory_space=pltpu.SMEM),
                  pl.BlockSpec((1, 1, tt * TOP_K), lambda i: (jnp.minimum(i + 1, steps - 1), 0, 0),
                               memory_space=pltpu.SMEM),
                  pl.BlockSpec((tt, LANES), lambda i: (i, 0)),
                  pl.BlockSpec((tt, d), lambda i: (i, 0)),
                  pl.BlockSpec((1, 6, d), _mod_index_map(n_ctx_tiles, tiles_per_lat)),
                  pl.BlockSpec((1, d), lambda i: (0, 0)),
                  pl.BlockSpec(memory_space=pl.ANY)],
        out_specs=pl.BlockSpec((tt, d), lambda i: (i, 0)),
        out_shape=jax.ShapeDtypeStruct((t, d), F32),
        scratch_shapes=[pltpu.VMEM((2, TOP_K, tt * d // LANES, LANES), F32), pltpu.SemaphoreType.DMA((2,))],
        compiler_params=_cparams("arbitrary"),
        name="combine",
    )(dest, dest, tg, x1, mods, final_g.reshape(1, d), ys)


def _moe_plan(counts, rows, n_blocks):
    n_exp = counts.shape[0]
    padded = (counts + rows - 1) // rows * rows
    pad_end = jnp.cumsum(padded)
    block_row = jnp.arange(n_blocks, dtype=jnp.int32) * rows
    block_e = jnp.minimum(jnp.sum(pad_end[None, :] <= block_row[:, None], axis=1), n_exp - 1).astype(jnp.int32)
    n_used = (pad_end[-1:] // rows).astype(jnp.int32)
    pad_start = jnp.pad((pad_end - padded).astype(F32), (0, LANES - n_exp)).reshape(1, LANES)
    return pad_start, block_e, n_used


def _sequence_mixers(z_hy, z_qk, z_v, z_o, z_g, lw, state, row_w):
    (hy_cw, hy_cb, filt_params, hy_b, ml_cw, ml_cb, ml_gb, ml_ng) = lw
    y_hy = _hyena(z_hy, hy_cw, hy_cb, filt_params, hy_b, row_w)
    qk = _short_conv(z_qk, ml_cw, ml_cb, row_w, silu=True)
    y_ml, st = _mlstm(qk, z_v, z_o, z_g, ml_gb, ml_ng, *state)
    return y_hy, y_ml, st


def kernel(x_prompt, x_sample, state_mlstm_C, state_mlstm_n, state_mlstm_m, c, c_ctx, ada_w, ada_b, norm1_g,
           w_in, hy_conv_w, hy_conv_b, filt_w1, filt_b1, filt_w2, filt_b2, filt_w3, filt_freq, hy_bias,
           ml_conv_w, ml_conv_b, ml_gate_b, ml_norm_g, w_out, norm2_g, router_w, router_b, moe_w_gu,
           moe_b_gu, moe_w_down, moe_b_down, final_g):
    bp, lp, d = x_prompt.shape
    bs, ls, _ = x_sample.shape
    depth = ada_w.shape[0]
    heads = ml_gate_b.shape[-1]
    hy_w = hy_bias.shape[-1]
    ml_w = ml_norm_g.shape[-1]
    dh = ml_w // heads
    n_exp = router_w.shape[-1]
    t_ctx, t_lat = bp * lp, bs * ls
    t = t_ctx + t_lat
    tm = min(ROW_TILE, lp)
    tt = min(TOK_TILE, lp)
    ng = N_DIR * 2 * heads
    seg_widths = (3 * hy_w, 2 * ml_w, ml_w, ml_w)
    n_main = 3 * hy_w + 4 * ml_w
    n_blocks = -(-(t * TOP_K) // MOE_ROWS) + n_exp

    x = jnp.concatenate([x_prompt.reshape(t_ctx, d), x_sample.reshape(t_lat, d)], axis=0)
    cond = jnp.concatenate([c_ctx[None], c, jnp.zeros((SUBLANES - 1 - bs, d), F32)], axis=0)
    zero_state = (jnp.zeros((bp, N_DIR, heads, dh, dh), F32), jnp.zeros((bp, N_DIR, heads, dh), F32),
                  jnp.zeros((bp, N_DIR, heads), F32))
    new_c, new_n, new_m = [], [], []
    for l in range(depth):
        mods = _ada(cond, ada_w[l], ada_b[l]).reshape(SUBLANES, 6, d)
        w_gate = jnp.pad(w_in[l][:, n_main:], ((0, 0), (0, LANES - ng)))
        z = _inproj(x, mods, norm1_g[l], w_in[l][:, :n_main].astype(BF16), w_gate, seg_widths,
                    t_ctx // tm, ls // tm, tm)
        lw = (hy_conv_w[l], hy_conv_b[l],
              (filt_w1[l], filt_b1[l], filt_w2[l], filt_b2[l], filt_w3[l], filt_freq[l]), hy_bias[l],
              ml_conv_w[l], ml_conv_b[l], ml_gate_b[l], ml_norm_g[l])
        z_ctx = [a[:t_ctx].reshape(bp, lp, a.shape[1]) for a in z]
        z_lat = [a[t_ctx:].reshape(bs, ls, a.shape[1]) for a in z]
        cache = (state_mlstm_C[:, l], state_mlstm_n[:, l], state_mlstm_m[:, l])
        hy_c, ml_c, st = _sequence_mixers(*z_ctx, lw, zero_state, lp)
        hy_l, ml_l, _ = _sequence_mixers(*z_lat, lw, cache, GRID_W)
        new_c.append(st[0])
        new_n.append(st[1])
        new_m.append(st[2])
        y_hy = jnp.concatenate([hy_c.reshape(t_ctx, hy_w), hy_l.reshape(t_lat, hy_w)], axis=0)
        y_ml = jnp.concatenate([ml_c.reshape(t_ctx, ml_w), ml_l.reshape(t_lat, ml_w)], axis=0)
        x1, h2, ti, tg, rk, cnt = _post(y_hy, y_ml, x, mods, w_out[l], norm2_g[l], router_w[l], router_b[l],
                                        t_ctx // tm, ls // tm, tm)
        pad_start, block_e, n_used = _moe_plan(cnt[0, :n_exp].astype(jnp.int32), MOE_ROWS, n_blocks)
        dest = _dest_rows(ti, rk, pad_start, tm)[:, :TOP_K].reshape(t // tt, 1, tt * TOP_K)
        xs = _dispatch(dest, h2, n_blocks * MOE_ROWS, tt)
        ys = _ffn(block_e, n_used, xs, moe_w_gu[l], moe_b_gu[l], moe_w_down[l], moe_b_down[l], MOE_ROWS)
        x = _combine(dest, tg, x1, mods, final_g, ys, t_ctx // tt, ls // tt, tt, final_norm=l == depth - 1)
    y_prompt = x[:t_ctx].reshape(bp, lp, d)
    y_sample = x[t_ctx:].reshape(bs, ls, d)
    return (y_prompt, y_sample, jnp.stack(new_c, axis=1), jnp.stack(new_n, axis=1), jnp.stack(new_m, axis=1))
```

```python
import functools
import math

import numpy as np
import jax
import jax.numpy as jnp
from jax import lax
from jax.experimental import pallas as pl
from jax.experimental.pallas import tpu as pltpu

F32 = jnp.float32
BF16 = jnp.bfloat16
HIGHEST = lax.Precision.HIGHEST
EPS = 1e-6

LANES = 128
SUBLANES = 8
VMEM_LIMIT_BYTES = 56 * 1024 * 1024

GRID_W = 64
ML_HEADS = 4
N_DIR = 2
HY_ORDER = 2
FILT_BANDS = 8
DECAY_TARGET = 1e-2
FAST_DECAY_PCT = 0.3
SLOW_DECAY_PCT = 1.5
TOP_K = 4
SWIGLU_LIMIT = 7.0
SWIGLU_ALPHA = 1.702

FFT_N2 = 128
MLSTM_CHUNK = 128
MOE_ROWS = 512
ROW_TILE = 512
TOK_TILE = 256


def _cparams(*sem):
    return pltpu.CompilerParams(dimension_semantics=sem, vmem_limit_bytes=VMEM_LIMIT_BYTES)


def _lane_tile(c, cap):
    return max(t for t in range(LANES, min(c, cap) + 1, LANES) if c % t == 0)


def _bdot(a, b):
    return jnp.dot(a.astype(BF16), b.astype(BF16), preferred_element_type=F32)


def _hdot(a, b):
    return jnp.dot(a, b, precision=HIGHEST, preferred_element_type=F32)


def _ada_kernel(c_ref, w_ref, b_ref, o_ref):
    c = c_ref[...]
    o_ref[...] = _hdot(c * jax.nn.sigmoid(c), w_ref[...]) + b_ref[...]


def _ada(cond, w, b):
    r, d = cond.shape
    n = w.shape[1]
    tn = _lane_tile(n, 1024)
    return pl.pallas_call(
        _ada_kernel,
        grid=(n // tn,),
        in_specs=[pl.BlockSpec((r, d), lambda j: (0, 0)),
                  pl.BlockSpec((d, tn), lambda j: (0, j)),
                  pl.BlockSpec((1, tn), lambda j: (0, j))],
        out_specs=pl.BlockSpec((r, tn), lambda j: (0, j)),
        out_shape=jax.ShapeDtypeStruct((r, n), F32),
        compiler_params=_cparams("arbitrary"),
        name="ada",
    )(cond, w, b.reshape(1, n))


def _mod_index_map(n_ctx_tiles, tiles_per_lat):
    def index_map(i):
        return (jnp.where(i < n_ctx_tiles, 0, 1 + (i - n_ctx_tiles) // tiles_per_lat), 0, 0)
    return index_map


def _inproj_kernel(x_ref, mod_ref, g_ref, w_ref, wg_ref, *out_refs, offsets):
    x = x_ref[...]
    h = x * lax.rsqrt(jnp.mean(x * x, axis=-1, keepdims=True) + EPS) * g_ref[...]
    h = h * (1.0 + mod_ref[0, 1:2, :]) + mod_ref[0, 0:1, :]
    hb = h.astype(BF16)
    for o_ref, (lo, hi) in zip(out_refs[:-1], offsets):
        o_ref[...] = jnp.dot(hb, w_ref[:, lo:hi], preferred_element_type=F32)
    h_lo = (h - hb.astype(F32)).astype(BF16)
    g = jnp.dot(hb, wg_ref[...], preferred_element_type=F32)
    out_refs[-1][...] = (g[:, :LANES] + g[:, LANES:]
                         + jnp.dot(h_lo, wg_ref[:, :LANES], preferred_element_type=F32))


def _inproj(x, mods, norm_g, w_main, w_gate, seg_widths, n_ctx_tiles, tiles_per_lat, tm):
    t, d = x.shape
    offsets, lo = [], 0
    for wd in seg_widths:
        offsets.append((lo, lo + wd))
        lo += wd
    wg_hi = w_gate.astype(BF16)
    wg = jnp.concatenate([wg_hi, (w_gate - wg_hi.astype(F32)).astype(BF16)], axis=1)
    widths = tuple(seg_widths) + (LANES,)
    return pl.pallas_call(
        functools.partial(_inproj_kernel, offsets=tuple(offsets)),
        grid=(t // tm,),
        in_specs=[pl.BlockSpec((tm, d), lambda i: (i, 0)),
                  pl.BlockSpec((1, 6, d), _mod_index_map(n_ctx_tiles, tiles_per_lat)),
                  pl.BlockSpec((1, d), lambda i: (0, 0)),
                  pl.BlockSpec(w_main.shape, lambda i: (0, 0)),
                  pl.BlockSpec(wg.shape, lambda i: (0, 0))],
        out_specs=[pl.BlockSpec((tm, wd), lambda i: (i, 0)) for wd in widths],
        out_shape=[jax.ShapeDtypeStruct((t, wd), F32) for wd in widths],
        compiler_params=_cparams("arbitrary"),
        name="inproj",
    )(x, mods, norm_g.reshape(1, d), w_main, wg)


def _short_conv_kernel(x_ref, w_ref, b_ref, o_ref, *, row_w, silu, transpose):
    x = x_ref[0]
    l = x.shape[0]
    pos = lax.broadcasted_iota(jnp.int32, x.shape, 0) % row_w
    prev = jnp.where(pos == 0, 0.0, pltpu.roll(x, 1, 0))
    nxt = jnp.where(pos == row_w - 1, 0.0, pltpu.roll(x, l - 1, 0))
    y = prev * w_ref[0:1, :] + x * w_ref[1:2, :] + nxt * w_ref[2:3, :] + b_ref[...]
    if silu:
        y = y * jax.nn.sigmoid(y)
    o_ref[0] = y.T if transpose else y


def _short_conv(x, w, b, row_w, silu, col0=0, ncols=None, transpose=False):
    bsz, l, c = x.shape
    ncols = c if ncols is None else ncols
    ct = _lane_tile(math.gcd(ncols, col0) if col0 else ncols, LANES if transpose else 256)
    j0 = col0 // ct
    out_shape, out_block, out_map = (bsz, l, ncols), (1, l, ct), lambda i, j: (i, 0, j)
    if transpose:
        out_shape, out_block, out_map = (bsz, ncols, l), (1, ct, l), lambda i, j: (i, j, 0)
    return pl.pallas_call(
        functools.partial(_short_conv_kernel, row_w=row_w, silu=silu, transpose=transpose),
        grid=(bsz, ncols // ct),
        in_specs=[pl.BlockSpec((1, l, ct), lambda i, j: (i, 0, j0 + j)),
                  pl.BlockSpec((3, ct), lambda i, j: (0, j0 + j)),
                  pl.BlockSpec((1, ct), lambda i, j: (0, j0 + j))],
        out_specs=pl.BlockSpec(out_block, out_map),
        out_shape=jax.ShapeDtypeStruct(out_shape, F32),
        compiler_params=_cparams("arbitrary", "arbitrary"),
        name="short_conv",
    )(x, w, b.reshape(1, c))


def _dft_direct_tables(l):
    n = 2 * l
    k = np.arange(n)[:, None].astype(np.float64)
    t = np.arange(n)[None, :].astype(np.float64)
    ang = 2.0 * np.pi * ((k * t) % n) / n
    cm, sm = np.cos(ang), np.sin(ang)
    fwd = np.block([[cm[:, :l], sm[:, :l]], [-sm[:, :l], cm[:, :l]]])
    filt = np.concatenate([cm, -sm], axis=0)
    return fwd, filt


def _dft_two_level_tables(l, n2):
    n = 2 * l
    n1 = n // n2
    k1 = np.arange(n1)[:, None].astype(np.float64)
    a = np.arange(n1)[None, :].astype(np.float64)
    ang1 = 2.0 * np.pi * ((k1 * a) % n1) / n1
    c1, s1 = np.cos(ang1), np.sin(ang1)
    h = n1 // 2
    m1 = np.block([[c1[:, :h], s1[:, :h]], [-s1[:, :h], c1[:, :h]]])
    m1f = np.concatenate([c1, -s1], axis=0)
    kk = (np.arange(n1)[:, None, None] + n1 * np.arange(n2)[None, :, None]).astype(np.float64)
    b = np.arange(n2)[None, None, :].astype(np.float64)
    ang = 2.0 * np.pi * ((kk * b) % n) / n
    cg, sg = np.cos(ang), np.sin(ang)
    gt = np.concatenate([np.concatenate([cg, sg], axis=2),
                         np.concatenate([-sg, cg], axis=2)], axis=1)
    return m1, m1f, gt


def _circular_lag(n0, rows, l):
    n = n0 + lax.broadcasted_iota(jnp.int32, (rows, 1), 0)
    t = jnp.where(n < l, n, 2 * l - n).astype(F32)
    return n, t, t / float(max(l - 1, 1))


def _filter_hidden_kernel(bandv_ref, w1_ref, b1_ref, w2_ref, b2_ref, freq_ref, o_ref, *, l):
    rows = o_ref.shape[0]
    _, t, t01 = _circular_lag(pl.program_id(0) * rows, rows, l)
    lane = lax.broadcasted_iota(jnp.int32, (rows, LANES), 1)
    ang = (2.0 * math.pi / l) * t * bandv_ref[...]
    feats = jnp.where(lane == 0, t01,
                      jnp.where(lane <= FILT_BANDS, jnp.cos(ang),
                                jnp.where(lane <= 2 * FILT_BANDS, -jnp.sin(ang), 0.0)))
    fr = freq_ref[...]
    h = jnp.sin(fr * (_hdot(feats, w1_ref[...]) + b1_ref[...]))
    o_ref[...] = jnp.sin(fr * (_hdot(h, w2_ref[...]) + b2_ref[...]))


def _filter_hidden(l, f_w1, f_b1, f_w2, f_b2, f_freq):
    emb, hid = f_w1.shape
    n = 2 * l
    rows = min(n, 512)
    bands = jnp.linspace(1e-4, FILT_BANDS - 1, FILT_BANDS, dtype=F32)
    bandv = jnp.zeros((1, LANES), F32).at[0, 1:1 + FILT_BANDS].set(bands)
    bandv = bandv.at[0, 1 + FILT_BANDS:1 + 2 * FILT_BANDS].set(bands)
    w1p = jnp.zeros((LANES, hid), F32).at[:emb].set(f_w1)
    c0 = lambda i: (0, 0)
    return pl.pallas_call(
        functools.partial(_filter_hidden_kernel, l=l),
        grid=(n // rows,),
        in_specs=[pl.BlockSpec((1, LANES), c0), pl.BlockSpec((LANES, hid), c0), pl.BlockSpec((1, hid), c0),
                  pl.BlockSpec((hid, hid), c0), pl.BlockSpec((1, hid), c0), pl.BlockSpec((1, hid), c0)],
        out_specs=pl.BlockSpec((rows, hid), lambda i: (i, 0)),
        out_shape=jax.ShapeDtypeStruct((n, hid), F32),
        compiler_params=_cparams("arbitrary"),
        name="filter_hidden",
    )(bandv, w1p, f_b1.reshape(1, hid), f_w2, f_b2.reshape(1, hid), f_freq.reshape(1, hid))


def _filter_rows(n0, rows, l, hid_ref, w3_ref, delta_ref):
    n, _, t01 = _circular_lag(n0, rows, l)
    h = hid_ref[pl.ds(n0, rows), :]
    hf = _bdot(h, w3_ref[0, 0])
    hb = _bdot(h, w3_ref[0, 1])
    window = jnp.exp(-t01 * delta_ref[...])
    return jnp.where(n < l, hf, jnp.where(n > l, hb, 0.0)) * window


def _filter_direct_kernel(hid_ref, w3_ref, delta_ref, ff_ref, h_ref, *, l):
    hc = _filter_rows(0, 2 * l, l, hid_ref, w3_ref, delta_ref)
    h_ref[0] = _bdot(ff_ref[...], hc) * (1.0 / (2 * l))


def _filter_two_level_kernel(hid_ref, w3_ref, delta_ref, m1f_ref, gt_ref, h_ref, hc_buf, a_buf, *, l, n2, rows):
    n = 2 * l
    n1 = n // n2

    def fill(i, carry):
        r0 = pl.multiple_of(i * rows, rows)
        hc_buf[pl.ds(r0, rows), :] = _filter_rows(r0, rows, l, hid_ref, w3_ref, delta_ref)
        return carry
    lax.fori_loop(0, n // rows, fill, 0)

    def step1(b, carry):
        col = hc_buf[pl.ds(b, n1, stride=n2), :]
        a = _bdot(m1f_ref[...], col)
        a_buf[pl.ds(b, n1, stride=2 * n2), :] = a[:n1]
        a_buf[pl.ds(n2 + b, n1, stride=2 * n2), :] = a[n1:]
        return carry
    lax.fori_loop(0, n2, step1, 0)

    def step2(k1, carry):
        r0 = pl.multiple_of(k1 * 2 * n2, 2 * n2)
        h_ref[0, k1] = (_bdot(gt_ref[k1], a_buf[pl.ds(r0, 2 * n2), :]) * (1.0 / n)).astype(h_ref.dtype)
        return carry
    lax.fori_loop(0, n1, step2, 0)


def _filter_tail_inputs(hy_w, f_w3):
    hid = f_w3.shape[0]
    w3 = f_w3.reshape(hid, HY_ORDER, N_DIR, hy_w).transpose(1, 2, 0, 3)
    max_decay = math.log(DECAY_TARGET) / FAST_DECAY_PCT
    min_decay = math.log(DECAY_TARGET) / SLOW_DECAY_PCT
    deltas = jnp.abs(jnp.linspace(min_decay, max_decay, hy_w, dtype=F32)).reshape(1, hy_w)
    return w3, deltas


def _filter_spectrum_direct(l, hy_w, filt_params, ff):
    f_w1, f_b1, f_w2, f_b2, f_w3, f_freq = filt_params
    hidden = _filter_hidden(l, f_w1, f_b1, f_w2, f_b2, f_freq)
    w3, deltas = _filter_tail_inputs(hy_w, f_w3)
    hid = f_w3.shape[0]
    n = 2 * l
    return pl.pallas_call(
        functools.partial(_filter_direct_kernel, l=l),
        grid=(HY_ORDER,),
        in_specs=[pl.BlockSpec((n, hid), lambda o: (0, 0)),
                  pl.BlockSpec((1, N_DIR, hid, hy_w), lambda o: (o, 0, 0, 0)),
                  pl.BlockSpec((1, hy_w), lambda o: (0, 0)),
                  pl.BlockSpec((2 * n, n), lambda o: (0, 0))],
        out_specs=pl.BlockSpec((1, 2 * n, hy_w), lambda o: (o, 0, 0)),
        out_shape=jax.ShapeDtypeStruct((HY_ORDER, 2 * n, hy_w), F32),
        compiler_params=_cparams("arbitrary"),
        name="filter_direct",
    )(hidden, w3, deltas, ff)


def _filter_spectrum_two_level(l, hy_w, filt_params, m1f, gt, ct):
    f_w1, f_b1, f_w2, f_b2, f_w3, f_freq = filt_params
    hidden = _filter_hidden(l, f_w1, f_b1, f_w2, f_b2, f_freq)
    w3, deltas = _filter_tail_inputs(hy_w, f_w3)
    hid = f_w3.shape[0]
    n = 2 * l
    n2 = FFT_N2
    n1 = n // n2
    return pl.pallas_call(
        functools.partial(_filter_two_level_kernel, l=l, n2=n2, rows=min(n, 512)),
        grid=(HY_ORDER, hy_w // ct),
        in_specs=[pl.BlockSpec((n, hid), lambda o, j: (0, 0)),
                  pl.BlockSpec((1, N_DIR, hid, ct), lambda o, j: (o, 0, 0, j)),
                  pl.BlockSpec((1, ct), lambda o, j: (0, j)),
                  pl.BlockSpec((2 * n1, n1), lambda o, j: (0, 0)),
                  pl.BlockSpec((n1, 2 * n2, 2 * n2), lambda o, j: (0, 0, 0))],
        out_specs=pl.BlockSpec((1, n1, 2 * n2, ct), lambda o, j: (o, 0, 0, j)),
        out_shape=jax.ShapeDtypeStruct((HY_ORDER, n1, 2 * n2, hy_w), BF16),
        scratch_shapes=[pltpu.VMEM((n, ct), F32), pltpu.VMEM((n1 * 2 * n2, ct), F32)],
        compiler_params=_cparams("arbitrary", "arbitrary"),
        name="filter_two_level",
    )(hidden, w3, deltas, m1f, gt)


def _complex_mul(x, h, half):
    xr, xi = x[:half], x[half:]
    hr, hi = h[:half], h[half:]
    return jnp.concatenate([xr * hr - xi * hi, xr * hi + xi * hr], axis=0)


def _conv_direct_kernel(z_ref, gate_ref, bias_ref, h_ref, fwd_ref, inv_ref, o_ref):
    l = z_ref.shape[1]
    z = jnp.concatenate([z_ref[0], z_ref[1]], axis=0)
    x = _bdot(fwd_ref[...], z)
    y = _bdot(inv_ref[...], _complex_mul(x, h_ref[0], 2 * l))
    bias = bias_ref[0]
    o_ref[0] = gate_ref[0] * (y[:l] + bias * z_ref[0])
    o_ref[1] = gate_ref[1] * (y[l:] + bias * z_ref[1])


def _conv_two_level_kernel(z_ref, gate_ref, bias_ref, h_ref, m1_ref, m1i_ref, gt_ref, o_ref,
                           a_buf, *, n2):
    l = z_ref.shape[1]
    n1 = 2 * l // n2
    hn = n1 // 2

    def step1(b, carry):
        za = z_ref[0, pl.ds(b, hn, stride=n2), :]
        zb = z_ref[1, pl.ds(b, hn, stride=n2), :]
        a = _bdot(m1_ref[...], jnp.concatenate([za, zb], axis=0))
        a_buf[pl.ds(b, n1, stride=2 * n2), :] = a[:n1]
        a_buf[pl.ds(n2 + b, n1, stride=2 * n2), :] = a[n1:]
        return carry
    lax.fori_loop(0, n2, step1, 0)

    def step2(k1, carry):
        r0 = pl.multiple_of(k1 * 2 * n2, 2 * n2)
        x = _bdot(gt_ref[k1], a_buf[pl.ds(r0, 2 * n2), :])
        y = _complex_mul(x, h_ref[0, k1].astype(F32), n2).astype(BF16)
        a_buf[pl.ds(r0, 2 * n2), :] = lax.dot_general(gt_ref[k1], y, (((0,), (0,)), ((), ())),
                                                      preferred_element_type=F32)
        return carry
    lax.fori_loop(0, n1, step2, 0)

    def step3(b, carry):
        br = a_buf[pl.ds(b, n1, stride=2 * n2), :]
        bi = a_buf[pl.ds(n2 + b, n1, stride=2 * n2), :]
        y = _bdot(m1i_ref[...], jnp.concatenate([br, bi], axis=0))
        o_ref[0, pl.ds(b, hn, stride=n2), :] = y[:hn]
        o_ref[1, pl.ds(b, hn, stride=n2), :] = y[hn:]
        return carry
    lax.fori_loop(0, n2, step3, 0)

    bias = bias_ref[0]
    for s in range(2):
        o_ref[s] = gate_ref[s] * (o_ref[s] + bias * z_ref[s])


def _long_conv_gated(u, z, z_col, gate_col, spectrum, order, bias, tables, ct):
    bsz, l, _ = u.shape
    c = spectrum.shape[-1]
    nct = c // ct
    zspec = pl.BlockSpec((2, l, ct), lambda i, j: (i, 0, z_col * nct + j))
    gspec = pl.BlockSpec((2, l, ct), lambda i, j: (i, 0, gate_col * nct + j))
    bspec = pl.BlockSpec((1, 1, ct), lambda i, j: (order, 0, j))
    ospec = pl.BlockSpec((2, l, ct), lambda i, j: (i, 0, j))
    out_shape = jax.ShapeDtypeStruct((bsz, l, c), F32)
    bias3 = bias.reshape(HY_ORDER, 1, c)
    if len(tables) == 2:
        fwd, inv = tables
        n = 2 * l
        return pl.pallas_call(
            _conv_direct_kernel,
            grid=(bsz // 2, nct),
            in_specs=[zspec, gspec, bspec,
                      pl.BlockSpec((1, 2 * n, ct), lambda i, j: (order, 0, j)),
                      pl.BlockSpec(fwd.shape, lambda i, j: (0, 0)),
                      pl.BlockSpec(inv.shape, lambda i, j: (0, 0))],
            out_specs=ospec, out_shape=out_shape,
            compiler_params=_cparams("arbitrary", "arbitrary"),
            name="long_conv_direct",
        )(z, u, bias3, spectrum, fwd, inv)
    m1, m1i, gt = tables
    n2 = FFT_N2
    n1 = 2 * l // n2
    const2 = lambda i, j: (0, 0)
    const3 = lambda i, j: (0, 0, 0)
    return pl.pallas_call(
        functools.partial(_conv_two_level_kernel, n2=n2),
        grid=(nct, bsz // 2),
        in_specs=[pl.BlockSpec((2, l, ct), lambda j, i: (i, 0, z_col * nct + j)),
                  pl.BlockSpec((2, l, ct), lambda j, i: (i, 0, gate_col * nct + j)),
                  pl.BlockSpec((1, 1, ct), lambda j, i: (order, 0, j)),
                  pl.BlockSpec((1, n1, 2 * n2, ct), lambda j, i: (order, 0, 0, j)),
                  pl.BlockSpec(m1.shape, const2), pl.BlockSpec(m1i.shape, const2),
                  pl.BlockSpec(gt.shape, const3)],
        out_specs=pl.BlockSpec((2, l, ct), lambda j, i: (i, 0, j)),
        out_shape=out_shape,
        scratch_shapes=[pltpu.VMEM((n1 * 2 * n2, ct), F32)],
        compiler_params=_cparams("arbitrary", "arbitrary"),
        name="long_conv_two_level",
    )(z, u, bias3, spectrum, m1, m1i, gt)


def _hyena(z_hy, conv_w, conv_b, filt_params, hy_bias, row_w):
    bsz, l, c3 = z_hy.shape
    c = c3 // 3
    u = _short_conv(z_hy, conv_w, conv_b, row_w, silu=False)
    table = lambda a: jnp.asarray(a, F32).astype(BF16)
    if 2 * l // FFT_N2 <= 4:
        fwd, filt = _dft_direct_tables(l)
        tables = (table(fwd), table(fwd.T))
        spectrum = _filter_spectrum_direct(l, c, filt_params, table(filt))
        ct = c
    else:
        m1, m1f, gt = _dft_two_level_tables(l, FFT_N2)
        tables = (table(m1), table(m1.T), table(gt))
        ct = LANES
        spectrum = _filter_spectrum_two_level(l, c, filt_params, table(m1f), tables[2], ct)
    z1 = _long_conv_gated(u, u, 0, 1, spectrum, 0, hy_bias, tables, ct)
    return _long_conv_gated(u, z1, 0, 2, spectrum, 1, hy_bias, tables, ct)


def _log_sigmoid(x):
    return jnp.minimum(x, 0.0) - jnp.log1p(jnp.exp(-jnp.abs(x)))


def _split3_dot(a, b, split_lhs):
    x = a if split_lhs else b
    hi = x.astype(BF16)
    rest = x - hi.astype(F32)
    mid = rest.astype(BF16)
    parts = (hi, mid, (rest - mid.astype(F32)).astype(BF16))
    if split_lhs:
        return functools.reduce(lambda u, w: u + w, [jnp.dot(p, b, preferred_element_type=F32) for p in parts])
    return functools.reduce(lambda u, w: u + w, [jnp.dot(a, p, preferred_element_type=F32) for p in parts])


def _mlstm_chunk(qb, kt, vo, bq, irow, brow, btot, mask, cn, m):
    dh = qb.shape[1]
    dm = jnp.where(mask, bq - brow + irow, -jnp.inf)
    m_loc = jnp.max(dm, axis=-1, keepdims=True)
    s = jnp.dot(qb, kt.astype(BF16), preferred_element_type=F32) * jnp.exp(dm - m_loc)
    sv = jnp.dot(s.astype(BF16), vo, preferred_element_type=F32)
    gl = btot - brow + irow
    g_loc = jnp.max(gl, axis=-1, keepdims=True)
    kv = jnp.dot((kt * jnp.exp(gl - g_loc)).astype(BF16), vo, preferred_element_type=F32)
    inter = bq + m
    mj = jnp.maximum(m_loc, inter)
    w_int = jnp.exp(inter - mj)
    w_loc = jnp.exp(m_loc - mj)
    qc = jnp.dot(qb, cn.astype(BF16), preferred_element_type=F32)
    num = w_int * qc[:, :dh] + w_loc * sv[:, :dh]
    den = w_int * qc[:, dh:] + w_loc * sv[:, dh:]
    h = num / jnp.maximum(jnp.abs(den), jnp.exp(-mj))
    m_new = jnp.maximum(btot + m, g_loc)
    cn_new = jnp.exp(btot + m - m_new) * cn + jnp.exp(g_loc - m_new) * kv
    return h, cn_new, m_new


def _mlstm_kernel(q_ref, kt_ref, v_ref, o_ref, g_ref, gt_ref, gb_ref, gbt_ref, ng_ref, c0_ref, n0_ref, m0_ref,
                  y_ref, c_ref, n_ref, m_ref, hf_buf, hb_buf, cn_buf, *, chunk):
    l, dh = q_ref.shape[1], q_ref.shape[2]
    nc = l // chunk
    scale = dh ** -0.5
    row = lax.broadcasted_iota(jnp.int32, (chunk, chunk), 0)
    col = lax.broadcasted_iota(jnp.int32, (chunk, chunk), 1)
    lower, upper = col <= row, col >= row
    tri_l, tri_u = lower.astype(BF16), upper.astype(BF16)
    gate_row = lax.broadcasted_iota(jnp.int32, (SUBLANES, chunk), 0)
    ones = jnp.ones((chunk, dh), BF16)

    def direction(r0, d, cn, m):
        lf = _log_sigmoid(g_ref[0, 0, pl.ds(r0, chunk), :] + gb_ref[0])[:, 2 * d + 1:2 * d + 2]
        bq = _split3_dot(tri_u if d else tri_l, jnp.broadcast_to(lf, (chunk, LANES)), split_lhs=False)
        gt = gt_ref[0, 0, :, pl.ds(r0, chunk)] + gbt_ref[0]
        gt = jnp.where(gate_row % 2 == 1, _log_sigmoid(gt), gt)
        brow = _split3_dot(gt, tri_l if d else tri_u, split_lhs=True)[2 * d + 1:2 * d + 2, :]
        btot = brow[:, 0:1] if d else brow[:, chunk - 1:chunk]
        vo = jnp.concatenate([v_ref[0, pl.ds(r0, chunk), :].astype(BF16), ones], axis=1)
        return _mlstm_chunk(q_ref[0, pl.ds(r0, chunk), :].astype(BF16), kt_ref[0, :, pl.ds(r0, chunk)] * scale,
                            vo, bq, gt[2 * d:2 * d + 1, :], brow, btot, upper if d else lower, cn, m)

    for d in range(N_DIR):
        n_rep = jnp.broadcast_to(n0_ref[0, 0, d:d + 1, :], (dh, dh)).T
        cn_buf[d] = jnp.concatenate([c0_ref[0, d, 0], n_rep], axis=1)

    def body(j, carry):
        mf, mb = carry
        rf = pl.multiple_of(j * chunk, chunk)
        hf, cn, mf = direction(rf, 0, cn_buf[0], mf)
        cn_buf[0] = cn
        hf_buf[pl.ds(rf, chunk), :] = hf
        rb = pl.multiple_of((nc - 1 - j) * chunk, chunk)
        hb, cn, mb = direction(rb, 1, cn_buf[1], mb)
        cn_buf[1] = cn
        hb_buf[pl.ds(rb, chunk), :] = hb
        return mf, mb

    m_fin = lax.fori_loop(0, nc, body, (m0_ref[0, 0, 0:1, 0:1], m0_ref[0, 0, 1:2, 0:1]), unroll=2)
    for d in range(N_DIR):
        c_ref[0, d, 0] = cn_buf[d, :, :dh]
        n_ref[0, 0, d:d + 1, :] = cn_buf[d, :, dh:].T[0:1, :]
        m_ref[0, 0, d:d + 1, :] = jnp.broadcast_to(m_fin[d], (1, LANES))

    def finish(j, carry):
        r0 = pl.multiple_of(j * chunk, chunk)
        hs = hf_buf[pl.ds(r0, chunk), :] + hb_buf[pl.ds(r0, chunk), :]
        hs = hs * lax.rsqrt(jnp.mean(hs * hs, axis=-1, keepdims=True) + EPS) * ng_ref[...]
        y_ref[0, pl.ds(r0, chunk), :] = jax.nn.sigmoid(o_ref[0, pl.ds(r0, chunk), :]) * hs
        return carry
    lax.fori_loop(0, nc, finish, 0)


def _mlstm(q, kt, z_v, z_o, z_g, gate_b, norm_g, c0, n0, m0):
    bsz, l, ml = z_v.shape
    heads = gate_b.shape[-1]
    dh = ml // heads
    chunk = MLSTM_CHUNK
    assert dh == LANES and chunk == LANES and l % chunk == 0
    n_gate = N_DIR * 2
    g4 = z_g[..., :n_gate * heads].reshape(bsz, l, n_gate, heads)
    gh = jnp.pad(g4.transpose(0, 3, 1, 2), ((0, 0), (0, 0), (0, 0), (0, LANES - n_gate)))
    ght = jnp.pad(g4.transpose(0, 3, 2, 1), ((0, 0), (0, 0), (0, SUBLANES - n_gate), (0, 0)))
    gb4 = gate_b.reshape(n_gate, heads).T
    gb = jnp.pad(gb4, ((0, 0), (0, LANES - n_gate))).reshape(heads, 1, LANES)
    gbt = jnp.broadcast_to(jnp.pad(gb4, ((0, 0), (0, SUBLANES - n_gate)))[..., None], (heads, SUBLANES, LANES))
    n0h = n0.transpose(0, 2, 1, 3)
    m0h = jnp.broadcast_to(m0.transpose(0, 2, 1)[..., None], (bsz, heads, N_DIR, LANES))
    seq = pl.BlockSpec((1, l, dh), lambda b, h: (b, 0, h))
    cspec = pl.BlockSpec((1, N_DIR, 1, dh, dh), lambda b, h: (b, 0, h, 0, 0))
    sspec = pl.BlockSpec((1, 1, N_DIR, dh), lambda b, h: (b, h, 0, 0))
    mspec = pl.BlockSpec((1, 1, N_DIR, LANES), lambda b, h: (b, h, 0, 0))
    y, c, n, m = pl.pallas_call(
        functools.partial(_mlstm_kernel, chunk=chunk),
        grid=(bsz, heads),
        in_specs=[seq, pl.BlockSpec((1, dh, l), lambda b, h: (b, h, 0)), seq, seq,
                  pl.BlockSpec((1, 1, l, LANES), lambda b, h: (b, h, 0, 0)),
                  pl.BlockSpec((1, 1, SUBLANES, l), lambda b, h: (b, h, 0, 0)),
                  pl.BlockSpec((1, 1, LANES), lambda b, h: (h, 0, 0)),
                  pl.BlockSpec((1, SUBLANES, LANES), lambda b, h: (h, 0, 0)),
                  pl.BlockSpec((1, dh), lambda b, h: (0, h)),
                  cspec, sspec, mspec],
        out_specs=[seq, cspec, sspec, mspec],
        out_shape=[jax.ShapeDtypeStruct((bsz, l, ml), F32),
                   jax.ShapeDtypeStruct((bsz, N_DIR, heads, dh, dh), F32),
                   jax.ShapeDtypeStruct((bsz, heads, N_DIR, dh), F32),
                   jax.ShapeDtypeStruct((bsz, heads, N_DIR, LANES), F32)],
        scratch_shapes=[pltpu.VMEM((l, dh), F32), pltpu.VMEM((l, dh), F32),
                        pltpu.VMEM((N_DIR, dh, 2 * dh), F32)],
        compiler_params=_cparams("arbitrary", "arbitrary"),
        name="mlstm",
    )(q, kt, z_v, z_o, gh, ght, gb, gbt, norm_g.reshape(1, ml), c0, n0h, m0h)
    return y, (c, n.transpose(0, 2, 1, 3), m[..., 0].transpose(0, 2, 1))


def _rows_to_tiles(tile_ref, x):
    r, d = x.shape
    s = d // LANES
    for k in range(s):
        tile_ref[pl.ds(k, r, stride=s), :] = x[:, LANES * k:LANES * (k + 1)]


def _tiles_to_rows(tile_ref, r):
    s = tile_ref.shape[0] // r
    return jnp.concatenate([tile_ref[pl.ds(k, r, stride=s), :] for k in range(s)], axis=1)


def _lane_pack(cols, lane):
    out = jnp.zeros(lane.shape, cols[0].dtype)
    for j, colv in enumerate(cols):
        out = jnp.where(lane == j, colv, out)
    return out


def _post_kernel(yhy_ref, yml_ref, x_ref, mod_ref, wo_ref, n2g_ref, rw_ref, rb_ref,
                 x1_ref, h2_ref, ti_ref, tg_ref, rk_ref, cnt_ref, carry, *, n_exp, top_k):
    @pl.when(pl.program_id(0) == 0)
    def _():
        carry[...] = jnp.zeros_like(carry)

    hy_w = yhy_ref.shape[1]
    tm = x_ref.shape[0]
    proj = _bdot(yhy_ref[...], wo_ref[:hy_w, :]) + _bdot(yml_ref[...], wo_ref[hy_w:, :])
    x1 = x_ref[...] + mod_ref[0, 2:3, :] * proj
    x1_ref[...] = x1
    h2 = x1 * lax.rsqrt(jnp.mean(x1 * x1, axis=-1, keepdims=True) + EPS) * n2g_ref[...]
    h2 = h2 * (1.0 + mod_ref[0, 4:5, :]) + mod_ref[0, 3:4, :]
    _rows_to_tiles(h2_ref, h2)

    lane = lax.broadcasted_iota(jnp.int32, (tm, LANES), 1)
    work = jnp.where(lane < n_exp, _hdot(h2, rw_ref[...]) + rb_ref[...], -jnp.inf)
    vals, idxs, hots = [], [], []
    for _ in range(top_k):
        mx = jnp.max(work, axis=-1, keepdims=True)
        idx = jnp.min(jnp.where(work == mx, lane, LANES), axis=-1, keepdims=True)
        hot = lane == idx
        vals.append(mx)
        idxs.append(idx)
        hots.append(hot)
        work = jnp.where(hot, -jnp.inf, work)
    exps = [jnp.exp(v - vals[0]) for v in vals]
    tot = functools.reduce(lambda a, b: a + b, exps)
    ti_ref[...] = _lane_pack(idxs, lane)
    tg_ref[...] = _lane_pack([e / tot for e in exps], lane)

    hot_sum = functools.reduce(lambda a, b: a + b, [h.astype(F32) for h in hots])
    row = lax.broadcasted_iota(jnp.int32, (tm, tm), 0)
    col = lax.broadcasted_iota(jnp.int32, (tm, tm), 1)
    before = _bdot((col < row).astype(F32), hot_sum) + carry[...]
    ranks = [jnp.sum(jnp.where(h, before, 0.0), axis=-1, keepdims=True).astype(jnp.int32) for h in hots]
    rk_ref[...] = _lane_pack(ranks, lane)
    carry[...] = carry[...] + jnp.sum(hot_sum, axis=0, keepdims=True)
    cnt_ref[...] = jnp.broadcast_to(carry[...], cnt_ref.shape)


def _post(y_hy, y_ml, x, mods, w_out, norm2_g, r_w, r_b, n_ctx_tiles, tiles_per_lat, tm):
    t, d = x.shape
    n_exp = r_w.shape[1]
    rwp = jnp.pad(r_w, ((0, 0), (0, LANES - n_exp)))
    rbp = jnp.pad(r_b, (0, LANES - n_exp)).reshape(1, LANES)
    row = lambda wd: pl.BlockSpec((tm, wd), lambda i: (i, 0))
    const = lambda shape: pl.BlockSpec(shape, lambda i: (0, 0))
    return pl.pallas_call(
        functools.partial(_post_kernel, n_exp=n_exp, top_k=TOP_K),
        grid=(t // tm,),
        in_specs=[row(y_hy.shape[1]), row(y_ml.shape[1]), row(d),
                  pl.BlockSpec((1, 6, d), _mod_index_map(n_ctx_tiles, tiles_per_lat)),
                  const(w_out.shape), const((1, d)), const((d, LANES)), const((1, LANES))],
        out_specs=[row(d), pl.BlockSpec((tm * d // LANES, LANES), lambda i: (i, 0)),
                   row(LANES), row(LANES), row(LANES), const((SUBLANES, LANES))],
        out_shape=[jax.ShapeDtypeStruct((t, d), F32), jax.ShapeDtypeStruct((t * d // LANES, LANES), F32),
                   jax.ShapeDtypeStruct((t, LANES), jnp.int32), jax.ShapeDtypeStruct((t, LANES), F32),
                   jax.ShapeDtypeStruct((t, LANES), jnp.int32), jax.ShapeDtypeStruct((SUBLANES, LANES), F32)],
        scratch_shapes=[pltpu.VMEM((1, LANES), F32)],
        compiler_params=_cparams("arbitrary"),
        name="post",
    )(y_hy, y_ml, x, mods, w_out.astype(BF16), norm2_g.reshape(1, d), rwp, rbp)


def _dest_kernel(ti_ref, rk_ref, ps_ref, d_ref, *, top_k):
    lane = lax.broadcasted_iota(jnp.int32, ti_ref.shape, 1)
    ti, rk = ti_ref[...], rk_ref[...]
    cols = []
    for j in range(top_k):
        start = jnp.sum(jnp.where(lane == ti[:, j:j + 1], ps_ref[...], 0.0), axis=-1, keepdims=True)
        cols.append(start.astype(jnp.int32) + rk[:, j:j + 1])
    d_ref[...] = _lane_pack(cols, lane)


def _dest_rows(ti, rk, pad_start, tm):
    t = ti.shape[0]
    row = pl.BlockSpec((tm, LANES), lambda i: (i, 0))
    return pl.pallas_call(
        functools.partial(_dest_kernel, top_k=TOP_K),
        grid=(t // tm,),
        in_specs=[row, row, pl.BlockSpec((1, LANES), lambda i: (0, 0))],
        out_specs=row,
        out_shape=jax.ShapeDtypeStruct((t, LANES), jnp.int32),
        compiler_params=_cparams("arbitrary"),
        name="dest_rows",
    )(ti, rk, pad_start)


def _dispatch_kernel(dest_ref, h_ref, zero_ref, xs_ref, sem, *, top_k, s):
    del zero_ref
    tt = dest_ref.shape[2] // top_k

    def issue(t, carry):
        src = pl.multiple_of(t * s, s)
        for j in range(top_k):
            dst = pl.multiple_of(dest_ref[0, 0, t * top_k + j] * s, s)
            pltpu.make_async_copy(h_ref.at[pl.ds(src, s)], xs_ref.at[pl.ds(dst, s)], sem).start()
        return carry
    lax.fori_loop(0, tt, issue, 0, unroll=2)
    for _ in range(top_k):
        pltpu.make_async_copy(h_ref, xs_ref.at[pl.ds(0, tt * s)], sem).wait()


def _dispatch(dest, h2t, n_rows, tt):
    s = h2t.shape[0] * TOP_K // dest.size
    zeros = jnp.zeros((n_rows * s, LANES), F32)
    return pl.pallas_call(
        functools.partial(_dispatch_kernel, top_k=TOP_K, s=s),
        grid=(dest.shape[0],),
        in_specs=[pl.BlockSpec((1, 1, tt * TOP_K), lambda i: (i, 0, 0), memory_space=pltpu.SMEM),
                  pl.BlockSpec((tt * s, LANES), lambda i: (i, 0)),
                  pl.BlockSpec(memory_space=pl.ANY)],
        out_specs=pl.BlockSpec(memory_space=pl.ANY),
        out_shape=jax.ShapeDtypeStruct((n_rows * s, LANES), F32),
        scratch_shapes=[pltpu.SemaphoreType.DMA(())],
        input_output_aliases={2: 0},
        compiler_params=_cparams("arbitrary"),
        name="dispatch",
    )(dest, h2t, zeros)


def _deinterleave_table():
    p = np.zeros((2 * LANES, 2 * LANES), np.float32)
    j = np.arange(LANES)
    p[2 * j, j] = 1.0
    p[2 * j + 1, LANES + j] = 1.0
    return p


def _ffn_kernel(be_ref, nu_ref, xs_ref, wgu_ref, bg_ref, bl_ref, wd_ref, bd_ref, perm_ref, ys_ref,
                wg_buf, wl_buf, wd_buf, acc_buf, *, ft):
    i = pl.program_id(0)
    f = wg_buf.shape[1]
    rows = acc_buf.shape[0]
    live = i < nu_ref[0]

    @pl.when(live & ((i == 0) | (be_ref[i] != be_ref[jnp.maximum(i - 1, 0)])))
    def _():
        for j in range(f // LANES):
            blk = wgu_ref[0, :, 2 * LANES * j:2 * LANES * (j + 1)].astype(BF16)
            split = jnp.dot(blk, perm_ref[...], preferred_element_type=F32)
            wg_buf[:, LANES * j:LANES * (j + 1)] = split[:, :LANES].astype(BF16)
            wl_buf[:, LANES * j:LANES * (j + 1)] = split[:, LANES:].astype(BF16)
        wd_buf[...] = wd_ref[0].astype(BF16)

    @pl.when(live)
    def _():
        x = _tiles_to_rows(xs_ref, rows).astype(BF16)
        for j, f0 in enumerate(range(0, f, ft)):
            g = jnp.dot(x, wg_buf[:, f0:f0 + ft], preferred_element_type=F32) + bg_ref[0, :, f0:f0 + ft]
            lin = jnp.dot(x, wl_buf[:, f0:f0 + ft], preferred_element_type=F32) + bl_ref[0, :, f0:f0 + ft]
            gate = jnp.minimum(g, SWIGLU_LIMIT)
            lin = jnp.clip(lin, -SWIGLU_LIMIT, SWIGLU_LIMIT)
            act = (lin + 1.0) * gate * jax.nn.sigmoid(SWIGLU_ALPHA * gate)
            part = jnp.dot(act.astype(BF16), wd_buf[f0:f0 + ft, :], preferred_element_type=F32)
            if j == 0:
                acc_buf[...] = part + bd_ref[0]
            else:
                acc_buf[...] += part
        _rows_to_tiles(ys_ref, acc_buf[...])

    @pl.when(jnp.logical_not(live))
    def _():
        ys_ref[...] = jnp.zeros_like(ys_ref)


def _ffn(block_e, n_used, xs, w_gu, b_gu, w_d, b_d, rows):
    n_exp, d, f2 = w_gu.shape
    f = f2 // 2
    s = d // LANES
    n_rows = xs.shape[0] // s
    live = lambda i, nu: jnp.minimum(i, nu[0] - 1)
    wmap = lambda i, be, nu: (be[live(i, nu)], 0, 0)
    perm = jnp.asarray(_deinterleave_table(), BF16)
    return pl.pallas_call(
        functools.partial(_ffn_kernel, ft=min(f, 512)),
        grid_spec=pltpu.PrefetchScalarGridSpec(
            num_scalar_prefetch=2,
            grid=(n_rows // rows,),
            in_specs=[pl.BlockSpec((rows * s, LANES), lambda i, be, nu: (live(i, nu), 0)),
                      pl.BlockSpec((1, d, f2), wmap),
                      pl.BlockSpec((1, 1, f), wmap), pl.BlockSpec((1, 1, f), wmap),
                      pl.BlockSpec((1, f, d), wmap), pl.BlockSpec((1, 1, d), wmap),
                      pl.BlockSpec(perm.shape, lambda i, be, nu: (0, 0))],
            out_specs=pl.BlockSpec((rows * s, LANES), lambda i, be, nu: (i, 0)),
            scratch_shapes=[pltpu.VMEM((d, f), BF16), pltpu.VMEM((d, f), BF16), pltpu.VMEM((f, d), BF16),
                            pltpu.VMEM((rows, d), F32)]),
        out_shape=jax.ShapeDtypeStruct(xs.shape, F32),
        compiler_params=_cparams("arbitrary"),
        name="expert_ffn",
    )(block_e, n_used, xs, w_gu, b_gu[:, 0::2].reshape(n_exp, 1, f), b_gu[:, 1::2].reshape(n_exp, 1, f),
      w_d, b_d.reshape(n_exp, 1, d), perm)


def _combine_kernel(dest_ref, next_ref, tg_ref, x1_ref, mod_ref, fg_ref, ys_ref, o_ref, ybuf, sem, *,
                    top_k, final_norm):
    i = pl.program_id(0)
    tt = x1_ref.shape[0]
    s = ybuf.shape[2] // tt
    slot = i % 2

    def gather(d_ref, to):
        def issue(t, carry):
            dst = pl.multiple_of(t * s, s)
            for j in range(top_k):
                src = pl.multiple_of(d_ref[0, 0, t * top_k + j] * s, s)
                pltpu.make_async_copy(ys_ref.at[pl.ds(src, s)], ybuf.at[to, j, pl.ds(dst, s)], sem.at[to]).start()
            return carry
        lax.fori_loop(0, tt, issue, 0, unroll=2)

    @pl.when(i == 0)
    def _():
        gather(dest_ref, 0)

    @pl.when(i + 1 < pl.num_programs(0))
    def _():
        gather(next_ref, 1 - slot)

    for j in range(top_k):
        pltpu.make_async_copy(ys_ref.at[pl.ds(0, tt * s)], ybuf.at[slot, j], sem.at[slot]).wait()
    tg = tg_ref[...]
    moe = tg[:, 0:1] * _tiles_to_rows(ybuf.at[slot, 0], tt)
    for j in range(1, top_k):
        moe = moe + tg[:, j:j + 1] * _tiles_to_rows(ybuf.at[slot, j], tt)
    x2 = x1_ref[...] + mod_ref[0, 5:6, :] * moe
    if final_norm:
        x2 = x2 * lax.rsqrt(jnp.mean(x2 * x2, axis=-1, keepdims=True) + EPS) * fg_ref[...]
    o_ref[...] = x2


def _combine(dest, tg, x1, mods, final_g, ys, n_ctx_tiles, tiles_per_lat, tt, final_norm):
    t, d = x1.shape
    steps = t // tt
    return pl.pallas_call(
        functools.partial(_combine_kernel, top_k=TOP_K, final_norm=final_norm),
        grid=(steps,),
        in_specs=[pl.BlockSpec((1, 1, tt * TOP_K), lambda i: (i, 0, 0), memory_space=pltpu.SMEM),
                  pl.BlockSpec((1, 1, tt * TOP_K), lambda i: (jnp.minimum(i + 1, steps - 1), 0, 0),
                               memory_space=pltpu.SMEM),
                  pl.BlockSpec((tt, LANES), lambda i: (i, 0)),
                  pl.BlockSpec((tt, d), lambda i: (i, 0)),
                  pl.BlockSpec((1, 6, d), _mod_index_map(n_ctx_tiles, tiles_per_lat)),
                  pl.BlockSpec((1, d), lambda i: (0, 0)),
                  pl.BlockSpec(memory_space=pl.ANY)],
        out_specs=pl.BlockSpec((tt, d), lambda i: (i, 0)),
        out_shape=jax.ShapeDtypeStruct((t, d), F32),
        scratch_shapes=[pltpu.VMEM((2, TOP_K, tt * d // LANES, LANES), F32), pltpu.SemaphoreType.DMA((2,))],
        compiler_params=_cparams("arbitrary"),
        name="combine",
    )(dest, dest, tg, x1, mods, final_g.reshape(1, d), ys)


def _moe_plan(counts, rows, n_blocks):
    n_exp = counts.shape[0]
    padded = (counts + rows - 1) // rows * rows
    pad_end = jnp.cumsum(padded)
    block_row = jnp.arange(n_blocks, dtype=jnp.int32) * rows
    block_e = jnp.minimum(jnp.sum(pad_end[None, :] <= block_row[:, None], axis=1), n_exp - 1).astype(jnp.int32)
    n_used = (pad_end[-1:] // rows).astype(jnp.int32)
    pad_start = jnp.pad((pad_end - padded).astype(F32), (0, LANES - n_exp)).reshape(1, LANES)
    return pad_start, block_e, n_used


def _sequence_mixers(z_hy, z_qk, z_v, z_o, z_g, lw, state, row_w):
    (hy_cw, hy_cb, filt_params, hy_b, ml_cw, ml_cb, ml_gb, ml_ng) = lw
    y_hy = _hyena(z_hy, hy_cw, hy_cb, filt_params, hy_b, row_w)
    ml_w = z_v.shape[-1]
    q = _short_conv(z_qk, ml_cw, ml_cb, row_w, silu=True, col0=0, ncols=ml_w)
    kt = _short_conv(z_qk, ml_cw, ml_cb, row_w, silu=True, col0=ml_w, ncols=ml_w, transpose=True)
    y_ml, st = _mlstm(q, kt, z_v, z_o, z_g, ml_gb, ml_ng, *state)
    return y_hy, y_ml, st


def kernel(x_prompt, x_sample, state_mlstm_C, state_mlstm_n, state_mlstm_m, c, c_ctx, ada_w, ada_b, norm1_g,
           w_in, hy_conv_w, hy_conv_b, filt_w1, filt_b1, filt_w2, filt_b2, filt_w3, filt_freq, hy_bias,
           ml_conv_w, ml_conv_b, ml_gate_b, ml_norm_g, w_out, norm2_g, router_w, router_b, moe_w_gu,
           moe_b_gu, moe_w_down, moe_b_down, final_g):
    bp, lp, d = x_prompt.shape
    bs, ls, _ = x_sample.shape
    depth = ada_w.shape[0]
    heads = ml_gate_b.shape[-1]
    hy_w = hy_bias.shape[-1]
    ml_w = ml_norm_g.shape[-1]
    dh = ml_w // heads
    n_exp = router_w.shape[-1]
    t_ctx, t_lat = bp * lp, bs * ls
    t = t_ctx + t_lat
    tm = min(ROW_TILE, lp)
    tt = min(TOK_TILE, lp)
    ng = N_DIR * 2 * heads
    seg_widths = (3 * hy_w, 2 * ml_w, ml_w, ml_w)
    n_main = 3 * hy_w + 4 * ml_w
    n_blocks = -(-(t * TOP_K) // MOE_ROWS) + n_exp

    x = jnp.concatenate([x_prompt.reshape(t_ctx, d), x_sample.reshape(t_lat, d)], axis=0)
    cond = jnp.concatenate([c_ctx[None], c, jnp.zeros((SUBLANES - 1 - bs, d), F32)], axis=0)
    zero_state = (jnp.zeros((bp, N_DIR, heads, dh, dh), F32), jnp.zeros((bp, N_DIR, heads, dh), F32),
                  jnp.zeros((bp, N_DIR, heads), F32))
    new_c, new_n, new_m = [], [], []
    for l in range(depth):
        mods = _ada(cond, ada_w[l], ada_b[l]).reshape(SUBLANES, 6, d)
        w_gate = jnp.pad(w_in[l][:, n_main:], ((0, 0), (0, LANES - ng)))
        z = _inproj(x, mods, norm1_g[l], w_in[l][:, :n_main].astype(BF16), w_gate, seg_widths,
                    t_ctx // tm, ls // tm, tm)
        lw = (hy_conv_w[l], hy_conv_b[l],
              (filt_w1[l], filt_b1[l], filt_w2[l], filt_b2[l], filt_w3[l], filt_freq[l]), hy_bias[l],
              ml_conv_w[l], ml_conv_b[l], ml_gate_b[l], ml_norm_g[l])
        z_ctx = [a[:t_ctx].reshape(bp, lp, a.shape[1]) for a in z]
        z_lat = [a[t_ctx:].reshape(bs, ls, a.shape[1]) for a in z]
        cache = (state_mlstm_C[:, l], state_mlstm_n[:, l], state_mlstm_m[:, l])
        hy_c, ml_c, st = _sequence_mixers(*z_ctx, lw, zero_state, lp)
        hy_l, ml_l, _ = _sequence_mixers(*z_lat, lw, cache, GRID_W)
        new_c.append(st[0])
        new_n.append(st[1])
        new_m.append(st[2])
        y_hy = jnp.concatenate([hy_c.reshape(t_ctx, hy_w), hy_l.reshape(t_lat, hy_w)], axis=0)
        y_ml = jnp.concatenate([ml_c.reshape(t_ctx, ml_w), ml_l.reshape(t_lat, ml_w)], axis=0)
        x1, h2, ti, tg, rk, cnt = _post(y_hy, y_ml, x, mods, w_out[l], norm2_g[l], router_w[l], router_b[l],
                                        t_ctx // tm, ls // tm, tm)
        pad_start, block_e, n_used = _moe_plan(cnt[0, :n_exp].astype(jnp.int32), MOE_ROWS, n_blocks)
        dest = _dest_rows(ti, rk, pad_start, tm)[:, :TOP_K].reshape(t // tt, 1, tt * TOP_K)
        xs = _dispatch(dest, h2, n_blocks * MOE_ROWS, tt)
        ys = _ffn(block_e, n_used, xs, moe_w_gu[l], moe_b_gu[l], moe_w_down[l], moe_b_down[l], MOE_ROWS)
        x = _combine(dest, tg, x1, mods, final_g, ys, t_ctx // tt, ls // tt, tt, final_norm=l == depth - 1)
    y_prompt = x[:t_ctx].reshape(bp, lp, d)
    y_sample = x[t_ctx:].reshape(bs, ls, d)
    return (y_prompt, y_sample, jnp.stack(new_c, axis=1), jnp.stack(new_n, axis=1), jnp.stack(new_m, axis=1))
```

```python
import functools
import math

import numpy as np
import jax
import jax.numpy as jnp
from jax import lax
from jax.experimental import pallas as pl
from jax.experimental.pallas import tpu as pltpu

F32 = jnp.float32
BF16 = jnp.bfloat16
HIGHEST = lax.Precision.HIGHEST
EPS = 1e-6

LANES = 128
SUBLANES = 8
VMEM_LIMIT_BYTES = 56 * 1024 * 1024

GRID_W = 64
ML_HEADS = 4
N_DIR = 2
HY_ORDER = 2
FILT_BANDS = 8
DECAY_TARGET = 1e-2
FAST_DECAY_PCT = 0.3
SLOW_DECAY_PCT = 1.5
TOP_K = 4
SWIGLU_LIMIT = 7.0
SWIGLU_ALPHA = 1.702

FFT_N2 = 128
MLSTM_CHUNK = 128
MOE_ROWS = 512
ROW_TILE = 512
TOK_TILE = 256


def _cparams(*sem):
    return pltpu.CompilerParams(dimension_semantics=sem, vmem_limit_bytes=VMEM_LIMIT_BYTES)


def _lane_tile(c, cap):
    return max(t for t in range(LANES, min(c, cap) + 1, LANES) if c % t == 0)


def _bdot(a, b):
    return jnp.dot(a.astype(BF16), b.astype(BF16), preferred_element_type=F32)


def _hdot(a, b):
    return jnp.dot(a, b, precision=HIGHEST, preferred_element_type=F32)


def _ada_kernel(c_ref, w_ref, b_ref, o_ref):
    c = c_ref[...]
    o_ref[...] = _hdot(c * jax.nn.sigmoid(c), w_ref[...]) + b_ref[...]


def _ada(cond, w, b):
    r, d = cond.shape
    n = w.shape[1]
    tn = _lane_tile(n, 1024)
    return pl.pallas_call(
        _ada_kernel,
        grid=(n // tn,),
        in_specs=[pl.BlockSpec((r, d), lambda j: (0, 0)),
                  pl.BlockSpec((d, tn), lambda j: (0, j)),
                  pl.BlockSpec((1, tn), lambda j: (0, j))],
        out_specs=pl.BlockSpec((r, tn), lambda j: (0, j)),
        out_shape=jax.ShapeDtypeStruct((r, n), F32),
        compiler_params=_cparams("arbitrary"),
        name="ada",
    )(cond, w, b.reshape(1, n))


def _mod_index_map(n_ctx_tiles, tiles_per_lat):
    def index_map(i):
        return (jnp.where(i < n_ctx_tiles, 0, 1 + (i - n_ctx_tiles) // tiles_per_lat), 0, 0)
    return index_map


def _inproj_kernel(x_ref, mod_ref, g_ref, w_ref, wg_ref, *out_refs, offsets):
    x = x_ref[...]
    h = x * lax.rsqrt(jnp.mean(x * x, axis=-1, keepdims=True) + EPS) * g_ref[...]
    h = h * (1.0 + mod_ref[0, 1:2, :]) + mod_ref[0, 0:1, :]
    hb = h.astype(BF16)
    for o_ref, (lo, hi) in zip(out_refs[:-1], offsets):
        o_ref[...] = jnp.dot(hb, w_ref[:, lo:hi], preferred_element_type=F32)
    h_lo = (h - hb.astype(F32)).astype(BF16)
    g = jnp.dot(hb, wg_ref[...], preferred_element_type=F32)
    out_refs[-1][...] = (g[:, :LANES] + g[:, LANES:]
                         + jnp.dot(h_lo, wg_ref[:, :LANES], preferred_element_type=F32))


def _inproj(x, mods, norm_g, w_main, w_gate, seg_widths, n_ctx_tiles, tiles_per_lat, tm):
    t, d = x.shape
    offsets, lo = [], 0
    for wd in seg_widths:
        offsets.append((lo, lo + wd))
        lo += wd
    wg_hi = w_gate.astype(BF16)
    wg = jnp.concatenate([wg_hi, (w_gate - wg_hi.astype(F32)).astype(BF16)], axis=1)
    widths = tuple(seg_widths) + (LANES,)
    return pl.pallas_call(
        functools.partial(_inproj_kernel, offsets=tuple(offsets)),
        grid=(t // tm,),
        in_specs=[pl.BlockSpec((tm, d), lambda i: (i, 0)),
                  pl.BlockSpec((1, 6, d), _mod_index_map(n_ctx_tiles, tiles_per_lat)),
                  pl.BlockSpec((1, d), lambda i: (0, 0)),
                  pl.BlockSpec(w_main.shape, lambda i: (0, 0)),
                  pl.BlockSpec(wg.shape, lambda i: (0, 0))],
        out_specs=[pl.BlockSpec((tm, wd), lambda i: (i, 0)) for wd in widths],
        out_shape=[jax.ShapeDtypeStruct((t, wd), F32) for wd in widths],
        compiler_params=_cparams("arbitrary"),
        name="inproj",
    )(x, mods, norm_g.reshape(1, d), w_main, wg)


def _short_conv_kernel(x_ref, w_ref, b_ref, o_ref, *, row_w, silu, transpose):
    x = x_ref[0]
    l = x.shape[0]
    pos = lax.broadcasted_iota(jnp.int32, x.shape, 0) % row_w
    prev = jnp.where(pos == 0, 0.0, pltpu.roll(x, 1, 0))
    nxt = jnp.where(pos == row_w - 1, 0.0, pltpu.roll(x, l - 1, 0))
    y = prev * w_ref[0:1, :] + x * w_ref[1:2, :] + nxt * w_ref[2:3, :] + b_ref[...]
    if silu:
        y = y * jax.nn.sigmoid(y)
    o_ref[0] = y.T if transpose else y


def _short_conv(x, w, b, row_w, silu, col0=0, ncols=None, transpose=False):
    bsz, l, c = x.shape
    ncols = c if ncols is None else ncols
    ct = _lane_tile(math.gcd(ncols, col0) if col0 else ncols, LANES if transpose else 256)
    j0 = col0 // ct
    out_shape, out_block, out_map = (bsz, l, ncols), (1, l, ct), lambda i, j: (i, 0, j)
    if transpose:
        out_shape, out_block, out_map = (bsz, ncols, l), (1, ct, l), lambda i, j: (i, j, 0)
    return pl.pallas_call(
        functools.partial(_short_conv_kernel, row_w=row_w, silu=silu, transpose=transpose),
        grid=(bsz, ncols // ct),
        in_specs=[pl.BlockSpec((1, l, ct), lambda i, j: (i, 0, j0 + j)),
                  pl.BlockSpec((3, ct), lambda i, j: (0, j0 + j)),
                  pl.BlockSpec((1, ct), lambda i, j: (0, j0 + j))],
        out_specs=pl.BlockSpec(out_block, out_map),
        out_shape=jax.ShapeDtypeStruct(out_shape, F32),
        compiler_params=_cparams("arbitrary", "arbitrary"),
        name="short_conv",
    )(x, w, b.reshape(1, c))


def _dft_direct_tables(l):
    n = 2 * l
    k = np.arange(n)[:, None].astype(np.float64)
    t = np.arange(n)[None, :].astype(np.float64)
    ang = 2.0 * np.pi * ((k * t) % n) / n
    cm, sm = np.cos(ang), np.sin(ang)
    fwd = np.block([[cm[:, :l], sm[:, :l]], [-sm[:, :l], cm[:, :l]]])
    filt = np.concatenate([cm, -sm], axis=0)
    return fwd, filt


def _dft_two_level_tables(l, n2):
    n = 2 * l
    n1 = n // n2
    k1 = np.arange(n1)[:, None].astype(np.float64)
    a = np.arange(n1)[None, :].astype(np.float64)
    ang1 = 2.0 * np.pi * ((k1 * a) % n1) / n1
    c1, s1 = np.cos(ang1), np.sin(ang1)
    h = n1 // 2
    m1 = np.block([[c1[:, :h], s1[:, :h]], [-s1[:, :h], c1[:, :h]]])
    m1f = np.concatenate([c1, -s1], axis=0)
    kk = (np.arange(n1)[:, None, None] + n1 * np.arange(n2)[None, :, None]).astype(np.float64)
    b = np.arange(n2)[None, None, :].astype(np.float64)
    ang = 2.0 * np.pi * ((kk * b) % n) / n
    cg, sg = np.cos(ang), np.sin(ang)
    gt = np.concatenate([np.concatenate([cg, sg], axis=2),
                         np.concatenate([-sg, cg], axis=2)], axis=1)
    return m1, m1f, gt


def _circular_lag(n0, rows, l):
    n = n0 + lax.broadcasted_iota(jnp.int32, (rows, 1), 0)
    t = jnp.where(n < l, n, 2 * l - n).astype(F32)
    return n, t, t / float(max(l - 1, 1))


def _filter_hidden_kernel(bandv_ref, w1_ref, b1_ref, w2_ref, b2_ref, freq_ref, o_ref, *, l):
    rows = o_ref.shape[0]
    _, t, t01 = _circular_lag(pl.program_id(0) * rows, rows, l)
    lane = lax.broadcasted_iota(jnp.int32, (rows, LANES), 1)
    ang = (2.0 * math.pi / l) * t * bandv_ref[...]
    feats = jnp.where(lane == 0, t01,
                      jnp.where(lane <= FILT_BANDS, jnp.cos(ang),
                                jnp.where(lane <= 2 * FILT_BANDS, -jnp.sin(ang), 0.0)))
    fr = freq_ref[...]
    h = jnp.sin(fr * (_hdot(feats, w1_ref[...]) + b1_ref[...]))
    o_ref[...] = jnp.sin(fr * (_hdot(h, w2_ref[...]) + b2_ref[...]))


def _filter_hidden(l, f_w1, f_b1, f_w2, f_b2, f_freq):
    emb, hid = f_w1.shape
    n = 2 * l
    rows = min(n, 512)
    bands = jnp.linspace(1e-4, FILT_BANDS - 1, FILT_BANDS, dtype=F32)
    bandv = jnp.zeros((1, LANES), F32).at[0, 1:1 + FILT_BANDS].set(bands)
    bandv = bandv.at[0, 1 + FILT_BANDS:1 + 2 * FILT_BANDS].set(bands)
    w1p = jnp.zeros((LANES, hid), F32).at[:emb].set(f_w1)
    c0 = lambda i: (0, 0)
    return pl.pallas_call(
        functools.partial(_filter_hidden_kernel, l=l),
        grid=(n // rows,),
        in_specs=[pl.BlockSpec((1, LANES), c0), pl.BlockSpec((LANES, hid), c0), pl.BlockSpec((1, hid), c0),
                  pl.BlockSpec((hid, hid), c0), pl.BlockSpec((1, hid), c0), pl.BlockSpec((1, hid), c0)],
        out_specs=pl.BlockSpec((rows, hid), lambda i: (i, 0)),
        out_shape=jax.ShapeDtypeStruct((n, hid), F32),
        compiler_params=_cparams("arbitrary"),
        name="filter_hidden",
    )(bandv, w1p, f_b1.reshape(1, hid), f_w2, f_b2.reshape(1, hid), f_freq.reshape(1, hid))


def _filter_rows(n0, rows, l, hid_ref, w3_ref, delta_ref):
    n, _, t01 = _circular_lag(n0, rows, l)
    h = hid_ref[pl.ds(n0, rows), :]
    hf = _bdot(h, w3_ref[0, 0])
    hb = _bdot(h, w3_ref[0, 1])
    window = jnp.exp(-t01 * delta_ref[...])
    return jnp.where(n < l, hf, jnp.where(n > l, hb, 0.0)) * window


def _filter_direct_kernel(hid_ref, w3_ref, delta_ref, ff_ref, h_ref, *, l):
    hc = _filter_rows(0, 2 * l, l, hid_ref, w3_ref, delta_ref)
    h_ref[0] = _bdot(ff_ref[...], hc) * (1.0 / (2 * l))


def _filter_two_level_kernel(hid_ref, w3_ref, delta_ref, m1f_ref, gt_ref, h_ref, hc_buf, a_buf, *, l, n2, rows):
    n = 2 * l
    n1 = n // n2

    def fill(i, carry):
        r0 = pl.multiple_of(i * rows, rows)
        hc_buf[pl.ds(r0, rows), :] = _filter_rows(r0, rows, l, hid_ref, w3_ref, delta_ref)
        return carry
    lax.fori_loop(0, n // rows, fill, 0)

    def step1(b, carry):
        col = hc_buf[pl.ds(b, n1, stride=n2), :]
        a = _bdot(m1f_ref[...], col)
        a_buf[pl.ds(b, n1, stride=2 * n2), :] = a[:n1]
        a_buf[pl.ds(n2 + b, n1, stride=2 * n2), :] = a[n1:]
        return carry
    lax.fori_loop(0, n2, step1, 0, unroll=8)

    def step2(k1, carry):
        r0 = pl.multiple_of(k1 * 2 * n2, 2 * n2)
        h_ref[0, k1] = (_bdot(gt_ref[k1], a_buf[pl.ds(r0, 2 * n2), :]) * (1.0 / n)).astype(h_ref.dtype)
        return carry
    lax.fori_loop(0, n1, step2, 0, unroll=4)


def _filter_tail_inputs(hy_w, f_w3):
    hid = f_w3.shape[0]
    w3 = f_w3.reshape(hid, HY_ORDER, N_DIR, hy_w).transpose(1, 2, 0, 3)
    max_decay = math.log(DECAY_TARGET) / FAST_DECAY_PCT
    min_decay = math.log(DECAY_TARGET) / SLOW_DECAY_PCT
    deltas = jnp.abs(jnp.linspace(min_decay, max_decay, hy_w, dtype=F32)).reshape(1, hy_w)
    return w3, deltas


def _filter_spectrum_direct(l, hy_w, filt_params, ff):
    f_w1, f_b1, f_w2, f_b2, f_w3, f_freq = filt_params
    hidden = _filter_hidden(l, f_w1, f_b1, f_w2, f_b2, f_freq)
    w3, deltas = _filter_tail_inputs(hy_w, f_w3)
    hid = f_w3.shape[0]
    n = 2 * l
    return pl.pallas_call(
        functools.partial(_filter_direct_kernel, l=l),
        grid=(HY_ORDER,),
        in_specs=[pl.BlockSpec((n, hid), lambda o: (0, 0)),
                  pl.BlockSpec((1, N_DIR, hid, hy_w), lambda o: (o, 0, 0, 0)),
                  pl.BlockSpec((1, hy_w), lambda o: (0, 0)),
                  pl.BlockSpec((2 * n, n), lambda o: (0, 0))],
        out_specs=pl.BlockSpec((1, 2 * n, hy_w), lambda o: (o, 0, 0)),
        out_shape=jax.ShapeDtypeStruct((HY_ORDER, 2 * n, hy_w), F32),
        compiler_params=_cparams("arbitrary"),
        name="filter_direct",
    )(hidden, w3, deltas, ff)


def _filter_spectrum_two_level(l, hy_w, filt_params, m1f, gt, ct):
    f_w1, f_b1, f_w2, f_b2, f_w3, f_freq = filt_params
    hidden = _filter_hidden(l, f_w1, f_b1, f_w2, f_b2, f_freq)
    w3, deltas = _filter_tail_inputs(hy_w, f_w3)
    hid = f_w3.shape[0]
    n = 2 * l
    n2 = FFT_N2
    n1 = n // n2
    return pl.pallas_call(
        functools.partial(_filter_two_level_kernel, l=l, n2=n2, rows=min(n, 512)),
        grid=(HY_ORDER, hy_w // ct),
        in_specs=[pl.BlockSpec((n, hid), lambda o, j: (0, 0)),
                  pl.BlockSpec((1, N_DIR, hid, ct), lambda o, j: (o, 0, 0, j)),
                  pl.BlockSpec((1, ct), lambda o, j: (0, j)),
                  pl.BlockSpec((2 * n1, n1), lambda o, j: (0, 0)),
                  pl.BlockSpec((n1, 2 * n2, 2 * n2), lambda o, j: (0, 0, 0))],
        out_specs=pl.BlockSpec((1, n1, 2 * n2, ct), lambda o, j: (o, 0, 0, j)),
        out_shape=jax.ShapeDtypeStruct((HY_ORDER, n1, 2 * n2, hy_w), BF16),
        scratch_shapes=[pltpu.VMEM((n, ct), F32), pltpu.VMEM((n1 * 2 * n2, ct), F32)],
        compiler_params=_cparams("arbitrary", "arbitrary"),
        name="filter_two_level",
    )(hidden, w3, deltas, m1f, gt)


def _complex_mul(x, h, half):
    xr, xi = x[:half], x[half:]
    hr, hi = h[:half], h[half:]
    return jnp.concatenate([xr * hr - xi * hi, xr * hi + xi * hr], axis=0)


def _conv_direct_kernel(z_ref, gate_ref, bias_ref, h_ref, fwd_ref, inv_ref, o_ref):
    l = z_ref.shape[1]
    z = jnp.concatenate([z_ref[0], z_ref[1]], axis=0)
    x = _bdot(fwd_ref[...], z)
    y = _bdot(inv_ref[...], _complex_mul(x, h_ref[0], 2 * l))
    bias = bias_ref[0]
    o_ref[0] = gate_ref[0] * (y[:l] + bias * z_ref[0])
    o_ref[1] = gate_ref[1] * (y[l:] + bias * z_ref[1])


def _conv_two_level_kernel(z_ref, gate_ref, bias_ref, h_ref, m1_ref, m1i_ref, gt_ref, o_ref,
                           a_buf, *, n2):
    l = z_ref.shape[1]
    n1 = 2 * l // n2
    hn = n1 // 2

    def step1(b, carry):
        za = z_ref[0, pl.ds(b, hn, stride=n2), :]
        zb = z_ref[1, pl.ds(b, hn, stride=n2), :]
        a = _bdot(m1_ref[...], jnp.concatenate([za, zb], axis=0))
        a_buf[pl.ds(b, n1, stride=2 * n2), :] = a[:n1]
        a_buf[pl.ds(n2 + b, n1, stride=2 * n2), :] = a[n1:]
        return carry
    lax.fori_loop(0, n2, step1, 0, unroll=8)

    def step2(k1, carry):
        r0 = pl.multiple_of(k1 * 2 * n2, 2 * n2)
        x = _bdot(gt_ref[k1], a_buf[pl.ds(r0, 2 * n2), :])
        y = _complex_mul(x, h_ref[0, k1].astype(F32), n2).astype(BF16)
        a_buf[pl.ds(r0, 2 * n2), :] = lax.dot_general(gt_ref[k1], y, (((0,), (0,)), ((), ())),
                                                      preferred_element_type=F32)
        return carry
    lax.fori_loop(0, n1, step2, 0, unroll=4)

    def step3(b, carry):
        br = a_buf[pl.ds(b, n1, stride=2 * n2), :]
        bi = a_buf[pl.ds(n2 + b, n1, stride=2 * n2), :]
        y = _bdot(m1i_ref[...], jnp.concatenate([br, bi], axis=0))
        o_ref[0, pl.ds(b, hn, stride=n2), :] = y[:hn]
        o_ref[1, pl.ds(b, hn, stride=n2), :] = y[hn:]
        return carry
    lax.fori_loop(0, n2, step3, 0, unroll=8)

    bias = bias_ref[0]
    for s in range(2):
        o_ref[s] = gate_ref[s] * (o_ref[s] + bias * z_ref[s])


def _long_conv_gated(u, z, z_col, gate_col, spectrum, order, bias, tables, ct):
    bsz, l, _ = u.shape
    c = spectrum.shape[-1]
    nct = c // ct
    zspec = pl.BlockSpec((2, l, ct), lambda i, j: (i, 0, z_col * nct + j))
    gspec = pl.BlockSpec((2, l, ct), lambda i, j: (i, 0, gate_col * nct + j))
    bspec = pl.BlockSpec((1, 1, ct), lambda i, j: (order, 0, j))
    ospec = pl.BlockSpec((2, l, ct), lambda i, j: (i, 0, j))
    out_shape = jax.ShapeDtypeStruct((bsz, l, c), F32)
    bias3 = bias.reshape(HY_ORDER, 1, c)
    if len(tables) == 2:
        fwd, inv = tables
        n = 2 * l
        return pl.pallas_call(
            _conv_direct_kernel,
            grid=(bsz // 2, nct),
            in_specs=[zspec, gspec, bspec,
                      pl.BlockSpec((1, 2 * n, ct), lambda i, j: (order, 0, j)),
                      pl.BlockSpec(fwd.shape, lambda i, j: (0, 0)),
                      pl.BlockSpec(inv.shape, lambda i, j: (0, 0))],
            out_specs=ospec, out_shape=out_shape,
            compiler_params=_cparams("arbitrary", "arbitrary"),
            name="long_conv_direct",
        )(z, u, bias3, spectrum, fwd, inv)
    m1, m1i, gt = tables
    n2 = FFT_N2
    n1 = 2 * l // n2
    const2 = lambda i, j: (0, 0)
    const3 = lambda i, j: (0, 0, 0)
    return pl.pallas_call(
        functools.partial(_conv_two_level_kernel, n2=n2),
        grid=(nct, bsz // 2),
        in_specs=[pl.BlockSpec((2, l, ct), lambda j, i: (i, 0, z_col * nct + j)),
                  pl.BlockSpec((2, l, ct), lambda j, i: (i, 0, gate_col * nct + j)),
                  pl.BlockSpec((1, 1, ct), lambda j, i: (order, 0, j)),
                  pl.BlockSpec((1, n1, 2 * n2, ct), lambda j, i: (order, 0, 0, j)),
                  pl.BlockSpec(m1.shape, const2), pl.BlockSpec(m1i.shape, const2),
                  pl.BlockSpec(gt.shape, const3)],
        out_specs=pl.BlockSpec((2, l, ct), lambda j, i: (i, 0, j)),
        out_shape=out_shape,
        scratch_shapes=[pltpu.VMEM((n1 * 2 * n2, ct), F32)],
        compiler_params=_cparams("arbitrary", "arbitrary"),
        name="long_conv_two_level",
    )(z, u, bias3, spectrum, m1, m1i, gt)


def _hyena(z_hy, conv_w, conv_b, filt_params, hy_bias, row_w):
    bsz, l, c3 = z_hy.shape
    c = c3 // 3
    u = _short_conv(z_hy, conv_w, conv_b, row_w, silu=False)
    table = lambda a: jnp.asarray(a, F32).astype(BF16)
    if 2 * l // FFT_N2 <= 4:
        fwd, filt = _dft_direct_tables(l)
        tables = (table(fwd), table(fwd.T))
        spectrum = _filter_spectrum_direct(l, c, filt_params, table(filt))
        ct = c
    else:
        m1, m1f, gt = _dft_two_level_tables(l, FFT_N2)
        tables = (table(m1), table(m1.T), table(gt))
        ct = LANES
        spectrum = _filter_spectrum_two_level(l, c, filt_params, table(m1f), tables[2], ct)
    z1 = _long_conv_gated(u, u, 0, 1, spectrum, 0, hy_bias, tables, ct)
    return _long_conv_gated(u, z1, 0, 2, spectrum, 1, hy_bias, tables, ct)


def _log_sigmoid(x):
    return jnp.minimum(x, 0.0) - jnp.log1p(jnp.exp(-jnp.abs(x)))


def _split3_dot(a, b, split_lhs):
    x = a if split_lhs else b
    hi = x.astype(BF16)
    rest = x - hi.astype(F32)
    mid = rest.astype(BF16)
    parts = (hi, mid, (rest - mid.astype(F32)).astype(BF16))
    if split_lhs:
        return functools.reduce(lambda u, w: u + w, [jnp.dot(p, b, preferred_element_type=F32) for p in parts])
    return functools.reduce(lambda u, w: u + w, [jnp.dot(a, p, preferred_element_type=F32) for p in parts])


def _mlstm_chunk(qb, kt, vo, bq, irow, brow, btot, mask, cn, m):
    dh = qb.shape[1]
    dm = jnp.where(mask, bq - brow + irow, -jnp.inf)
    m_loc = jnp.max(dm, axis=-1, keepdims=True)
    s = jnp.dot(qb, kt.astype(BF16), preferred_element_type=F32) * jnp.exp(dm - m_loc)
    sv = jnp.dot(s.astype(BF16), vo, preferred_element_type=F32)
    gl = btot - brow + irow
    g_loc = jnp.max(gl, axis=-1, keepdims=True)
    kv = jnp.dot((kt * jnp.exp(gl - g_loc)).astype(BF16), vo, preferred_element_type=F32)
    inter = bq + m
    mj = jnp.maximum(m_loc, inter)
    w_int = jnp.exp(inter - mj)
    w_loc = jnp.exp(m_loc - mj)
    qc = jnp.dot(qb, cn.astype(BF16), preferred_element_type=F32)
    num = w_int * qc[:, :dh] + w_loc * sv[:, :dh]
    den = w_int * qc[:, dh:] + w_loc * sv[:, dh:]
    h = num / jnp.maximum(jnp.abs(den), jnp.exp(-mj))
    m_new = jnp.maximum(btot + m, g_loc)
    cn_new = jnp.exp(btot + m - m_new) * cn + jnp.exp(g_loc - m_new) * kv
    return h, cn_new, m_new


def _mlstm_kernel(q_ref, kt_ref, v_ref, o_ref, g_ref, gt_ref, gb_ref, gbt_ref, ng_ref, c0_ref, n0_ref, m0_ref,
                  y_ref, c_ref, n_ref, m_ref, hf_buf, hb_buf, cn_buf, *, chunk):
    l, dh = q_ref.shape[1], q_ref.shape[2]
    nc = l // chunk
    scale = dh ** -0.5
    row = lax.broadcasted_iota(jnp.int32, (chunk, chunk), 0)
    col = lax.broadcasted_iota(jnp.int32, (chunk, chunk), 1)
    lower, upper = col <= row, col >= row
    tri_l, tri_u = lower.astype(BF16), upper.astype(BF16)
    gate_row = lax.broadcasted_iota(jnp.int32, (SUBLANES, chunk), 0)
    ones = jnp.ones((chunk, dh), BF16)

    def direction(r0, d, cn, m):
        lf = _log_sigmoid(g_ref[0, 0, pl.ds(r0, chunk), :] + gb_ref[0])[:, 2 * d + 1:2 * d + 2]
        bq = _split3_dot(tri_u if d else tri_l, jnp.broadcast_to(lf, (chunk, LANES)), split_lhs=False)
        gt = gt_ref[0, 0, :, pl.ds(r0, chunk)] + gbt_ref[0]
        gt = jnp.where(gate_row % 2 == 1, _log_sigmoid(gt), gt)
        brow = _split3_dot(gt, tri_l if d else tri_u, split_lhs=True)[2 * d + 1:2 * d + 2, :]
        btot = brow[:, 0:1] if d else brow[:, chunk - 1:chunk]
        vo = jnp.concatenate([v_ref[0, pl.ds(r0, chunk), :].astype(BF16), ones], axis=1)
        return _mlstm_chunk(q_ref[0, pl.ds(r0, chunk), :].astype(BF16), kt_ref[0, :, pl.ds(r0, chunk)] * scale,
                            vo, bq, gt[2 * d:2 * d + 1, :], brow, btot, upper if d else lower, cn, m)

    for d in range(N_DIR):
        n_rep = jnp.broadcast_to(n0_ref[0, 0, d:d + 1, :], (dh, dh)).T
        cn_buf[d] = jnp.concatenate([c0_ref[0, d, 0], n_rep], axis=1)

    def body(j, carry):
        mf, mb = carry
        rf = pl.multiple_of(j * chunk, chunk)
        hf, cn, mf = direction(rf, 0, cn_buf[0], mf)
        cn_buf[0] = cn
        hf_buf[pl.ds(rf, chunk), :] = hf
        rb = pl.multiple_of((nc - 1 - j) * chunk, chunk)
        hb, cn, mb = direction(rb, 1, cn_buf[1], mb)
        cn_buf[1] = cn
        hb_buf[pl.ds(rb, chunk), :] = hb
        return mf, mb

    m_fin = lax.fori_loop(0, nc, body, (m0_ref[0, 0, 0:1, 0:1], m0_ref[0, 0, 1:2, 0:1]), unroll=2)
    for d in range(N_DIR):
        c_ref[0, d, 0] = cn_buf[d, :, :dh]
        n_ref[0, 0, d:d + 1, :] = cn_buf[d, :, dh:].T[0:1, :]
        m_ref[0, 0, d:d + 1, :] = jnp.broadcast_to(m_fin[d], (1, LANES))

    def finish(j, carry):
        r0 = pl.multiple_of(j * chunk, chunk)
        hs = hf_buf[pl.ds(r0, chunk), :] + hb_buf[pl.ds(r0, chunk), :]
        hs = hs * lax.rsqrt(jnp.mean(hs * hs, axis=-1, keepdims=True) + EPS) * ng_ref[...]
        y_ref[0, pl.ds(r0, chunk), :] = jax.nn.sigmoid(o_ref[0, pl.ds(r0, chunk), :]) * hs
        return carry
    lax.fori_loop(0, nc, finish, 0)


def _mlstm(q, kt, z_v, z_o, z_g, gate_b, norm_g, c0, n0, m0):
    bsz, l, ml = z_v.shape
    heads = gate_b.shape[-1]
    dh = ml // heads
    chunk = MLSTM_CHUNK
    assert dh == LANES and chunk == LANES and l % chunk == 0
    n_gate = N_DIR * 2
    g4 = z_g[..., :n_gate * heads].reshape(bsz, l, n_gate, heads)
    gh = jnp.pad(g4.transpose(0, 3, 1, 2), ((0, 0), (0, 0), (0, 0), (0, LANES - n_gate)))
    ght = jnp.pad(g4.transpose(0, 3, 2, 1), ((0, 0), (0, 0), (0, SUBLANES - n_gate), (0, 0)))
    gb4 = gate_b.reshape(n_gate, heads).T
    gb = jnp.pad(gb4, ((0, 0), (0, LANES - n_gate))).reshape(heads, 1, LANES)
    gbt = jnp.broadcast_to(jnp.pad(gb4, ((0, 0), (0, SUBLANES - n_gate)))[..., None], (heads, SUBLANES, LANES))
    n0h = n0.transpose(0, 2, 1, 3)
    m0h = jnp.broadcast_to(m0.transpose(0, 2, 1)[..., None], (bsz, heads, N_DIR, LANES))
    seq = pl.BlockSpec((1, l, dh), lambda b, h: (b, 0, h))
    cspec = pl.BlockSpec((1, N_DIR, 1, dh, dh), lambda b, h: (b, 0, h, 0, 0))
    sspec = pl.BlockSpec((1, 1, N_DIR, dh), lambda b, h: (b, h, 0, 0))
    mspec = pl.BlockSpec((1, 1, N_DIR, LANES), lambda b, h: (b, h, 0, 0))
    y, c, n, m = pl.pallas_call(
        functools.partial(_mlstm_kernel, chunk=chunk),
        grid=(bsz, heads),
        in_specs=[seq, pl.BlockSpec((1, dh, l), lambda b, h: (b, h, 0)), seq, seq,
                  pl.BlockSpec((1, 1, l, LANES), lambda b, h: (b, h, 0, 0)),
                  pl.BlockSpec((1, 1, SUBLANES, l), lambda b, h: (b, h, 0, 0)),
                  pl.BlockSpec((1, 1, LANES), lambda b, h: (h, 0, 0)),
                  pl.BlockSpec((1, SUBLANES, LANES), lambda b, h: (h, 0, 0)),
                  pl.BlockSpec((1, dh), lambda b, h: (0, h)),
                  cspec, sspec, mspec],
        out_specs=[seq, cspec, sspec, mspec],
        out_shape=[jax.ShapeDtypeStruct((bsz, l, ml), F32),
                   jax.ShapeDtypeStruct((bsz, N_DIR, heads, dh, dh), F32),
                   jax.ShapeDtypeStruct((bsz, heads, N_DIR, dh), F32),
                   jax.ShapeDtypeStruct((bsz, heads, N_DIR, LANES), F32)],
        scratch_shapes=[pltpu.VMEM((l, dh), F32), pltpu.VMEM((l, dh), F32),
                        pltpu.VMEM((N_DIR, dh, 2 * dh), F32)],
        compiler_params=_cparams("arbitrary", "arbitrary"),
        name="mlstm",
    )(q, kt, z_v, z_o, gh, ght, gb, gbt, norm_g.reshape(1, ml), c0, n0h, m0h)
    return y, (c, n.transpose(0, 2, 1, 3), m[..., 0].transpose(0, 2, 1))


def _rows_to_tiles(tile_ref, x):
    r, d = x.shape
    s = d // LANES
    for k in range(s):
        tile_ref[pl.ds(k, r, stride=s), :] = x[:, LANES * k:LANES * (k + 1)]


def _tiles_to_rows(tile_ref, r):
    s = tile_ref.shape[0] // r
    return jnp.concatenate([tile_ref[pl.ds(k, r, stride=s), :] for k in range(s)], axis=1)


def _lane_pack(cols, lane):
    out = jnp.zeros(lane.shape, cols[0].dtype)
    for j, colv in enumerate(cols):
        out = jnp.where(lane == j, colv, out)
    return out


def _post_kernel(yhy_ref, yml_ref, x_ref, mod_ref, wo_ref, n2g_ref, rw_ref, rb_ref,
                 x1_ref, h2_ref, ti_ref, tg_ref, rk_ref, cnt_ref, carry, *, n_exp, top_k):
    @pl.when(pl.program_id(0) == 0)
    def _():
        carry[...] = jnp.zeros_like(carry)

    hy_w = yhy_ref.shape[1]
    tm = x_ref.shape[0]
    proj = _bdot(yhy_ref[...], wo_ref[:hy_w, :]) + _bdot(yml_ref[...], wo_ref[hy_w:, :])
    x1 = x_ref[...] + mod_ref[0, 2:3, :] * proj
    x1_ref[...] = x1
    h2 = x1 * lax.rsqrt(jnp.mean(x1 * x1, axis=-1, keepdims=True) + EPS) * n2g_ref[...]
    h2 = h2 * (1.0 + mod_ref[0, 4:5, :]) + mod_ref[0, 3:4, :]
    _rows_to_tiles(h2_ref, h2)

    lane = lax.broadcasted_iota(jnp.int32, (tm, LANES), 1)
    work = jnp.where(lane < n_exp, _hdot(h2, rw_ref[...]) + rb_ref[...], -jnp.inf)
    vals, idxs, hots = [], [], []
    for _ in range(top_k):
        mx = jnp.max(work, axis=-1, keepdims=True)
        idx = jnp.min(jnp.where(work == mx, lane, LANES), axis=-1, keepdims=True)
        hot = lane == idx
        vals.append(mx)
        idxs.append(idx)
        hots.append(hot)
        work = jnp.where(hot, -jnp.inf, work)
    exps = [jnp.exp(v - vals[0]) for v in vals]
    tot = functools.reduce(lambda a, b: a + b, exps)
    ti_ref[...] = _lane_pack(idxs, lane)
    tg_ref[...] = _lane_pack([e / tot for e in exps], lane)

    hot_sum = functools.reduce(lambda a, b: a + b, [h.astype(F32) for h in hots])
    row = lax.broadcasted_iota(jnp.int32, (tm, tm), 0)
    col = lax.broadcasted_iota(jnp.int32, (tm, tm), 1)
    before = _bdot((col < row).astype(F32), hot_sum) + carry[...]
    ranks = [jnp.sum(jnp.where(h, before, 0.0), axis=-1, keepdims=True).astype(jnp.int32) for h in hots]
    rk_ref[...] = _lane_pack(ranks, lane)
    carry[...] = carry[...] + jnp.sum(hot_sum, axis=0, keepdims=True)
    cnt_ref[...] = jnp.broadcast_to(carry[...], cnt_ref.shape)


def _post(y_hy, y_ml, x, mods, w_out, norm2_g, r_w, r_b, n_ctx_tiles, tiles_per_lat, tm):
    t, d = x.shape
    n_exp = r_w.shape[1]
    rwp = jnp.pad(r_w, ((0, 0), (0, LANES - n_exp)))
    rbp = jnp.pad(r_b, (0, LANES - n_exp)).reshape(1, LANES)
    row = lambda wd: pl.BlockSpec((tm, wd), lambda i: (i, 0))
    const = lambda shape: pl.BlockSpec(shape, lambda i: (0, 0))
    return pl.pallas_call(
        functools.partial(_post_kernel, n_exp=n_exp, top_k=TOP_K),
        grid=(t // tm,),
        in_specs=[row(y_hy.shape[1]), row(y_ml.shape[1]), row(d),
                  pl.BlockSpec((1, 6, d), _mod_index_map(n_ctx_tiles, tiles_per_lat)),
                  const(w_out.shape), const((1, d)), const((d, LANES)), const((1, LANES))],
        out_specs=[row(d), pl.BlockSpec((tm * d // LANES, LANES), lambda i: (i, 0)),
                   row(LANES), row(LANES), row(LANES), const((SUBLANES, LANES))],
        out_shape=[jax.ShapeDtypeStruct((t, d), F32), jax.ShapeDtypeStruct((t * d // LANES, LANES), F32),
                   jax.ShapeDtypeStruct((t, LANES), jnp.int32), jax.ShapeDtypeStruct((t, LANES), F32),
                   jax.ShapeDtypeStruct((t, LANES), jnp.int32), jax.ShapeDtypeStruct((SUBLANES, LANES), F32)],
        scratch_shapes=[pltpu.VMEM((1, LANES), F32)],
        compiler_params=_cparams("arbitrary"),
        name="post",
    )(y_hy, y_ml, x, mods, w_out.astype(BF16), norm2_g.reshape(1, d), rwp, rbp)


def _dest_kernel(ti_ref, rk_ref, ps_ref, d_ref, *, top_k):
    lane = lax.broadcasted_iota(jnp.int32, ti_ref.shape, 1)
    ti, rk = ti_ref[...], rk_ref[...]
    cols = []
    for j in range(top_k):
        start = jnp.sum(jnp.where(lane == ti[:, j:j + 1], ps_ref[...], 0.0), axis=-1, keepdims=True)
        cols.append(start.astype(jnp.int32) + rk[:, j:j + 1])
    d_ref[...] = _lane_pack(cols, lane)


def _dest_rows(ti, rk, pad_start, tm):
    t = ti.shape[0]
    row = pl.BlockSpec((tm, LANES), lambda i: (i, 0))
    return pl.pallas_call(
        functools.partial(_dest_kernel, top_k=TOP_K),
        grid=(t // tm,),
        in_specs=[row, row, pl.BlockSpec((1, LANES), lambda i: (0, 0))],
        out_specs=row,
        out_shape=jax.ShapeDtypeStruct((t, LANES), jnp.int32),
        compiler_params=_cparams("arbitrary"),
        name="dest_rows",
    )(ti, rk, pad_start)


def _dispatch_kernel(dest_ref, h_ref, zero_ref, xs_ref, sem, *, top_k, s):
    del zero_ref
    tt = dest_ref.shape[2] // top_k

    def issue(t, carry):
        src = pl.multiple_of(t * s, s)
        for j in range(top_k):
            dst = pl.multiple_of(dest_ref[0, 0, t * top_k + j] * s, s)
            pltpu.make_async_copy(h_ref.at[pl.ds(src, s)], xs_ref.at[pl.ds(dst, s)], sem).start()
        return carry
    lax.fori_loop(0, tt, issue, 0, unroll=2)
    for _ in range(top_k):
        pltpu.make_async_copy(h_ref, xs_ref.at[pl.ds(0, tt * s)], sem).wait()


def _dispatch(dest, h2t, n_rows, tt):
    s = h2t.shape[0] * TOP_K // dest.size
    zeros = jnp.zeros((n_rows * s, LANES), F32)
    return pl.pallas_call(
        functools.partial(_dispatch_kernel, top_k=TOP_K, s=s),
        grid=(dest.shape[0],),
        in_specs=[pl.BlockSpec((1, 1, tt * TOP_K), lambda i: (i, 0, 0), memory_space=pltpu.SMEM),
                  pl.BlockSpec((tt * s, LANES), lambda i: (i, 0)),
                  pl.BlockSpec(memory_space=pl.ANY)],
        out_specs=pl.BlockSpec(memory_space=pl.ANY),
        out_shape=jax.ShapeDtypeStruct((n_rows * s, LANES), F32),
        scratch_shapes=[pltpu.SemaphoreType.DMA(())],
        input_output_aliases={2: 0},
        compiler_params=_cparams("arbitrary"),
        name="dispatch",
    )(dest, h2t, zeros)


def _deinterleave_table():
    p = np.zeros((2 * LANES, 2 * LANES), np.float32)
    j = np.arange(LANES)
    p[2 * j, j] = 1.0
    p[2 * j + 1, LANES + j] = 1.0
    return p


def _ffn_kernel(be_ref, nu_ref, xs_ref, wgu_ref, bg_ref, bl_ref, wd_ref, bd_ref, perm_ref, ys_ref,
                wg_buf, wl_buf, wd_buf, acc_buf, *, ft):
    i = pl.program_id(0)
    f = wg_buf.shape[1]
    rows = acc_buf.shape[0]
    live = i < nu_ref[0]

    @pl.when(live & ((i == 0) | (be_ref[i] != be_ref[jnp.maximum(i - 1, 0)])))
    def _():
        for j in range(f // LANES):
            blk = wgu_ref[0, :, 2 * LANES * j:2 * LANES * (j + 1)].astype(BF16)
            split = jnp.dot(blk, perm_ref[...], preferred_element_type=F32)
            wg_buf[:, LANES * j:LANES * (j + 1)] = split[:, :LANES].astype(BF16)
            wl_buf[:, LANES * j:LANES * (j + 1)] = split[:, LANES:].astype(BF16)
        wd_buf[...] = wd_ref[0].astype(BF16)

    @pl.when(live)
    def _():
        x = _tiles_to_rows(xs_ref, rows).astype(BF16)
        for j, f0 in enumerate(range(0, f, ft)):
            g = jnp.dot(x, wg_buf[:, f0:f0 + ft], preferred_element_type=F32) + bg_ref[0, :, f0:f0 + ft]
            lin = jnp.dot(x, wl_buf[:, f0:f0 + ft], preferred_element_type=F32) + bl_ref[0, :, f0:f0 + ft]
            gate = jnp.minimum(g, SWIGLU_LIMIT)
            lin = jnp.clip(lin, -SWIGLU_LIMIT, SWIGLU_LIMIT)
            act = (lin + 1.0) * gate * jax.nn.sigmoid(SWIGLU_ALPHA * gate)
            part = jnp.dot(act.astype(BF16), wd_buf[f0:f0 + ft, :], preferred_element_type=F32)
            if j == 0:
                acc_buf[...] = part + bd_ref[0]
            else:
                acc_buf[...] += part
        _rows_to_tiles(ys_ref, acc_buf[...])

    @pl.when(jnp.logical_not(live))
    def _():
        ys_ref[...] = jnp.zeros_like(ys_ref)


def _ffn(block_e, n_used, xs, w_gu, b_gu, w_d, b_d, rows):
    n_exp, d, f2 = w_gu.shape
    f = f2 // 2
    s = d // LANES
    n_rows = xs.shape[0] // s
    live = lambda i, nu: jnp.minimum(i, nu[0] - 1)
    wmap = lambda i, be, nu: (be[live(i, nu)], 0, 0)
    perm = jnp.asarray(_deinterleave_table(), BF16)
    return pl.pallas_call(
        functools.partial(_ffn_kernel, ft=min(f, 512)),
        grid_spec=pltpu.PrefetchScalarGridSpec(
            num_scalar_prefetch=2,
            grid=(n_rows // rows,),
            in_specs=[pl.BlockSpec((rows * s, LANES), lambda i, be, nu: (live(i, nu), 0)),
                      pl.BlockSpec((1, d, f2), wmap),
                      pl.BlockSpec((1, 1, f), wmap), pl.BlockSpec((1, 1, f), wmap),
                      pl.BlockSpec((1, f, d), wmap), pl.BlockSpec((1, 1, d), wmap),
                      pl.BlockSpec(perm.shape, lambda i, be, nu: (0, 0))],
            out_specs=pl.BlockSpec((rows * s, LANES), lambda i, be, nu: (i, 0)),
            scratch_shapes=[pltpu.VMEM((d, f), BF16), pltpu.VMEM((d, f), BF16), pltpu.VMEM((f, d), BF16),
                            pltpu.VMEM((rows, d), F32)]),
        out_shape=jax.ShapeDtypeStruct(xs.shape, F32),
        compiler_params=_cparams("arbitrary"),
        name="expert_ffn",
    )(block_e, n_used, xs, w_gu, b_gu[:, 0::2].reshape(n_exp, 1, f), b_gu[:, 1::2].reshape(n_exp, 1, f),
      w_d, b_d.reshape(n_exp, 1, d), perm)


def _combine_kernel(dest_ref, next_ref, tg_ref, x1_ref, mod_ref, fg_ref, ys_ref, o_ref, ybuf, sem, *,
                    top_k, final_norm):
    i = pl.program_id(0)
    tt = x1_ref.shape[0]
    s = ybuf.shape[2] // tt
    slot = i % 2

    def gather(d_ref, to):
        def issue(t, carry):
            dst = pl.multiple_of(t * s, s)
            for j in range(top_k):
                src = pl.multiple_of(d_ref[0, 0, t * top_k + j] * s, s)
                pltpu.make_async_copy(ys_ref.at[pl.ds(src, s)], ybuf.at[to, j, pl.ds(dst, s)], sem.at[to]).start()
            return carry
        lax.fori_loop(0, tt, issue, 0, unroll=2)

    @pl.when(i == 0)
    def _():
        gather(dest_ref, 0)

    @pl.when(i + 1 < pl.num_programs(0))
    def _():
        gather(next_ref, 1 - slot)

    for j in range(top_k):
        pltpu.make_async_copy(ys_ref.at[pl.ds(0, tt * s)], ybuf.at[slot, j], sem.at[slot]).wait()
    tg = tg_ref[...]
    moe = tg[:, 0:1] * _tiles_to_rows(ybuf.at[slot, 0], tt)
    for j in range(1, top_k):
        moe = moe + tg[:, j:j + 1] * _tiles_to_rows(ybuf.at[slot, j], tt)
    x2 = x1_ref[...] + mod_ref[0, 5:6, :] * moe
    if final_norm:
        x2 = x2 * lax.rsqrt(jnp.mean(x2 * x2, axis=-1, keepdims=True) + EPS) * fg_ref[...]
    o_ref[...] = x2


def _combine(dest, tg, x1, mods, final_g, ys, n_ctx_tiles, tiles_per_lat, tt, final_norm):
    t, d = x1.shape
    steps = t // tt
    return pl.pallas_call(
        functools.partial(_combine_kernel, top_k=TOP_K, final_norm=final_norm),
        grid=(steps,),
        in_specs=[pl.BlockSpec((1, 1, tt * TOP_K), lambda i: (i, 0, 0), memory_space=pltpu.SMEM),
                  pl.BlockSpec((1, 1, tt * TOP_K), lambda i: (jnp.minimum(i + 1, steps - 1), 0, 0),
                               memory_space=pltpu.SMEM),
                  pl.BlockSpec((tt, LANES), lambda i: (i, 0)),
                  pl.BlockSpec((tt, d), lambda i: (i, 0)),
                  pl.BlockSpec((1, 6, d), _mod_index_map(n_ctx_tiles, tiles_per_lat)),
                  pl.BlockSpec((1, d), lambda i: (0, 0)),
                  pl.BlockSpec(memory_space=pl.ANY)],
        out_specs=pl.BlockSpec((tt, d), lambda i: (i, 0)),
        out_shape=jax.ShapeDtypeStruct((t, d), F32),
        scratch_shapes=[pltpu.VMEM((2, TOP_K, tt * d // LANES, LANES), F32), pltpu.SemaphoreType.DMA((2,))],
        compiler_params=_cparams("arbitrary"),
        name="combine",
    )(dest, dest, tg, x1, mods, final_g.reshape(1, d), ys)


def _moe_plan(counts, rows, n_blocks):
    n_exp = counts.shape[0]
    padded = (counts + rows - 1) // rows * rows
    pad_end = jnp.cumsum(padded)
    block_row = jnp.arange(n_blocks, dtype=jnp.int32) * rows
    block_e = jnp.minimum(jnp.sum(pad_end[None, :] <= block_row[:, None], axis=1), n_exp - 1).astype(jnp.int32)
    n_used = (pad_end[-1:] // rows).astype(jnp.int32)
    pad_start = jnp.pad((pad_end - padded).astype(F32), (0, LANES - n_exp)).reshape(1, LANES)
    return pad_start, block_e, n_used


def _sequence_mixers(z_hy, z_qk, z_v, z_o, z_g, lw, state, row_w):
    (hy_cw, hy_cb, filt_params, hy_b, ml_cw, ml_cb, ml_gb, ml_ng) = lw
    y_hy = _hyena(z_hy, hy_cw, hy_cb, filt_params, hy_b, row_w)
    ml_w = z_v.shape[-1]
    q = _short_conv(z_qk, ml_cw, ml_cb, row_w, silu=True, col0=0, ncols=ml_w)
    kt = _short_conv(z_qk, ml_cw, ml_cb, row_w, silu=True, col0=ml_w, ncols=ml_w, transpose=True)
    y_ml, st = _mlstm(q, kt, z_v, z_o, z_g, ml_gb, ml_ng, *state)
    return y_hy, y_ml, st


def kernel(x_prompt, x_sample, state_mlstm_C, state_mlstm_n, state_mlstm_m, c, c_ctx, ada_w, ada_b, norm1_g,
           w_in, hy_conv_w, hy_conv_b, filt_w1, filt_b1, filt_w2, filt_b2, filt_w3, filt_freq, hy_bias,
           ml_conv_w, ml_conv_b, ml_gate_b, ml_norm_g, w_out, norm2_g, router_w, router_b, moe_w_gu,
           moe_b_gu, moe_w_down, moe_b_down, final_g):
    bp, lp, d = x_prompt.shape
    bs, ls, _ = x_sample.shape
    depth = ada_w.shape[0]
    heads = ml_gate_b.shape[-1]
    hy_w = hy_bias.shape[-1]
    ml_w = ml_norm_g.shape[-1]
    dh = ml_w // heads
    n_exp = router_w.shape[-1]
    t_ctx, t_lat = bp * lp, bs * ls
    t = t_ctx + t_lat
    tm = min(ROW_TILE, lp)
    tt = min(TOK_TILE, lp)
    ng = N_DIR * 2 * heads
    seg_widths = (3 * hy_w, 2 * ml_w, ml_w, ml_w)
    n_main = 3 * hy_w + 4 * ml_w
    n_blocks = -(-(t * TOP_K) // MOE_ROWS) + n_exp

    x = jnp.concatenate([x_prompt.reshape(t_ctx, d), x_sample.reshape(t_lat, d)], axis=0)
    cond = jnp.concatenate([c_ctx[None], c, jnp.zeros((SUBLANES - 1 - bs, d), F32)], axis=0)
    zero_state = (jnp.zeros((bp, N_DIR, heads, dh, dh), F32), jnp.zeros((bp, N_DIR, heads, dh), F32),
                  jnp.zeros((bp, N_DIR, heads), F32))
    new_c, new_n, new_m = [], [], []
    for l in range(depth):
        mods = _ada(cond, ada_w[l], ada_b[l]).reshape(SUBLANES, 6, d)
        w_gate = jnp.pad(w_in[l][:, n_main:], ((0, 0), (0, LANES - ng)))
        z = _inproj(x, mods, norm1_g[l], w_in[l][:, :n_main].astype(BF16), w_gate, seg_widths,
                    t_ctx // tm, ls // tm, tm)
        lw = (hy_conv_w[l], hy_conv_b[l],
              (filt_w1[l], filt_b1[l], filt_w2[l], filt_b2[l], filt_w3[l], filt_freq[l]), hy_bias[l],
              ml_conv_w[l], ml_conv_b[l], ml_gate_b[l], ml_norm_g[l])
        z_ctx = [a[:t_ctx].reshape(bp, lp, a.shape[1]) for a in z]
        z_lat = [a[t_ctx:].reshape(bs, ls, a.shape[1]) for a in z]
        cache = (state_mlstm_C[:, l], state_mlstm_n[:, l], state_mlstm_m[:, l])
        hy_c, ml_c, st = _sequence_mixers(*z_ctx, lw, zero_state, lp)
        hy_l, ml_l, _ = _sequence_mixers(*z_lat, lw, cache, GRID_W)
        new_c.append(st[0])
        new_n.append(st[1])
        new_m.append(st[2])
        y_hy = jnp.concatenate([hy_c.reshape(t_ctx, hy_w), hy_l.reshape(t_lat, hy_w)], axis=0)
        y_ml = jnp.concatenate([ml_c.reshape(t_ctx, ml_w), ml_l.reshape(t_lat, ml_w)], axis=0)
        x1, h2, ti, tg, rk, cnt = _post(y_hy, y_ml, x, mods, w_out[l], norm2_g[l], router_w[l], router_b[l],
                                        t_ctx // tm, ls // tm, tm)
        pad_start, block_e, n_used = _moe_plan(cnt[0, :n_exp].astype(jnp.int32), MOE_ROWS, n_blocks)
        dest = _dest_rows(ti, rk, pad_start, tm)[:, :TOP_K].reshape(t // tt, 1, tt * TOP_K)
        xs = _dispatch(dest, h2, n_blocks * MOE_ROWS, tt)
        ys = _ffn(block_e, n_used, xs, moe_w_gu[l], moe_b_gu[l], moe_w_down[l], moe_b_down[l], MOE_ROWS)
        x = _combine(dest, tg, x1, mods, final_g, ys, t_ctx // tt, ls // tt, tt, final_norm=l == depth - 1)
    y_prompt = x[:t_ctx].reshape(bp, lp, d)
    y_sample = x[t_ctx:].reshape(bs, ls, d)
    return (y_prompt, y_sample, jnp.stack(new_c, axis=1), jnp.stack(new_n, axis=1), jnp.stack(new_m, axis=1))
```

```python
import functools
import math

import numpy as np
import jax
import jax.numpy as jnp
from jax import lax
from jax.experimental import pallas as pl
from jax.experimental.pallas import tpu as pltpu

F32 = jnp.float32
BF16 = jnp.bfloat16
HIGHEST = lax.Precision.HIGHEST
EPS = 1e-6

LANES = 128
SUBLANES = 8
VMEM_LIMIT_BYTES = 56 * 1024 * 1024

GRID_W = 64
ML_HEADS = 4
N_DIR = 2
HY_ORDER = 2
FILT_BANDS = 8
DECAY_TARGET = 1e-2
FAST_DECAY_PCT = 0.3
SLOW_DECAY_PCT = 1.5
TOP_K = 4
SWIGLU_LIMIT = 7.0
SWIGLU_ALPHA = 1.702

FFT_N2 = 128
MLSTM_CHUNK = 128
MOE_ROWS = 512
ROW_TILE = 512
TOK_TILE = 256


def _cparams(*sem):
    return pltpu.CompilerParams(dimension_semantics=sem, vmem_limit_bytes=VMEM_LIMIT_BYTES)


def _lane_tile(c, cap):
    return max(t for t in range(LANES, min(c, cap) + 1, LANES) if c % t == 0)


def _bdot(a, b):
    return jnp.dot(a.astype(BF16), b.astype(BF16), preferred_element_type=F32)


def _hdot(a, b):
    return jnp.dot(a, b, precision=HIGHEST, preferred_element_type=F32)


def _ada_kernel(c_ref, w_ref, b_ref, o_ref):
    c = c_ref[...]
    o_ref[...] = _hdot(c * jax.nn.sigmoid(c), w_ref[...]) + b_ref[...]


def _ada(cond, w, b):
    r, d = cond.shape
    n = w.shape[1]
    tn = _lane_tile(n, 1024)
    return pl.pallas_call(
        _ada_kernel,
        grid=(n // tn,),
        in_specs=[pl.BlockSpec((r, d), lambda j: (0, 0)),
                  pl.BlockSpec((d, tn), lambda j: (0, j)),
                  pl.BlockSpec((1, tn), lambda j: (0, j))],
        out_specs=pl.BlockSpec((r, tn), lambda j: (0, j)),
        out_shape=jax.ShapeDtypeStruct((r, n), F32),
        compiler_params=_cparams("arbitrary"),
        name="ada",
    )(cond, w, b.reshape(1, n))


def _mod_index_map(mod0, mod_step, tiles_per_seq):
    def index_map(i):
        return (mod0 + (i // tiles_per_seq) * mod_step, 0, 0)
    return index_map


def _inproj_kernel(x_ref, mod_ref, g_ref, w_ref, wg_ref, *out_refs, offsets):
    x = x_ref[...]
    h = x * lax.rsqrt(jnp.mean(x * x, axis=-1, keepdims=True) + EPS) * g_ref[...]
    h = h * (1.0 + mod_ref[0, 1:2, :]) + mod_ref[0, 0:1, :]
    hb = h.astype(BF16)
    for o_ref, (lo, hi) in zip(out_refs[:-1], offsets):
        o_ref[...] = jnp.dot(hb, w_ref[:, lo:hi], preferred_element_type=F32)
    h_lo = (h - hb.astype(F32)).astype(BF16)
    g = jnp.dot(hb, wg_ref[...], preferred_element_type=F32)
    out_refs[-1][...] = (g[:, :LANES] + g[:, LANES:]
                         + jnp.dot(h_lo, wg_ref[:, :LANES], preferred_element_type=F32))


def _inproj(x, mods, norm_g, w_main, w_gate, seg_widths, mod_map, tm):
    t, d = x.shape
    offsets, lo = [], 0
    for wd in seg_widths:
        offsets.append((lo, lo + wd))
        lo += wd
    wg_hi = w_gate.astype(BF16)
    wg = jnp.concatenate([wg_hi, (w_gate - wg_hi.astype(F32)).astype(BF16)], axis=1)
    widths = tuple(seg_widths) + (LANES,)
    return pl.pallas_call(
        functools.partial(_inproj_kernel, offsets=tuple(offsets)),
        grid=(t // tm,),
        in_specs=[pl.BlockSpec((tm, d), lambda i: (i, 0)),
                  pl.BlockSpec((1, 6, d), mod_map),
                  pl.BlockSpec((1, d), lambda i: (0, 0)),
                  pl.BlockSpec(w_main.shape, lambda i: (0, 0)),
                  pl.BlockSpec(wg.shape, lambda i: (0, 0))],
        out_specs=[pl.BlockSpec((tm, wd), lambda i: (i, 0)) for wd in widths],
        out_shape=[jax.ShapeDtypeStruct((t, wd), F32) for wd in widths],
        compiler_params=_cparams("arbitrary"),
        name="inproj",
    )(x, mods, norm_g.reshape(1, d), w_main, wg)


def _short_conv_kernel(x_ref, w_ref, b_ref, o_ref, *, row_w, silu, transpose):
    x = x_ref[0]
    l = x.shape[0]
    pos = lax.broadcasted_iota(jnp.int32, x.shape, 0) % row_w
    prev = jnp.where(pos == 0, 0.0, pltpu.roll(x, 1, 0))
    nxt = jnp.where(pos == row_w - 1, 0.0, pltpu.roll(x, l - 1, 0))
    y = prev * w_ref[0:1, :] + x * w_ref[1:2, :] + nxt * w_ref[2:3, :] + b_ref[...]
    if silu:
        y = y * jax.nn.sigmoid(y)
    o_ref[0] = y.T if transpose else y


def _short_conv(x, w, b, row_w, silu, col0=0, ncols=None, transpose=False):
    bsz, l, c = x.shape
    ncols = c if ncols is None else ncols
    ct = _lane_tile(math.gcd(ncols, col0) if col0 else ncols, LANES if transpose else 256)
    j0 = col0 // ct
    out_shape, out_block, out_map = (bsz, l, ncols), (1, l, ct), lambda i, j: (i, 0, j)
    if transpose:
        out_shape, out_block, out_map = (bsz, ncols, l), (1, ct, l), lambda i, j: (i, j, 0)
    return pl.pallas_call(
        functools.partial(_short_conv_kernel, row_w=row_w, silu=silu, transpose=transpose),
        grid=(bsz, ncols // ct),
        in_specs=[pl.BlockSpec((1, l, ct), lambda i, j: (i, 0, j0 + j)),
                  pl.BlockSpec((3, ct), lambda i, j: (0, j0 + j)),
                  pl.BlockSpec((1, ct), lambda i, j: (0, j0 + j))],
        out_specs=pl.BlockSpec(out_block, out_map),
        out_shape=jax.ShapeDtypeStruct(out_shape, F32),
        compiler_params=_cparams("arbitrary", "arbitrary"),
        name="short_conv",
    )(x, w, b.reshape(1, c))


def _dft_direct_tables(l):
    n = 2 * l
    k = np.arange(n)[:, None].astype(np.float64)
    t = np.arange(n)[None, :].astype(np.float64)
    ang = 2.0 * np.pi * ((k * t) % n) / n
    cm, sm = np.cos(ang), np.sin(ang)
    fwd = np.block([[cm[:, :l], sm[:, :l]], [-sm[:, :l], cm[:, :l]]])
    filt = np.concatenate([cm, -sm], axis=0)
    return fwd, filt


def _dft_two_level_tables(l, n2):
    n = 2 * l
    n1 = n // n2
    k1 = np.arange(n1)[:, None].astype(np.float64)
    a = np.arange(n1)[None, :].astype(np.float64)
    ang1 = 2.0 * np.pi * ((k1 * a) % n1) / n1
    c1, s1 = np.cos(ang1), np.sin(ang1)
    h = n1 // 2
    m1 = np.block([[c1[:, :h], s1[:, :h]], [-s1[:, :h], c1[:, :h]]])
    m1f = np.concatenate([c1, -s1], axis=0)
    kk = (np.arange(n1)[:, None, None] + n1 * np.arange(n2)[None, :, None]).astype(np.float64)
    b = np.arange(n2)[None, None, :].astype(np.float64)
    ang = 2.0 * np.pi * ((kk * b) % n) / n
    cg, sg = np.cos(ang), np.sin(ang)
    gt = np.concatenate([np.concatenate([cg, sg], axis=2),
                         np.concatenate([-sg, cg], axis=2)], axis=1)
    return m1, m1f, gt


def _circular_lag(n0, rows, l):
    n = n0 + lax.broadcasted_iota(jnp.int32, (rows, 1), 0)
    t = jnp.where(n < l, n, 2 * l - n).astype(F32)
    return n, t, t / float(max(l - 1, 1))


def _filter_hidden_kernel(bandv_ref, w1_ref, b1_ref, w2_ref, b2_ref, freq_ref, o_ref, *, l):
    rows = o_ref.shape[0]
    _, t, t01 = _circular_lag(pl.program_id(0) * rows, rows, l)
    lane = lax.broadcasted_iota(jnp.int32, (rows, LANES), 1)
    ang = (2.0 * math.pi / l) * t * bandv_ref[...]
    feats = jnp.where(lane == 0, t01,
                      jnp.where(lane <= FILT_BANDS, jnp.cos(ang),
                                jnp.where(lane <= 2 * FILT_BANDS, -jnp.sin(ang), 0.0)))
    fr = freq_ref[...]
    h = jnp.sin(fr * (_hdot(feats, w1_ref[...]) + b1_ref[...]))
    o_ref[...] = jnp.sin(fr * (_hdot(h, w2_ref[...]) + b2_ref[...]))


def _filter_hidden(l, f_w1, f_b1, f_w2, f_b2, f_freq):
    emb, hid = f_w1.shape
    n = 2 * l
    rows = min(n, 512)
    bands = jnp.linspace(1e-4, FILT_BANDS - 1, FILT_BANDS, dtype=F32)
    bandv = jnp.zeros((1, LANES), F32).at[0, 1:1 + FILT_BANDS].set(bands)
    bandv = bandv.at[0, 1 + FILT_BANDS:1 + 2 * FILT_BANDS].set(bands)
    w1p = jnp.zeros((LANES, hid), F32).at[:emb].set(f_w1)
    c0 = lambda i: (0, 0)
    return pl.pallas_call(
        functools.partial(_filter_hidden_kernel, l=l),
        grid=(n // rows,),
        in_specs=[pl.BlockSpec((1, LANES), c0), pl.BlockSpec((LANES, hid), c0), pl.BlockSpec((1, hid), c0),
                  pl.BlockSpec((hid, hid), c0), pl.BlockSpec((1, hid), c0), pl.BlockSpec((1, hid), c0)],
        out_specs=pl.BlockSpec((rows, hid), lambda i: (i, 0)),
        out_shape=jax.ShapeDtypeStruct((n, hid), F32),
        compiler_params=_cparams("arbitrary"),
        name="filter_hidden",
    )(bandv, w1p, f_b1.reshape(1, hid), f_w2, f_b2.reshape(1, hid), f_freq.reshape(1, hid))


def _filter_rows(n0, rows, l, hid_ref, w3_ref, delta_ref):
    n, _, t01 = _circular_lag(n0, rows, l)
    h = hid_ref[pl.ds(n0, rows), :]
    hf = _bdot(h, w3_ref[0, 0])
    hb = _bdot(h, w3_ref[0, 1])
    window = jnp.exp(-t01 * delta_ref[...])
    return jnp.where(n < l, hf, jnp.where(n > l, hb, 0.0)) * window


def _filter_direct_kernel(hid_ref, w3_ref, delta_ref, ff_ref, h_ref, *, l):
    hc = _filter_rows(0, 2 * l, l, hid_ref, w3_ref, delta_ref)
    h_ref[0] = _bdot(ff_ref[...], hc) * (1.0 / (2 * l))


def _filter_two_level_kernel(hid_ref, w3_ref, delta_ref, m1f_ref, gt_ref, h_ref, hc_buf, a_buf, *, l, n2, rows):
    n = 2 * l
    n1 = n // n2

    def fill(i, carry):
        r0 = pl.multiple_of(i * rows, rows)
        hc_buf[pl.ds(r0, rows), :] = _filter_rows(r0, rows, l, hid_ref, w3_ref, delta_ref)
        return carry
    lax.fori_loop(0, n // rows, fill, 0)

    def step1(b, carry):
        col = hc_buf[pl.ds(b, n1, stride=n2), :]
        a = _bdot(m1f_ref[...], col)
        a_buf[pl.ds(b, n1, stride=2 * n2), :] = a[:n1]
        a_buf[pl.ds(n2 + b, n1, stride=2 * n2), :] = a[n1:]
        return carry
    lax.fori_loop(0, n2, step1, 0, unroll=8)

    def step2(k1, carry):
        r0 = pl.multiple_of(k1 * 2 * n2, 2 * n2)
        h_ref[0, k1] = (_bdot(gt_ref[k1], a_buf[pl.ds(r0, 2 * n2), :]) * (1.0 / n)).astype(h_ref.dtype)
        return carry
    lax.fori_loop(0, n1, step2, 0, unroll=4)


def _filter_tail_inputs(hy_w, f_w3):
    hid = f_w3.shape[0]
    w3 = f_w3.reshape(hid, HY_ORDER, N_DIR, hy_w).transpose(1, 2, 0, 3)
    max_decay = math.log(DECAY_TARGET) / FAST_DECAY_PCT
    min_decay = math.log(DECAY_TARGET) / SLOW_DECAY_PCT
    deltas = jnp.abs(jnp.linspace(min_decay, max_decay, hy_w, dtype=F32)).reshape(1, hy_w)
    return w3, deltas


def _filter_spectrum_direct(l, hy_w, filt_params, ff):
    f_w1, f_b1, f_w2, f_b2, f_w3, f_freq = filt_params
    hidden = _filter_hidden(l, f_w1, f_b1, f_w2, f_b2, f_freq)
    w3, deltas = _filter_tail_inputs(hy_w, f_w3)
    hid = f_w3.shape[0]
    n = 2 * l
    return pl.pallas_call(
        functools.partial(_filter_direct_kernel, l=l),
        grid=(HY_ORDER,),
        in_specs=[pl.BlockSpec((n, hid), lambda o: (0, 0)),
                  pl.BlockSpec((1, N_DIR, hid, hy_w), lambda o: (o, 0, 0, 0)),
                  pl.BlockSpec((1, hy_w), lambda o: (0, 0)),
                  pl.BlockSpec((2 * n, n), lambda o: (0, 0))],
        out_specs=pl.BlockSpec((1, 2 * n, hy_w), lambda o: (o, 0, 0)),
        out_shape=jax.ShapeDtypeStruct((HY_ORDER, 2 * n, hy_w), F32),
        compiler_params=_cparams("arbitrary"),
        name="filter_direct",
    )(hidden, w3, deltas, ff)


def _filter_spectrum_two_level(l, hy_w, filt_params, m1f, gt, ct):
    f_w1, f_b1, f_w2, f_b2, f_w3, f_freq = filt_params
    hidden = _filter_hidden(l, f_w1, f_b1, f_w2, f_b2, f_freq)
    w3, deltas = _filter_tail_inputs(hy_w, f_w3)
    hid = f_w3.shape[0]
    n = 2 * l
    n2 = FFT_N2
    n1 = n // n2
    return pl.pallas_call(
        functools.partial(_filter_two_level_kernel, l=l, n2=n2, rows=min(n, 512)),
        grid=(HY_ORDER, hy_w // ct),
        in_specs=[pl.BlockSpec((n, hid), lambda o, j: (0, 0)),
                  pl.BlockSpec((1, N_DIR, hid, ct), lambda o, j: (o, 0, 0, j)),
                  pl.BlockSpec((1, ct), lambda o, j: (0, j)),
                  pl.BlockSpec((2 * n1, n1), lambda o, j: (0, 0)),
                  pl.BlockSpec((n1, 2 * n2, 2 * n2), lambda o, j: (0, 0, 0))],
        out_specs=pl.BlockSpec((1, n1, 2 * n2, ct), lambda o, j: (o, 0, 0, j)),
        out_shape=jax.ShapeDtypeStruct((HY_ORDER, n1, 2 * n2, hy_w), BF16),
        scratch_shapes=[pltpu.VMEM((n, ct), F32), pltpu.VMEM((n1 * 2 * n2, ct), F32)],
        compiler_params=_cparams("arbitrary", "arbitrary"),
        name="filter_two_level",
    )(hidden, w3, deltas, m1f, gt)


def _complex_mul(x, h, half):
    xr, xi = x[:half], x[half:]
    hr, hi = h[:half], h[half:]
    return jnp.concatenate([xr * hr - xi * hi, xr * hi + xi * hr], axis=0)


def _conv_direct_kernel(z_ref, gate_ref, bias_ref, h_ref, fwd_ref, inv_ref, o_ref):
    l = z_ref.shape[1]
    z = jnp.concatenate([z_ref[0], z_ref[1]], axis=0)
    x = _bdot(fwd_ref[...], z)
    y = _bdot(inv_ref[...], _complex_mul(x, h_ref[0], 2 * l))
    bias = bias_ref[0]
    o_ref[0] = gate_ref[0] * (y[:l] + bias * z_ref[0])
    o_ref[1] = gate_ref[1] * (y[l:] + bias * z_ref[1])


def _conv_two_level_kernel(z_ref, gate_ref, bias_ref, h_ref, m1_ref, m1i_ref, gt_ref, o_ref,
                           a_buf, *, n2):
    l = z_ref.shape[1]
    n1 = 2 * l // n2
    hn = n1 // 2

    def step1(b, carry):
        za = z_ref[0, pl.ds(b, hn, stride=n2), :]
        zb = z_ref[1, pl.ds(b, hn, stride=n2), :]
        a = _bdot(m1_ref[...], jnp.concatenate([za, zb], axis=0))
        a_buf[pl.ds(b, n1, stride=2 * n2), :] = a[:n1]
        a_buf[pl.ds(n2 + b, n1, stride=2 * n2), :] = a[n1:]
        return carry
    lax.fori_loop(0, n2, step1, 0, unroll=8)

    def step2(k1, carry):
        r0 = pl.multiple_of(k1 * 2 * n2, 2 * n2)
        x = _bdot(gt_ref[k1], a_buf[pl.ds(r0, 2 * n2), :])
        y = _complex_mul(x, h_ref[0, k1].astype(F32), n2).astype(BF16)
        a_buf[pl.ds(r0, 2 * n2), :] = lax.dot_general(gt_ref[k1], y, (((0,), (0,)), ((), ())),
                                                      preferred_element_type=F32)
        return carry
    lax.fori_loop(0, n1, step2, 0, unroll=4)

    def step3(b, carry):
        br = a_buf[pl.ds(b, n1, stride=2 * n2), :]
        bi = a_buf[pl.ds(n2 + b, n1, stride=2 * n2), :]
        y = _bdot(m1i_ref[...], jnp.concatenate([br, bi], axis=0))
        o_ref[0, pl.ds(b, hn, stride=n2), :] = y[:hn]
        o_ref[1, pl.ds(b, hn, stride=n2), :] = y[hn:]
        return carry
    lax.fori_loop(0, n2, step3, 0, unroll=8)

    bias = bias_ref[0]
    for s in range(2):
        o_ref[s] = gate_ref[s] * (o_ref[s] + bias * z_ref[s])


def _long_conv_gated(u, z, z_col, gate_col, spectrum, order, bias, tables, ct):
    bsz, l, _ = u.shape
    c = spectrum.shape[-1]
    nct = c // ct
    zspec = pl.BlockSpec((2, l, ct), lambda i, j: (i, 0, z_col * nct + j))
    gspec = pl.BlockSpec((2, l, ct), lambda i, j: (i, 0, gate_col * nct + j))
    bspec = pl.BlockSpec((1, 1, ct), lambda i, j: (order, 0, j))
    ospec = pl.BlockSpec((2, l, ct), lambda i, j: (i, 0, j))
    out_shape = jax.ShapeDtypeStruct((bsz, l, c), F32)
    bias3 = bias.reshape(HY_ORDER, 1, c)
    if len(tables) == 2:
        fwd, inv = tables
        n = 2 * l
        return pl.pallas_call(
            _conv_direct_kernel,
            grid=(bsz // 2, nct),
            in_specs=[zspec, gspec, bspec,
                      pl.BlockSpec((1, 2 * n, ct), lambda i, j: (order, 0, j)),
                      pl.BlockSpec(fwd.shape, lambda i, j: (0, 0)),
                      pl.BlockSpec(inv.shape, lambda i, j: (0, 0))],
            out_specs=ospec, out_shape=out_shape,
            compiler_params=_cparams("arbitrary", "arbitrary"),
            name="long_conv_direct",
        )(z, u, bias3, spectrum, fwd, inv)
    m1, m1i, gt = tables
    n2 = FFT_N2
    n1 = 2 * l // n2
    const2 = lambda i, j: (0, 0)
    const3 = lambda i, j: (0, 0, 0)
    return pl.pallas_call(
        functools.partial(_conv_two_level_kernel, n2=n2),
        grid=(nct, bsz // 2),
        in_specs=[pl.BlockSpec((2, l, ct), lambda j, i: (i, 0, z_col * nct + j)),
                  pl.BlockSpec((2, l, ct), lambda j, i: (i, 0, gate_col * nct + j)),
                  pl.BlockSpec((1, 1, ct), lambda j, i: (order, 0, j)),
                  pl.BlockSpec((1, n1, 2 * n2, ct), lambda j, i: (order, 0, 0, j)),
                  pl.BlockSpec(m1.shape, const2), pl.BlockSpec(m1i.shape, const2),
                  pl.BlockSpec(gt.shape, const3)],
        out_specs=pl.BlockSpec((2, l, ct), lambda j, i: (i, 0, j)),
        out_shape=out_shape,
        scratch_shapes=[pltpu.VMEM((n1 * 2 * n2, ct), F32)],
        compiler_params=_cparams("arbitrary", "arbitrary"),
        name="long_conv_two_level",
    )(z, u, bias3, spectrum, m1, m1i, gt)


def _hyena(z_hy, conv_w, conv_b, filt_params, hy_bias, row_w):
    bsz, l, c3 = z_hy.shape
    c = c3 // 3
    u = _short_conv(z_hy, conv_w, conv_b, row_w, silu=False)
    table = lambda a: jnp.asarray(a, F32).astype(BF16)
    if 2 * l // FFT_N2 <= 4:
        fwd, filt = _dft_direct_tables(l)
        tables = (table(fwd), table(fwd.T))
        spectrum = _filter_spectrum_direct(l, c, filt_params, table(filt))
        ct = c
    else:
        m1, m1f, gt = _dft_two_level_tables(l, FFT_N2)
        tables = (table(m1), table(m1.T), table(gt))
        ct = LANES
        spectrum = _filter_spectrum_two_level(l, c, filt_params, table(m1f), tables[2], ct)
    z1 = _long_conv_gated(u, u, 0, 1, spectrum, 0, hy_bias, tables, ct)
    return _long_conv_gated(u, z1, 0, 2, spectrum, 1, hy_bias, tables, ct)


def _log_sigmoid(x):
    return jnp.minimum(x, 0.0) - jnp.log1p(jnp.exp(-jnp.abs(x)))


def _split3_dot(a, b, split_lhs):
    x = a if split_lhs else b
    hi = x.astype(BF16)
    rest = x - hi.astype(F32)
    mid = rest.astype(BF16)
    parts = (hi, mid, (rest - mid.astype(F32)).astype(BF16))
    if split_lhs:
        return functools.reduce(lambda u, w: u + w, [jnp.dot(p, b, preferred_element_type=F32) for p in parts])
    return functools.reduce(lambda u, w: u + w, [jnp.dot(a, p, preferred_element_type=F32) for p in parts])


def _mlstm_chunk(qb, kt, vo, bq, irow, brow, btot, mask, cn, m):
    dh = qb.shape[1]
    dm = jnp.where(mask, bq - brow + irow, -jnp.inf)
    m_loc = jnp.max(dm, axis=-1, keepdims=True)
    s = jnp.dot(qb, kt.astype(BF16), preferred_element_type=F32) * jnp.exp(dm - m_loc)
    sv = jnp.dot(s.astype(BF16), vo, preferred_element_type=F32)
    gl = btot - brow + irow
    g_loc = jnp.max(gl, axis=-1, keepdims=True)
    kv = jnp.dot((kt * jnp.exp(gl - g_loc)).astype(BF16), vo, preferred_element_type=F32)
    inter = bq + m
    mj = jnp.maximum(m_loc, inter)
    w_int = jnp.exp(inter - mj)
    w_loc = jnp.exp(m_loc - mj)
    qc = jnp.dot(qb, cn.astype(BF16), preferred_element_type=F32)
    num = w_int * qc[:, :dh] + w_loc * sv[:, :dh]
    den = w_int * qc[:, dh:] + w_loc * sv[:, dh:]
    h = num / jnp.maximum(jnp.abs(den), jnp.exp(-mj))
    m_new = jnp.maximum(btot + m, g_loc)
    cn_new = jnp.exp(btot + m - m_new) * cn + jnp.exp(g_loc - m_new) * kv
    return h, cn_new, m_new


def _mlstm_kernel(q_ref, kt_ref, v_ref, o_ref, g_ref, gt_ref, gb_ref, gbt_ref, ng_ref, c0_ref, n0_ref, m0_ref,
                  y_ref, c_ref, n_ref, m_ref, hf_buf, hb_buf, cn_buf, *, chunk, heads):
    l, dh = q_ref.shape[1], q_ref.shape[2]
    nc = l // chunk
    scale = dh ** -0.5
    row = lax.broadcasted_iota(jnp.int32, (chunk, chunk), 0)
    col = lax.broadcasted_iota(jnp.int32, (chunk, chunk), 1)
    lower, upper = col <= row, col >= row
    tri_l, tri_u = lower.astype(BF16), upper.astype(BF16)
    gate_row = lax.broadcasted_iota(jnp.int32, (SUBLANES, chunk), 0)
    ones = jnp.ones((chunk, dh), BF16)

    sel_row = lax.broadcasted_iota(jnp.int32, (LANES, LANES), 0)
    head = pl.program_id(1)

    def direction(r0, d, cn, m):
        pick = (sel_row == (2 * d + 1) * heads + head).astype(BF16)
        lf = _split3_dot(_log_sigmoid(g_ref[0, pl.ds(r0, chunk), :] + gb_ref[...]), pick, split_lhs=True)
        bq = _split3_dot(tri_u if d else tri_l, lf, split_lhs=False)
        gt = gt_ref[0, 0, :, pl.ds(r0, chunk)] + gbt_ref[0]
        gt = jnp.where(gate_row % 2 == 1, _log_sigmoid(gt), gt)
        brow = _split3_dot(gt, tri_l if d else tri_u, split_lhs=True)[2 * d + 1:2 * d + 2, :]
        btot = brow[:, 0:1] if d else brow[:, chunk - 1:chunk]
        vo = jnp.concatenate([v_ref[0, pl.ds(r0, chunk), :].astype(BF16), ones], axis=1)
        return _mlstm_chunk(q_ref[0, pl.ds(r0, chunk), :].astype(BF16), kt_ref[0, :, pl.ds(r0, chunk)] * scale,
                            vo, bq, gt[2 * d:2 * d + 1, :], brow, btot, upper if d else lower, cn, m)

    for d in range(N_DIR):
        n_rep = jnp.broadcast_to(n0_ref[0, 0, d:d + 1, :], (dh, dh)).T
        cn_buf[d] = jnp.concatenate([c0_ref[0, d, 0], n_rep], axis=1)

    def body(j, carry):
        mf, mb = carry
        rf = pl.multiple_of(j * chunk, chunk)
        hf, cn, mf = direction(rf, 0, cn_buf[0], mf)
        cn_buf[0] = cn
        hf_buf[pl.ds(rf, chunk), :] = hf
        rb = pl.multiple_of((nc - 1 - j) * chunk, chunk)
        hb, cn, mb = direction(rb, 1, cn_buf[1], mb)
        cn_buf[1] = cn
        hb_buf[pl.ds(rb, chunk), :] = hb
        return mf, mb

    m_fin = lax.fori_loop(0, nc, body, (m0_ref[0, 0, 0:1, 0:1], m0_ref[0, 0, 1:2, 0:1]), unroll=2)
    for d in range(N_DIR):
        c_ref[0, d, 0] = cn_buf[d, :, :dh]
        n_ref[0, 0, d:d + 1, :] = cn_buf[d, :, dh:].T[0:1, :]
        m_ref[0, 0, d:d + 1, :] = jnp.broadcast_to(m_fin[d], (1, LANES))

    def finish(j, carry):
        r0 = pl.multiple_of(j * chunk, chunk)
        hs = hf_buf[pl.ds(r0, chunk), :] + hb_buf[pl.ds(r0, chunk), :]
        hs = hs * lax.rsqrt(jnp.mean(hs * hs, axis=-1, keepdims=True) + EPS) * ng_ref[...]
        y_ref[0, pl.ds(r0, chunk), :] = jax.nn.sigmoid(o_ref[0, pl.ds(r0, chunk), :]) * hs
        return carry
    lax.fori_loop(0, nc, finish, 0)


def _mlstm(q, kt, z_v, z_o, z_g, gate_b, norm_g, c0, n0, m0):
    bsz, l, ml = z_v.shape
    heads = gate_b.shape[-1]
    dh = ml // heads
    chunk = MLSTM_CHUNK
    assert dh == LANES and chunk == LANES and l % chunk == 0
    n_gate = N_DIR * 2
    assert z_g.shape[-1] == LANES
    g4 = z_g[..., :n_gate * heads].reshape(bsz, l, n_gate, heads)
    ght = jnp.pad(g4.transpose(0, 3, 2, 1), ((0, 0), (0, 0), (0, SUBLANES - n_gate), (0, 0)))
    gb4 = gate_b.reshape(n_gate, heads).T
    gb = jnp.pad(gate_b.reshape(1, n_gate * heads), ((0, 0), (0, LANES - n_gate * heads)))
    gbt = jnp.broadcast_to(jnp.pad(gb4, ((0, 0), (0, SUBLANES - n_gate)))[..., None], (heads, SUBLANES, LANES))
    n0h = n0.transpose(0, 2, 1, 3)
    m0h = jnp.broadcast_to(m0.transpose(0, 2, 1)[..., None], (bsz, heads, N_DIR, LANES))
    seq = pl.BlockSpec((1, l, dh), lambda b, h: (b, 0, h))
    cspec = pl.BlockSpec((1, N_DIR, 1, dh, dh), lambda b, h: (b, 0, h, 0, 0))
    sspec = pl.BlockSpec((1, 1, N_DIR, dh), lambda b, h: (b, h, 0, 0))
    mspec = pl.BlockSpec((1, 1, N_DIR, LANES), lambda b, h: (b, h, 0, 0))
    y, c, n, m = pl.pallas_call(
        functools.partial(_mlstm_kernel, chunk=chunk, heads=heads),
        grid=(bsz, heads),
        in_specs=[seq, pl.BlockSpec((1, dh, l), lambda b, h: (b, h, 0)), seq, seq,
                  pl.BlockSpec((1, l, LANES), lambda b, h: (b, 0, 0)),
                  pl.BlockSpec((1, 1, SUBLANES, l), lambda b, h: (b, h, 0, 0)),
                  pl.BlockSpec((1, LANES), lambda b, h: (0, 0)),
                  pl.BlockSpec((1, SUBLANES, LANES), lambda b, h: (h, 0, 0)),
                  pl.BlockSpec((1, dh), lambda b, h: (0, h)),
                  cspec, sspec, mspec],
        out_specs=[seq, cspec, sspec, mspec],
        out_shape=[jax.ShapeDtypeStruct((bsz, l, ml), F32),
                   jax.ShapeDtypeStruct((bsz, N_DIR, heads, dh, dh), F32),
                   jax.ShapeDtypeStruct((bsz, heads, N_DIR, dh), F32),
                   jax.ShapeDtypeStruct((bsz, heads, N_DIR, LANES), F32)],
        scratch_shapes=[pltpu.VMEM((l, dh), F32), pltpu.VMEM((l, dh), F32),
                        pltpu.VMEM((N_DIR, dh, 2 * dh), F32)],
        compiler_params=_cparams("arbitrary", "arbitrary"),
        name="mlstm",
    )(q, kt, z_v, z_o, z_g, ght, gb, gbt, norm_g.reshape(1, ml), c0, n0h, m0h)
    return y, (c, n.transpose(0, 2, 1, 3), m[..., 0].transpose(0, 2, 1))


def _rows_to_tiles(tile_ref, x):
    r, d = x.shape
    s = d // LANES
    for k in range(s):
        tile_ref[pl.ds(k, r, stride=s), :] = x[:, LANES * k:LANES * (k + 1)]


def _tiles_to_rows(tile_ref, r):
    s = tile_ref.shape[0] // r
    return jnp.concatenate([tile_ref[pl.ds(k, r, stride=s), :] for k in range(s)], axis=1)


def _lane_pack(cols, lane):
    out = jnp.zeros(lane.shape, cols[0].dtype)
    for j, colv in enumerate(cols):
        out = jnp.where(lane == j, colv, out)
    return out


def _post_kernel(yhy_ref, yml_ref, x_ref, mod_ref, wo_ref, n2g_ref, rw_ref, rb_ref, cnt0_ref,
                 x1_ref, h2_ref, ti_ref, tg_ref, rk_ref, cnt_ref, carry, *, n_exp, top_k):
    @pl.when(pl.program_id(0) == 0)
    def _():
        carry[...] = cnt0_ref[0:1, :]

    hy_w = yhy_ref.shape[1]
    tm = x_ref.shape[0]
    proj = _bdot(yhy_ref[...], wo_ref[:hy_w, :]) + _bdot(yml_ref[...], wo_ref[hy_w:, :])
    x1 = x_ref[...] + mod_ref[0, 2:3, :] * proj
    x1_ref[...] = x1
    h2 = x1 * lax.rsqrt(jnp.mean(x1 * x1, axis=-1, keepdims=True) + EPS) * n2g_ref[...]
    h2 = h2 * (1.0 + mod_ref[0, 4:5, :]) + mod_ref[0, 3:4, :]
    _rows_to_tiles(h2_ref, h2)

    lane = lax.broadcasted_iota(jnp.int32, (tm, LANES), 1)
    work = jnp.where(lane < n_exp, _hdot(h2, rw_ref[...]) + rb_ref[...], -jnp.inf)
    vals, idxs, hots = [], [], []
    for _ in range(top_k):
        mx = jnp.max(work, axis=-1, keepdims=True)
        idx = jnp.min(jnp.where(work == mx, lane, LANES), axis=-1, keepdims=True)
        hot = lane == idx
        vals.append(mx)
        idxs.append(idx)
        hots.append(hot)
        work = jnp.where(hot, -jnp.inf, work)
    exps = [jnp.exp(v - vals[0]) for v in vals]
    tot = functools.reduce(lambda a, b: a + b, exps)
    ti_ref[...] = _lane_pack(idxs, lane)
    tg_ref[...] = _lane_pack([e / tot for e in exps], lane)

    hot_sum = functools.reduce(lambda a, b: a + b, [h.astype(F32) for h in hots])
    row = lax.broadcasted_iota(jnp.int32, (tm, tm), 0)
    col = lax.broadcasted_iota(jnp.int32, (tm, tm), 1)
    before = _bdot((col < row).astype(F32), hot_sum) + carry[...]
    ranks = [jnp.sum(jnp.where(h, before, 0.0), axis=-1, keepdims=True).astype(jnp.int32) for h in hots]
    rk_ref[...] = _lane_pack(ranks, lane)
    carry[...] = carry[...] + jnp.sum(hot_sum, axis=0, keepdims=True)
    cnt_ref[...] = jnp.broadcast_to(carry[...], cnt_ref.shape)


def _post(y_hy, y_ml, x, mods, w_out, norm2_g, r_w, r_b, counts, mod_map, tm):
    t, d = x.shape
    n_exp = r_w.shape[1]
    rwp = jnp.pad(r_w, ((0, 0), (0, LANES - n_exp)))
    rbp = jnp.pad(r_b, (0, LANES - n_exp)).reshape(1, LANES)
    row = lambda wd: pl.BlockSpec((tm, wd), lambda i: (i, 0))
    const = lambda shape: pl.BlockSpec(shape, lambda i: (0, 0))
    return pl.pallas_call(
        functools.partial(_post_kernel, n_exp=n_exp, top_k=TOP_K),
        grid=(t // tm,),
        in_specs=[row(y_hy.shape[1]), row(y_ml.shape[1]), row(d),
                  pl.BlockSpec((1, 6, d), mod_map),
                  const(w_out.shape), const((1, d)), const((d, LANES)), const((1, LANES)),
                  const((SUBLANES, LANES))],
        out_specs=[row(d), pl.BlockSpec((tm * d // LANES, LANES), lambda i: (i, 0)),
                   row(LANES), row(LANES), row(LANES), const((SUBLANES, LANES))],
        out_shape=[jax.ShapeDtypeStruct((t, d), F32), jax.ShapeDtypeStruct((t * d // LANES, LANES), F32),
                   jax.ShapeDtypeStruct((t, LANES), jnp.int32), jax.ShapeDtypeStruct((t, LANES), F32),
                   jax.ShapeDtypeStruct((t, LANES), jnp.int32), jax.ShapeDtypeStruct((SUBLANES, LANES), F32)],
        scratch_shapes=[pltpu.VMEM((1, LANES), F32)],
        compiler_params=_cparams("arbitrary"),
        name="post",
    )(y_hy, y_ml, x, mods, w_out.astype(BF16), norm2_g.reshape(1, d), rwp, rbp, counts)


def _dest_kernel(ti_ref, rk_ref, ps_ref, d_ref, *, top_k):
    lane = lax.broadcasted_iota(jnp.int32, ti_ref.shape, 1)
    ti, rk = ti_ref[...], rk_ref[...]
    cols = []
    for j in range(top_k):
        start = jnp.sum(jnp.where(lane == ti[:, j:j + 1], ps_ref[...], 0.0), axis=-1, keepdims=True)
        cols.append(start.astype(jnp.int32) + rk[:, j:j + 1])
    d_ref[...] = _lane_pack(cols, lane)


def _dest_rows(ti, rk, pad_start, tm):
    t = ti.shape[0]
    row = pl.BlockSpec((tm, LANES), lambda i: (i, 0))
    return pl.pallas_call(
        functools.partial(_dest_kernel, top_k=TOP_K),
        grid=(t // tm,),
        in_specs=[row, row, pl.BlockSpec((1, LANES), lambda i: (0, 0))],
        out_specs=row,
        out_shape=jax.ShapeDtypeStruct((t, LANES), jnp.int32),
        compiler_params=_cparams("arbitrary"),
        name="dest_rows",
    )(ti, rk, pad_start)


def _dispatch_kernel(first_ref, count_ref, nu_ref, dest_ref, *refs, top_k, s, rows, tiles):
    h_refs = refs[:len(tiles)]
    xs_ref, zbuf, sem, zsem = refs[len(tiles):]
    i = pl.program_id(0)
    tt = dest_ref.shape[2] // top_k
    half = zbuf.shape[0] // s
    n_blocks = xs_ref.shape[0] // (rows * s)

    def pad_rows(e, carry, wait):
        off, n = first_ref[e], count_ref[e]
        for k in range(half.bit_length()):
            bit = half >> k

            @pl.when((n & bit) != 0)
            def _():
                dst = pl.multiple_of(off * s, s)
                copy = pltpu.make_async_copy(zbuf.at[pl.ds(0, bit * s)], xs_ref.at[pl.ds(dst, bit * s)], zsem)
                copy.wait() if wait else copy.start()
            off = off + (n & bit)
        return carry

    def spare_block(b, carry, wait):
        for part in range(2):
            dst = pl.multiple_of((b * 2 + part) * half * s, half * s)
            copy = pltpu.make_async_copy(zbuf, xs_ref.at[pl.ds(dst, half * s)], zsem)
            copy.wait() if wait else copy.start()
        return carry

    @pl.when(i == 0)
    def _():
        zbuf[...] = jnp.zeros_like(zbuf)
        for wait in (False, True):
            lax.fori_loop(0, first_ref.shape[0], functools.partial(pad_rows, wait=wait), 0)
            lax.fori_loop(nu_ref[0], n_blocks, functools.partial(spare_block, wait=wait), 0)

    def scatter(h_ref):
        def issue(t, carry):
            src = pl.multiple_of(t * s, s)
            for j in range(top_k):
                dst = pl.multiple_of(dest_ref[0, 0, t * top_k + j] * s, s)
                pltpu.make_async_copy(h_ref.at[pl.ds(src, s)], xs_ref.at[pl.ds(dst, s)], sem).start()
            return carry
        lax.fori_loop(0, tt, issue, 0, unroll=2)
        for _ in range(top_k):
            pltpu.make_async_copy(h_ref, xs_ref.at[pl.ds(0, tt * s)], sem).wait()

    lo = 0
    for h_ref, n_tiles in zip(h_refs, tiles):
        @pl.when((i >= lo) & (i < lo + n_tiles))
        def _():
            scatter(h_ref)
        lo += n_tiles


def _dispatch(dest, h2ts, pad_first, pad_count, n_used, n_blocks, rows, tt):
    s = sum(h.shape[0] for h in h2ts) * TOP_K // dest.size
    tiles = tuple(h.shape[0] // (tt * s) for h in h2ts)
    starts = [sum(tiles[:k]) for k in range(len(tiles))]
    hspec = lambda lo, n: pl.BlockSpec((tt * s, LANES), lambda i, *_: (jnp.clip(i - lo, 0, n - 1), 0))
    return pl.pallas_call(
        functools.partial(_dispatch_kernel, top_k=TOP_K, s=s, rows=rows, tiles=tiles),
        grid_spec=pltpu.PrefetchScalarGridSpec(
            num_scalar_prefetch=3,
            grid=(dest.shape[0],),
            in_specs=[pl.BlockSpec((1, 1, tt * TOP_K), lambda i, *_: (i, 0, 0), memory_space=pltpu.SMEM)]
            + [hspec(lo, n) for lo, n in zip(starts, tiles)],
            out_specs=pl.BlockSpec(memory_space=pl.ANY),
            scratch_shapes=[pltpu.VMEM((rows // 2 * s, LANES), F32),
                            pltpu.SemaphoreType.DMA(()), pltpu.SemaphoreType.DMA(())]),
        out_shape=jax.ShapeDtypeStruct((n_blocks * rows * s, LANES), F32),
        compiler_params=_cparams("arbitrary"),
        name="dispatch",
    )(pad_first, pad_count, n_used, dest, *h2ts)


def _deinterleave_table():
    p = np.zeros((2 * LANES, 2 * LANES), np.float32)
    j = np.arange(LANES)
    p[2 * j, j] = 1.0
    p[2 * j + 1, LANES + j] = 1.0
    return p


def _ffn_kernel(be_ref, nu_ref, xs_ref, wgu_ref, bg_ref, bl_ref, wd_ref, bd_ref, perm_ref, ys_ref,
                wg_buf, wl_buf, wd_buf, acc_buf, *, ft):
    i = pl.program_id(0)
    f = wg_buf.shape[1]
    rows = acc_buf.shape[0]
    live = i < nu_ref[0]

    @pl.when(live & ((i == 0) | (be_ref[i] != be_ref[jnp.maximum(i - 1, 0)])))
    def _():
        for j in range(f // LANES):
            blk = wgu_ref[0, :, 2 * LANES * j:2 * LANES * (j + 1)].astype(BF16)
            split = jnp.dot(blk, perm_ref[...], preferred_element_type=F32)
            wg_buf[:, LANES * j:LANES * (j + 1)] = split[:, :LANES].astype(BF16)
            wl_buf[:, LANES * j:LANES * (j + 1)] = split[:, LANES:].astype(BF16)
        wd_buf[...] = wd_ref[0].astype(BF16)

    @pl.when(live)
    def _():
        x = _tiles_to_rows(xs_ref, rows).astype(BF16)
        for j, f0 in enumerate(range(0, f, ft)):
            g = jnp.dot(x, wg_buf[:, f0:f0 + ft], preferred_element_type=F32) + bg_ref[0, :, f0:f0 + ft]
            lin = jnp.dot(x, wl_buf[:, f0:f0 + ft], preferred_element_type=F32) + bl_ref[0, :, f0:f0 + ft]
            gate = jnp.minimum(g, SWIGLU_LIMIT)
            lin = jnp.clip(lin, -SWIGLU_LIMIT, SWIGLU_LIMIT)
            act = (lin + 1.0) * gate * jax.nn.sigmoid(SWIGLU_ALPHA * gate)
            part = jnp.dot(act.astype(BF16), wd_buf[f0:f0 + ft, :], preferred_element_type=F32)
            if j == 0:
                acc_buf[...] = part + bd_ref[0]
            else:
                acc_buf[...] += part
        _rows_to_tiles(ys_ref, acc_buf[...])

    @pl.when(jnp.logical_not(live))
    def _():
        ys_ref[...] = jnp.zeros_like(ys_ref)


def _ffn(block_e, n_used, xs, w_gu, b_gu, w_d, b_d, rows):
    n_exp, d, f2 = w_gu.shape
    f = f2 // 2
    s = d // LANES
    n_rows = xs.shape[0] // s
    live = lambda i, nu: jnp.minimum(i, nu[0] - 1)
    wmap = lambda i, be, nu: (be[live(i, nu)], 0, 0)
    perm = jnp.asarray(_deinterleave_table(), BF16)
    return pl.pallas_call(
        functools.partial(_ffn_kernel, ft=min(f, 512)),
        grid_spec=pltpu.PrefetchScalarGridSpec(
            num_scalar_prefetch=2,
            grid=(n_rows // rows,),
            in_specs=[pl.BlockSpec((rows * s, LANES), lambda i, be, nu: (live(i, nu), 0)),
                      pl.BlockSpec((1, d, f2), wmap),
                      pl.BlockSpec((1, 1, f), wmap), pl.BlockSpec((1, 1, f), wmap),
                      pl.BlockSpec((1, f, d), wmap), pl.BlockSpec((1, 1, d), wmap),
                      pl.BlockSpec(perm.shape, lambda i, be, nu: (0, 0))],
            out_specs=pl.BlockSpec((rows * s, LANES), lambda i, be, nu: (i, 0)),
            scratch_shapes=[pltpu.VMEM((d, f), BF16), pltpu.VMEM((d, f), BF16), pltpu.VMEM((f, d), BF16),
                            pltpu.VMEM((rows, d), F32)]),
        out_shape=jax.ShapeDtypeStruct(xs.shape, F32),
        compiler_params=_cparams("arbitrary"),
        name="expert_ffn",
    )(block_e, n_used, xs, w_gu, b_gu[:, 0::2].reshape(n_exp, 1, f), b_gu[:, 1::2].reshape(n_exp, 1, f),
      w_d, b_d.reshape(n_exp, 1, d), perm)


def _combine_kernel(dest_ref, next_ref, tg_ref, x1_ref, mod_ref, fg_ref, ys_ref, o_ref, ybuf, sem, *,
                    top_k, final_norm):
    i = pl.program_id(0)
    tt = x1_ref.shape[0]
    s = ybuf.shape[2] // tt
    slot = i % 2

    def gather(d_ref, to):
        def issue(t, carry):
            dst = pl.multiple_of(t * s, s)
            for j in range(top_k):
                src = pl.multiple_of(d_ref[0, 0, t * top_k + j] * s, s)
                pltpu.make_async_copy(ys_ref.at[pl.ds(src, s)], ybuf.at[to, j, pl.ds(dst, s)], sem.at[to]).start()
            return carry
        lax.fori_loop(0, tt, issue, 0, unroll=2)

    @pl.when(i == 0)
    def _():
        gather(dest_ref, 0)

    @pl.when(i + 1 < pl.num_programs(0))
    def _():
        gather(next_ref, 1 - slot)

    for j in range(top_k):
        pltpu.make_async_copy(ys_ref.at[pl.ds(0, tt * s)], ybuf.at[slot, j], sem.at[slot]).wait()
    tg = tg_ref[...]
    moe = tg[:, 0:1] * _tiles_to_rows(ybuf.at[slot, 0], tt)
    for j in range(1, top_k):
        moe = moe + tg[:, j:j + 1] * _tiles_to_rows(ybuf.at[slot, j], tt)
    x2 = x1_ref[...] + mod_ref[0, 5:6, :] * moe
    if final_norm:
        x2 = x2 * lax.rsqrt(jnp.mean(x2 * x2, axis=-1, keepdims=True) + EPS) * fg_ref[...]
    o_ref[...] = x2


def _combine(dest, tg, x1, mods, final_g, ys, mod_map, tt, final_norm):
    t, d = x1.shape
    steps = t // tt
    return pl.pallas_call(
        functools.partial(_combine_kernel, top_k=TOP_K, final_norm=final_norm),
        grid=(steps,),
        in_specs=[pl.BlockSpec((1, 1, tt * TOP_K), lambda i: (i, 0, 0), memory_space=pltpu.SMEM),
                  pl.BlockSpec((1, 1, tt * TOP_K), lambda i: (jnp.minimum(i + 1, steps - 1), 0, 0),
                               memory_space=pltpu.SMEM),
                  pl.BlockSpec((tt, LANES), lambda i: (i, 0)),
                  pl.BlockSpec((tt, d), lambda i: (i, 0)),
                  pl.BlockSpec((1, 6, d), mod_map),
                  pl.BlockSpec((1, d), lambda i: (0, 0)),
                  pl.BlockSpec(memory_space=pl.ANY)],
        out_specs=pl.BlockSpec((tt, d), lambda i: (i, 0)),
        out_shape=jax.ShapeDtypeStruct((t, d), F32),
        scratch_shapes=[pltpu.VMEM((2, TOP_K, tt * d // LANES, LANES), F32), pltpu.SemaphoreType.DMA((2,))],
        compiler_params=_cparams("arbitrary"),
        name="combine",
    )(dest, dest, tg, x1, mods, final_g.reshape(1, d), ys)


def _moe_plan(counts, rows, n_blocks):
    n_exp = counts.shape[0]
    padded = (counts + rows - 1) // rows * rows
    pad_end = jnp.cumsum(padded)
    block_row = jnp.arange(n_blocks, dtype=jnp.int32) * rows
    block_e = jnp.minimum(jnp.sum(pad_end[None, :] <= block_row[:, None], axis=1), n_exp - 1).astype(jnp.int32)
    n_used = (pad_end[-1:] // rows).astype(jnp.int32)
    start = pad_end - padded
    pad_start = jnp.pad(start.astype(F32), (0, LANES - n_exp)).reshape(1, LANES)
    return pad_start, block_e, n_used, (start + counts).astype(jnp.int32), (padded - counts).astype(jnp.int32)


def _sequence_mixers(z_hy, z_qk, z_v, z_o, z_g, lw, state, row_w):
    (hy_cw, hy_cb, filt_params, hy_b, ml_cw, ml_cb, ml_gb, ml_ng) = lw
    y_hy = _hyena(z_hy, hy_cw, hy_cb, filt_params, hy_b, row_w)
    ml_w = z_v.shape[-1]
    q = _short_conv(z_qk, ml_cw, ml_cb, row_w, silu=True, col0=0, ncols=ml_w)
    kt = _short_conv(z_qk, ml_cw, ml_cb, row_w, silu=True, col0=ml_w, ncols=ml_w, transpose=True)
    y_ml, st = _mlstm(q, kt, z_v, z_o, z_g, ml_gb, ml_ng, *state)
    return y_hy, y_ml, st


def kernel(x_prompt, x_sample, state_mlstm_C, state_mlstm_n, state_mlstm_m, c, c_ctx, ada_w, ada_b, norm1_g,
           w_in, hy_conv_w, hy_conv_b, filt_w1, filt_b1, filt_w2, filt_b2, filt_w3, filt_freq, hy_bias,
           ml_conv_w, ml_conv_b, ml_gate_b, ml_norm_g, w_out, norm2_g, router_w, router_b, moe_w_gu,
           moe_b_gu, moe_w_down, moe_b_down, final_g):
    bp, lp, d = x_prompt.shape
    bs, ls, _ = x_sample.shape
    depth = ada_w.shape[0]
    heads = ml_gate_b.shape[-1]
    hy_w = hy_bias.shape[-1]
    ml_w = ml_norm_g.shape[-1]
    dh = ml_w // heads
    n_exp = router_w.shape[-1]
    t = bp * lp + bs * ls
    ng = N_DIR * 2 * heads
    seg_widths = (3 * hy_w, 2 * ml_w, ml_w, ml_w)
    n_main = 3 * hy_w + 4 * ml_w
    n_blocks = -(-(t * TOP_K) // MOE_ROWS) + n_exp

    cond = jnp.concatenate([c_ctx[None], c, jnp.zeros((SUBLANES - 1 - bs, d), F32)], axis=0)
    zero_state = (jnp.zeros((bp, N_DIR, heads, dh, dh), F32), jnp.zeros((bp, N_DIR, heads, dh), F32),
                  jnp.zeros((bp, N_DIR, heads), F32))
    xs_paths = [x_prompt, x_sample]
    path_cfg = [(0, 0, lp), (1, 1, GRID_W)]
    new_c, new_n, new_m = [], [], []
    for l in range(depth):
        mods = _ada(cond, ada_w[l], ada_b[l]).reshape(SUBLANES, 6, d)
        w_main = w_in[l][:, :n_main].astype(BF16)
        w_gate = jnp.pad(w_in[l][:, n_main:], ((0, 0), (0, LANES - ng)))
        lw = (hy_conv_w[l], hy_conv_b[l],
              (filt_w1[l], filt_b1[l], filt_w2[l], filt_b2[l], filt_w3[l], filt_freq[l]), hy_bias[l],
              ml_conv_w[l], ml_conv_b[l], ml_gate_b[l], ml_norm_g[l])
        states = [zero_state, (state_mlstm_C[:, l], state_mlstm_n[:, l], state_mlstm_m[:, l])]
        counts = jnp.zeros((SUBLANES, LANES), F32)
        routed = []
        for x3, (mod0, mod_step, row_w), state in zip(xs_paths, path_cfg, states):
            bsz, lseq, _ = x3.shape
            tm, tt = min(ROW_TILE, lseq), min(TOK_TILE, lseq)
            xf = x3.reshape(bsz * lseq, d)
            z = _inproj(xf, mods, norm1_g[l], w_main, w_gate, seg_widths,
                        _mod_index_map(mod0, mod_step, lseq // tm), tm)
            y_hy, y_ml, st = _sequence_mixers(*[a.reshape(bsz, lseq, a.shape[1]) for a in z], lw, state, row_w)
            x1, h2t, ti, tg, rk, counts = _post(
                y_hy.reshape(bsz * lseq, hy_w), y_ml.reshape(bsz * lseq, ml_w), xf, mods, w_out[l], norm2_g[l],
                router_w[l], router_b[l], counts, _mod_index_map(mod0, mod_step, lseq // tm), tm)
            routed.append((x1, h2t, ti, tg, rk, tm, tt, _mod_index_map(mod0, mod_step, lseq // tt), st))
        new_c.append(routed[0][-1][0])
        new_n.append(routed[0][-1][1])
        new_m.append(routed[0][-1][2])

        pad_start, block_e, n_used, pad_first, pad_count = _moe_plan(
            counts[0, :n_exp].astype(jnp.int32), MOE_ROWS, n_blocks)
        tt = routed[0][6]
        assert all(r[6] == tt for r in routed)
        dests = [_dest_rows(ti, rk, pad_start, tm)[:, :TOP_K].reshape(-1, 1, tt * TOP_K)
                 for _, _, ti, _, rk, tm, _, _, _ in routed]
        xs = _dispatch(jnp.concatenate(dests, axis=0), [r[1] for r in routed], pad_first, pad_count, n_used,
                       n_blocks, MOE_ROWS, tt)
        ys = _ffn(block_e, n_used, xs, moe_w_gu[l], moe_b_gu[l], moe_w_down[l], moe_b_down[l], MOE_ROWS)
        xs_paths = [
            _combine(dest, tg, x1, mods, final_g, ys, mod_map, tt, final_norm=l == depth - 1).reshape(x3.shape)
            for dest, (x1, _, _, tg, _, _, tt, mod_map, _), x3 in zip(dests, routed, xs_paths)]
    return (xs_paths[0], xs_paths[1],
            jnp.stack(new_c, axis=1), jnp.stack(new_n, axis=1), jnp.stack(new_m, axis=1))
```

```python
import functools
import math

import numpy as np
import jax
import jax.numpy as jnp
from jax import lax
from jax.experimental import pallas as pl
from jax.experimental.pallas import tpu as pltpu

F32 = jnp.float32
BF16 = jnp.bfloat16
HIGHEST = lax.Precision.HIGHEST
EPS = 1e-6

LANES = 128
SUBLANES = 8
VMEM_LIMIT_BYTES = 56 * 1024 * 1024

GRID_W = 64
ML_HEADS = 4
N_DIR = 2
HY_ORDER = 2
FILT_BANDS = 8
DECAY_TARGET = 1e-2
FAST_DECAY_PCT = 0.3
SLOW_DECAY_PCT = 1.5
TOP_K = 4
SWIGLU_LIMIT = 7.0
SWIGLU_ALPHA = 1.702

FFT_N2 = 128
MLSTM_CHUNK = 128
MLSTM_GROUP = 8
MLSTM_PAIR = 4
MOE_ROWS = 512
ROW_TILE = 512
TOK_TILE = 256


def _cparams(*sem):
    return pltpu.CompilerParams(dimension_semantics=sem, vmem_limit_bytes=VMEM_LIMIT_BYTES)


def _lane_tile(c, cap):
    return max(t for t in range(LANES, min(c, cap) + 1, LANES) if c % t == 0)


def _bdot(a, b):
    return jnp.dot(a.astype(BF16), b.astype(BF16), preferred_element_type=F32)


def _hdot(a, b):
    return jnp.dot(a, b, precision=HIGHEST, preferred_element_type=F32)


def _dot3(a, b):
    a_hi, b_hi = a.astype(BF16), b.astype(BF16)
    a_lo = (a - a_hi.astype(F32)).astype(BF16)
    b_lo = (b - b_hi.astype(F32)).astype(BF16)
    dot = functools.partial(jnp.dot, preferred_element_type=F32)
    return dot(a_hi, b_hi) + dot(a_hi, b_lo) + dot(a_lo, b_hi)


def _ada_kernel(c_ref, w_ref, b_ref, o_ref):
    c = c_ref[...]
    o_ref[...] = _hdot(c * jax.nn.sigmoid(c), w_ref[...]) + b_ref[...]


def _ada(cond, w, b):
    r, d = cond.shape
    n = w.shape[1]
    tn = _lane_tile(n, 1024)
    return pl.pallas_call(
        _ada_kernel,
        grid=(n // tn,),
        in_specs=[pl.BlockSpec((r, d), lambda j: (0, 0)),
                  pl.BlockSpec((d, tn), lambda j: (0, j)),
                  pl.BlockSpec((1, tn), lambda j: (0, j))],
        out_specs=pl.BlockSpec((r, tn), lambda j: (0, j)),
        out_shape=jax.ShapeDtypeStruct((r, n), F32),
        compiler_params=_cparams("arbitrary"),
        name="ada",
    )(cond, w, b.reshape(1, n))


def _mod_index_map(mod0, mod_step, tiles_per_seq):
    def index_map(i):
        return (mod0 + (i // tiles_per_seq) * mod_step, 0, 0)
    return index_map


def _inproj_kernel(x_ref, mod_ref, g_ref, w_ref, wg_ref, *out_refs, offsets):
    x = x_ref[...]
    h = x * lax.rsqrt(jnp.mean(x * x, axis=-1, keepdims=True) + EPS) * g_ref[...]
    h = h * (1.0 + mod_ref[0, 1:2, :]) + mod_ref[0, 0:1, :]
    hb = h.astype(BF16)
    for o_ref, (lo, hi) in zip(out_refs[:-1], offsets):
        o_ref[...] = jnp.dot(hb, w_ref[:, lo:hi], preferred_element_type=F32)
    h_lo = (h - hb.astype(F32)).astype(BF16)
    g = jnp.dot(hb, wg_ref[...], preferred_element_type=F32)
    out_refs[-1][...] = (g[:, :LANES] + g[:, LANES:]
                         + jnp.dot(h_lo, wg_ref[:, :LANES], preferred_element_type=F32))


def _inproj(x, mods, norm_g, w_main, w_gate, seg_widths, mod_map, tm):
    t, d = x.shape
    offsets, lo = [], 0
    for wd in seg_widths:
        offsets.append((lo, lo + wd))
        lo += wd
    wg_hi = w_gate.astype(BF16)
    wg = jnp.concatenate([wg_hi, (w_gate - wg_hi.astype(F32)).astype(BF16)], axis=1)
    widths = tuple(seg_widths) + (LANES,)
    return pl.pallas_call(
        functools.partial(_inproj_kernel, offsets=tuple(offsets)),
        grid=(t // tm,),
        in_specs=[pl.BlockSpec((tm, d), lambda i: (i, 0)),
                  pl.BlockSpec((1, 6, d), mod_map),
                  pl.BlockSpec((1, d), lambda i: (0, 0)),
                  pl.BlockSpec(w_main.shape, lambda i: (0, 0)),
                  pl.BlockSpec(wg.shape, lambda i: (0, 0))],
        out_specs=[pl.BlockSpec((tm, wd), lambda i: (i, 0)) for wd in widths],
        out_shape=[jax.ShapeDtypeStruct((t, wd), F32) for wd in widths],
        compiler_params=_cparams("arbitrary"),
        name="inproj",
    )(x, mods, norm_g.reshape(1, d), w_main, wg)


def _short_conv_kernel(x_ref, w_ref, b_ref, o_ref, *, row_w, silu, transpose):
    x = x_ref[0]
    l = x.shape[0]
    pos = lax.broadcasted_iota(jnp.int32, x.shape, 0) % row_w
    prev = jnp.where(pos == 0, 0.0, pltpu.roll(x, 1, 0))
    nxt = jnp.where(pos == row_w - 1, 0.0, pltpu.roll(x, l - 1, 0))
    y = prev * w_ref[0:1, :] + x * w_ref[1:2, :] + nxt * w_ref[2:3, :] + b_ref[...]
    if silu:
        y = y * jax.nn.sigmoid(y)
    o_ref[0] = y.T if transpose else y


def _short_conv(x, w, b, row_w, silu, col0=0, ncols=None, transpose=False):
    bsz, l, c = x.shape
    ncols = c if ncols is None else ncols
    ct = _lane_tile(math.gcd(ncols, col0) if col0 else ncols, LANES if transpose else 256)
    j0 = col0 // ct
    out_shape, out_block, out_map = (bsz, l, ncols), (1, l, ct), lambda i, j: (i, 0, j)
    if transpose:
        out_shape, out_block, out_map = (bsz, ncols, l), (1, ct, l), lambda i, j: (i, j, 0)
    return pl.pallas_call(
        functools.partial(_short_conv_kernel, row_w=row_w, silu=silu, transpose=transpose),
        grid=(bsz, ncols // ct),
        in_specs=[pl.BlockSpec((1, l, ct), lambda i, j: (i, 0, j0 + j)),
                  pl.BlockSpec((3, ct), lambda i, j: (0, j0 + j)),
                  pl.BlockSpec((1, ct), lambda i, j: (0, j0 + j))],
        out_specs=pl.BlockSpec(out_block, out_map),
        out_shape=jax.ShapeDtypeStruct(out_shape, F32),
        compiler_params=_cparams("arbitrary", "arbitrary"),
        name="short_conv",
    )(x, w, b.reshape(1, c))


def _dft_direct_tables(l):
    n = 2 * l
    k = np.arange(n)[:, None].astype(np.float64)
    t = np.arange(n)[None, :].astype(np.float64)
    ang = 2.0 * np.pi * ((k * t) % n) / n
    cm, sm = np.cos(ang), np.sin(ang)
    fwd = np.block([[cm[:, :l], sm[:, :l]], [-sm[:, :l], cm[:, :l]]])
    filt = np.concatenate([cm, -sm], axis=0)
    return fwd, filt


def _dft_two_level_tables(l, n2):
    n = 2 * l
    n1 = n // n2
    k1 = np.arange(n1)[:, None].astype(np.float64)
    a = np.arange(n1)[None, :].astype(np.float64)
    ang1 = 2.0 * np.pi * ((k1 * a) % n1) / n1
    c1, s1 = np.cos(ang1), np.sin(ang1)
    h = n1 // 2
    m1 = np.block([[c1[:, :h], s1[:, :h]], [-s1[:, :h], c1[:, :h]]])
    m1f = np.concatenate([c1, -s1], axis=0)
    kk = (np.arange(n1)[:, None, None] + n1 * np.arange(n2)[None, :, None]).astype(np.float64)
    b = np.arange(n2)[None, None, :].astype(np.float64)
    ang = 2.0 * np.pi * ((kk * b) % n) / n
    cg, sg = np.cos(ang), np.sin(ang)
    gt = np.concatenate([np.concatenate([cg, sg], axis=2),
                         np.concatenate([-sg, cg], axis=2)], axis=1)
    return m1, m1f, gt


def _circular_lag(n0, rows, l):
    n = n0 + lax.broadcasted_iota(jnp.int32, (rows, 1), 0)
    t = jnp.where(n < l, n, 2 * l - n).astype(F32)
    return n, t, t / float(max(l - 1, 1))


def _filter_hidden_kernel(bandv_ref, w1_ref, b1_ref, w2_ref, b2_ref, freq_ref, o_ref, *, l):
    rows = o_ref.shape[0]
    _, t, t01 = _circular_lag(pl.program_id(0) * rows, rows, l)
    lane = lax.broadcasted_iota(jnp.int32, (rows, LANES), 1)
    ang = (2.0 * math.pi / l) * t * bandv_ref[...]
    feats = jnp.where(lane == 0, t01,
                      jnp.where(lane <= FILT_BANDS, jnp.cos(ang),
                                jnp.where(lane <= 2 * FILT_BANDS, -jnp.sin(ang), 0.0)))
    fr = freq_ref[...]
    h = jnp.sin(fr * (_hdot(feats, w1_ref[...]) + b1_ref[...]))
    o_ref[...] = jnp.sin(fr * (_hdot(h, w2_ref[...]) + b2_ref[...]))


def _filter_hidden(l, f_w1, f_b1, f_w2, f_b2, f_freq):
    emb, hid = f_w1.shape
    n = 2 * l
    rows = min(n, 512)
    bands = jnp.linspace(1e-4, FILT_BANDS - 1, FILT_BANDS, dtype=F32)
    bandv = jnp.zeros((1, LANES), F32).at[0, 1:1 + FILT_BANDS].set(bands)
    bandv = bandv.at[0, 1 + FILT_BANDS:1 + 2 * FILT_BANDS].set(bands)
    w1p = jnp.zeros((LANES, hid), F32).at[:emb].set(f_w1)
    c0 = lambda i: (0, 0)
    return pl.pallas_call(
        functools.partial(_filter_hidden_kernel, l=l),
        grid=(n // rows,),
        in_specs=[pl.BlockSpec((1, LANES), c0), pl.BlockSpec((LANES, hid), c0), pl.BlockSpec((1, hid), c0),
                  pl.BlockSpec((hid, hid), c0), pl.BlockSpec((1, hid), c0), pl.BlockSpec((1, hid), c0)],
        out_specs=pl.BlockSpec((rows, hid), lambda i: (i, 0)),
        out_shape=jax.ShapeDtypeStruct((n, hid), F32),
        compiler_params=_cparams("arbitrary"),
        name="filter_hidden",
    )(bandv, w1p, f_b1.reshape(1, hid), f_w2, f_b2.reshape(1, hid), f_freq.reshape(1, hid))


def _filter_rows(n0, rows, l, hid_ref, w3_ref, delta_ref):
    n, _, t01 = _circular_lag(n0, rows, l)
    h = hid_ref[pl.ds(n0, rows), :]
    hf = _bdot(h, w3_ref[0, 0])
    hb = _bdot(h, w3_ref[0, 1])
    window = jnp.exp(-t01 * delta_ref[...])
    return jnp.where(n < l, hf, jnp.where(n > l, hb, 0.0)) * window


def _filter_direct_kernel(hid_ref, w3_ref, delta_ref, ff_ref, h_ref, *, l):
    hc = _filter_rows(0, 2 * l, l, hid_ref, w3_ref, delta_ref)
    h_ref[0] = _bdot(ff_ref[...], hc) * (1.0 / (2 * l))


def _filter_two_level_kernel(hid_ref, w3_ref, delta_ref, m1f_ref, gt_ref, h_ref, hc_buf, a_buf, *, l, n2, rows):
    n = 2 * l
    n1 = n // n2

    def fill(i, carry):
        r0 = pl.multiple_of(i * rows, rows)
        hc_buf[pl.ds(r0, rows), :] = _filter_rows(r0, rows, l, hid_ref, w3_ref, delta_ref)
        return carry
    lax.fori_loop(0, n // rows, fill, 0)

    def step1(b, carry):
        col = hc_buf[pl.ds(b, n1, stride=n2), :]
        a = _bdot(m1f_ref[...], col)
        a_buf[pl.ds(b, n1, stride=2 * n2), :] = a[:n1]
        a_buf[pl.ds(n2 + b, n1, stride=2 * n2), :] = a[n1:]
        return carry
    lax.fori_loop(0, n2, step1, 0, unroll=8)

    def step2(k1, carry):
        r0 = pl.multiple_of(k1 * 2 * n2, 2 * n2)
        h_ref[0, k1] = (_bdot(gt_ref[k1], a_buf[pl.ds(r0, 2 * n2), :]) * (1.0 / n)).astype(h_ref.dtype)
        return carry
    lax.fori_loop(0, n1, step2, 0, unroll=4)


def _filter_tail_inputs(hy_w, f_w3):
    hid = f_w3.shape[0]
    w3 = f_w3.reshape(hid, HY_ORDER, N_DIR, hy_w).transpose(1, 2, 0, 3)
    max_decay = math.log(DECAY_TARGET) / FAST_DECAY_PCT
    min_decay = math.log(DECAY_TARGET) / SLOW_DECAY_PCT
    deltas = jnp.abs(jnp.linspace(min_decay, max_decay, hy_w, dtype=F32)).reshape(1, hy_w)
    return w3, deltas


def _filter_spectrum_direct(l, hy_w, filt_params, ff):
    f_w1, f_b1, f_w2, f_b2, f_w3, f_freq = filt_params
    hidden = _filter_hidden(l, f_w1, f_b1, f_w2, f_b2, f_freq)
    w3, deltas = _filter_tail_inputs(hy_w, f_w3)
    hid = f_w3.shape[0]
    n = 2 * l
    return pl.pallas_call(
        functools.partial(_filter_direct_kernel, l=l),
        grid=(HY_ORDER,),
        in_specs=[pl.BlockSpec((n, hid), lambda o: (0, 0)),
                  pl.BlockSpec((1, N_DIR, hid, hy_w), lambda o: (o, 0, 0, 0)),
                  pl.BlockSpec((1, hy_w), lambda o: (0, 0)),
                  pl.BlockSpec((2 * n, n), lambda o: (0, 0))],
        out_specs=pl.BlockSpec((1, 2 * n, hy_w), lambda o: (o, 0, 0)),
        out_shape=jax.ShapeDtypeStruct((HY_ORDER, 2 * n, hy_w), F32),
        compiler_params=_cparams("arbitrary"),
        name="filter_direct",
    )(hidden, w3, deltas, ff)


def _filter_spectrum_two_level(l, hy_w, filt_params, m1f, gt, ct):
    f_w1, f_b1, f_w2, f_b2, f_w3, f_freq = filt_params
    hidden = _filter_hidden(l, f_w1, f_b1, f_w2, f_b2, f_freq)
    w3, deltas = _filter_tail_inputs(hy_w, f_w3)
    hid = f_w3.shape[0]
    n = 2 * l
    n2 = FFT_N2
    n1 = n // n2
    return pl.pallas_call(
        functools.partial(_filter_two_level_kernel, l=l, n2=n2, rows=min(n, 512)),
        grid=(HY_ORDER, hy_w // ct),
        in_specs=[pl.BlockSpec((n, hid), lambda o, j: (0, 0)),
                  pl.BlockSpec((1, N_DIR, hid, ct), lambda o, j: (o, 0, 0, j)),
                  pl.BlockSpec((1, ct), lambda o, j: (0, j)),
                  pl.BlockSpec((2 * n1, n1), lambda o, j: (0, 0)),
                  pl.BlockSpec((n1, 2 * n2, 2 * n2), lambda o, j: (0, 0, 0))],
        out_specs=pl.BlockSpec((1, n1, 2 * n2, ct), lambda o, j: (o, 0, 0, j)),
        out_shape=jax.ShapeDtypeStruct((HY_ORDER, n1, 2 * n2, hy_w), BF16),
        scratch_shapes=[pltpu.VMEM((n, ct), F32), pltpu.VMEM((n1 * 2 * n2, ct), F32)],
        compiler_params=_cparams("arbitrary", "arbitrary"),
        name="filter_two_level",
    )(hidden, w3, deltas, m1f, gt)


def _complex_mul(x, h, half):
    xr, xi = x[:half], x[half:]
    hr, hi = h[:half], h[half:]
    return jnp.concatenate([xr * hr - xi * hi, xr * hi + xi * hr], axis=0)


def _conv_direct_kernel(z_ref, gate_ref, bias_ref, h_ref, fwd_ref, inv_ref, o_ref):
    l = z_ref.shape[1]
    z = jnp.concatenate([z_ref[0], z_ref[1]], axis=0)
    x = _bdot(fwd_ref[...], z)
    y = _bdot(inv_ref[...], _complex_mul(x, h_ref[0], 2 * l))
    bias = bias_ref[0]
    o_ref[0] = gate_ref[0] * (y[:l] + bias * z_ref[0])
    o_ref[1] = gate_ref[1] * (y[l:] + bias * z_ref[1])


def _conv_two_level_kernel(z_ref, gate_ref, bias_ref, h_ref, m1_ref, m1i_ref, gt_ref, o_ref,
                           a_buf, *, n2):
    l = z_ref.shape[1]
    n1 = 2 * l // n2
    hn = n1 // 2

    def step1(b, carry):
        za = z_ref[0, pl.ds(b, hn, stride=n2), :]
        zb = z_ref[1, pl.ds(b, hn, stride=n2), :]
        a = _bdot(m1_ref[...], jnp.concatenate([za, zb], axis=0))
        a_buf[pl.ds(b, n1, stride=2 * n2), :] = a[:n1]
        a_buf[pl.ds(n2 + b, n1, stride=2 * n2), :] = a[n1:]
        return carry
    lax.fori_loop(0, n2, step1, 0, unroll=8)

    def step2(k1, carry):
        r0 = pl.multiple_of(k1 * 2 * n2, 2 * n2)
        x = _bdot(gt_ref[k1], a_buf[pl.ds(r0, 2 * n2), :])
        y = _complex_mul(x, h_ref[0, k1].astype(F32), n2).astype(BF16)
        a_buf[pl.ds(r0, 2 * n2), :] = lax.dot_general(gt_ref[k1], y, (((0,), (0,)), ((), ())),
                                                      preferred_element_type=F32)
        return carry
    lax.fori_loop(0, n1, step2, 0, unroll=4)

    def step3(b, carry):
        br = a_buf[pl.ds(b, n1, stride=2 * n2), :]
        bi = a_buf[pl.ds(n2 + b, n1, stride=2 * n2), :]
        y = _bdot(m1i_ref[...], jnp.concatenate([br, bi], axis=0))
        o_ref[0, pl.ds(b, hn, stride=n2), :] = y[:hn]
        o_ref[1, pl.ds(b, hn, stride=n2), :] = y[hn:]
        return carry
    lax.fori_loop(0, n2, step3, 0, unroll=8)

    bias = bias_ref[0]
    for s in range(2):
        o_ref[s] = gate_ref[s] * (o_ref[s] + bias * z_ref[s])


def _long_conv_gated(u, z, z_col, gate_col, spectrum, order, bias, tables, ct):
    bsz, l, _ = u.shape
    c = spectrum.shape[-1]
    nct = c // ct
    zspec = pl.BlockSpec((2, l, ct), lambda i, j: (i, 0, z_col * nct + j))
    gspec = pl.BlockSpec((2, l, ct), lambda i, j: (i, 0, gate_col * nct + j))
    bspec = pl.BlockSpec((1, 1, ct), lambda i, j: (order, 0, j))
    ospec = pl.BlockSpec((2, l, ct), lambda i, j: (i, 0, j))
    out_shape = jax.ShapeDtypeStruct((bsz, l, c), F32)
    bias3 = bias.reshape(HY_ORDER, 1, c)
    if len(tables) == 2:
        fwd, inv = tables
        n = 2 * l
        return pl.pallas_call(
            _conv_direct_kernel,
            grid=(bsz // 2, nct),
            in_specs=[zspec, gspec, bspec,
                      pl.BlockSpec((1, 2 * n, ct), lambda i, j: (order, 0, j)),
                      pl.BlockSpec(fwd.shape, lambda i, j: (0, 0)),
                      pl.BlockSpec(inv.shape, lambda i, j: (0, 0))],
            out_specs=ospec, out_shape=out_shape,
            compiler_params=_cparams("arbitrary", "arbitrary"),
            name="long_conv_direct",
        )(z, u, bias3, spectrum, fwd, inv)
    m1, m1i, gt = tables
    n2 = FFT_N2
    n1 = 2 * l // n2
    const2 = lambda i, j: (0, 0)
    const3 = lambda i, j: (0, 0, 0)
    return pl.pallas_call(
        functools.partial(_conv_two_level_kernel, n2=n2),
        grid=(nct, bsz // 2),
        in_specs=[pl.BlockSpec((2, l, ct), lambda j, i: (i, 0, z_col * nct + j)),
                  pl.BlockSpec((2, l, ct), lambda j, i: (i, 0, gate_col * nct + j)),
                  pl.BlockSpec((1, 1, ct), lambda j, i: (order, 0, j)),
                  pl.BlockSpec((1, n1, 2 * n2, ct), lambda j, i: (order, 0, 0, j)),
                  pl.BlockSpec(m1.shape, const2), pl.BlockSpec(m1i.shape, const2),
                  pl.BlockSpec(gt.shape, const3)],
        out_specs=pl.BlockSpec((2, l, ct), lambda j, i: (i, 0, j)),
        out_shape=out_shape,
        scratch_shapes=[pltpu.VMEM((n1 * 2 * n2, ct), F32)],
        compiler_params=_cparams("arbitrary", "arbitrary"),
        name="long_conv_two_level",
    )(z, u, bias3, spectrum, m1, m1i, gt)


def _hyena(z_hy, conv_w, conv_b, filt_params, hy_bias, row_w):
    bsz, l, c3 = z_hy.shape
    c = c3 // 3
    u = _short_conv(z_hy, conv_w, conv_b, row_w, silu=False)
    table = lambda a: jnp.asarray(a, F32).astype(BF16)
    if 2 * l // FFT_N2 <= 4:
        fwd, filt = _dft_direct_tables(l)
        tables = (table(fwd), table(fwd.T))
        spectrum = _filter_spectrum_direct(l, c, filt_params, table(filt))
        ct = c
    else:
        m1, m1f, gt = _dft_two_level_tables(l, FFT_N2)
        tables = (table(m1), table(m1.T), table(gt))
        ct = LANES
        spectrum = _filter_spectrum_two_level(l, c, filt_params, table(m1f), tables[2], ct)
    z1 = _long_conv_gated(u, u, 0, 1, spectrum, 0, hy_bias, tables, ct)
    return _long_conv_gated(u, z1, 0, 2, spectrum, 1, hy_bias, tables, ct)


def _log_sigmoid(x):
    return jnp.minimum(x, 0.0) - jnp.log1p(jnp.exp(-jnp.abs(x)))


def _split3_dot(a, b, split_lhs):
    x = a if split_lhs else b
    hi = x.astype(BF16)
    rest = x - hi.astype(F32)
    mid = rest.astype(BF16)
    parts = (hi, mid, (rest - mid.astype(F32)).astype(BF16))
    if split_lhs:
        return functools.reduce(lambda u, w: u + w, [jnp.dot(p, b, preferred_element_type=F32) for p in parts])
    return functools.reduce(lambda u, w: u + w, [jnp.dot(a, p, preferred_element_type=F32) for p in parts])


def _mlstm_recur(qb, sv, m_loc, kv, g_loc, bq, btot, cn, m):
    dh = qb.shape[1]
    inter = bq + m
    mj = jnp.maximum(m_loc, inter)
    w_int = jnp.exp(inter - mj)
    w_loc = jnp.exp(m_loc - mj)
    qc = jnp.dot(qb, cn.astype(BF16), preferred_element_type=F32)
    num = w_int * qc[:, :dh] + w_loc * sv[:, :dh]
    den = w_int * qc[:, dh:] + w_loc * sv[:, dh:]
    h = num / jnp.maximum(jnp.abs(den), jnp.exp(-mj))
    m_new = jnp.maximum(btot + m, g_loc)
    cn_new = jnp.exp(btot + m - m_new) * cn + jnp.exp(g_loc - m_new) * kv
    return h, cn_new, m_new


def _mlstm_kernel(q_ref, kt_ref, v_ref, o_ref, g_ref, gt_ref, gb_ref, gbt_ref, ng_ref, c0_ref, n0_ref, m0_ref,
                  y_ref, c_ref, n_ref, m_ref, hf_buf, hb_buf, cn_buf, sv_buf, kv_buf, bq_buf, ml_buf, sc_buf,
                  *, chunk, heads):
    l, dh = q_ref.shape[1], q_ref.shape[2]
    nc = l // chunk
    group = sv_buf.shape[1]
    pair = math.gcd(group, MLSTM_PAIR)
    scale = dh ** -0.5
    row = lax.broadcasted_iota(jnp.int32, (chunk, chunk), 0)
    col = lax.broadcasted_iota(jnp.int32, (chunk, chunk), 1)
    lower, upper = col <= row, col >= row
    tri_l, tri_u = lower.astype(BF16), upper.astype(BF16)
    gate_row = lax.broadcasted_iota(jnp.int32, (SUBLANES, chunk), 0)
    ones = jnp.ones((chunk, dh), BF16)

    sel_row = lax.broadcasted_iota(jnp.int32, (LANES, LANES), 0)
    head = pl.program_id(1)

    def chunk_start(j, d):
        return pl.multiple_of(((nc - 1 - j) if d else j) * chunk, chunk)

    def local(gp, carry, j0):
        jobs = [(gp * pair + k, d) for k in range(pair) for d in range(N_DIR)]
        r0s = [chunk_start(j0 + g, d) for g, d in jobs]
        picks = [(sel_row == (2 * d + 1) * heads + head).astype(BF16) for d in range(N_DIR)]
        lfs = [_log_sigmoid(g_ref[0, pl.ds(r0, chunk), :] + gb_ref[...]) for r0 in r0s]
        gts = [gt_ref[0, 0, :, pl.ds(r0, chunk)] + gbt_ref[0] for r0 in r0s]
        gts = [jnp.where(gate_row % 2 == 1, _log_sigmoid(gt), gt) for gt in gts]
        lfs = [_split3_dot(lf, picks[d], split_lhs=True) for lf, (_, d) in zip(lfs, jobs)]
        brows = [_split3_dot(gt, tri_l if d else tri_u, split_lhs=True)[2 * d + 1:2 * d + 2, :]
                 for gt, (_, d) in zip(gts, jobs)]
        bqs = [_split3_dot(tri_u if d else tri_l, lf, split_lhs=False) for lf, (_, d) in zip(lfs, jobs)]
        btots = [brow[:, 0:1] if d else brow[:, chunk - 1:chunk] for brow, (_, d) in zip(brows, jobs)]
        irows = [gt[2 * d:2 * d + 1, :] for gt, (_, d) in zip(gts, jobs)]
        qbs = [q_ref[0, pl.ds(r0, chunk), :].astype(BF16) for r0 in r0s]
        kts = [kt_ref[0, :, pl.ds(r0, chunk)] * scale for r0 in r0s]
        vos = [jnp.concatenate([v_ref[0, pl.ds(r0, chunk), :].astype(BF16), ones], axis=1) for r0 in r0s]
        qks = [jnp.dot(qb, kt.astype(BF16), preferred_element_type=F32) for qb, kt in zip(qbs, kts)]
        dms = [jnp.where(upper if d else lower, bq - brow + irow, -jnp.inf)
               for bq, brow, irow, (_, d) in zip(bqs, brows, irows, jobs)]
        m_locs = [jnp.max(dm, axis=-1, keepdims=True) for dm in dms]
        ss = [(qk * jnp.exp(dm - m_loc)).astype(BF16) for qk, dm, m_loc in zip(qks, dms, m_locs)]
        gls = [btot - brow + irow for btot, brow, irow in zip(btots, brows, irows)]
        g_locs = [jnp.max(gl, axis=-1, keepdims=True) for gl in gls]
        wks = [(kt * jnp.exp(gl - g_loc)).astype(BF16) for kt, gl, g_loc in zip(kts, gls, g_locs)]
        svs = [jnp.dot(s, vo, preferred_element_type=F32) for s, vo in zip(ss, vos)]
        kvs = [jnp.dot(wk, vo, preferred_element_type=F32) for wk, vo in zip(wks, vos)]
        for (g, d), sv, kv, bq, m_loc, btot, g_loc in zip(jobs, svs, kvs, bqs, m_locs, btots, g_locs):
            sv_buf[d, g] = sv
            kv_buf[d, g] = kv
            bq_buf[d, g] = bq
            ml_buf[d, g] = jnp.broadcast_to(m_loc, (chunk, LANES))
            sc_buf[d, g, 0:1, :] = jnp.broadcast_to(btot, (1, LANES))
            sc_buf[d, g, 1:2, :] = jnp.broadcast_to(g_loc, (1, LANES))
        return carry

    def recur(g, carry, j0):
        ms = list(carry)
        for d in range(N_DIR):
            r0 = chunk_start(j0 + g, d)
            h, cn, ms[d] = _mlstm_recur(
                q_ref[0, pl.ds(r0, chunk), :].astype(BF16), sv_buf[d, g], ml_buf[d, g], kv_buf[d, g],
                sc_buf[d, g, 1:2, 0:1], bq_buf[d, g], sc_buf[d, g, 0:1, 0:1], cn_buf[d], ms[d])
            cn_buf[d] = cn
            (hb_buf if d else hf_buf)[pl.ds(r0, chunk), :] = h
        return tuple(ms)

    for d in range(N_DIR):
        n_rep = jnp.broadcast_to(n0_ref[0, 0, d:d + 1, :], (dh, dh)).T
        cn_buf[d] = jnp.concatenate([c0_ref[0, d, 0], n_rep], axis=1)

    def block(jb, carry):
        j0 = jb * group
        lax.fori_loop(0, group // pair, functools.partial(local, j0=j0), 0)
        return lax.fori_loop(0, group, functools.partial(recur, j0=j0), carry, unroll=2)

    m_fin = lax.fori_loop(0, nc // group, block, (m0_ref[0, 0, 0:1, 0:1], m0_ref[0, 0, 1:2, 0:1]))
    for d in range(N_DIR):
        c_ref[0, d, 0] = cn_buf[d, :, :dh]
        n_ref[0, 0, d:d + 1, :] = cn_buf[d, :, dh:].T[0:1, :]
        m_ref[0, 0, d:d + 1, :] = jnp.broadcast_to(m_fin[d], (1, LANES))

    def finish(j, carry):
        r0 = pl.multiple_of(j * chunk, chunk)
        hs = hf_buf[pl.ds(r0, chunk), :] + hb_buf[pl.ds(r0, chunk), :]
        hs = hs * lax.rsqrt(jnp.mean(hs * hs, axis=-1, keepdims=True) + EPS) * ng_ref[...]
        y_ref[0, pl.ds(r0, chunk), :] = jax.nn.sigmoid(o_ref[0, pl.ds(r0, chunk), :]) * hs
        return carry
    lax.fori_loop(0, nc, finish, 0)


def _mlstm(q, kt, z_v, z_o, z_g, gate_b, norm_g, c0, n0, m0):
    bsz, l, ml = z_v.shape
    heads = gate_b.shape[-1]
    dh = ml // heads
    chunk = MLSTM_CHUNK
    assert dh == LANES and chunk == LANES and l % chunk == 0
    group = math.gcd(l // chunk, MLSTM_GROUP)
    n_gate = N_DIR * 2
    assert z_g.shape[-1] == LANES
    g4 = z_g[..., :n_gate * heads].reshape(bsz, l, n_gate, heads)
    ght = jnp.pad(g4.transpose(0, 3, 2, 1), ((0, 0), (0, 0), (0, SUBLANES - n_gate), (0, 0)))
    gb4 = gate_b.reshape(n_gate, heads).T
    gb = jnp.pad(gate_b.reshape(1, n_gate * heads), ((0, 0), (0, LANES - n_gate * heads)))
    gbt = jnp.broadcast_to(jnp.pad(gb4, ((0, 0), (0, SUBLANES - n_gate)))[..., None], (heads, SUBLANES, LANES))
    n0h = n0.transpose(0, 2, 1, 3)
    m0h = jnp.broadcast_to(m0.transpose(0, 2, 1)[..., None], (bsz, heads, N_DIR, LANES))
    seq = pl.BlockSpec((1, l, dh), lambda b, h: (b, 0, h))
    cspec = pl.BlockSpec((1, N_DIR, 1, dh, dh), lambda b, h: (b, 0, h, 0, 0))
    sspec = pl.BlockSpec((1, 1, N_DIR, dh), lambda b, h: (b, h, 0, 0))
    mspec = pl.BlockSpec((1, 1, N_DIR, LANES), lambda b, h: (b, h, 0, 0))
    y, c, n, m = pl.pallas_call(
        functools.partial(_mlstm_kernel, chunk=chunk, heads=heads),
        grid=(bsz, heads),
        in_specs=[seq, pl.BlockSpec((1, dh, l), lambda b, h: (b, h, 0)), seq, seq,
                  pl.BlockSpec((1, l, LANES), lambda b, h: (b, 0, 0)),
                  pl.BlockSpec((1, 1, SUBLANES, l), lambda b, h: (b, h, 0, 0)),
                  pl.BlockSpec((1, LANES), lambda b, h: (0, 0)),
                  pl.BlockSpec((1, SUBLANES, LANES), lambda b, h: (h, 0, 0)),
                  pl.BlockSpec((1, dh), lambda b, h: (0, h)),
                  cspec, sspec, mspec],
        out_specs=[seq, cspec, sspec, mspec],
        out_shape=[jax.ShapeDtypeStruct((bsz, l, ml), F32),
                   jax.ShapeDtypeStruct((bsz, N_DIR, heads, dh, dh), F32),
                   jax.ShapeDtypeStruct((bsz, heads, N_DIR, dh), F32),
                   jax.ShapeDtypeStruct((bsz, heads, N_DIR, LANES), F32)],
        scratch_shapes=[pltpu.VMEM((l, dh), F32), pltpu.VMEM((l, dh), F32),
                        pltpu.VMEM((N_DIR, dh, 2 * dh), F32),
                        pltpu.VMEM((N_DIR, group, chunk, 2 * dh), F32),
                        pltpu.VMEM((N_DIR, group, dh, 2 * dh), F32),
                        pltpu.VMEM((N_DIR, group, chunk, LANES), F32),
                        pltpu.VMEM((N_DIR, group, chunk, LANES), F32),
                        pltpu.VMEM((N_DIR, group, SUBLANES, LANES), F32)],
        compiler_params=_cparams("arbitrary", "arbitrary"),
        name="mlstm",
    )(q, kt, z_v, z_o, z_g, ght, gb, gbt, norm_g.reshape(1, ml), c0, n0h, m0h)
    return y, (c, n.transpose(0, 2, 1, 3), m[..., 0].transpose(0, 2, 1))


def _rows_to_tiles(tile_ref, x):
    r, d = x.shape
    s = d // LANES
    for k in range(s):
        tile_ref[pl.ds(k, r, stride=s), :] = x[:, LANES * k:LANES * (k + 1)]


def _tiles_to_rows(tile_ref, r):
    s = tile_ref.shape[0] // r
    return jnp.concatenate([tile_ref[pl.ds(k, r, stride=s), :] for k in range(s)], axis=1)


def _lane_pack(cols, lane):
    out = jnp.zeros(lane.shape, cols[0].dtype)
    for j, colv in enumerate(cols):
        out = jnp.where(lane == j, colv, out)
    return out


def _post_kernel(yhy_ref, yml_ref, x_ref, mod_ref, wo_ref, n2g_ref, rw_ref, rb_ref, cnt0_ref,
                 x1_ref, h2_ref, ti_ref, tg_ref, rk_ref, cnt_ref, carry, *, n_exp, top_k):
    @pl.when(pl.program_id(0) == 0)
    def _():
        carry[...] = cnt0_ref[0:1, :]

    hy_w = yhy_ref.shape[1]
    tm = x_ref.shape[0]
    proj = _bdot(yhy_ref[...], wo_ref[:hy_w, :]) + _bdot(yml_ref[...], wo_ref[hy_w:, :])
    x1 = x_ref[...] + mod_ref[0, 2:3, :] * proj
    x1_ref[...] = x1
    h2 = x1 * lax.rsqrt(jnp.mean(x1 * x1, axis=-1, keepdims=True) + EPS) * n2g_ref[...]
    h2 = h2 * (1.0 + mod_ref[0, 4:5, :]) + mod_ref[0, 3:4, :]
    _rows_to_tiles(h2_ref, h2)

    lane = lax.broadcasted_iota(jnp.int32, (tm, LANES), 1)
    work = jnp.where(lane < n_exp, _dot3(h2, rw_ref[...]) + rb_ref[...], -jnp.inf)
    vals, idxs, hots = [], [], []
    for _ in range(top_k):
        mx = jnp.max(work, axis=-1, keepdims=True)
        idx = jnp.min(jnp.where(work == mx, lane, LANES), axis=-1, keepdims=True)
        hot = lane == idx
        vals.append(mx)
        idxs.append(idx)
        hots.append(hot)
        work = jnp.where(hot, -jnp.inf, work)
    exps = [jnp.exp(v - vals[0]) for v in vals]
    tot = functools.reduce(lambda a, b: a + b, exps)
    ti_ref[...] = _lane_pack(idxs, lane)
    tg_ref[...] = _lane_pack([e / tot for e in exps], lane)

    hot_sum = functools.reduce(lambda a, b: a + b, [h.astype(F32) for h in hots])
    row = lax.broadcasted_iota(jnp.int32, (tm, tm), 0)
    col = lax.broadcasted_iota(jnp.int32, (tm, tm), 1)
    before = _bdot((col < row).astype(F32), hot_sum) + carry[...]
    ranks = [jnp.sum(jnp.where(h, before, 0.0), axis=-1, keepdims=True).astype(jnp.int32) for h in hots]
    rk_ref[...] = _lane_pack(ranks, lane)
    carry[...] = carry[...] + jnp.sum(hot_sum, axis=0, keepdims=True)
    cnt_ref[...] = jnp.broadcast_to(carry[...], cnt_ref.shape)


def _post(y_hy, y_ml, x, mods, w_out, norm2_g, r_w, r_b, counts, mod_map, tm):
    t, d = x.shape
    n_exp = r_w.shape[1]
    rwp = jnp.pad(r_w, ((0, 0), (0, LANES - n_exp)))
    rbp = jnp.pad(r_b, (0, LANES - n_exp)).reshape(1, LANES)
    row = lambda wd: pl.BlockSpec((tm, wd), lambda i: (i, 0))
    const = lambda shape: pl.BlockSpec(shape, lambda i: (0, 0))
    return pl.pallas_call(
        functools.partial(_post_kernel, n_exp=n_exp, top_k=TOP_K),
        grid=(t // tm,),
        in_specs=[row(y_hy.shape[1]), row(y_ml.shape[1]), row(d),
                  pl.BlockSpec((1, 6, d), mod_map),
                  const(w_out.shape), const((1, d)), const((d, LANES)), const((1, LANES)),
                  const((SUBLANES, LANES))],
        out_specs=[row(d), pl.BlockSpec((tm * d // LANES, LANES), lambda i: (i, 0)),
                   row(LANES), row(LANES), row(LANES), const((SUBLANES, LANES))],
        out_shape=[jax.ShapeDtypeStruct((t, d), F32), jax.ShapeDtypeStruct((t * d // LANES, LANES), F32),
                   jax.ShapeDtypeStruct((t, LANES), jnp.int32), jax.ShapeDtypeStruct((t, LANES), F32),
                   jax.ShapeDtypeStruct((t, LANES), jnp.int32), jax.ShapeDtypeStruct((SUBLANES, LANES), F32)],
        scratch_shapes=[pltpu.VMEM((1, LANES), F32)],
        compiler_params=_cparams("arbitrary"),
        name="post",
    )(y_hy, y_ml, x, mods, w_out.astype(BF16), norm2_g.reshape(1, d), rwp, rbp, counts)


def _dest_kernel(ti_ref, rk_ref, ps_ref, d_ref, *, top_k):
    lane = lax.broadcasted_iota(jnp.int32, ti_ref.shape, 1)
    ti, rk = ti_ref[...], rk_ref[...]
    cols = []
    for j in range(top_k):
        start = jnp.sum(jnp.where(lane == ti[:, j:j + 1], ps_ref[...], 0.0), axis=-1, keepdims=True)
        cols.append(start.astype(jnp.int32) + rk[:, j:j + 1])
    d_ref[...] = _lane_pack(cols, lane)


def _dest_rows(ti, rk, pad_start, tm):
    t = ti.shape[0]
    row = pl.BlockSpec((tm, LANES), lambda i: (i, 0))
    return pl.pallas_call(
        functools.partial(_dest_kernel, top_k=TOP_K),
        grid=(t // tm,),
        in_specs=[row, row, pl.BlockSpec((1, LANES), lambda i: (0, 0))],
        out_specs=row,
        out_shape=jax.ShapeDtypeStruct((t, LANES), jnp.int32),
        compiler_params=_cparams("arbitrary"),
        name="dest_rows",
    )(ti, rk, pad_start)


def _dispatch_kernel(first_ref, count_ref, nu_ref, dest_ref, *refs, top_k, s, rows, tiles):
    h_refs = refs[:len(tiles)]
    xs_ref, zbuf, sem, zsem = refs[len(tiles):]
    i = pl.program_id(0)
    tt = dest_ref.shape[2] // top_k
    half = zbuf.shape[0] // s
    n_blocks = xs_ref.shape[0] // (rows * s)

    def pad_rows(e, carry, wait):
        off, n = first_ref[e], count_ref[e]
        for k in range(half.bit_length()):
            bit = half >> k

            @pl.when((n & bit) != 0)
            def _():
                dst = pl.multiple_of(off * s, s)
                copy = pltpu.make_async_copy(zbuf.at[pl.ds(0, bit * s)], xs_ref.at[pl.ds(dst, bit * s)], zsem)
                copy.wait() if wait else copy.start()
            off = off + (n & bit)
        return carry

    def spare_block(b, carry, wait):
        for part in range(2):
            dst = pl.multiple_of((b * 2 + part) * half * s, half * s)
            copy = pltpu.make_async_copy(zbuf, xs_ref.at[pl.ds(dst, half * s)], zsem)
            copy.wait() if wait else copy.start()
        return carry

    @pl.when(i == 0)
    def _():
        zbuf[...] = jnp.zeros_like(zbuf)
        for wait in (False, True):
            lax.fori_loop(0, first_ref.shape[0], functools.partial(pad_rows, wait=wait), 0)
            lax.fori_loop(nu_ref[0], n_blocks, functools.partial(spare_block, wait=wait), 0)

    def scatter(h_ref):
        def issue(t, carry):
            src = pl.multiple_of(t * s, s)
            for j in range(top_k):
                dst = pl.multiple_of(dest_ref[0, 0, t * top_k + j] * s, s)
                pltpu.make_async_copy(h_ref.at[pl.ds(src, s)], xs_ref.at[pl.ds(dst, s)], sem).start()
            return carry
        lax.fori_loop(0, tt, issue, 0, unroll=2)
        for _ in range(top_k):
            pltpu.make_async_copy(h_ref, xs_ref.at[pl.ds(0, tt * s)], sem).wait()

    lo = 0
    for h_ref, n_tiles in zip(h_refs, tiles):
        @pl.when((i >= lo) & (i < lo + n_tiles))
        def _():
            scatter(h_ref)
        lo += n_tiles


def _dispatch(dest, h2ts, pad_first, pad_count, n_used, n_blocks, rows, tt):
    s = sum(h.shape[0] for h in h2ts) * TOP_K // dest.size
    tiles = tuple(h.shape[0] // (tt * s) for h in h2ts)
    starts = [sum(tiles[:k]) for k in range(len(tiles))]
    hspec = lambda lo, n: pl.BlockSpec((tt * s, LANES), lambda i, *_: (jnp.clip(i - lo, 0, n - 1), 0))
    return pl.pallas_call(
        functools.partial(_dispatch_kernel, top_k=TOP_K, s=s, rows=rows, tiles=tiles),
        grid_spec=pltpu.PrefetchScalarGridSpec(
            num_scalar_prefetch=3,
            grid=(dest.shape[0],),
            in_specs=[pl.BlockSpec((1, 1, tt * TOP_K), lambda i, *_: (i, 0, 0), memory_space=pltpu.SMEM)]
            + [hspec(lo, n) for lo, n in zip(starts, tiles)],
            out_specs=pl.BlockSpec(memory_space=pl.ANY),
            scratch_shapes=[pltpu.VMEM((rows // 2 * s, LANES), F32),
                            pltpu.SemaphoreType.DMA(()), pltpu.SemaphoreType.DMA(())]),
        out_shape=jax.ShapeDtypeStruct((n_blocks * rows * s, LANES), F32),
        compiler_params=_cparams("arbitrary"),
        name="dispatch",
    )(pad_first, pad_count, n_used, dest, *h2ts)


def _deinterleave_table():
    p = np.zeros((2 * LANES, 2 * LANES), np.float32)
    j = np.arange(LANES)
    p[2 * j, j] = 1.0
    p[2 * j + 1, LANES + j] = 1.0
    return p


def _ffn_kernel(be_ref, nu_ref, xs_ref, wgu_ref, bg_ref, bl_ref, wd_ref, bd_ref, perm_ref, ys_ref,
                wg_buf, wl_buf, wd_buf, acc_buf, *, ft):
    i = pl.program_id(0)
    f = wg_buf.shape[1]
    rows = acc_buf.shape[0]
    live = i < nu_ref[0]

    @pl.when(live & ((i == 0) | (be_ref[i] != be_ref[jnp.maximum(i - 1, 0)])))
    def _():
        for j in range(f // LANES):
            blk = wgu_ref[0, :, 2 * LANES * j:2 * LANES * (j + 1)].astype(BF16)
            split = jnp.dot(blk, perm_ref[...], preferred_element_type=F32)
            wg_buf[:, LANES * j:LANES * (j + 1)] = split[:, :LANES].astype(BF16)
            wl_buf[:, LANES * j:LANES * (j + 1)] = split[:, LANES:].astype(BF16)
        wd_buf[...] = wd_ref[0].astype(BF16)

    @pl.when(live)
    def _():
        x = _tiles_to_rows(xs_ref, rows).astype(BF16)
        for j, f0 in enumerate(range(0, f, ft)):
            g = jnp.dot(x, wg_buf[:, f0:f0 + ft], preferred_element_type=F32) + bg_ref[0, :, f0:f0 + ft]
            lin = jnp.dot(x, wl_buf[:, f0:f0 + ft], preferred_element_type=F32) + bl_ref[0, :, f0:f0 + ft]
            gate = jnp.minimum(g, SWIGLU_LIMIT)
            lin = jnp.clip(lin, -SWIGLU_LIMIT, SWIGLU_LIMIT)
            act = (lin + 1.0) * gate * jax.nn.sigmoid(SWIGLU_ALPHA * gate)
            part = jnp.dot(act.astype(BF16), wd_buf[f0:f0 + ft, :], preferred_element_type=F32)
            if j == 0:
                acc_buf[...] = part + bd_ref[0]
            else:
                acc_buf[...] += part
        _rows_to_tiles(ys_ref, acc_buf[...])

    @pl.when(jnp.logical_not(live))
    def _():
        ys_ref[...] = jnp.zeros_like(ys_ref)


def _ffn(block_e, n_used, xs, w_gu, b_gu, w_d, b_d, rows):
    n_exp, d, f2 = w_gu.shape
    f = f2 // 2
    s = d // LANES
    n_rows = xs.shape[0] // s
    live = lambda i, nu: jnp.minimum(i, nu[0] - 1)
    wmap = lambda i, be, nu: (be[live(i, nu)], 0, 0)
    perm = jnp.asarray(_deinterleave_table(), BF16)
    return pl.pallas_call(
        functools.partial(_ffn_kernel, ft=min(f, 512)),
        grid_spec=pltpu.PrefetchScalarGridSpec(
            num_scalar_prefetch=2,
            grid=(n_rows // rows,),
            in_specs=[pl.BlockSpec((rows * s, LANES), lambda i, be, nu: (live(i, nu), 0)),
                      pl.BlockSpec((1, d, f2), wmap),
                      pl.BlockSpec((1, 1, f), wmap), pl.BlockSpec((1, 1, f), wmap),
                      pl.BlockSpec((1, f, d), wmap), pl.BlockSpec((1, 1, d), wmap),
                      pl.BlockSpec(perm.shape, lambda i, be, nu: (0, 0))],
            out_specs=pl.BlockSpec((rows * s, LANES), lambda i, be, nu: (i, 0)),
            scratch_shapes=[pltpu.VMEM((d, f), BF16), pltpu.VMEM((d, f), BF16), pltpu.VMEM((f, d), BF16),
                            pltpu.VMEM((rows, d), F32)]),
        out_shape=jax.ShapeDtypeStruct(xs.shape, F32),
        compiler_params=_cparams("arbitrary"),
        name="expert_ffn",
    )(block_e, n_used, xs, w_gu, b_gu[:, 0::2].reshape(n_exp, 1, f), b_gu[:, 1::2].reshape(n_exp, 1, f),
      w_d, b_d.reshape(n_exp, 1, d), perm)


def _combine_kernel(dest_ref, next_ref, tg_ref, x1_ref, mod_ref, fg_ref, ys_ref, o_ref, ybuf, sem, *,
                    top_k, final_norm):
    i = pl.program_id(0)
    tt = x1_ref.shape[0]
    s = ybuf.shape[2] // tt
    slot = i % 2

    def gather(d_ref, to):
        def issue(t, carry):
            dst = pl.multiple_of(t * s, s)
            for j in range(top_k):
                src = pl.multiple_of(d_ref[0, 0, t * top_k + j] * s, s)
                pltpu.make_async_copy(ys_ref.at[pl.ds(src, s)], ybuf.at[to, j, pl.ds(dst, s)], sem.at[to]).start()
            return carry
        lax.fori_loop(0, tt, issue, 0, unroll=2)

    @pl.when(i == 0)
    def _():
        gather(dest_ref, 0)

    @pl.when(i + 1 < pl.num_programs(0))
    def _():
        gather(next_ref, 1 - slot)

    for j in range(top_k):
        pltpu.make_async_copy(ys_ref.at[pl.ds(0, tt * s)], ybuf.at[slot, j], sem.at[slot]).wait()
    tg = tg_ref[...]
    moe = tg[:, 0:1] * _tiles_to_rows(ybuf.at[slot, 0], tt)
    for j in range(1, top_k):
        moe = moe + tg[:, j:j + 1] * _tiles_to_rows(ybuf.at[slot, j], tt)
    x2 = x1_ref[...] + mod_ref[0, 5:6, :] * moe
    if final_norm:
        x2 = x2 * lax.rsqrt(jnp.mean(x2 * x2, axis=-1, keepdims=True) + EPS) * fg_ref[...]
    o_ref[...] = x2


def _combine(dest, tg, x1, mods, final_g, ys, mod_map, tt, final_norm):
    t, d = x1.shape
    steps = t // tt
    return pl.pallas_call(
        functools.partial(_combine_kernel, top_k=TOP_K, final_norm=final_norm),
        grid=(steps,),
        in_specs=[pl.BlockSpec((1, 1, tt * TOP_K), lambda i: (i, 0, 0), memory_space=pltpu.SMEM),
                  pl.BlockSpec((1, 1, tt * TOP_K), lambda i: (jnp.minimum(i + 1, steps - 1), 0, 0),
                               memory_space=pltpu.SMEM),
                  pl.BlockSpec((tt, LANES), lambda i: (i, 0)),
                  pl.BlockSpec((tt, d), lambda i: (i, 0)),
                  pl.BlockSpec((1, 6, d), mod_map),
                  pl.BlockSpec((1, d), lambda i: (0, 0)),
                  pl.BlockSpec(memory_space=pl.ANY)],
        out_specs=pl.BlockSpec((tt, d), lambda i: (i, 0)),
        out_shape=jax.ShapeDtypeStruct((t, d), F32),
        scratch_shapes=[pltpu.VMEM((2, TOP_K, tt * d // LANES, LANES), F32), pltpu.SemaphoreType.DMA((2,))],
        compiler_params=_cparams("arbitrary"),
        name="combine",
    )(dest, dest, tg, x1, mods, final_g.reshape(1, d), ys)


def _moe_plan(counts, rows, n_blocks):
    n_exp = counts.shape[0]
    padded = (counts + rows - 1) // rows * rows
    pad_end = jnp.cumsum(padded)
    block_row = jnp.arange(n_blocks, dtype=jnp.int32) * rows
    block_e = jnp.minimum(jnp.sum(pad_end[None, :] <= block_row[:, None], axis=1), n_exp - 1).astype(jnp.int32)
    n_used = (pad_end[-1:] // rows).astype(jnp.int32)
    start = pad_end - padded
    pad_start = jnp.pad(start.astype(F32), (0, LANES - n_exp)).reshape(1, LANES)
    return pad_start, block_e, n_used, (start + counts).astype(jnp.int32), (padded - counts).astype(jnp.int32)


def _sequence_mixers(z_hy, z_qk, z_v, z_o, z_g, lw, state, row_w):
    (hy_cw, hy_cb, filt_params, hy_b, ml_cw, ml_cb, ml_gb, ml_ng) = lw
    y_hy = _hyena(z_hy, hy_cw, hy_cb, filt_params, hy_b, row_w)
    ml_w = z_v.shape[-1]
    q = _short_conv(z_qk, ml_cw, ml_cb, row_w, silu=True, col0=0, ncols=ml_w)
    kt = _short_conv(z_qk, ml_cw, ml_cb, row_w, silu=True, col0=ml_w, ncols=ml_w, transpose=True)
    y_ml, st = _mlstm(q, kt, z_v, z_o, z_g, ml_gb, ml_ng, *state)
    return y_hy, y_ml, st


def kernel(x_prompt, x_sample, state_mlstm_C, state_mlstm_n, state_mlstm_m, c, c_ctx, ada_w, ada_b, norm1_g,
           w_in, hy_conv_w, hy_conv_b, filt_w1, filt_b1, filt_w2, filt_b2, filt_w3, filt_freq, hy_bias,
           ml_conv_w, ml_conv_b, ml_gate_b, ml_norm_g, w_out, norm2_g, router_w, router_b, moe_w_gu,
           moe_b_gu, moe_w_down, moe_b_down, final_g):
    bp, lp, d = x_prompt.shape
    bs, ls, _ = x_sample.shape
    depth = ada_w.shape[0]
    heads = ml_gate_b.shape[-1]
    hy_w = hy_bias.shape[-1]
    ml_w = ml_norm_g.shape[-1]
    dh = ml_w // heads
    n_exp = router_w.shape[-1]
    t = bp * lp + bs * ls
    ng = N_DIR * 2 * heads
    seg_widths = (3 * hy_w, 2 * ml_w, ml_w, ml_w)
    n_main = 3 * hy_w + 4 * ml_w
    n_blocks = -(-(t * TOP_K) // MOE_ROWS) + n_exp

    cond = jnp.concatenate([c_ctx[None], c, jnp.zeros((SUBLANES - 1 - bs, d), F32)], axis=0)
    zero_state = (jnp.zeros((bp, N_DIR, heads, dh, dh), F32), jnp.zeros((bp, N_DIR, heads, dh), F32),
                  jnp.zeros((bp, N_DIR, heads), F32))
    xs_paths = [x_prompt, x_sample]
    path_cfg = [(0, 0, lp), (1, 1, GRID_W)]
    new_c, new_n, new_m = [], [], []
    for l in range(depth):
        mods = _ada(cond, ada_w[l], ada_b[l]).reshape(SUBLANES, 6, d)
        w_main = w_in[l][:, :n_main].astype(BF16)
        w_gate = jnp.pad(w_in[l][:, n_main:], ((0, 0), (0, LANES - ng)))
        lw = (hy_conv_w[l], hy_conv_b[l],
              (filt_w1[l], filt_b1[l], filt_w2[l], filt_b2[l], filt_w3[l], filt_freq[l]), hy_bias[l],
              ml_conv_w[l], ml_conv_b[l], ml_gate_b[l], ml_norm_g[l])
        states = [zero_state, (state_mlstm_C[:, l], state_mlstm_n[:, l], state_mlstm_m[:, l])]
        counts = jnp.zeros((SUBLANES, LANES), F32)
        routed = []
        for x3, (mod0, mod_step, row_w), state in zip(xs_paths, path_cfg, states):
            bsz, lseq, _ = x3.shape
            tm, tt = min(ROW_TILE, lseq), min(TOK_TILE, lseq)
            xf = x3.reshape(bsz * lseq, d)
            z = _inproj(xf, mods, norm1_g[l], w_main, w_gate, seg_widths,
                        _mod_index_map(mod0, mod_step, lseq // tm), tm)
            y_hy, y_ml, st = _sequence_mixers(*[a.reshape(bsz, lseq, a.shape[1]) for a in z], lw, state, row_w)
            x1, h2t, ti, tg, rk, counts = _post(
                y_hy.reshape(bsz * lseq, hy_w), y_ml.reshape(bsz * lseq, ml_w), xf, mods, w_out[l], norm2_g[l],
                router_w[l], router_b[l], counts, _mod_index_map(mod0, mod_step, lseq // tm), tm)
            routed.append((x1, h2t, ti, tg, rk, tm, tt, _mod_index_map(mod0, mod_step, lseq // tt), st))
        new_c.append(routed[0][-1][0])
        new_n.append(routed[0][-1][1])
        new_m.append(routed[0][-1][2])

        pad_start, block_e, n_used, pad_first, pad_count = _moe_plan(
            counts[0, :n_exp].astype(jnp.int32), MOE_ROWS, n_blocks)
        tt = routed[0][6]
        assert all(r[6] == tt for r in routed)
        dests = [_dest_rows(ti, rk, pad_start, tm)[:, :TOP_K].reshape(-1, 1, tt * TOP_K)
                 for _, _, ti, _, rk, tm, _, _, _ in routed]
        xs = _dispatch(jnp.concatenate(dests, axis=0), [r[1] for r in routed], pad_first, pad_count, n_used,
                       n_blocks, MOE_ROWS, tt)
        ys = _ffn(block_e, n_used, xs, moe_w_gu[l], moe_b_gu[l], moe_w_down[l], moe_b_down[l], MOE_ROWS)
        xs_paths = [
            _combine(dest, tg, x1, mods, final_g, ys, mod_map, tt, final_norm=l == depth - 1).reshape(x3.shape)
            for dest, (x1, _, _, tg, _, _, tt, mod_map, _), x3 in zip(dests, routed, xs_paths)]
    return (xs_paths[0], xs_paths[1],
            jnp.stack(new_c, axis=1), jnp.stack(new_n, axis=1), jnp.stack(new_m, axis=1))
```

```python
import functools
import math

import numpy as np
import jax
import jax.numpy as jnp
from jax import lax
from jax.experimental import pallas as pl
from jax.experimental.pallas import tpu as pltpu

F32 = jnp.float32
BF16 = jnp.bfloat16
HIGHEST = lax.Precision.HIGHEST
EPS = 1e-6

LANES = 128
SUBLANES = 8
VMEM_LIMIT_BYTES = 56 * 1024 * 1024

GRID_W = 64
ML_HEADS = 4
N_DIR = 2
HY_ORDER = 2
FILT_BANDS = 8
DECAY_TARGET = 1e-2
FAST_DECAY_PCT = 0.3
SLOW_DECAY_PCT = 1.5
TOP_K = 4
SWIGLU_LIMIT = 7.0
SWIGLU_ALPHA = 1.702

FFT_N2 = 128
MLSTM_CHUNK = 128
MLSTM_GROUP = 8
MLSTM_PAIR = 4
MOE_ROWS = 512
ROW_TILE = 512
TOK_TILE = 256


def _cparams(*sem):
    return pltpu.CompilerParams(dimension_semantics=sem, vmem_limit_bytes=VMEM_LIMIT_BYTES)


def _lane_tile(c, cap):
    return max(t for t in range(LANES, min(c, cap) + 1, LANES) if c % t == 0)


def _bdot(a, b):
    return jnp.dot(a.astype(BF16), b.astype(BF16), preferred_element_type=F32)


def _hdot(a, b):
    return jnp.dot(a, b, precision=HIGHEST, preferred_element_type=F32)


def _dot3(a, b):
    a_hi, b_hi = a.astype(BF16), b.astype(BF16)
    a_lo = (a - a_hi.astype(F32)).astype(BF16)
    b_lo = (b - b_hi.astype(F32)).astype(BF16)
    dot = functools.partial(jnp.dot, preferred_element_type=F32)
    return dot(a_hi, b_hi) + dot(a_hi, b_lo) + dot(a_lo, b_hi)


def _ada_kernel(c_ref, w_ref, b_ref, o_ref):
    c = c_ref[...]
    o_ref[...] = _hdot(c * jax.nn.sigmoid(c), w_ref[...]) + b_ref[...]


def _ada(cond, w, b):
    r, d = cond.shape
    n = w.shape[1]
    tn = _lane_tile(n, 1024)
    return pl.pallas_call(
        _ada_kernel,
        grid=(n // tn,),
        in_specs=[pl.BlockSpec((r, d), lambda j: (0, 0)),
                  pl.BlockSpec((d, tn), lambda j: (0, j)),
                  pl.BlockSpec((1, tn), lambda j: (0, j))],
        out_specs=pl.BlockSpec((r, tn), lambda j: (0, j)),
        out_shape=jax.ShapeDtypeStruct((r, n), F32),
        compiler_params=_cparams("arbitrary"),
        name="ada",
    )(cond, w, b.reshape(1, n))


def _mod_index_map(mod0, mod_step, tiles_per_seq):
    def index_map(i):
        return (mod0 + (i // tiles_per_seq) * mod_step, 0, 0)
    return index_map


def _inproj_kernel(x_ref, mod_ref, g_ref, w_ref, wg_ref, *out_refs, offsets):
    x = x_ref[...]
    h = x * lax.rsqrt(jnp.mean(x * x, axis=-1, keepdims=True) + EPS) * g_ref[...]
    h = h * (1.0 + mod_ref[0, 1:2, :]) + mod_ref[0, 0:1, :]
    hb = h.astype(BF16)
    for o_ref, (lo, hi) in zip(out_refs[:-1], offsets):
        o_ref[...] = jnp.dot(hb, w_ref[:, lo:hi], preferred_element_type=F32)
    h_lo = (h - hb.astype(F32)).astype(BF16)
    g = jnp.dot(hb, wg_ref[...], preferred_element_type=F32)
    out_refs[-1][...] = (g[:, :LANES] + g[:, LANES:]
                         + jnp.dot(h_lo, wg_ref[:, :LANES], preferred_element_type=F32))


def _inproj(x, mods, norm_g, w_main, w_gate, seg_widths, mod_map, tm):
    t, d = x.shape
    offsets, lo = [], 0
    for wd in seg_widths:
        offsets.append((lo, lo + wd))
        lo += wd
    wg_hi = w_gate.astype(BF16)
    wg = jnp.concatenate([wg_hi, (w_gate - wg_hi.astype(F32)).astype(BF16)], axis=1)
    widths = tuple(seg_widths) + (LANES,)
    return pl.pallas_call(
        functools.partial(_inproj_kernel, offsets=tuple(offsets)),
        grid=(t // tm,),
        in_specs=[pl.BlockSpec((tm, d), lambda i: (i, 0)),
                  pl.BlockSpec((1, 6, d), mod_map),
                  pl.BlockSpec((1, d), lambda i: (0, 0)),
                  pl.BlockSpec(w_main.shape, lambda i: (0, 0)),
                  pl.BlockSpec(wg.shape, lambda i: (0, 0))],
        out_specs=[pl.BlockSpec((tm, wd), lambda i: (i, 0)) for wd in widths],
        out_shape=[jax.ShapeDtypeStruct((t, wd), F32) for wd in widths],
        compiler_params=_cparams("arbitrary"),
        name="inproj",
    )(x, mods, norm_g.reshape(1, d), w_main, wg)


def _short_conv_kernel(x_ref, w_ref, b_ref, o_ref, *, row_w, silu, transpose):
    x = x_ref[0]
    l = x.shape[0]
    pos = lax.broadcasted_iota(jnp.int32, x.shape, 0) % row_w
    prev = jnp.where(pos == 0, 0.0, pltpu.roll(x, 1, 0))
    nxt = jnp.where(pos == row_w - 1, 0.0, pltpu.roll(x, l - 1, 0))
    y = prev * w_ref[0:1, :] + x * w_ref[1:2, :] + nxt * w_ref[2:3, :] + b_ref[...]
    if silu:
        y = y * jax.nn.sigmoid(y)
    o_ref[0] = y.T if transpose else y


def _short_conv(x, w, b, row_w, silu, col0=0, ncols=None, transpose=False):
    bsz, l, c = x.shape
    ncols = c if ncols is None else ncols
    ct = _lane_tile(math.gcd(ncols, col0) if col0 else ncols, LANES if transpose else 512)
    j0 = col0 // ct
    out_shape, out_block, out_map = (bsz, l, ncols), (1, l, ct), lambda i, j: (i, 0, j)
    if transpose:
        out_shape, out_block, out_map = (bsz, ncols, l), (1, ct, l), lambda i, j: (i, j, 0)
    return pl.pallas_call(
        functools.partial(_short_conv_kernel, row_w=row_w, silu=silu, transpose=transpose),
        grid=(bsz, ncols // ct),
        in_specs=[pl.BlockSpec((1, l, ct), lambda i, j: (i, 0, j0 + j)),
                  pl.BlockSpec((3, ct), lambda i, j: (0, j0 + j)),
                  pl.BlockSpec((1, ct), lambda i, j: (0, j0 + j))],
        out_specs=pl.BlockSpec(out_block, out_map),
        out_shape=jax.ShapeDtypeStruct(out_shape, F32),
        compiler_params=_cparams("arbitrary", "arbitrary"),
        name="short_conv",
    )(x, w, b.reshape(1, c))


def _dft_direct_tables(l):
    n = 2 * l
    k = np.arange(n)[:, None].astype(np.float64)
    t = np.arange(n)[None, :].astype(np.float64)
    ang = 2.0 * np.pi * ((k * t) % n) / n
    cm, sm = np.cos(ang), np.sin(ang)
    fwd = np.block([[cm[:, :l], sm[:, :l]], [-sm[:, :l], cm[:, :l]]])
    filt = np.concatenate([cm, -sm], axis=0)
    return fwd, filt


def _dft_two_level_tables(l, n2):
    n = 2 * l
    n1 = n // n2
    k1 = np.arange(n1)[:, None].astype(np.float64)
    a = np.arange(n1)[None, :].astype(np.float64)
    ang1 = 2.0 * np.pi * ((k1 * a) % n1) / n1
    c1, s1 = np.cos(ang1), np.sin(ang1)
    h = n1 // 2
    m1 = np.block([[c1[:, :h], s1[:, :h]], [-s1[:, :h], c1[:, :h]]])
    m1f = np.concatenate([c1, -s1], axis=0)
    kk = (np.arange(n1)[:, None, None] + n1 * np.arange(n2)[None, :, None]).astype(np.float64)
    b = np.arange(n2)[None, None, :].astype(np.float64)
    ang = 2.0 * np.pi * ((kk * b) % n) / n
    cg, sg = np.cos(ang), np.sin(ang)
    gt = np.concatenate([np.concatenate([cg, sg], axis=2),
                         np.concatenate([-sg, cg], axis=2)], axis=1)
    return m1, m1f, gt


def _circular_lag(n0, rows, l):
    n = n0 + lax.broadcasted_iota(jnp.int32, (rows, 1), 0)
    t = jnp.where(n < l, n, 2 * l - n).astype(F32)
    return n, t, t / float(max(l - 1, 1))


def _filter_hidden_kernel(bandv_ref, w1_ref, b1_ref, w2_ref, b2_ref, freq_ref, o_ref, *, l):
    rows = o_ref.shape[0]
    _, t, t01 = _circular_lag(pl.program_id(0) * rows, rows, l)
    lane = lax.broadcasted_iota(jnp.int32, (rows, LANES), 1)
    ang = (2.0 * math.pi / l) * t * bandv_ref[...]
    feats = jnp.where(lane == 0, t01,
                      jnp.where(lane <= FILT_BANDS, jnp.cos(ang),
                                jnp.where(lane <= 2 * FILT_BANDS, -jnp.sin(ang), 0.0)))
    fr = freq_ref[...]
    h = jnp.sin(fr * (_hdot(feats, w1_ref[...]) + b1_ref[...]))
    o_ref[...] = jnp.sin(fr * (_hdot(h, w2_ref[...]) + b2_ref[...]))


def _filter_hidden(l, f_w1, f_b1, f_w2, f_b2, f_freq):
    emb, hid = f_w1.shape
    n = 2 * l
    rows = min(n, 512)
    bands = jnp.linspace(1e-4, FILT_BANDS - 1, FILT_BANDS, dtype=F32)
    bandv = jnp.zeros((1, LANES), F32).at[0, 1:1 + FILT_BANDS].set(bands)
    bandv = bandv.at[0, 1 + FILT_BANDS:1 + 2 * FILT_BANDS].set(bands)
    w1p = jnp.zeros((LANES, hid), F32).at[:emb].set(f_w1)
    c0 = lambda i: (0, 0)
    return pl.pallas_call(
        functools.partial(_filter_hidden_kernel, l=l),
        grid=(n // rows,),
        in_specs=[pl.BlockSpec((1, LANES), c0), pl.BlockSpec((LANES, hid), c0), pl.BlockSpec((1, hid), c0),
                  pl.BlockSpec((hid, hid), c0), pl.BlockSpec((1, hid), c0), pl.BlockSpec((1, hid), c0)],
        out_specs=pl.BlockSpec((rows, hid), lambda i: (i, 0)),
        out_shape=jax.ShapeDtypeStruct((n, hid), F32),
        compiler_params=_cparams("arbitrary"),
        name="filter_hidden",
    )(bandv, w1p, f_b1.reshape(1, hid), f_w2, f_b2.reshape(1, hid), f_freq.reshape(1, hid))


def _filter_rows(n0, rows, l, hid_ref, w3_ref, delta_ref):
    n, _, t01 = _circular_lag(n0, rows, l)
    h = hid_ref[pl.ds(n0, rows), :]
    hf = _bdot(h, w3_ref[0, 0])
    hb = _bdot(h, w3_ref[0, 1])
    window = jnp.exp(-t01 * delta_ref[...])
    return jnp.where(n < l, hf, jnp.where(n > l, hb, 0.0)) * window


def _filter_direct_kernel(hid_ref, w3_ref, delta_ref, ff_ref, h_ref, *, l):
    hc = _filter_rows(0, 2 * l, l, hid_ref, w3_ref, delta_ref)
    h_ref[0] = _bdot(ff_ref[...], hc) * (1.0 / (2 * l))


def _filter_two_level_kernel(hid_ref, w3_ref, delta_ref, m1f_ref, gt_ref, h_ref, hc_buf, a_buf, *, l, n2, rows):
    n = 2 * l
    n1 = n // n2

    def fill(i, carry):
        r0 = pl.multiple_of(i * rows, rows)
        hc_buf[pl.ds(r0, rows), :] = _filter_rows(r0, rows, l, hid_ref, w3_ref, delta_ref)
        return carry
    lax.fori_loop(0, n // rows, fill, 0)

    def step1(b, carry):
        col = hc_buf[pl.ds(b, n1, stride=n2), :]
        a = _bdot(m1f_ref[...], col)
        a_buf[pl.ds(b, n1, stride=2 * n2), :] = a[:n1]
        a_buf[pl.ds(n2 + b, n1, stride=2 * n2), :] = a[n1:]
        return carry
    lax.fori_loop(0, n2, step1, 0, unroll=8)

    def step2(k1, carry):
        r0 = pl.multiple_of(k1 * 2 * n2, 2 * n2)
        h_ref[0, k1] = (_bdot(gt_ref[k1], a_buf[pl.ds(r0, 2 * n2), :]) * (1.0 / n)).astype(h_ref.dtype)
        return carry
    lax.fori_loop(0, n1, step2, 0, unroll=4)


def _filter_tail_inputs(hy_w, f_w3):
    hid = f_w3.shape[0]
    w3 = f_w3.reshape(hid, HY_ORDER, N_DIR, hy_w).transpose(1, 2, 0, 3)
    max_decay = math.log(DECAY_TARGET) / FAST_DECAY_PCT
    min_decay = math.log(DECAY_TARGET) / SLOW_DECAY_PCT
    deltas = jnp.abs(jnp.linspace(min_decay, max_decay, hy_w, dtype=F32)).reshape(1, hy_w)
    return w3, deltas


def _filter_spectrum_direct(l, hy_w, filt_params, ff):
    f_w1, f_b1, f_w2, f_b2, f_w3, f_freq = filt_params
    hidden = _filter_hidden(l, f_w1, f_b1, f_w2, f_b2, f_freq)
    w3, deltas = _filter_tail_inputs(hy_w, f_w3)
    hid = f_w3.shape[0]
    n = 2 * l
    return pl.pallas_call(
        functools.partial(_filter_direct_kernel, l=l),
        grid=(HY_ORDER,),
        in_specs=[pl.BlockSpec((n, hid), lambda o: (0, 0)),
                  pl.BlockSpec((1, N_DIR, hid, hy_w), lambda o: (o, 0, 0, 0)),
                  pl.BlockSpec((1, hy_w), lambda o: (0, 0)),
                  pl.BlockSpec((2 * n, n), lambda o: (0, 0))],
        out_specs=pl.BlockSpec((1, 2 * n, hy_w), lambda o: (o, 0, 0)),
        out_shape=jax.ShapeDtypeStruct((HY_ORDER, 2 * n, hy_w), F32),
        compiler_params=_cparams("arbitrary"),
        name="filter_direct",
    )(hidden, w3, deltas, ff)


def _filter_spectrum_two_level(l, hy_w, filt_params, m1f, gt, ct):
    f_w1, f_b1, f_w2, f_b2, f_w3, f_freq = filt_params
    hidden = _filter_hidden(l, f_w1, f_b1, f_w2, f_b2, f_freq)
    w3, deltas = _filter_tail_inputs(hy_w, f_w3)
    hid = f_w3.shape[0]
    n = 2 * l
    n2 = FFT_N2
    n1 = n // n2
    return pl.pallas_call(
        functools.partial(_filter_two_level_kernel, l=l, n2=n2, rows=min(n, 512)),
        grid=(HY_ORDER, hy_w // ct),
        in_specs=[pl.BlockSpec((n, hid), lambda o, j: (0, 0)),
                  pl.BlockSpec((1, N_DIR, hid, ct), lambda o, j: (o, 0, 0, j)),
                  pl.BlockSpec((1, ct), lambda o, j: (0, j)),
                  pl.BlockSpec((2 * n1, n1), lambda o, j: (0, 0)),
                  pl.BlockSpec((n1, 2 * n2, 2 * n2), lambda o, j: (0, 0, 0))],
        out_specs=pl.BlockSpec((1, n1, 2 * n2, ct), lambda o, j: (o, 0, 0, j)),
        out_shape=jax.ShapeDtypeStruct((HY_ORDER, n1, 2 * n2, hy_w), BF16),
        scratch_shapes=[pltpu.VMEM((n, ct), F32), pltpu.VMEM((n1 * 2 * n2, ct), F32)],
        compiler_params=_cparams("arbitrary", "arbitrary"),
        name="filter_two_level",
    )(hidden, w3, deltas, m1f, gt)


def _complex_mul(x, h, half):
    xr, xi = x[:half], x[half:]
    hr, hi = h[:half], h[half:]
    return jnp.concatenate([xr * hr - xi * hi, xr * hi + xi * hr], axis=0)


def _conv_direct_kernel(z_ref, gate_ref, bias_ref, h_ref, fwd_ref, inv_ref, o_ref):
    l = z_ref.shape[1]
    z = jnp.concatenate([z_ref[0], z_ref[1]], axis=0)
    x = _bdot(fwd_ref[...], z)
    y = _bdot(inv_ref[...], _complex_mul(x, h_ref[0], 2 * l))
    bias = bias_ref[0]
    o_ref[0] = gate_ref[0] * (y[:l] + bias * z_ref[0])
    o_ref[1] = gate_ref[1] * (y[l:] + bias * z_ref[1])


def _conv_two_level_kernel(z_ref, gate_ref, bias_ref, h_ref, m1_ref, m1i_ref, gt_ref, o_ref,
                           a_buf, *, n2):
    l = z_ref.shape[1]
    n1 = 2 * l // n2
    hn = n1 // 2

    def step1(b, carry):
        za = z_ref[0, pl.ds(b, hn, stride=n2), :]
        zb = z_ref[1, pl.ds(b, hn, stride=n2), :]
        a = _bdot(m1_ref[...], jnp.concatenate([za, zb], axis=0))
        a_buf[pl.ds(b, n1, stride=2 * n2), :] = a[:n1]
        a_buf[pl.ds(n2 + b, n1, stride=2 * n2), :] = a[n1:]
        return carry
    lax.fori_loop(0, n2, step1, 0, unroll=8)

    def step2(k1, carry):
        r0 = pl.multiple_of(k1 * 2 * n2, 2 * n2)
        x = _bdot(gt_ref[k1], a_buf[pl.ds(r0, 2 * n2), :])
        y = _complex_mul(x, h_ref[0, k1].astype(F32), n2).astype(BF16)
        a_buf[pl.ds(r0, 2 * n2), :] = lax.dot_general(gt_ref[k1], y, (((0,), (0,)), ((), ())),
                                                      preferred_element_type=F32)
        return carry
    lax.fori_loop(0, n1, step2, 0, unroll=4)

    def step3(b, carry):
        br = a_buf[pl.ds(b, n1, stride=2 * n2), :]
        bi = a_buf[pl.ds(n2 + b, n1, stride=2 * n2), :]
        y = _bdot(m1i_ref[...], jnp.concatenate([br, bi], axis=0))
        o_ref[0, pl.ds(b, hn, stride=n2), :] = y[:hn]
        o_ref[1, pl.ds(b, hn, stride=n2), :] = y[hn:]
        return carry
    lax.fori_loop(0, n2, step3, 0, unroll=8)

    bias = bias_ref[0]
    for s in range(2):
        o_ref[s] = gate_ref[s] * (o_ref[s] + bias * z_ref[s])


def _long_conv_gated(u, z, z_col, gate_col, spectrum, order, bias, tables, ct):
    bsz, l, _ = u.shape
    c = spectrum.shape[-1]
    nct = c // ct
    zspec = pl.BlockSpec((2, l, ct), lambda i, j: (i, 0, z_col * nct + j))
    gspec = pl.BlockSpec((2, l, ct), lambda i, j: (i, 0, gate_col * nct + j))
    bspec = pl.BlockSpec((1, 1, ct), lambda i, j: (order, 0, j))
    ospec = pl.BlockSpec((2, l, ct), lambda i, j: (i, 0, j))
    out_shape = jax.ShapeDtypeStruct((bsz, l, c), F32)
    bias3 = bias.reshape(HY_ORDER, 1, c)
    if len(tables) == 2:
        fwd, inv = tables
        n = 2 * l
        return pl.pallas_call(
            _conv_direct_kernel,
            grid=(bsz // 2, nct),
            in_specs=[zspec, gspec, bspec,
                      pl.BlockSpec((1, 2 * n, ct), lambda i, j: (order, 0, j)),
                      pl.BlockSpec(fwd.shape, lambda i, j: (0, 0)),
                      pl.BlockSpec(inv.shape, lambda i, j: (0, 0))],
            out_specs=ospec, out_shape=out_shape,
            compiler_params=_cparams("arbitrary", "arbitrary"),
            name="long_conv_direct",
        )(z, u, bias3, spectrum, fwd, inv)
    m1, m1i, gt = tables
    n2 = FFT_N2
    n1 = 2 * l // n2
    const2 = lambda i, j: (0, 0)
    const3 = lambda i, j: (0, 0, 0)
    return pl.pallas_call(
        functools.partial(_conv_two_level_kernel, n2=n2),
        grid=(nct, bsz // 2),
        in_specs=[pl.BlockSpec((2, l, ct), lambda j, i: (i, 0, z_col * nct + j)),
                  pl.BlockSpec((2, l, ct), lambda j, i: (i, 0, gate_col * nct + j)),
                  pl.BlockSpec((1, 1, ct), lambda j, i: (order, 0, j)),
                  pl.BlockSpec((1, n1, 2 * n2, ct), lambda j, i: (order, 0, 0, j)),
                  pl.BlockSpec(m1.shape, const2), pl.BlockSpec(m1i.shape, const2),
                  pl.BlockSpec(gt.shape, const3)],
        out_specs=pl.BlockSpec((2, l, ct), lambda j, i: (i, 0, j)),
        out_shape=out_shape,
        scratch_shapes=[pltpu.VMEM((n1 * 2 * n2, ct), F32)],
        compiler_params=_cparams("arbitrary", "arbitrary"),
        name="long_conv_two_level",
    )(z, u, bias3, spectrum, m1, m1i, gt)


def _hyena(z_hy, conv_w, conv_b, filt_params, hy_bias, row_w):
    bsz, l, c3 = z_hy.shape
    c = c3 // 3
    u = _short_conv(z_hy, conv_w, conv_b, row_w, silu=False)
    table = lambda a: jnp.asarray(a, F32).astype(BF16)
    if 2 * l // FFT_N2 <= 4:
        fwd, filt = _dft_direct_tables(l)
        tables = (table(fwd), table(fwd.T))
        spectrum = _filter_spectrum_direct(l, c, filt_params, table(filt))
        ct = c
    else:
        m1, m1f, gt = _dft_two_level_tables(l, FFT_N2)
        tables = (table(m1), table(m1.T), table(gt))
        ct = LANES
        spectrum = _filter_spectrum_two_level(l, c, filt_params, table(m1f), tables[2], ct)
    z1 = _long_conv_gated(u, u, 0, 1, spectrum, 0, hy_bias, tables, ct)
    return _long_conv_gated(u, z1, 0, 2, spectrum, 1, hy_bias, tables, ct)


def _log_sigmoid(x):
    return jnp.minimum(x, 0.0) - jnp.log1p(jnp.exp(-jnp.abs(x)))


def _split3_dot(a, b, split_lhs):
    x = a if split_lhs else b
    hi = x.astype(BF16)
    rest = x - hi.astype(F32)
    mid = rest.astype(BF16)
    parts = (hi, mid, (rest - mid.astype(F32)).astype(BF16))
    if split_lhs:
        return functools.reduce(lambda u, w: u + w, [jnp.dot(p, b, preferred_element_type=F32) for p in parts])
    return functools.reduce(lambda u, w: u + w, [jnp.dot(a, p, preferred_element_type=F32) for p in parts])


def _mlstm_recur(qb, sv, m_loc, kv, g_loc, bq, btot, cn, m):
    dh = qb.shape[1]
    inter = bq + m
    mj = jnp.maximum(m_loc, inter)
    w_int = jnp.exp(inter - mj)
    w_loc = jnp.exp(m_loc - mj)
    qc = jnp.dot(qb, cn.astype(BF16), preferred_element_type=F32)
    num = w_int * qc[:, :dh] + w_loc * sv[:, :dh]
    den = w_int * qc[:, dh:] + w_loc * sv[:, dh:]
    h = num / jnp.maximum(jnp.abs(den), jnp.exp(-mj))
    m_new = jnp.maximum(btot + m, g_loc)
    cn_new = jnp.exp(btot + m - m_new) * cn + jnp.exp(g_loc - m_new) * kv
    return h, cn_new, m_new


def _mlstm_kernel(q_ref, kt_ref, v_ref, o_ref, g_ref, gt_ref, gb_ref, gbt_ref, ng_ref, c0_ref, n0_ref, m0_ref,
                  y_ref, c_ref, n_ref, m_ref, hf_buf, hb_buf, cn_buf, sv_buf, kv_buf, bq_buf, ml_buf, sc_buf,
                  *, chunk, heads):
    l, dh = q_ref.shape[1], q_ref.shape[2]
    nc = l // chunk
    group = sv_buf.shape[1]
    pair = math.gcd(group, MLSTM_PAIR)
    scale = dh ** -0.5
    row = lax.broadcasted_iota(jnp.int32, (chunk, chunk), 0)
    col = lax.broadcasted_iota(jnp.int32, (chunk, chunk), 1)
    lower, upper = col <= row, col >= row
    tri_l, tri_u = lower.astype(BF16), upper.astype(BF16)
    gate_row = lax.broadcasted_iota(jnp.int32, (SUBLANES, chunk), 0)
    ones = jnp.ones((chunk, dh), BF16)

    sel_row = lax.broadcasted_iota(jnp.int32, (LANES, LANES), 0)
    head = pl.program_id(1)

    def chunk_start(j, d):
        return pl.multiple_of(((nc - 1 - j) if d else j) * chunk, chunk)

    def local(gp, carry, j0):
        jobs = [(gp * pair + k, d) for k in range(pair) for d in range(N_DIR)]
        r0s = [chunk_start(j0 + g, d) for g, d in jobs]
        picks = [(sel_row == (2 * d + 1) * heads + head).astype(BF16) for d in range(N_DIR)]
        lfs = [_log_sigmoid(g_ref[0, pl.ds(r0, chunk), :] + gb_ref[...]) for r0 in r0s]
        gts = [gt_ref[0, 0, :, pl.ds(r0, chunk)] + gbt_ref[0] for r0 in r0s]
        gts = [jnp.where(gate_row % 2 == 1, _log_sigmoid(gt), gt) for gt in gts]
        lfs = [_split3_dot(lf, picks[d], split_lhs=True) for lf, (_, d) in zip(lfs, jobs)]
        brows = [_split3_dot(gt, tri_l if d else tri_u, split_lhs=True)[2 * d + 1:2 * d + 2, :]
                 for gt, (_, d) in zip(gts, jobs)]
        bqs = [_split3_dot(tri_u if d else tri_l, lf, split_lhs=False) for lf, (_, d) in zip(lfs, jobs)]
        btots = [brow[:, 0:1] if d else brow[:, chunk - 1:chunk] for brow, (_, d) in zip(brows, jobs)]
        irows = [gt[2 * d:2 * d + 1, :] for gt, (_, d) in zip(gts, jobs)]
        qbs = [q_ref[0, pl.ds(r0, chunk), :].astype(BF16) for r0 in r0s]
        kts = [kt_ref[0, :, pl.ds(r0, chunk)] * scale for r0 in r0s]
        vos = [jnp.concatenate([v_ref[0, pl.ds(r0, chunk), :].astype(BF16), ones], axis=1) for r0 in r0s]
        qks = [jnp.dot(qb, kt.astype(BF16), preferred_element_type=F32) for qb, kt in zip(qbs, kts)]
        dms = [jnp.where(upper if d else lower, bq - brow + irow, -jnp.inf)
               for bq, brow, irow, (_, d) in zip(bqs, brows, irows, jobs)]
        m_locs = [jnp.max(dm, axis=-1, keepdims=True) for dm in dms]
        ss = [(qk * jnp.exp(dm - m_loc)).astype(BF16) for qk, dm, m_loc in zip(qks, dms, m_locs)]
        gls = [btot - brow + irow for btot, brow, irow in zip(btots, brows, irows)]
        g_locs = [jnp.max(gl, axis=-1, keepdims=True) for gl in gls]
        wks = [(kt * jnp.exp(gl - g_loc)).astype(BF16) for kt, gl, g_loc in zip(kts, gls, g_locs)]
        svs = [jnp.dot(s, vo, preferred_element_type=F32) for s, vo in zip(ss, vos)]
        kvs = [jnp.dot(wk, vo, preferred_element_type=F32) for wk, vo in zip(wks, vos)]
        for (g, d), sv, kv, bq, m_loc, btot, g_loc in zip(jobs, svs, kvs, bqs, m_locs, btots, g_locs):
            sv_buf[d, g] = sv
            kv_buf[d, g] = kv
            bq_buf[d, g] = bq
            ml_buf[d, g] = jnp.broadcast_to(m_loc, (chunk, LANES))
            sc_buf[d, g, 0:1, :] = jnp.broadcast_to(btot, (1, LANES))
            sc_buf[d, g, 1:2, :] = jnp.broadcast_to(g_loc, (1, LANES))
        return carry

    def recur(g, carry, j0):
        ms = list(carry)
        for d in range(N_DIR):
            r0 = chunk_start(j0 + g, d)
            h, cn, ms[d] = _mlstm_recur(
                q_ref[0, pl.ds(r0, chunk), :].astype(BF16), sv_buf[d, g], ml_buf[d, g], kv_buf[d, g],
                sc_buf[d, g, 1:2, 0:1], bq_buf[d, g], sc_buf[d, g, 0:1, 0:1], cn_buf[d], ms[d])
            cn_buf[d] = cn
            (hb_buf if d else hf_buf)[pl.ds(r0, chunk), :] = h
        return tuple(ms)

    for d in range(N_DIR):
        n_rep = jnp.broadcast_to(n0_ref[0, 0, d:d + 1, :], (dh, dh)).T
        cn_buf[d] = jnp.concatenate([c0_ref[0, d, 0], n_rep], axis=1)

    def block(jb, carry):
        j0 = jb * group
        lax.fori_loop(0, group // pair, functools.partial(local, j0=j0), 0)
        return lax.fori_loop(0, group, functools.partial(recur, j0=j0), carry, unroll=2)

    m_fin = lax.fori_loop(0, nc // group, block, (m0_ref[0, 0, 0:1, 0:1], m0_ref[0, 0, 1:2, 0:1]))
    for d in range(N_DIR):
        c_ref[0, d, 0] = cn_buf[d, :, :dh]
        n_ref[0, 0, d:d + 1, :] = cn_buf[d, :, dh:].T[0:1, :]
        m_ref[0, 0, d:d + 1, :] = jnp.broadcast_to(m_fin[d], (1, LANES))

    def finish(j, carry):
        r0 = pl.multiple_of(j * chunk, chunk)
        hs = hf_buf[pl.ds(r0, chunk), :] + hb_buf[pl.ds(r0, chunk), :]
        hs = hs * lax.rsqrt(jnp.mean(hs * hs, axis=-1, keepdims=True) + EPS) * ng_ref[...]
        y_ref[0, pl.ds(r0, chunk), :] = jax.nn.sigmoid(o_ref[0, pl.ds(r0, chunk), :]) * hs
        return carry
    lax.fori_loop(0, nc, finish, 0)


def _mlstm(q, kt, z_v, z_o, z_g, gate_b, norm_g, c0, n0, m0):
    bsz, l, ml = z_v.shape
    heads = gate_b.shape[-1]
    dh = ml // heads
    chunk = MLSTM_CHUNK
    assert dh == LANES and chunk == LANES and l % chunk == 0
    group = math.gcd(l // chunk, MLSTM_GROUP)
    n_gate = N_DIR * 2
    assert z_g.shape[-1] == LANES
    g4 = z_g[..., :n_gate * heads].reshape(bsz, l, n_gate, heads)
    ght = jnp.pad(g4.transpose(0, 3, 2, 1), ((0, 0), (0, 0), (0, SUBLANES - n_gate), (0, 0)))
    gb4 = gate_b.reshape(n_gate, heads).T
    gb = jnp.pad(gate_b.reshape(1, n_gate * heads), ((0, 0), (0, LANES - n_gate * heads)))
    gbt = jnp.broadcast_to(jnp.pad(gb4, ((0, 0), (0, SUBLANES - n_gate)))[..., None], (heads, SUBLANES, LANES))
    n0h = n0.transpose(0, 2, 1, 3)
    m0h = jnp.broadcast_to(m0.transpose(0, 2, 1)[..., None], (bsz, heads, N_DIR, LANES))
    seq = pl.BlockSpec((1, l, dh), lambda b, h: (b, 0, h))
    cspec = pl.BlockSpec((1, N_DIR, 1, dh, dh), lambda b, h: (b, 0, h, 0, 0))
    sspec = pl.BlockSpec((1, 1, N_DIR, dh), lambda b, h: (b, h, 0, 0))
    mspec = pl.BlockSpec((1, 1, N_DIR, LANES), lambda b, h: (b, h, 0, 0))
    y, c, n, m = pl.pallas_call(
        functools.partial(_mlstm_kernel, chunk=chunk, heads=heads),
        grid=(bsz, heads),
        in_specs=[seq, pl.BlockSpec((1, dh, l), lambda b, h: (b, h, 0)), seq, seq,
                  pl.BlockSpec((1, l, LANES), lambda b, h: (b, 0, 0)),
                  pl.BlockSpec((1, 1, SUBLANES, l), lambda b, h: (b, h, 0, 0)),
                  pl.BlockSpec((1, LANES), lambda b, h: (0, 0)),
                  pl.BlockSpec((1, SUBLANES, LANES), lambda b, h: (h, 0, 0)),
                  pl.BlockSpec((1, dh), lambda b, h: (0, h)),
                  cspec, sspec, mspec],
        out_specs=[seq, cspec, sspec, mspec],
        out_shape=[jax.ShapeDtypeStruct((bsz, l, ml), F32),
                   jax.ShapeDtypeStruct((bsz, N_DIR, heads, dh, dh), F32),
                   jax.ShapeDtypeStruct((bsz, heads, N_DIR, dh), F32),
                   jax.ShapeDtypeStruct((bsz, heads, N_DIR, LANES), F32)],
        scratch_shapes=[pltpu.VMEM((l, dh), F32), pltpu.VMEM((l, dh), F32),
                        pltpu.VMEM((N_DIR, dh, 2 * dh), F32),
                        pltpu.VMEM((N_DIR, group, chunk, 2 * dh), F32),
                        pltpu.VMEM((N_DIR, group, dh, 2 * dh), F32),
                        pltpu.VMEM((N_DIR, group, chunk, LANES), F32),
                        pltpu.VMEM((N_DIR, group, chunk, LANES), F32),
                        pltpu.VMEM((N_DIR, group, SUBLANES, LANES), F32)],
        compiler_params=_cparams("arbitrary", "arbitrary"),
        name="mlstm",
    )(q, kt, z_v, z_o, z_g, ght, gb, gbt, norm_g.reshape(1, ml), c0, n0h, m0h)
    return y, (c, n.transpose(0, 2, 1, 3), m[..., 0].transpose(0, 2, 1))


def _rows_to_tiles(tile_ref, x):
    r, d = x.shape
    s = d // LANES
    for k in range(s):
        tile_ref[pl.ds(k, r, stride=s), :] = x[:, LANES * k:LANES * (k + 1)]


def _tiles_to_rows(tile_ref, r):
    s = tile_ref.shape[0] // r
    return jnp.concatenate([tile_ref[pl.ds(k, r, stride=s), :] for k in range(s)], axis=1)


def _lane_pack(cols, lane):
    out = jnp.zeros(lane.shape, cols[0].dtype)
    for j, colv in enumerate(cols):
        out = jnp.where(lane == j, colv, out)
    return out


def _post_kernel(yhy_ref, yml_ref, x_ref, mod_ref, wo_ref, n2g_ref, rw_ref, rb_ref, cnt0_ref,
                 x1_ref, h2_ref, ti_ref, tg_ref, rk_ref, cnt_ref, carry, *, n_exp, top_k):
    @pl.when(pl.program_id(0) == 0)
    def _():
        carry[...] = cnt0_ref[0:1, :]

    hy_w = yhy_ref.shape[1]
    tm = x_ref.shape[0]
    proj = _bdot(yhy_ref[...], wo_ref[:hy_w, :]) + _bdot(yml_ref[...], wo_ref[hy_w:, :])
    x1 = x_ref[...] + mod_ref[0, 2:3, :] * proj
    x1_ref[...] = x1
    h2 = x1 * lax.rsqrt(jnp.mean(x1 * x1, axis=-1, keepdims=True) + EPS) * n2g_ref[...]
    h2 = h2 * (1.0 + mod_ref[0, 4:5, :]) + mod_ref[0, 3:4, :]
    _rows_to_tiles(h2_ref, h2)

    lane = lax.broadcasted_iota(jnp.int32, (tm, LANES), 1)
    work = jnp.where(lane < n_exp, _dot3(h2, rw_ref[...]) + rb_ref[...], -jnp.inf)
    vals, idxs, hots = [], [], []
    for _ in range(top_k):
        mx = jnp.max(work, axis=-1, keepdims=True)
        idx = jnp.min(jnp.where(work == mx, lane, LANES), axis=-1, keepdims=True)
        hot = lane == idx
        vals.append(mx)
        idxs.append(idx)
        hots.append(hot)
        work = jnp.where(hot, -jnp.inf, work)
    exps = [jnp.exp(v - vals[0]) for v in vals]
    tot = functools.reduce(lambda a, b: a + b, exps)
    ti_ref[...] = _lane_pack(idxs, lane)
    tg_ref[...] = _lane_pack([e / tot for e in exps], lane)

    hot_sum = functools.reduce(lambda a, b: a + b, [h.astype(F32) for h in hots])
    row = lax.broadcasted_iota(jnp.int32, (tm, tm), 0)
    col = lax.broadcasted_iota(jnp.int32, (tm, tm), 1)
    before = _bdot((col < row).astype(F32), hot_sum) + carry[...]
    ranks = [jnp.sum(jnp.where(h, before, 0.0), axis=-1, keepdims=True).astype(jnp.int32) for h in hots]
    rk_ref[...] = _lane_pack(ranks, lane)
    carry[...] = carry[...] + jnp.sum(hot_sum, axis=0, keepdims=True)
    cnt_ref[...] = jnp.broadcast_to(carry[...], cnt_ref.shape)


def _post(y_hy, y_ml, x, mods, w_out, norm2_g, r_w, r_b, counts, mod_map, tm):
    t, d = x.shape
    n_exp = r_w.shape[1]
    rwp = jnp.pad(r_w, ((0, 0), (0, LANES - n_exp)))
    rbp = jnp.pad(r_b, (0, LANES - n_exp)).reshape(1, LANES)
    row = lambda wd: pl.BlockSpec((tm, wd), lambda i: (i, 0))
    const = lambda shape: pl.BlockSpec(shape, lambda i: (0, 0))
    return pl.pallas_call(
        functools.partial(_post_kernel, n_exp=n_exp, top_k=TOP_K),
        grid=(t // tm,),
        in_specs=[row(y_hy.shape[1]), row(y_ml.shape[1]), row(d),
                  pl.BlockSpec((1, 6, d), mod_map),
                  const(w_out.shape), const((1, d)), const((d, LANES)), const((1, LANES)),
                  const((SUBLANES, LANES))],
        out_specs=[row(d), pl.BlockSpec((tm * d // LANES, LANES), lambda i: (i, 0)),
                   row(LANES), row(LANES), row(LANES), const((SUBLANES, LANES))],
        out_shape=[jax.ShapeDtypeStruct((t, d), F32), jax.ShapeDtypeStruct((t * d // LANES, LANES), F32),
                   jax.ShapeDtypeStruct((t, LANES), jnp.int32), jax.ShapeDtypeStruct((t, LANES), F32),
                   jax.ShapeDtypeStruct((t, LANES), jnp.int32), jax.ShapeDtypeStruct((SUBLANES, LANES), F32)],
        scratch_shapes=[pltpu.VMEM((1, LANES), F32)],
        compiler_params=_cparams("arbitrary"),
        name="post",
    )(y_hy, y_ml, x, mods, w_out.astype(BF16), norm2_g.reshape(1, d), rwp, rbp, counts)


def _dest_kernel(ti_ref, rk_ref, ps_ref, d_ref, *, top_k):
    lane = lax.broadcasted_iota(jnp.int32, ti_ref.shape, 1)
    ti, rk = ti_ref[...], rk_ref[...]
    cols = []
    for j in range(top_k):
        start = jnp.sum(jnp.where(lane == ti[:, j:j + 1], ps_ref[...], 0.0), axis=-1, keepdims=True)
        cols.append(start.astype(jnp.int32) + rk[:, j:j + 1])
    d_ref[...] = _lane_pack(cols, lane)


def _dest_rows(ti, rk, pad_start, tm):
    t = ti.shape[0]
    row = pl.BlockSpec((tm, LANES), lambda i: (i, 0))
    return pl.pallas_call(
        functools.partial(_dest_kernel, top_k=TOP_K),
        grid=(t // tm,),
        in_specs=[row, row, pl.BlockSpec((1, LANES), lambda i: (0, 0))],
        out_specs=row,
        out_shape=jax.ShapeDtypeStruct((t, LANES), jnp.int32),
        compiler_params=_cparams("arbitrary"),
        name="dest_rows",
    )(ti, rk, pad_start)


def _dispatch_kernel(first_ref, count_ref, nu_ref, dest_ref, *refs, top_k, s, rows, tiles):
    h_refs = refs[:len(tiles)]
    xs_ref, zbuf, sem, zsem = refs[len(tiles):]
    i = pl.program_id(0)
    tt = dest_ref.shape[2] // top_k
    half = zbuf.shape[0] // s
    n_blocks = xs_ref.shape[0] // (rows * s)

    def pad_rows(e, carry, wait):
        off, n = first_ref[e], count_ref[e]
        for k in range(half.bit_length()):
            bit = half >> k

            @pl.when((n & bit) != 0)
            def _():
                dst = pl.multiple_of(off * s, s)
                copy = pltpu.make_async_copy(zbuf.at[pl.ds(0, bit * s)], xs_ref.at[pl.ds(dst, bit * s)], zsem)
                copy.wait() if wait else copy.start()
            off = off + (n & bit)
        return carry

    def spare_block(b, carry, wait):
        for part in range(2):
            dst = pl.multiple_of((b * 2 + part) * half * s, half * s)
            copy = pltpu.make_async_copy(zbuf, xs_ref.at[pl.ds(dst, half * s)], zsem)
            copy.wait() if wait else copy.start()
        return carry

    @pl.when(i == 0)
    def _():
        zbuf[...] = jnp.zeros_like(zbuf)
        for wait in (False, True):
            lax.fori_loop(0, first_ref.shape[0], functools.partial(pad_rows, wait=wait), 0)
            lax.fori_loop(nu_ref[0], n_blocks, functools.partial(spare_block, wait=wait), 0)

    def scatter(h_ref):
        def issue(t, carry):
            src = pl.multiple_of(t * s, s)
            for j in range(top_k):
                dst = pl.multiple_of(dest_ref[0, 0, t * top_k + j] * s, s)
                pltpu.make_async_copy(h_ref.at[pl.ds(src, s)], xs_ref.at[pl.ds(dst, s)], sem).start()
            return carry
        lax.fori_loop(0, tt, issue, 0, unroll=2)
        for _ in range(top_k):
            pltpu.make_async_copy(h_ref, xs_ref.at[pl.ds(0, tt * s)], sem).wait()

    lo = 0
    for h_ref, n_tiles in zip(h_refs, tiles):
        @pl.when((i >= lo) & (i < lo + n_tiles))
        def _():
            scatter(h_ref)
        lo += n_tiles


def _dispatch(dest, h2ts, pad_first, pad_count, n_used, n_blocks, rows, tt):
    s = sum(h.shape[0] for h in h2ts) * TOP_K // dest.size
    tiles = tuple(h.shape[0] // (tt * s) for h in h2ts)
    starts = [sum(tiles[:k]) for k in range(len(tiles))]
    hspec = lambda lo, n: pl.BlockSpec((tt * s, LANES), lambda i, *_: (jnp.clip(i - lo, 0, n - 1), 0))
    return pl.pallas_call(
        functools.partial(_dispatch_kernel, top_k=TOP_K, s=s, rows=rows, tiles=tiles),
        grid_spec=pltpu.PrefetchScalarGridSpec(
            num_scalar_prefetch=3,
            grid=(dest.shape[0],),
            in_specs=[pl.BlockSpec((1, 1, tt * TOP_K), lambda i, *_: (i, 0, 0), memory_space=pltpu.SMEM)]
            + [hspec(lo, n) for lo, n in zip(starts, tiles)],
            out_specs=pl.BlockSpec(memory_space=pl.ANY),
            scratch_shapes=[pltpu.VMEM((rows // 2 * s, LANES), F32),
                            pltpu.SemaphoreType.DMA(()), pltpu.SemaphoreType.DMA(())]),
        out_shape=jax.ShapeDtypeStruct((n_blocks * rows * s, LANES), F32),
        compiler_params=_cparams("arbitrary"),
        name="dispatch",
    )(pad_first, pad_count, n_used, dest, *h2ts)


def _deinterleave_table():
    p = np.zeros((2 * LANES, 2 * LANES), np.float32)
    j = np.arange(LANES)
    p[2 * j, j] = 1.0
    p[2 * j + 1, LANES + j] = 1.0
    return p


def _ffn_kernel(be_ref, nu_ref, xs_ref, wgu_ref, bg_ref, bl_ref, wd_ref, bd_ref, perm_ref, ys_ref,
                wg_buf, wl_buf, wd_buf, acc_buf, *, ft):
    i = pl.program_id(0)
    f = wg_buf.shape[1]
    rows = acc_buf.shape[0]
    live = i < nu_ref[0]

    @pl.when(live & ((i == 0) | (be_ref[i] != be_ref[jnp.maximum(i - 1, 0)])))
    def _():
        for j in range(f // LANES):
            blk = wgu_ref[0, :, 2 * LANES * j:2 * LANES * (j + 1)].astype(BF16)
            split = jnp.dot(blk, perm_ref[...], preferred_element_type=F32)
            wg_buf[:, LANES * j:LANES * (j + 1)] = split[:, :LANES].astype(BF16)
            wl_buf[:, LANES * j:LANES * (j + 1)] = split[:, LANES:].astype(BF16)
        wd_buf[...] = wd_ref[0].astype(BF16)

    @pl.when(live)
    def _():
        x = _tiles_to_rows(xs_ref, rows).astype(BF16)
        for j, f0 in enumerate(range(0, f, ft)):
            g = jnp.dot(x, wg_buf[:, f0:f0 + ft], preferred_element_type=F32) + bg_ref[0, :, f0:f0 + ft]
            lin = jnp.dot(x, wl_buf[:, f0:f0 + ft], preferred_element_type=F32) + bl_ref[0, :, f0:f0 + ft]
            gate = jnp.minimum(g, SWIGLU_LIMIT)
            lin = jnp.clip(lin, -SWIGLU_LIMIT, SWIGLU_LIMIT)
            act = (lin + 1.0) * gate * jax.nn.sigmoid(SWIGLU_ALPHA * gate)
            part = jnp.dot(act.astype(BF16), wd_buf[f0:f0 + ft, :], preferred_element_type=F32)
            if j == 0:
                acc_buf[...] = part + bd_ref[0]
            else:
                acc_buf[...] += part
        _rows_to_tiles(ys_ref, acc_buf[...])

    @pl.when(jnp.logical_not(live))
    def _():
        ys_ref[...] = jnp.zeros_like(ys_ref)


def _ffn(block_e, n_used, xs, w_gu, b_gu, w_d, b_d, rows):
    n_exp, d, f2 = w_gu.shape
    f = f2 // 2
    s = d // LANES
    n_rows = xs.shape[0] // s
    live = lambda i, nu: jnp.minimum(i, nu[0] - 1)
    wmap = lambda i, be, nu: (be[live(i, nu)], 0, 0)
    perm = jnp.asarray(_deinterleave_table(), BF16)
    return pl.pallas_call(
        functools.partial(_ffn_kernel, ft=min(f, 1024)),
        grid_spec=pltpu.PrefetchScalarGridSpec(
            num_scalar_prefetch=2,
            grid=(n_rows // rows,),
            in_specs=[pl.BlockSpec((rows * s, LANES), lambda i, be, nu: (live(i, nu), 0)),
                      pl.BlockSpec((1, d, f2), wmap),
                      pl.BlockSpec((1, 1, f), wmap), pl.BlockSpec((1, 1, f), wmap),
                      pl.BlockSpec((1, f, d), wmap), pl.BlockSpec((1, 1, d), wmap),
                      pl.BlockSpec(perm.shape, lambda i, be, nu: (0, 0))],
            out_specs=pl.BlockSpec((rows * s, LANES), lambda i, be, nu: (i, 0)),
            scratch_shapes=[pltpu.VMEM((d, f), BF16), pltpu.VMEM((d, f), BF16), pltpu.VMEM((f, d), BF16),
                            pltpu.VMEM((rows, d), F32)]),
        out_shape=jax.ShapeDtypeStruct(xs.shape, F32),
        compiler_params=_cparams("arbitrary"),
        name="expert_ffn",
    )(block_e, n_used, xs, w_gu, b_gu[:, 0::2].reshape(n_exp, 1, f), b_gu[:, 1::2].reshape(n_exp, 1, f),
      w_d, b_d.reshape(n_exp, 1, d), perm)


def _combine_kernel(dest_ref, next_ref, tg_ref, x1_ref, mod_ref, fg_ref, ys_ref, o_ref, ybuf, sem, *,
                    top_k, final_norm):
    i = pl.program_id(0)
    tt = x1_ref.shape[0]
    s = ybuf.shape[2] // tt
    slot = i % 2

    def gather(d_ref, to):
        def issue(t, carry):
            dst = pl.multiple_of(t * s, s)
            for j in range(top_k):
                src = pl.multiple_of(d_ref[0, 0, t * top_k + j] * s, s)
                pltpu.make_async_copy(ys_ref.at[pl.ds(src, s)], ybuf.at[to, j, pl.ds(dst, s)], sem.at[to]).start()
            return carry
        lax.fori_loop(0, tt, issue, 0, unroll=2)

    @pl.when(i == 0)
    def _():
        gather(dest_ref, 0)

    @pl.when(i + 1 < pl.num_programs(0))
    def _():
        gather(next_ref, 1 - slot)

    for j in range(top_k):
        pltpu.make_async_copy(ys_ref.at[pl.ds(0, tt * s)], ybuf.at[slot, j], sem.at[slot]).wait()
    tg = tg_ref[...]
    moe = tg[:, 0:1] * _tiles_to_rows(ybuf.at[slot, 0], tt)
    for j in range(1, top_k):
        moe = moe + tg[:, j:j + 1] * _tiles_to_rows(ybuf.at[slot, j], tt)
    x2 = x1_ref[...] + mod_ref[0, 5:6, :] * moe
    if final_norm:
        x2 = x2 * lax.rsqrt(jnp.mean(x2 * x2, axis=-1, keepdims=True) + EPS) * fg_ref[...]
    o_ref[...] = x2


def _combine(dest, tg, x1, mods, final_g, ys, mod_map, tt, final_norm):
    t, d = x1.shape
    steps = t // tt
    return pl.pallas_call(
        functools.partial(_combine_kernel, top_k=TOP_K, final_norm=final_norm),
        grid=(steps,),
        in_specs=[pl.BlockSpec((1, 1, tt * TOP_K), lambda i: (i, 0, 0), memory_space=pltpu.SMEM),
                  pl.BlockSpec((1, 1, tt * TOP_K), lambda i: (jnp.minimum(i + 1, steps - 1), 0, 0),
                               memory_space=pltpu.SMEM),
                  pl.BlockSpec((tt, LANES), lambda i: (i, 0)),
                  pl.BlockSpec((tt, d), lambda i: (i, 0)),
                  pl.BlockSpec((1, 6, d), mod_map),
                  pl.BlockSpec((1, d), lambda i: (0, 0)),
                  pl.BlockSpec(memory_space=pl.ANY)],
        out_specs=pl.BlockSpec((tt, d), lambda i: (i, 0)),
        out_shape=jax.ShapeDtypeStruct((t, d), F32),
        scratch_shapes=[pltpu.VMEM((2, TOP_K, tt * d // LANES, LANES), F32), pltpu.SemaphoreType.DMA((2,))],
        compiler_params=_cparams("arbitrary"),
        name="combine",
    )(dest, dest, tg, x1, mods, final_g.reshape(1, d), ys)


def _moe_plan(counts, rows, n_blocks):
    n_exp = counts.shape[0]
    padded = (counts + rows - 1) // rows * rows
    pad_end = jnp.cumsum(padded)
    block_row = jnp.arange(n_blocks, dtype=jnp.int32) * rows
    block_e = jnp.minimum(jnp.sum(pad_end[None, :] <= block_row[:, None], axis=1), n_exp - 1).astype(jnp.int32)
    n_used = (pad_end[-1:] // rows).astype(jnp.int32)
    start = pad_end - padded
    pad_start = jnp.pad(start.astype(F32), (0, LANES - n_exp)).reshape(1, LANES)
    return pad_start, block_e, n_used, (start + counts).astype(jnp.int32), (padded - counts).astype(jnp.int32)


def _sequence_mixers(z_hy, z_qk, z_v, z_o, z_g, lw, state, row_w):
    (hy_cw, hy_cb, filt_params, hy_b, ml_cw, ml_cb, ml_gb, ml_ng) = lw
    y_hy = _hyena(z_hy, hy_cw, hy_cb, filt_params, hy_b, row_w)
    ml_w = z_v.shape[-1]
    q = _short_conv(z_qk, ml_cw, ml_cb, row_w, silu=True, col0=0, ncols=ml_w)
    kt = _short_conv(z_qk, ml_cw, ml_cb, row_w, silu=True, col0=ml_w, ncols=ml_w, transpose=True)
    y_ml, st = _mlstm(q, kt, z_v, z_o, z_g, ml_gb, ml_ng, *state)
    return y_hy, y_ml, st


def kernel(x_prompt, x_sample, state_mlstm_C, state_mlstm_n, state_mlstm_m, c, c_ctx, ada_w, ada_b, norm1_g,
           w_in, hy_conv_w, hy_conv_b, filt_w1, filt_b1, filt_w2, filt_b2, filt_w3, filt_freq, hy_bias,
           ml_conv_w, ml_conv_b, ml_gate_b, ml_norm_g, w_out, norm2_g, router_w, router_b, moe_w_gu,
           moe_b_gu, moe_w_down, moe_b_down, final_g):
    bp, lp, d = x_prompt.shape
    bs, ls, _ = x_sample.shape
    depth = ada_w.shape[0]
    heads = ml_gate_b.shape[-1]
    hy_w = hy_bias.shape[-1]
    ml_w = ml_norm_g.shape[-1]
    dh = ml_w // heads
    n_exp = router_w.shape[-1]
    t = bp * lp + bs * ls
    ng = N_DIR * 2 * heads
    seg_widths = (3 * hy_w, 2 * ml_w, ml_w, ml_w)
    n_main = 3 * hy_w + 4 * ml_w
    n_blocks = -(-(t * TOP_K) // MOE_ROWS) + n_exp

    cond = jnp.concatenate([c_ctx[None], c, jnp.zeros((SUBLANES - 1 - bs, d), F32)], axis=0)
    zero_state = (jnp.zeros((bp, N_DIR, heads, dh, dh), F32), jnp.zeros((bp, N_DIR, heads, dh), F32),
                  jnp.zeros((bp, N_DIR, heads), F32))
    xs_paths = [x_prompt, x_sample]
    path_cfg = [(0, 0, lp), (1, 1, GRID_W)]
    new_c, new_n, new_m = [], [], []
    for l in range(depth):
        mods = _ada(cond, ada_w[l], ada_b[l]).reshape(SUBLANES, 6, d)
        w_main = w_in[l][:, :n_main].astype(BF16)
        w_gate = jnp.pad(w_in[l][:, n_main:], ((0, 0), (0, LANES - ng)))
        lw = (hy_conv_w[l], hy_conv_b[l],
              (filt_w1[l], filt_b1[l], filt_w2[l], filt_b2[l], filt_w3[l], filt_freq[l]), hy_bias[l],
              ml_conv_w[l], ml_conv_b[l], ml_gate_b[l], ml_norm_g[l])
        states = [zero_state, (state_mlstm_C[:, l], state_mlstm_n[:, l], state_mlstm_m[:, l])]
        counts = jnp.zeros((SUBLANES, LANES), F32)
        routed = []
        for x3, (mod0, mod_step, row_w), state in zip(xs_paths, path_cfg, states):
            bsz, lseq, _ = x3.shape
            tm, tt = min(ROW_TILE, lseq), min(TOK_TILE, lseq)
            xf = x3.reshape(bsz * lseq, d)
            z = _inproj(xf, mods, norm1_g[l], w_main, w_gate, seg_widths,
                        _mod_index_map(mod0, mod_step, lseq // tm), tm)
            y_hy, y_ml, st = _sequence_mixers(*[a.reshape(bsz, lseq, a.shape[1]) for a in z], lw, state, row_w)
            x1, h2t, ti, tg, rk, counts = _post(
                y_hy.reshape(bsz * lseq, hy_w), y_ml.reshape(bsz * lseq, ml_w), xf, mods, w_out[l], norm2_g[l],
                router_w[l], router_b[l], counts, _mod_index_map(mod0, mod_step, lseq // tm), tm)
            routed.append((x1, h2t, ti, tg, rk, tm, tt, _mod_index_map(mod0, mod_step, lseq // tt), st))
        new_c.append(routed[0][-1][0])
        new_n.append(routed[0][-1][1])
        new_m.append(routed[0][-1][2])

        pad_start, block_e, n_used, pad_first, pad_count = _moe_plan(
            counts[0, :n_exp].astype(jnp.int32), MOE_ROWS, n_blocks)
        tt = routed[0][6]
        assert all(r[6] == tt for r in routed)
        dests = [_dest_rows(ti, rk, pad_start, tm)[:, :TOP_K].reshape(-1, 1, tt * TOP_K)
                 for _, _, ti, _, rk, tm, _, _, _ in routed]
        xs = _dispatch(jnp.concatenate(dests, axis=0), [r[1] for r in routed], pad_first, pad_count, n_used,
                       n_blocks, MOE_ROWS, tt)
        ys = _ffn(block_e, n_used, xs, moe_w_gu[l], moe_b_gu[l], moe_w_down[l], moe_b_down[l], MOE_ROWS)
        xs_paths = [
            _combine(dest, tg, x1, mods, final_g, ys, mod_map, tt, final_norm=l == depth - 1).reshape(x3.shape)
            for dest, (x1, _, _, tg, _, _, tt, mod_map, _), x3 in zip(dests, routed, xs_paths)]
    return (xs_paths[0], xs_paths[1],
            jnp.stack(new_c, axis=1), jnp.stack(new_n, axis=1), jnp.stack(new_m, axis=1))
```

```python
import functools
import math

import numpy as np
import jax
import jax.numpy as jnp
from jax import lax
from jax.experimental import pallas as pl
from jax.experimental.pallas import tpu as pltpu

F32 = jnp.float32
BF16 = jnp.bfloat16
HIGHEST = lax.Precision.HIGHEST
EPS = 1e-6

LANES = 128
SUBLANES = 8
VMEM_LIMIT_BYTES = 56 * 1024 * 1024

GRID_W = 64
ML_HEADS = 4
N_DIR = 2
HY_ORDER = 2
FILT_BANDS = 8
DECAY_TARGET = 1e-2
FAST_DECAY_PCT = 0.3
SLOW_DECAY_PCT = 1.5
TOP_K = 4
SWIGLU_LIMIT = 7.0
SWIGLU_ALPHA = 1.702

FFT_N2 = 128
MLSTM_CHUNK = 128
MLSTM_GROUP = 8
MLSTM_PAIR = 4
MOE_ROWS = 512
ROW_TILE = 512
TOK_TILE = 256


def _cparams(*sem):
    return pltpu.CompilerParams(dimension_semantics=sem, vmem_limit_bytes=VMEM_LIMIT_BYTES)


def _lane_tile(c, cap):
    return max(t for t in range(LANES, min(c, cap) + 1, LANES) if c % t == 0)


def _bdot(a, b):
    return jnp.dot(a.astype(BF16), b.astype(BF16), preferred_element_type=F32)


def _hdot(a, b):
    return jnp.dot(a, b, precision=HIGHEST, preferred_element_type=F32)


def _dot3(a, b):
    a_hi, b_hi = a.astype(BF16), b.astype(BF16)
    a_lo = (a - a_hi.astype(F32)).astype(BF16)
    b_lo = (b - b_hi.astype(F32)).astype(BF16)
    dot = functools.partial(jnp.dot, preferred_element_type=F32)
    return dot(a_hi, b_hi) + dot(a_hi, b_lo) + dot(a_lo, b_hi)


def _ada_kernel(c_ref, w_ref, b_ref, o_ref):
    c = c_ref[...]
    o_ref[...] = _hdot(c * jax.nn.sigmoid(c), w_ref[...]) + b_ref[...]


def _ada(cond, w, b):
    r, d = cond.shape
    n = w.shape[1]
    tn = _lane_tile(n, 1024)
    return pl.pallas_call(
        _ada_kernel,
        grid=(n // tn,),
        in_specs=[pl.BlockSpec((r, d), lambda j: (0, 0)),
                  pl.BlockSpec((d, tn), lambda j: (0, j)),
                  pl.BlockSpec((1, tn), lambda j: (0, j))],
        out_specs=pl.BlockSpec((r, tn), lambda j: (0, j)),
        out_shape=jax.ShapeDtypeStruct((r, n), F32),
        compiler_params=_cparams("arbitrary"),
        name="ada",
    )(cond, w, b.reshape(1, n))


def _mod_index_map(mod0, mod_step, tiles_per_seq):
    def index_map(i):
        return (mod0 + (i // tiles_per_seq) * mod_step, 0, 0)
    return index_map


def _inproj_kernel(x_ref, mod_ref, g_ref, w_ref, wg_ref, *out_refs, offsets):
    x = x_ref[...]
    h = x * lax.rsqrt(jnp.mean(x * x, axis=-1, keepdims=True) + EPS) * g_ref[...]
    h = h * (1.0 + mod_ref[0, 1:2, :]) + mod_ref[0, 0:1, :]
    hb = h.astype(BF16)
    for o_ref, (lo, hi) in zip(out_refs[:-1], offsets):
        o_ref[...] = jnp.dot(hb, w_ref[:, lo:hi], preferred_element_type=F32)
    h_lo = (h - hb.astype(F32)).astype(BF16)
    g = jnp.dot(hb, wg_ref[...], preferred_element_type=F32)
    out_refs[-1][...] = (g[:, :LANES] + g[:, LANES:]
                         + jnp.dot(h_lo, wg_ref[:, :LANES], preferred_element_type=F32))


def _inproj(x, mods, norm_g, w_main, w_gate, seg_widths, mod_map, tm):
    t, d = x.shape
    offsets, lo = [], 0
    for wd in seg_widths:
        offsets.append((lo, lo + wd))
        lo += wd
    wg_hi = w_gate.astype(BF16)
    wg = jnp.concatenate([wg_hi, (w_gate - wg_hi.astype(F32)).astype(BF16)], axis=1)
    widths = tuple(seg_widths) + (LANES,)
    return pl.pallas_call(
        functools.partial(_inproj_kernel, offsets=tuple(offsets)),
        grid=(t // tm,),
        in_specs=[pl.BlockSpec((tm, d), lambda i: (i, 0)),
                  pl.BlockSpec((1, 6, d), mod_map),
                  pl.BlockSpec((1, d), lambda i: (0, 0)),
                  pl.BlockSpec(w_main.shape, lambda i: (0, 0)),
                  pl.BlockSpec(wg.shape, lambda i: (0, 0))],
        out_specs=[pl.BlockSpec((tm, wd), lambda i: (i, 0)) for wd in widths],
        out_shape=[jax.ShapeDtypeStruct((t, wd), F32) for wd in widths],
        compiler_params=_cparams("arbitrary"),
        name="inproj",
    )(x, mods, norm_g.reshape(1, d), w_main, wg)


def _short_conv_kernel(x_ref, w_ref, b_ref, o_ref, *, row_w, silu, transpose):
    x = x_ref[0]
    l = x.shape[0]
    pos = lax.broadcasted_iota(jnp.int32, x.shape, 0) % row_w
    prev = jnp.where(pos == 0, 0.0, pltpu.roll(x, 1, 0))
    nxt = jnp.where(pos == row_w - 1, 0.0, pltpu.roll(x, l - 1, 0))
    y = prev * w_ref[0:1, :] + x * w_ref[1:2, :] + nxt * w_ref[2:3, :] + b_ref[...]
    if silu:
        y = y * jax.nn.sigmoid(y)
    o_ref[0] = y.T if transpose else y


def _short_conv(x, w, b, row_w, silu, col0=0, ncols=None, transpose=False):
    bsz, l, c = x.shape
    ncols = c if ncols is None else ncols
    ct = _lane_tile(math.gcd(ncols, col0) if col0 else ncols, LANES if transpose else 512)
    j0 = col0 // ct
    out_shape, out_block, out_map = (bsz, l, ncols), (1, l, ct), lambda i, j: (i, 0, j)
    if transpose:
        out_shape, out_block, out_map = (bsz, ncols, l), (1, ct, l), lambda i, j: (i, j, 0)
    return pl.pallas_call(
        functools.partial(_short_conv_kernel, row_w=row_w, silu=silu, transpose=transpose),
        grid=(bsz, ncols // ct),
        in_specs=[pl.BlockSpec((1, l, ct), lambda i, j: (i, 0, j0 + j)),
                  pl.BlockSpec((3, ct), lambda i, j: (0, j0 + j)),
                  pl.BlockSpec((1, ct), lambda i, j: (0, j0 + j))],
        out_specs=pl.BlockSpec(out_block, out_map),
        out_shape=jax.ShapeDtypeStruct(out_shape, F32),
        compiler_params=_cparams("arbitrary", "arbitrary"),
        name="short_conv",
    )(x, w, b.reshape(1, c))


def _dft_direct_tables(l):
    n = 2 * l
    k = np.arange(n)[:, None].astype(np.float64)
    t = np.arange(n)[None, :].astype(np.float64)
    ang = 2.0 * np.pi * ((k * t) % n) / n
    cm, sm = np.cos(ang), np.sin(ang)
    fwd = np.block([[cm[:, :l], sm[:, :l]], [-sm[:, :l], cm[:, :l]]])
    filt = np.concatenate([cm, -sm], axis=0)
    return fwd, filt


def _dft_two_level_tables(l, n2):
    n = 2 * l
    n1 = n // n2
    k1 = np.arange(n1)[:, None].astype(np.float64)
    a = np.arange(n1)[None, :].astype(np.float64)
    ang1 = 2.0 * np.pi * ((k1 * a) % n1) / n1
    c1, s1 = np.cos(ang1), np.sin(ang1)
    h = n1 // 2
    m1 = np.block([[c1[:, :h], s1[:, :h]], [-s1[:, :h], c1[:, :h]]])
    m1f = np.concatenate([c1, -s1], axis=0)
    kk = (np.arange(n1)[:, None, None] + n1 * np.arange(n2)[None, :, None]).astype(np.float64)
    b = np.arange(n2)[None, None, :].astype(np.float64)
    ang = 2.0 * np.pi * ((kk * b) % n) / n
    cg, sg = np.cos(ang), np.sin(ang)
    gt = np.concatenate([np.concatenate([cg, sg], axis=2),
                         np.concatenate([-sg, cg], axis=2)], axis=1)
    return m1, m1f, gt


def _circular_lag(n0, rows, l):
    n = n0 + lax.broadcasted_iota(jnp.int32, (rows, 1), 0)
    t = jnp.where(n < l, n, 2 * l - n).astype(F32)
    return n, t, t / float(max(l - 1, 1))


def _filter_hidden_kernel(bandv_ref, w1_ref, b1_ref, w2_ref, b2_ref, freq_ref, o_ref, *, l):
    rows = o_ref.shape[0]
    _, t, t01 = _circular_lag(pl.program_id(0) * rows, rows, l)
    lane = lax.broadcasted_iota(jnp.int32, (rows, LANES), 1)
    ang = (2.0 * math.pi / l) * t * bandv_ref[...]
    feats = jnp.where(lane == 0, t01,
                      jnp.where(lane <= FILT_BANDS, jnp.cos(ang),
                                jnp.where(lane <= 2 * FILT_BANDS, -jnp.sin(ang), 0.0)))
    fr = freq_ref[...]
    h = jnp.sin(fr * (_hdot(feats, w1_ref[...]) + b1_ref[...]))
    o_ref[...] = jnp.sin(fr * (_hdot(h, w2_ref[...]) + b2_ref[...]))


def _filter_hidden(l, f_w1, f_b1, f_w2, f_b2, f_freq):
    emb, hid = f_w1.shape
    n = 2 * l
    rows = min(n, 512)
    bands = jnp.linspace(1e-4, FILT_BANDS - 1, FILT_BANDS, dtype=F32)
    bandv = jnp.zeros((1, LANES), F32).at[0, 1:1 + FILT_BANDS].set(bands)
    bandv = bandv.at[0, 1 + FILT_BANDS:1 + 2 * FILT_BANDS].set(bands)
    w1p = jnp.zeros((LANES, hid), F32).at[:emb].set(f_w1)
    c0 = lambda i: (0, 0)
    return pl.pallas_call(
        functools.partial(_filter_hidden_kernel, l=l),
        grid=(n // rows,),
        in_specs=[pl.BlockSpec((1, LANES), c0), pl.BlockSpec((LANES, hid), c0), pl.BlockSpec((1, hid), c0),
                  pl.BlockSpec((hid, hid), c0), pl.BlockSpec((1, hid), c0), pl.BlockSpec((1, hid), c0)],
        out_specs=pl.BlockSpec((rows, hid), lambda i: (i, 0)),
        out_shape=jax.ShapeDtypeStruct((n, hid), F32),
        compiler_params=_cparams("arbitrary"),
        name="filter_hidden",
    )(bandv, w1p, f_b1.reshape(1, hid), f_w2, f_b2.reshape(1, hid), f_freq.reshape(1, hid))


def _filter_rows(n0, rows, l, hid_ref, w3_ref, delta_ref):
    n, _, t01 = _circular_lag(n0, rows, l)
    h = hid_ref[pl.ds(n0, rows), :]
    hf = _bdot(h, w3_ref[0, 0])
    hb = _bdot(h, w3_ref[0, 1])
    window = jnp.exp(-t01 * delta_ref[...])
    return jnp.where(n < l, hf, jnp.where(n > l, hb, 0.0)) * window


def _filter_direct_kernel(hid_ref, w3_ref, delta_ref, ff_ref, h_ref, *, l):
    hc = _filter_rows(0, 2 * l, l, hid_ref, w3_ref, delta_ref)
    h_ref[0] = _bdot(ff_ref[...], hc) * (1.0 / (2 * l))


def _filter_two_level_kernel(hid_ref, w3_ref, delta_ref, m1f_ref, gt_ref, h_ref, hc_buf, a_buf, *, l, n2, rows):
    n = 2 * l
    n1 = n // n2

    def fill(i, carry):
        r0 = pl.multiple_of(i * rows, rows)
        hc_buf[pl.ds(r0, rows), :] = _filter_rows(r0, rows, l, hid_ref, w3_ref, delta_ref)
        return carry
    lax.fori_loop(0, n // rows, fill, 0)

    def step1(b, carry):
        col = hc_buf[pl.ds(b, n1, stride=n2), :]
        a = _bdot(m1f_ref[...], col)
        a_buf[pl.ds(b, n1, stride=2 * n2), :] = a[:n1]
        a_buf[pl.ds(n2 + b, n1, stride=2 * n2), :] = a[n1:]
        return carry
    lax.fori_loop(0, n2, step1, 0, unroll=8)

    def step2(k1, carry):
        r0 = pl.multiple_of(k1 * 2 * n2, 2 * n2)
        h_ref[0, k1] = (_bdot(gt_ref[k1], a_buf[pl.ds(r0, 2 * n2), :]) * (1.0 / n)).astype(h_ref.dtype)
        return carry
    lax.fori_loop(0, n1, step2, 0, unroll=4)


def _filter_tail_inputs(hy_w, f_w3):
    hid = f_w3.shape[0]
    w3 = f_w3.reshape(hid, HY_ORDER, N_DIR, hy_w).transpose(1, 2, 0, 3)
    max_decay = math.log(DECAY_TARGET) / FAST_DECAY_PCT
    min_decay = math.log(DECAY_TARGET) / SLOW_DECAY_PCT
    deltas = jnp.abs(jnp.linspace(min_decay, max_decay, hy_w, dtype=F32)).reshape(1, hy_w)
    return w3, deltas


def _filter_spectrum_direct(l, hy_w, filt_params, ff):
    f_w1, f_b1, f_w2, f_b2, f_w3, f_freq = filt_params
    hidden = _filter_hidden(l, f_w1, f_b1, f_w2, f_b2, f_freq)
    w3, deltas = _filter_tail_inputs(hy_w, f_w3)
    hid = f_w3.shape[0]
    n = 2 * l
    return pl.pallas_call(
        functools.partial(_filter_direct_kernel, l=l),
        grid=(HY_ORDER,),
        in_specs=[pl.BlockSpec((n, hid), lambda o: (0, 0)),
                  pl.BlockSpec((1, N_DIR, hid, hy_w), lambda o: (o, 0, 0, 0)),
                  pl.BlockSpec((1, hy_w), lambda o: (0, 0)),
                  pl.BlockSpec((2 * n, n), lambda o: (0, 0))],
        out_specs=pl.BlockSpec((1, 2 * n, hy_w), lambda o: (o, 0, 0)),
        out_shape=jax.ShapeDtypeStruct((HY_ORDER, 2 * n, hy_w), F32),
        compiler_params=_cparams("arbitrary"),
        name="filter_direct",
    )(hidden, w3, deltas, ff)


def _filter_spectrum_two_level(l, hy_w, filt_params, m1f, gt, ct):
    f_w1, f_b1, f_w2, f_b2, f_w3, f_freq = filt_params
    hidden = _filter_hidden(l, f_w1, f_b1, f_w2, f_b2, f_freq)
    w3, deltas = _filter_tail_inputs(hy_w, f_w3)
    hid = f_w3.shape[0]
    n = 2 * l
    n2 = FFT_N2
    n1 = n // n2
    return pl.pallas_call(
        functools.partial(_filter_two_level_kernel, l=l, n2=n2, rows=min(n, 512)),
        grid=(HY_ORDER, hy_w // ct),
        in_specs=[pl.BlockSpec((n, hid), lambda o, j: (0, 0)),
                  pl.BlockSpec((1, N_DIR, hid, ct), lambda o, j: (o, 0, 0, j)),
                  pl.BlockSpec((1, ct), lambda o, j: (0, j)),
                  pl.BlockSpec((2 * n1, n1), lambda o, j: (0, 0)),
                  pl.BlockSpec((n1, 2 * n2, 2 * n2), lambda o, j: (0, 0, 0))],
        out_specs=pl.BlockSpec((1, n1, 2 * n2, ct), lambda o, j: (o, 0, 0, j)),
        out_shape=jax.ShapeDtypeStruct((HY_ORDER, n1, 2 * n2, hy_w), BF16),
        scratch_shapes=[pltpu.VMEM((n, ct), F32), pltpu.VMEM((n1 * 2 * n2, ct), F32)],
        compiler_params=_cparams("arbitrary", "arbitrary"),
        name="filter_two_level",
    )(hidden, w3, deltas, m1f, gt)


def _complex_mul(x, h, half):
    xr, xi = x[:half], x[half:]
    hr, hi = h[:half], h[half:]
    return jnp.concatenate([xr * hr - xi * hi, xr * hi + xi * hr], axis=0)


def _conv_direct_kernel(z_ref, gate_ref, bias_ref, h_ref, fwd_ref, inv_ref, o_ref):
    l = z_ref.shape[1]
    z = jnp.concatenate([z_ref[0], z_ref[1]], axis=0)
    x = _bdot(fwd_ref[...], z)
    y = _bdot(inv_ref[...], _complex_mul(x, h_ref[0], 2 * l))
    bias = bias_ref[0]
    o_ref[0] = gate_ref[0] * (y[:l] + bias * z_ref[0])
    o_ref[1] = gate_ref[1] * (y[l:] + bias * z_ref[1])


def _conv_two_level_kernel(z_ref, gate_ref, bias_ref, h_ref, m1_ref, m1i_ref, gt_ref, o_ref,
                           a_buf, *, n2):
    l = z_ref.shape[1]
    n1 = 2 * l // n2
    hn = n1 // 2

    def step1(b, carry):
        za = z_ref[0, pl.ds(b, hn, stride=n2), :]
        zb = z_ref[1, pl.ds(b, hn, stride=n2), :]
        a = _bdot(m1_ref[...], jnp.concatenate([za, zb], axis=0))
        a_buf[pl.ds(b, n1, stride=2 * n2), :] = a[:n1]
        a_buf[pl.ds(n2 + b, n1, stride=2 * n2), :] = a[n1:]
        return carry
    lax.fori_loop(0, n2, step1, 0, unroll=8)

    def step2(k1, carry):
        r0 = pl.multiple_of(k1 * 2 * n2, 2 * n2)
        x = _bdot(gt_ref[k1], a_buf[pl.ds(r0, 2 * n2), :])
        y = _complex_mul(x, h_ref[0, k1].astype(F32), n2).astype(BF16)
        a_buf[pl.ds(r0, 2 * n2), :] = lax.dot_general(gt_ref[k1], y, (((0,), (0,)), ((), ())),
                                                      preferred_element_type=F32)
        return carry
    lax.fori_loop(0, n1, step2, 0, unroll=4)

    def step3(b, carry):
        br = a_buf[pl.ds(b, n1, stride=2 * n2), :]
        bi = a_buf[pl.ds(n2 + b, n1, stride=2 * n2), :]
        y = _bdot(m1i_ref[...], jnp.concatenate([br, bi], axis=0))
        o_ref[0, pl.ds(b, hn, stride=n2), :] = y[:hn]
        o_ref[1, pl.ds(b, hn, stride=n2), :] = y[hn:]
        return carry
    lax.fori_loop(0, n2, step3, 0, unroll=8)

    bias = bias_ref[0]
    for s in range(2):
        o_ref[s] = gate_ref[s] * (o_ref[s] + bias * z_ref[s])


def _long_conv_gated(u, z, z_col, gate_col, spectrum, order, bias, tables, ct):
    bsz, l, _ = u.shape
    c = spectrum.shape[-1]
    nct = c // ct
    zspec = pl.BlockSpec((2, l, ct), lambda i, j: (i, 0, z_col * nct + j))
    gspec = pl.BlockSpec((2, l, ct), lambda i, j: (i, 0, gate_col * nct + j))
    bspec = pl.BlockSpec((1, 1, ct), lambda i, j: (order, 0, j))
    ospec = pl.BlockSpec((2, l, ct), lambda i, j: (i, 0, j))
    out_shape = jax.ShapeDtypeStruct((bsz, l, c), F32)
    bias3 = bias.reshape(HY_ORDER, 1, c)
    if len(tables) == 2:
        fwd, inv = tables
        n = 2 * l
        return pl.pallas_call(
            _conv_direct_kernel,
            grid=(bsz // 2, nct),
            in_specs=[zspec, gspec, bspec,
                      pl.BlockSpec((1, 2 * n, ct), lambda i, j: (order, 0, j)),
                      pl.BlockSpec(fwd.shape, lambda i, j: (0, 0)),
                      pl.BlockSpec(inv.shape, lambda i, j: (0, 0))],
            out_specs=ospec, out_shape=out_shape,
            compiler_params=_cparams("arbitrary", "arbitrary"),
            name="long_conv_direct",
        )(z, u, bias3, spectrum, fwd, inv)
    m1, m1i, gt = tables
    n2 = FFT_N2
    n1 = 2 * l // n2
    const2 = lambda i, j: (0, 0)
    const3 = lambda i, j: (0, 0, 0)
    return pl.pallas_call(
        functools.partial(_conv_two_level_kernel, n2=n2),
        grid=(nct, bsz // 2),
        in_specs=[pl.BlockSpec((2, l, ct), lambda j, i: (i, 0, z_col * nct + j)),
                  pl.BlockSpec((2, l, ct), lambda j, i: (i, 0, gate_col * nct + j)),
                  pl.BlockSpec((1, 1, ct), lambda j, i: (order, 0, j)),
                  pl.BlockSpec((1, n1, 2 * n2, ct), lambda j, i: (order, 0, 0, j)),
                  pl.BlockSpec(m1.shape, const2), pl.BlockSpec(m1i.shape, const2),
                  pl.BlockSpec(gt.shape, const3)],
        out_specs=pl.BlockSpec((2, l, ct), lambda j, i: (i, 0, j)),
        out_shape=out_shape,
        scratch_shapes=[pltpu.VMEM((n1 * 2 * n2, ct), F32)],
        compiler_params=_cparams("arbitrary", "arbitrary"),
        name="long_conv_two_level",
    )(z, u, bias3, spectrum, m1, m1i, gt)


def _hyena(z_hy, conv_w, conv_b, filt_params, hy_bias, row_w):
    bsz, l, c3 = z_hy.shape
    c = c3 // 3
    u = _short_conv(z_hy, conv_w, conv_b, row_w, silu=False)
    table = lambda a: jnp.asarray(a, F32).astype(BF16)
    if 2 * l // FFT_N2 <= 4:
        fwd, filt = _dft_direct_tables(l)
        tables = (table(fwd), table(fwd.T))
        spectrum = _filter_spectrum_direct(l, c, filt_params, table(filt))
        ct = c
    else:
        m1, m1f, gt = _dft_two_level_tables(l, FFT_N2)
        tables = (table(m1), table(m1.T), table(gt))
        ct = LANES
        spectrum = _filter_spectrum_two_level(l, c, filt_params, table(m1f), tables[2], ct)
    z1 = _long_conv_gated(u, u, 0, 1, spectrum, 0, hy_bias, tables, ct)
    return _long_conv_gated(u, z1, 0, 2, spectrum, 1, hy_bias, tables, ct)


def _log_sigmoid(x):
    return jnp.minimum(x, 0.0) - jnp.log1p(jnp.exp(-jnp.abs(x)))


def _split3_dot(a, b, split_lhs):
    x = a if split_lhs else b
    hi = x.astype(BF16)
    rest = x - hi.astype(F32)
    mid = rest.astype(BF16)
    parts = (hi, mid, (rest - mid.astype(F32)).astype(BF16))
    if split_lhs:
        return functools.reduce(lambda u, w: u + w, [jnp.dot(p, b, preferred_element_type=F32) for p in parts])
    return functools.reduce(lambda u, w: u + w, [jnp.dot(a, p, preferred_element_type=F32) for p in parts])


def _mlstm_recur(qb, sv, m_loc, kv, g_loc, bq, btot, cn, m):
    dh = qb.shape[1]
    inter = bq + m
    mj = jnp.maximum(m_loc, inter)
    w_int = jnp.exp(inter - mj)
    w_loc = jnp.exp(m_loc - mj)
    qc = jnp.dot(qb, cn.astype(BF16), preferred_element_type=F32)
    num = w_int * qc[:, :dh] + w_loc * sv[:, :dh]
    den = w_int * qc[:, dh:] + w_loc * sv[:, dh:]
    h = num / jnp.maximum(jnp.abs(den), jnp.exp(-mj))
    m_new = jnp.maximum(btot + m, g_loc)
    cn_new = jnp.exp(btot + m - m_new) * cn + jnp.exp(g_loc - m_new) * kv
    return h, cn_new, m_new


def _mlstm_kernel(q_ref, kt_ref, v_ref, o_ref, g_ref, gt_ref, gb_ref, gbt_ref, ng_ref, c0_ref, n0_ref, m0_ref,
                  y_ref, c_ref, n_ref, m_ref, hf_buf, hb_buf, cn_buf, sv_buf, kv_buf, bq_buf, ml_buf, sc_buf,
                  *, chunk, heads):
    l, dh = q_ref.shape[1], q_ref.shape[2]
    nc = l // chunk
    group = sv_buf.shape[1]
    pair = math.gcd(group, MLSTM_PAIR)
    scale = dh ** -0.5
    row = lax.broadcasted_iota(jnp.int32, (chunk, chunk), 0)
    col = lax.broadcasted_iota(jnp.int32, (chunk, chunk), 1)
    lower, upper = col <= row, col >= row
    tri_l, tri_u = lower.astype(BF16), upper.astype(BF16)
    gate_row = lax.broadcasted_iota(jnp.int32, (SUBLANES, chunk), 0)
    ones = jnp.ones((chunk, dh), BF16)

    sel_row = lax.broadcasted_iota(jnp.int32, (LANES, LANES), 0)
    head = pl.program_id(1)

    def chunk_start(j, d):
        return pl.multiple_of(((nc - 1 - j) if d else j) * chunk, chunk)

    def local(gp, carry, j0):
        jobs = [(gp * pair + k, d) for k in range(pair) for d in range(N_DIR)]
        r0s = [chunk_start(j0 + g, d) for g, d in jobs]
        picks = [(sel_row == (2 * d + 1) * heads + head).astype(BF16) for d in range(N_DIR)]
        lfs = [_log_sigmoid(g_ref[0, pl.ds(r0, chunk), :] + gb_ref[...]) for r0 in r0s]
        gts = [gt_ref[0, 0, :, pl.ds(r0, chunk)] + gbt_ref[0] for r0 in r0s]
        gts = [jnp.where(gate_row % 2 == 1, _log_sigmoid(gt), gt) for gt in gts]
        lfs = [_split3_dot(lf, picks[d], split_lhs=True) for lf, (_, d) in zip(lfs, jobs)]
        brows = [_split3_dot(gt, tri_l if d else tri_u, split_lhs=True)[2 * d + 1:2 * d + 2, :]
                 for gt, (_, d) in zip(gts, jobs)]
        bqs = [_split3_dot(tri_u if d else tri_l, lf, split_lhs=False) for lf, (_, d) in zip(lfs, jobs)]
        btots = [brow[:, 0:1] if d else brow[:, chunk - 1:chunk] for brow, (_, d) in zip(brows, jobs)]
        irows = [gt[2 * d:2 * d + 1, :] for gt, (_, d) in zip(gts, jobs)]
        qbs = [q_ref[0, pl.ds(r0, chunk), :].astype(BF16) for r0 in r0s]
        kts = [kt_ref[0, :, pl.ds(r0, chunk)] * scale for r0 in r0s]
        vos = [jnp.concatenate([v_ref[0, pl.ds(r0, chunk), :].astype(BF16), ones], axis=1) for r0 in r0s]
        qks = [jnp.dot(qb, kt.astype(BF16), preferred_element_type=F32) for qb, kt in zip(qbs, kts)]
        dms = [jnp.where(upper if d else lower, bq - brow + irow, -jnp.inf)
               for bq, brow, irow, (_, d) in zip(bqs, brows, irows, jobs)]
        m_locs = [jnp.max(dm, axis=-1, keepdims=True) for dm in dms]
        ss = [(qk * jnp.exp(dm - m_loc)).astype(BF16) for qk, dm, m_loc in zip(qks, dms, m_locs)]
        gls = [btot - brow + irow for btot, brow, irow in zip(btots, brows, irows)]
        g_locs = [jnp.max(gl, axis=-1, keepdims=True) for gl in gls]
        wks = [(kt * jnp.exp(gl - g_loc)).astype(BF16) for kt, gl, g_loc in zip(kts, gls, g_locs)]
        svs = [jnp.dot(s, vo, preferred_element_type=F32) for s, vo in zip(ss, vos)]
        kvs = [jnp.dot(wk, vo, preferred_element_type=F32) for wk, vo in zip(wks, vos)]
        for (g, d), sv, kv, bq, m_loc, btot, g_loc in zip(jobs, svs, kvs, bqs, m_locs, btots, g_locs):
            sv_buf[d, g] = sv
            kv_buf[d, g] = kv
            bq_buf[d, g] = bq
            ml_buf[d, g] = jnp.broadcast_to(m_loc, (chunk, LANES))
            sc_buf[d, g, 0:1, :] = jnp.broadcast_to(btot, (1, LANES))
            sc_buf[d, g, 1:2, :] = jnp.broadcast_to(g_loc, (1, LANES))
        return carry

    def recur(g, carry, j0):
        ms = list(carry)
        for d in range(N_DIR):
            r0 = chunk_start(j0 + g, d)
            h, cn, ms[d] = _mlstm_recur(
                q_ref[0, pl.ds(r0, chunk), :].astype(BF16), sv_buf[d, g], ml_buf[d, g], kv_buf[d, g],
                sc_buf[d, g, 1:2, 0:1], bq_buf[d, g], sc_buf[d, g, 0:1, 0:1], cn_buf[d], ms[d])
            cn_buf[d] = cn
            (hb_buf if d else hf_buf)[pl.ds(r0, chunk), :] = h
        return tuple(ms)

    for d in range(N_DIR):
        n_rep = jnp.broadcast_to(n0_ref[0, 0, d:d + 1, :], (dh, dh)).T
        cn_buf[d] = jnp.concatenate([c0_ref[0, d, 0], n_rep], axis=1)

    def block(jb, carry):
        j0 = jb * group
        lax.fori_loop(0, group // pair, functools.partial(local, j0=j0), 0)
        return lax.fori_loop(0, group, functools.partial(recur, j0=j0), carry, unroll=2)

    m_fin = lax.fori_loop(0, nc // group, block, (m0_ref[0, 0, 0:1, 0:1], m0_ref[0, 0, 1:2, 0:1]))
    for d in range(N_DIR):
        c_ref[0, d, 0] = cn_buf[d, :, :dh]
        n_ref[0, 0, d:d + 1, :] = cn_buf[d, :, dh:].T[0:1, :]
        m_ref[0, 0, d:d + 1, :] = jnp.broadcast_to(m_fin[d], (1, LANES))

    def finish(j, carry):
        r0 = pl.multiple_of(j * chunk, chunk)
        hs = hf_buf[pl.ds(r0, chunk), :] + hb_buf[pl.ds(r0, chunk), :]
        hs = hs * lax.rsqrt(jnp.mean(hs * hs, axis=-1, keepdims=True) + EPS) * ng_ref[...]
        y_ref[0, pl.ds(r0, chunk), :] = jax.nn.sigmoid(o_ref[0, pl.ds(r0, chunk), :]) * hs
        return carry
    lax.fori_loop(0, nc, finish, 0)


def _mlstm(q, kt, z_v, z_o, z_g, gate_b, norm_g, c0, n0, m0):
    bsz, l, ml = z_v.shape
    heads = gate_b.shape[-1]
    dh = ml // heads
    chunk = MLSTM_CHUNK
    assert dh == LANES and chunk == LANES and l % chunk == 0
    group = math.gcd(l // chunk, MLSTM_GROUP)
    n_gate = N_DIR * 2
    assert z_g.shape[-1] == LANES
    g4 = z_g[..., :n_gate * heads].reshape(bsz, l, n_gate, heads)
    ght = jnp.pad(g4.transpose(0, 3, 2, 1), ((0, 0), (0, 0), (0, SUBLANES - n_gate), (0, 0)))
    gb4 = gate_b.reshape(n_gate, heads).T
    gb = jnp.pad(gate_b.reshape(1, n_gate * heads), ((0, 0), (0, LANES - n_gate * heads)))
    gbt = jnp.broadcast_to(jnp.pad(gb4, ((0, 0), (0, SUBLANES - n_gate)))[..., None], (heads, SUBLANES, LANES))
    n0h = n0.transpose(0, 2, 1, 3)
    m0h = jnp.broadcast_to(m0.transpose(0, 2, 1)[..., None], (bsz, heads, N_DIR, LANES))
    seq = pl.BlockSpec((1, l, dh), lambda b, h: (b, 0, h))
    cspec = pl.BlockSpec((1, N_DIR, 1, dh, dh), lambda b, h: (b, 0, h, 0, 0))
    sspec = pl.BlockSpec((1, 1, N_DIR, dh), lambda b, h: (b, h, 0, 0))
    mspec = pl.BlockSpec((1, 1, N_DIR, LANES), lambda b, h: (b, h, 0, 0))
    y, c, n, m = pl.pallas_call(
        functools.partial(_mlstm_kernel, chunk=chunk, heads=heads),
        grid=(bsz, heads),
        in_specs=[seq, pl.BlockSpec((1, dh, l), lambda b, h: (b, h, 0)), seq, seq,
                  pl.BlockSpec((1, l, LANES), lambda b, h: (b, 0, 0)),
                  pl.BlockSpec((1, 1, SUBLANES, l), lambda b, h: (b, h, 0, 0)),
                  pl.BlockSpec((1, LANES), lambda b, h: (0, 0)),
                  pl.BlockSpec((1, SUBLANES, LANES), lambda b, h: (h, 0, 0)),
                  pl.BlockSpec((1, dh), lambda b, h: (0, h)),
                  cspec, sspec, mspec],
        out_specs=[seq, cspec, sspec, mspec],
        out_shape=[jax.ShapeDtypeStruct((bsz, l, ml), F32),
                   jax.ShapeDtypeStruct((bsz, N_DIR, heads, dh, dh), F32),
                   jax.ShapeDtypeStruct((bsz, heads, N_DIR, dh), F32),
                   jax.ShapeDtypeStruct((bsz, heads, N_DIR, LANES), F32)],
        scratch_shapes=[pltpu.VMEM((l, dh), F32), pltpu.VMEM((l, dh), F32),
                        pltpu.VMEM((N_DIR, dh, 2 * dh), F32),
                        pltpu.VMEM((N_DIR, group, chunk, 2 * dh), F32),
                        pltpu.VMEM((N_DIR, group, dh, 2 * dh), F32),
                        pltpu.VMEM((N_DIR, group, chunk, LANES), F32),
                        pltpu.VMEM((N_DIR, group, chunk, LANES), F32),
                        pltpu.VMEM((N_DIR, group, SUBLANES, LANES), F32)],
        compiler_params=_cparams("arbitrary", "arbitrary"),
        name="mlstm",
    )(q, kt, z_v, z_o, z_g, ght, gb, gbt, norm_g.reshape(1, ml), c0, n0h, m0h)
    return y, (c, n.transpose(0, 2, 1, 3), m[..., 0].transpose(0, 2, 1))


def _rows_to_tiles(tile_ref, x):
    r, d = x.shape
    s = d // LANES
    for k in range(s):
        tile_ref[pl.ds(k, r, stride=s), :] = x[:, LANES * k:LANES * (k + 1)]


def _tiles_to_rows(tile_ref, r):
    s = tile_ref.shape[0] // r
    return jnp.concatenate([tile_ref[pl.ds(k, r, stride=s), :] for k in range(s)], axis=1)


def _lane_pack(cols, lane):
    out = jnp.zeros(lane.shape, cols[0].dtype)
    for j, colv in enumerate(cols):
        out = jnp.where(lane == j, colv, out)
    return out


def _post_kernel(yhy_ref, yml_ref, x_ref, mod_ref, wo_ref, n2g_ref, rw_ref, rb_ref, cnt0_ref,
                 x1_ref, h2_ref, ti_ref, tg_ref, rk_ref, cnt_ref, carry, *, n_exp, top_k):
    @pl.when(pl.program_id(0) == 0)
    def _():
        carry[...] = cnt0_ref[0:1, :]

    hy_w = yhy_ref.shape[1]
    tm = x_ref.shape[0]
    proj = _bdot(yhy_ref[...], wo_ref[:hy_w, :]) + _bdot(yml_ref[...], wo_ref[hy_w:, :])
    x1 = x_ref[...] + mod_ref[0, 2:3, :] * proj
    x1_ref[...] = x1
    h2 = x1 * lax.rsqrt(jnp.mean(x1 * x1, axis=-1, keepdims=True) + EPS) * n2g_ref[...]
    h2 = h2 * (1.0 + mod_ref[0, 4:5, :]) + mod_ref[0, 3:4, :]
    _rows_to_tiles(h2_ref, h2)

    lane = lax.broadcasted_iota(jnp.int32, (tm, LANES), 1)
    work = jnp.where(lane < n_exp, _dot3(h2, rw_ref[...]) + rb_ref[...], -jnp.inf)
    vals, idxs, hots = [], [], []
    for _ in range(top_k):
        mx = jnp.max(work, axis=-1, keepdims=True)
        idx = jnp.min(jnp.where(work == mx, lane, LANES), axis=-1, keepdims=True)
        hot = lane == idx
        vals.append(mx)
        idxs.append(idx)
        hots.append(hot)
        work = jnp.where(hot, -jnp.inf, work)
    exps = [jnp.exp(v - vals[0]) for v in vals]
    tot = functools.reduce(lambda a, b: a + b, exps)
    ti_ref[...] = _lane_pack(idxs, lane)
    tg_ref[...] = _lane_pack([e / tot for e in exps], lane)

    hot_sum = functools.reduce(lambda a, b: a + b, [h.astype(F32) for h in hots])
    row = lax.broadcasted_iota(jnp.int32, (tm, tm), 0)
    col = lax.broadcasted_iota(jnp.int32, (tm, tm), 1)
    before = _bdot((col < row).astype(F32), hot_sum) + carry[...]
    ranks = [jnp.sum(jnp.where(h, before, 0.0), axis=-1, keepdims=True).astype(jnp.int32) for h in hots]
    rk_ref[...] = _lane_pack(ranks, lane)
    carry[...] = carry[...] + jnp.sum(hot_sum, axis=0, keepdims=True)
    cnt_ref[...] = jnp.broadcast_to(carry[...], cnt_ref.shape)


def _post(y_hy, y_ml, x, mods, w_out, norm2_g, r_w, r_b, counts, mod_map, tm):
    t, d = x.shape
    n_exp = r_w.shape[1]
    rwp = jnp.pad(r_w, ((0, 0), (0, LANES - n_exp)))
    rbp = jnp.pad(r_b, (0, LANES - n_exp)).reshape(1, LANES)
    row = lambda wd: pl.BlockSpec((tm, wd), lambda i: (i, 0))
    const = lambda shape: pl.BlockSpec(shape, lambda i: (0, 0))
    return pl.pallas_call(
        functools.partial(_post_kernel, n_exp=n_exp, top_k=TOP_K),
        grid=(t // tm,),
        in_specs=[row(y_hy.shape[1]), row(y_ml.shape[1]), row(d),
                  pl.BlockSpec((1, 6, d), mod_map),
                  const(w_out.shape), const((1, d)), const((d, LANES)), const((1, LANES)),
                  const((SUBLANES, LANES))],
        out_specs=[row(d), pl.BlockSpec((tm * d // LANES, LANES), lambda i: (i, 0)),
                   row(LANES), row(LANES), row(LANES), const((SUBLANES, LANES))],
        out_shape=[jax.ShapeDtypeStruct((t, d), F32), jax.ShapeDtypeStruct((t * d // LANES, LANES), F32),
                   jax.ShapeDtypeStruct((t, LANES), jnp.int32), jax.ShapeDtypeStruct((t, LANES), F32),
                   jax.ShapeDtypeStruct((t, LANES), jnp.int32), jax.ShapeDtypeStruct((SUBLANES, LANES), F32)],
        scratch_shapes=[pltpu.VMEM((1, LANES), F32)],
        compiler_params=_cparams("arbitrary"),
        name="post",
    )(y_hy, y_ml, x, mods, w_out.astype(BF16), norm2_g.reshape(1, d), rwp, rbp, counts)


def _dest_kernel(ti_ref, rk_ref, ps_ref, d_ref, *, top_k):
    lane = lax.broadcasted_iota(jnp.int32, ti_ref.shape, 1)
    ti, rk = ti_ref[...], rk_ref[...]
    cols = []
    for j in range(top_k):
        start = jnp.sum(jnp.where(lane == ti[:, j:j + 1], ps_ref[...], 0.0), axis=-1, keepdims=True)
        cols.append(start.astype(jnp.int32) + rk[:, j:j + 1])
    d_ref[...] = _lane_pack(cols, lane)


def _dest_rows(ti, rk, pad_start, tm):
    t = ti.shape[0]
    row = pl.BlockSpec((tm, LANES), lambda i: (i, 0))
    return pl.pallas_call(
        functools.partial(_dest_kernel, top_k=TOP_K),
        grid=(t // tm,),
        in_specs=[row, row, pl.BlockSpec((1, LANES), lambda i: (0, 0))],
        out_specs=row,
        out_shape=jax.ShapeDtypeStruct((t, LANES), jnp.int32),
        compiler_params=_cparams("arbitrary"),
        name="dest_rows",
    )(ti, rk, pad_start)


def _dispatch_kernel(first_ref, count_ref, nu_ref, dest_ref, *refs, top_k, s, rows, tiles):
    h_refs = refs[:len(tiles)]
    xs_ref, zbuf, sem, zsem = refs[len(tiles):]
    i = pl.program_id(0)
    tt = dest_ref.shape[2] // top_k
    half = zbuf.shape[0] // s
    n_blocks = xs_ref.shape[0] // (rows * s)

    def pad_rows(e, carry, wait):
        off, n = first_ref[e], count_ref[e]
        for k in range(half.bit_length()):
            bit = half >> k

            @pl.when((n & bit) != 0)
            def _():
                dst = pl.multiple_of(off * s, s)
                copy = pltpu.make_async_copy(zbuf.at[pl.ds(0, bit * s)], xs_ref.at[pl.ds(dst, bit * s)], zsem)
                copy.wait() if wait else copy.start()
            off = off + (n & bit)
        return carry

    def spare_block(b, carry, wait):
        for part in range(2):
            dst = pl.multiple_of((b * 2 + part) * half * s, half * s)
            copy = pltpu.make_async_copy(zbuf, xs_ref.at[pl.ds(dst, half * s)], zsem)
            copy.wait() if wait else copy.start()
        return carry

    @pl.when(i == 0)
    def _():
        zbuf[...] = jnp.zeros_like(zbuf)
        for wait in (False, True):
            lax.fori_loop(0, first_ref.shape[0], functools.partial(pad_rows, wait=wait), 0)
            lax.fori_loop(nu_ref[0], n_blocks, functools.partial(spare_block, wait=wait), 0)

    def scatter(h_ref):
        def issue(t, carry):
            src = pl.multiple_of(t * s, s)
            for j in range(top_k):
                dst = pl.multiple_of(dest_ref[0, 0, t * top_k + j] * s, s)
                pltpu.make_async_copy(h_ref.at[pl.ds(src, s)], xs_ref.at[pl.ds(dst, s)], sem).start(priority=j % 2)
            return carry
        lax.fori_loop(0, tt, issue, 0, unroll=2)
        for _ in range(top_k):
            pltpu.make_async_copy(h_ref, xs_ref.at[pl.ds(0, tt * s)], sem).wait()

    lo = 0
    for h_ref, n_tiles in zip(h_refs, tiles):
        @pl.when((i >= lo) & (i < lo + n_tiles))
        def _():
            scatter(h_ref)
        lo += n_tiles


def _dispatch(dest, h2ts, pad_first, pad_count, n_used, n_blocks, rows, tt):
    s = sum(h.shape[0] for h in h2ts) * TOP_K // dest.size
    tiles = tuple(h.shape[0] // (tt * s) for h in h2ts)
    starts = [sum(tiles[:k]) for k in range(len(tiles))]
    hspec = lambda lo, n: pl.BlockSpec((tt * s, LANES), lambda i, *_: (jnp.clip(i - lo, 0, n - 1), 0))
    return pl.pallas_call(
        functools.partial(_dispatch_kernel, top_k=TOP_K, s=s, rows=rows, tiles=tiles),
        grid_spec=pltpu.PrefetchScalarGridSpec(
            num_scalar_prefetch=3,
            grid=(dest.shape[0],),
            in_specs=[pl.BlockSpec((1, 1, tt * TOP_K), lambda i, *_: (i, 0, 0), memory_space=pltpu.SMEM)]
            + [hspec(lo, n) for lo, n in zip(starts, tiles)],
            out_specs=pl.BlockSpec(memory_space=pl.ANY),
            scratch_shapes=[pltpu.VMEM((rows // 2 * s, LANES), F32),
                            pltpu.SemaphoreType.DMA(()), pltpu.SemaphoreType.DMA(())]),
        out_shape=jax.ShapeDtypeStruct((n_blocks * rows * s, LANES), F32),
        compiler_params=_cparams("arbitrary"),
        name="dispatch",
    )(pad_first, pad_count, n_used, dest, *h2ts)


def _deinterleave_table():
    p = np.zeros((2 * LANES, 2 * LANES), np.float32)
    j = np.arange(LANES)
    p[2 * j, j] = 1.0
    p[2 * j + 1, LANES + j] = 1.0
    return p


def _ffn_kernel(be_ref, nu_ref, xs_ref, wgu_ref, bg_ref, bl_ref, wd_ref, bd_ref, perm_ref, ys_ref,
                wg_buf, wl_buf, wd_buf, acc_buf, *, ft):
    i = pl.program_id(0)
    f = wg_buf.shape[1]
    rows = acc_buf.shape[0]
    live = i < nu_ref[0]

    @pl.when(live & ((i == 0) | (be_ref[i] != be_ref[jnp.maximum(i - 1, 0)])))
    def _():
        for j in range(f // LANES):
            blk = wgu_ref[0, :, 2 * LANES * j:2 * LANES * (j + 1)].astype(BF16)
            split = jnp.dot(blk, perm_ref[...], preferred_element_type=F32)
            wg_buf[:, LANES * j:LANES * (j + 1)] = split[:, :LANES].astype(BF16)
            wl_buf[:, LANES * j:LANES * (j + 1)] = split[:, LANES:].astype(BF16)
        wd_buf[...] = wd_ref[0].astype(BF16)

    @pl.when(live)
    def _():
        x = _tiles_to_rows(xs_ref, rows).astype(BF16)
        for j, f0 in enumerate(range(0, f, ft)):
            g = jnp.dot(x, wg_buf[:, f0:f0 + ft], preferred_element_type=F32) + bg_ref[0, :, f0:f0 + ft]
            lin = jnp.dot(x, wl_buf[:, f0:f0 + ft], preferred_element_type=F32) + bl_ref[0, :, f0:f0 + ft]
            gate = jnp.minimum(g, SWIGLU_LIMIT)
            lin = jnp.clip(lin, -SWIGLU_LIMIT, SWIGLU_LIMIT)
            act = (lin + 1.0) * gate * jax.nn.sigmoid(SWIGLU_ALPHA * gate)
            part = jnp.dot(act.astype(BF16), wd_buf[f0:f0 + ft, :], preferred_element_type=F32)
            if j == 0:
                acc_buf[...] = part + bd_ref[0]
            else:
                acc_buf[...] += part
        _rows_to_tiles(ys_ref, acc_buf[...])

    @pl.when(jnp.logical_not(live))
    def _():
        ys_ref[...] = jnp.zeros_like(ys_ref)


def _ffn(block_e, n_used, xs, w_gu, b_gu, w_d, b_d, rows):
    n_exp, d, f2 = w_gu.shape
    f = f2 // 2
    s = d // LANES
    n_rows = xs.shape[0] // s
    live = lambda i, nu: jnp.minimum(i, nu[0] - 1)
    wmap = lambda i, be, nu: (be[live(i, nu)], 0, 0)
    perm = jnp.asarray(_deinterleave_table(), BF16)
    return pl.pallas_call(
        functools.partial(_ffn_kernel, ft=min(f, 1024)),
        grid_spec=pltpu.PrefetchScalarGridSpec(
            num_scalar_prefetch=2,
            grid=(n_rows // rows,),
            in_specs=[pl.BlockSpec((rows * s, LANES), lambda i, be, nu: (live(i, nu), 0)),
                      pl.BlockSpec((1, d, f2), wmap),
                      pl.BlockSpec((1, 1, f), wmap), pl.BlockSpec((1, 1, f), wmap),
                      pl.BlockSpec((1, f, d), wmap), pl.BlockSpec((1, 1, d), wmap),
                      pl.BlockSpec(perm.shape, lambda i, be, nu: (0, 0))],
            out_specs=pl.BlockSpec((rows * s, LANES), lambda i, be, nu: (i, 0)),
            scratch_shapes=[pltpu.VMEM((d, f), BF16), pltpu.VMEM((d, f), BF16), pltpu.VMEM((f, d), BF16),
                            pltpu.VMEM((rows, d), F32)]),
        out_shape=jax.ShapeDtypeStruct(xs.shape, F32),
        compiler_params=_cparams("arbitrary"),
        name="expert_ffn",
    )(block_e, n_used, xs, w_gu, b_gu[:, 0::2].reshape(n_exp, 1, f), b_gu[:, 1::2].reshape(n_exp, 1, f),
      w_d, b_d.reshape(n_exp, 1, d), perm)


def _combine_kernel(dest_ref, next_ref, tg_ref, x1_ref, mod_ref, fg_ref, ys_ref, o_ref, ybuf, sem, *,
                    top_k, final_norm):
    i = pl.program_id(0)
    tt = x1_ref.shape[0]
    s = ybuf.shape[2] // tt
    slot = i % 2

    def gather(d_ref, to):
        def issue(t, carry):
            dst = pl.multiple_of(t * s, s)
            for j in range(top_k):
                src = pl.multiple_of(d_ref[0, 0, t * top_k + j] * s, s)
                pltpu.make_async_copy(ys_ref.at[pl.ds(src, s)], ybuf.at[to, j, pl.ds(dst, s)],
                                      sem.at[to]).start(priority=j % 2)
            return carry
        lax.fori_loop(0, tt, issue, 0, unroll=2)

    @pl.when(i == 0)
    def _():
        gather(dest_ref, 0)

    @pl.when(i + 1 < pl.num_programs(0))
    def _():
        gather(next_ref, 1 - slot)

    for j in range(top_k):
        pltpu.make_async_copy(ys_ref.at[pl.ds(0, tt * s)], ybuf.at[slot, j], sem.at[slot]).wait()
    tg = tg_ref[...]
    moe = tg[:, 0:1] * _tiles_to_rows(ybuf.at[slot, 0], tt)
    for j in range(1, top_k):
        moe = moe + tg[:, j:j + 1] * _tiles_to_rows(ybuf.at[slot, j], tt)
    x2 = x1_ref[...] + mod_ref[0, 5:6, :] * moe
    if final_norm:
        x2 = x2 * lax.rsqrt(jnp.mean(x2 * x2, axis=-1, keepdims=True) + EPS) * fg_ref[...]
    o_ref[...] = x2


def _combine(dest, tg, x1, mods, final_g, ys, mod_map, tt, final_norm):
    t, d = x1.shape
    steps = t // tt
    return pl.pallas_call(
        functools.partial(_combine_kernel, top_k=TOP_K, final_norm=final_norm),
        grid=(steps,),
        in_specs=[pl.BlockSpec((1, 1, tt * TOP_K), lambda i: (i, 0, 0), memory_space=pltpu.SMEM),
                  pl.BlockSpec((1, 1, tt * TOP_K), lambda i: (jnp.minimum(i + 1, steps - 1), 0, 0),
                               memory_space=pltpu.SMEM),
                  pl.BlockSpec((tt, LANES), lambda i: (i, 0)),
                  pl.BlockSpec((tt, d), lambda i: (i, 0)),
                  pl.BlockSpec((1, 6, d), mod_map),
                  pl.BlockSpec((1, d), lambda i: (0, 0)),
                  pl.BlockSpec(memory_space=pl.ANY)],
        out_specs=pl.BlockSpec((tt, d), lambda i: (i, 0)),
        out_shape=jax.ShapeDtypeStruct((t, d), F32),
        scratch_shapes=[pltpu.VMEM((2, TOP_K, tt * d // LANES, LANES), F32), pltpu.SemaphoreType.DMA((2,))],
        compiler_params=_cparams("arbitrary"),
        name="combine",
    )(dest, dest, tg, x1, mods, final_g.reshape(1, d), ys)


def _moe_plan(counts, rows, n_blocks):
    n_exp = counts.shape[0]
    padded = (counts + rows - 1) // rows * rows
    pad_end = jnp.cumsum(padded)
    block_row = jnp.arange(n_blocks, dtype=jnp.int32) * rows
    block_e = jnp.minimum(jnp.sum(pad_end[None, :] <= block_row[:, None], axis=1), n_exp - 1).astype(jnp.int32)
    n_used = (pad_end[-1:] // rows).astype(jnp.int32)
    start = pad_end - padded
    pad_start = jnp.pad(start.astype(F32), (0, LANES - n_exp)).reshape(1, LANES)
    return pad_start, block_e, n_used, (start + counts).astype(jnp.int32), (padded - counts).astype(jnp.int32)


def _sequence_mixers(z_hy, z_qk, z_v, z_o, z_g, lw, state, row_w):
    (hy_cw, hy_cb, filt_params, hy_b, ml_cw, ml_cb, ml_gb, ml_ng) = lw
    y_hy = _hyena(z_hy, hy_cw, hy_cb, filt_params, hy_b, row_w)
    ml_w = z_v.shape[-1]
    q = _short_conv(z_qk, ml_cw, ml_cb, row_w, silu=True, col0=0, ncols=ml_w)
    kt = _short_conv(z_qk, ml_cw, ml_cb, row_w, silu=True, col0=ml_w, ncols=ml_w, transpose=True)
    y_ml, st = _mlstm(q, kt, z_v, z_o, z_g, ml_gb, ml_ng, *state)
    return y_hy, y_ml, st


def kernel(x_prompt, x_sample, state_mlstm_C, state_mlstm_n, state_mlstm_m, c, c_ctx, ada_w, ada_b, norm1_g,
           w_in, hy_conv_w, hy_conv_b, filt_w1, filt_b1, filt_w2, filt_b2, filt_w3, filt_freq, hy_bias,
           ml_conv_w, ml_conv_b, ml_gate_b, ml_norm_g, w_out, norm2_g, router_w, router_b, moe_w_gu,
           moe_b_gu, moe_w_down, moe_b_down, final_g):
    bp, lp, d = x_prompt.shape
    bs, ls, _ = x_sample.shape
    depth = ada_w.shape[0]
    heads = ml_gate_b.shape[-1]
    hy_w = hy_bias.shape[-1]
    ml_w = ml_norm_g.shape[-1]
    dh = ml_w // heads
    n_exp = router_w.shape[-1]
    t = bp * lp + bs * ls
    ng = N_DIR * 2 * heads
    seg_widths = (3 * hy_w, 2 * ml_w, ml_w, ml_w)
    n_main = 3 * hy_w + 4 * ml_w
    n_blocks = -(-(t * TOP_K) // MOE_ROWS) + n_exp

    cond = jnp.concatenate([c_ctx[None], c, jnp.zeros((SUBLANES - 1 - bs, d), F32)], axis=0)
    zero_state = (jnp.zeros((bp, N_DIR, heads, dh, dh), F32), jnp.zeros((bp, N_DIR, heads, dh), F32),
                  jnp.zeros((bp, N_DIR, heads), F32))
    xs_paths = [x_prompt, x_sample]
    path_cfg = [(0, 0, lp), (1, 1, GRID_W)]
    new_c, new_n, new_m = [], [], []
    for l in range(depth):
        mods = _ada(cond, ada_w[l], ada_b[l]).reshape(SUBLANES, 6, d)
        w_main = w_in[l][:, :n_main].astype(BF16)
        w_gate = jnp.pad(w_in[l][:, n_main:], ((0, 0), (0, LANES - ng)))
        lw = (hy_conv_w[l], hy_conv_b[l],
              (filt_w1[l], filt_b1[l], filt_w2[l], filt_b2[l], filt_w3[l], filt_freq[l]), hy_bias[l],
              ml_conv_w[l], ml_conv_b[l], ml_gate_b[l], ml_norm_g[l])
        states = [zero_state, (state_mlstm_C[:, l], state_mlstm_n[:, l], state_mlstm_m[:, l])]
        counts = jnp.zeros((SUBLANES, LANES), F32)
        routed = []
        for x3, (mod0, mod_step, row_w), state in zip(xs_paths, path_cfg, states):
            bsz, lseq, _ = x3.shape
            tm, tt = min(ROW_TILE, lseq), min(TOK_TILE, lseq)
            xf = x3.reshape(bsz * lseq, d)
            z = _inproj(xf, mods, norm1_g[l], w_main, w_gate, seg_widths,
                        _mod_index_map(mod0, mod_step, lseq // tm), tm)
            y_hy, y_ml, st = _sequence_mixers(*[a.reshape(bsz, lseq, a.shape[1]) for a in z], lw, state, row_w)
            x1, h2t, ti, tg, rk, counts = _post(
                y_hy.reshape(bsz * lseq, hy_w), y_ml.reshape(bsz * lseq, ml_w), xf, mods, w_out[l], norm2_g[l],
                router_w[l], router_b[l], counts, _mod_index_map(mod0, mod_step, lseq // tm), tm)
            routed.append((x1, h2t, ti, tg, rk, tm, tt, _mod_index_map(mod0, mod_step, lseq // tt), st))
        new_c.append(routed[0][-1][0])
        new_n.append(routed[0][-1][1])
        new_m.append(routed[0][-1][2])

        pad_start, block_e, n_used, pad_first, pad_count = _moe_plan(
            counts[0, :n_exp].astype(jnp.int32), MOE_ROWS, n_blocks)
        tt = routed[0][6]
        assert all(r[6] == tt for r in routed)
        dests = [_dest_rows(ti, rk, pad_start, tm)[:, :TOP_K].reshape(-1, 1, tt * TOP_K)
                 for _, _, ti, _, rk, tm, _, _, _ in routed]
        xs = _dispatch(jnp.concatenate(dests, axis=0), [r[1] for r in routed], pad_first, pad_count, n_used,
                       n_blocks, MOE_ROWS, tt)
        ys = _ffn(block_e, n_used, xs, moe_w_gu[l], moe_b_gu[l], moe_w_down[l], moe_b_down[l], MOE_ROWS)
        xs_paths = [
            _combine(dest, tg, x1, mods, final_g, ys, mod_map, tt, final_norm=l == depth - 1).reshape(x3.shape)
            for dest, (x1, _, _, tg, _, _, tt, mod_map, _), x3 in zip(dests, routed, xs_paths)]
    return (xs_paths[0], xs_paths[1],
            jnp.stack(new_c, axis=1), jnp.stack(new_n, axis=1), jnp.stack(new_m, axis=1))
```

```python
import functools
import math

import numpy as np
import jax
import jax.numpy as jnp
from jax import lax
from jax.experimental import pallas as pl
from jax.experimental.pallas import tpu as pltpu

F32 = jnp.float32
BF16 = jnp.bfloat16
HIGHEST = lax.Precision.HIGHEST
EPS = 1e-6

LANES = 128
SUBLANES = 8
VMEM_LIMIT_BYTES = 56 * 1024 * 1024

GRID_W = 64
ML_HEADS = 4
N_DIR = 2
HY_ORDER = 2
FILT_BANDS = 8
DECAY_TARGET = 1e-2
FAST_DECAY_PCT = 0.3
SLOW_DECAY_PCT = 1.5
TOP_K = 4
SWIGLU_LIMIT = 7.0
SWIGLU_ALPHA = 1.702

FFT_N2 = 128
MLSTM_CHUNK = 128
MLSTM_GROUP = 8
MLSTM_PAIR = 4
MOE_ROWS = 512
ROW_TILE = 512
TOK_TILE = 256


def _cparams(*sem):
    return pltpu.CompilerParams(dimension_semantics=sem, vmem_limit_bytes=VMEM_LIMIT_BYTES)


def _lane_tile(c, cap):
    return max(t for t in range(LANES, min(c, cap) + 1, LANES) if c % t == 0)


def _bdot(a, b):
    return jnp.dot(a.astype(BF16), b.astype(BF16), preferred_element_type=F32)


def _hdot(a, b):
    return jnp.dot(a, b, precision=HIGHEST, preferred_element_type=F32)


def _dot3(a, b):
    a_hi, b_hi = a.astype(BF16), b.astype(BF16)
    a_lo = (a - a_hi.astype(F32)).astype(BF16)
    b_lo = (b - b_hi.astype(F32)).astype(BF16)
    dot = functools.partial(jnp.dot, preferred_element_type=F32)
    return dot(a_hi, b_hi) + dot(a_hi, b_lo) + dot(a_lo, b_hi)


def _ada_kernel(c_ref, w_ref, b_ref, o_ref):
    c = c_ref[...]
    o_ref[...] = _hdot(c * jax.nn.sigmoid(c), w_ref[...]) + b_ref[...]


def _ada(cond, w, b):
    r, d = cond.shape
    n = w.shape[1]
    tn = _lane_tile(n, 1024)
    return pl.pallas_call(
        _ada_kernel,
        grid=(n // tn,),
        in_specs=[pl.BlockSpec((r, d), lambda j: (0, 0)),
                  pl.BlockSpec((d, tn), lambda j: (0, j)),
                  pl.BlockSpec((1, tn), lambda j: (0, j))],
        out_specs=pl.BlockSpec((r, tn), lambda j: (0, j)),
        out_shape=jax.ShapeDtypeStruct((r, n), F32),
        compiler_params=_cparams("arbitrary"),
        name="ada",
    )(cond, w, b.reshape(1, n))


def _mod_index_map(mod0, mod_step, tiles_per_seq):
    def index_map(i):
        return (mod0 + (i // tiles_per_seq) * mod_step, 0, 0)
    return index_map


def _inproj_kernel(x_ref, mod_ref, g_ref, w_ref, wg_ref, *out_refs, offsets):
    x = x_ref[...]
    h = x * lax.rsqrt(jnp.mean(x * x, axis=-1, keepdims=True) + EPS) * g_ref[...]
    h = h * (1.0 + mod_ref[0, 1:2, :]) + mod_ref[0, 0:1, :]
    hb = h.astype(BF16)
    for o_ref, (lo, hi) in zip(out_refs[:-1], offsets):
        o_ref[...] = jnp.dot(hb, w_ref[:, lo:hi], preferred_element_type=F32)
    h_lo = (h - hb.astype(F32)).astype(BF16)
    g = jnp.dot(hb, wg_ref[...], preferred_element_type=F32)
    out_refs[-1][...] = (g[:, :LANES] + g[:, LANES:]
                         + jnp.dot(h_lo, wg_ref[:, :LANES], preferred_element_type=F32))


def _inproj(x, mods, norm_g, w_main, w_gate, seg_widths, mod_map, tm):
    t, d = x.shape
    offsets, lo = [], 0
    for wd in seg_widths:
        offsets.append((lo, lo + wd))
        lo += wd
    wg_hi = w_gate.astype(BF16)
    wg = jnp.concatenate([wg_hi, (w_gate - wg_hi.astype(F32)).astype(BF16)], axis=1)
    widths = tuple(seg_widths) + (LANES,)
    return pl.pallas_call(
        functools.partial(_inproj_kernel, offsets=tuple(offsets)),
        grid=(t // tm,),
        in_specs=[pl.BlockSpec((tm, d), lambda i: (i, 0)),
                  pl.BlockSpec((1, 6, d), mod_map),
                  pl.BlockSpec((1, d), lambda i: (0, 0)),
                  pl.BlockSpec(w_main.shape, lambda i: (0, 0)),
                  pl.BlockSpec(wg.shape, lambda i: (0, 0))],
        out_specs=[pl.BlockSpec((tm, wd), lambda i: (i, 0)) for wd in widths],
        out_shape=[jax.ShapeDtypeStruct((t, wd), F32) for wd in widths],
        compiler_params=_cparams("arbitrary"),
        name="inproj",
    )(x, mods, norm_g.reshape(1, d), w_main, wg)


def _short_conv_kernel(x_ref, w_ref, b_ref, o_ref, *, row_w, silu, transpose):
    x = x_ref[0]
    l = x.shape[0]
    pos = lax.broadcasted_iota(jnp.int32, x.shape, 0) % row_w
    prev = jnp.where(pos == 0, 0.0, pltpu.roll(x, 1, 0))
    nxt = jnp.where(pos == row_w - 1, 0.0, pltpu.roll(x, l - 1, 0))
    y = prev * w_ref[0:1, :] + x * w_ref[1:2, :] + nxt * w_ref[2:3, :] + b_ref[...]
    if silu:
        y = y * jax.nn.sigmoid(y)
    o_ref[0] = y.T if transpose else y


def _short_conv(x, w, b, row_w, silu, col0=0, ncols=None, transpose=False):
    bsz, l, c = x.shape
    ncols = c if ncols is None else ncols
    ct = _lane_tile(math.gcd(ncols, col0) if col0 else ncols, LANES if transpose else 512)
    j0 = col0 // ct
    out_shape, out_block, out_map = (bsz, l, ncols), (1, l, ct), lambda i, j: (i, 0, j)
    if transpose:
        out_shape, out_block, out_map = (bsz, ncols, l), (1, ct, l), lambda i, j: (i, j, 0)
    return pl.pallas_call(
        functools.partial(_short_conv_kernel, row_w=row_w, silu=silu, transpose=transpose),
        grid=(bsz, ncols // ct),
        in_specs=[pl.BlockSpec((1, l, ct), lambda i, j: (i, 0, j0 + j)),
                  pl.BlockSpec((3, ct), lambda i, j: (0, j0 + j)),
                  pl.BlockSpec((1, ct), lambda i, j: (0, j0 + j))],
        out_specs=pl.BlockSpec(out_block, out_map),
        out_shape=jax.ShapeDtypeStruct(out_shape, F32),
        compiler_params=_cparams("arbitrary", "arbitrary"),
        name="short_conv",
    )(x, w, b.reshape(1, c))


def _dft_direct_tables(l):
    n = 2 * l
    k = np.arange(n)[:, None].astype(np.float64)
    t = np.arange(n)[None, :].astype(np.float64)
    ang = 2.0 * np.pi * ((k * t) % n) / n
    cm, sm = np.cos(ang), np.sin(ang)
    fwd = np.block([[cm[:, :l], sm[:, :l]], [-sm[:, :l], cm[:, :l]]])
    filt = np.concatenate([cm, -sm], axis=0)
    return fwd, filt


def _dft_two_level_tables(l, n2):
    n = 2 * l
    n1 = n // n2
    k1 = np.arange(n1)[:, None].astype(np.float64)
    a = np.arange(n1)[None, :].astype(np.float64)
    ang1 = 2.0 * np.pi * ((k1 * a) % n1) / n1
    c1, s1 = np.cos(ang1), np.sin(ang1)
    h = n1 // 2
    m1 = np.block([[c1[:, :h], s1[:, :h]], [-s1[:, :h], c1[:, :h]]])
    m1f = np.concatenate([c1, -s1], axis=0)
    kk = (np.arange(n1)[:, None, None] + n1 * np.arange(n2)[None, :, None]).astype(np.float64)
    b = np.arange(n2)[None, None, :].astype(np.float64)
    ang = 2.0 * np.pi * ((kk * b) % n) / n
    cg, sg = np.cos(ang), np.sin(ang)
    gt = np.concatenate([np.concatenate([cg, sg], axis=2),
                         np.concatenate([-sg, cg], axis=2)], axis=1)
    return m1, m1f, gt


def _circular_lag(n0, rows, l):
    n = n0 + lax.broadcasted_iota(jnp.int32, (rows, 1), 0)
    t = jnp.where(n < l, n, 2 * l - n).astype(F32)
    return n, t, t / float(max(l - 1, 1))


def _filter_hidden_kernel(bandv_ref, w1_ref, b1_ref, w2_ref, b2_ref, freq_ref, o_ref, *, l):
    rows = o_ref.shape[0]
    _, t, t01 = _circular_lag(pl.program_id(0) * rows, rows, l)
    lane = lax.broadcasted_iota(jnp.int32, (rows, LANES), 1)
    ang = (2.0 * math.pi / l) * t * bandv_ref[...]
    feats = jnp.where(lane == 0, t01,
                      jnp.where(lane <= FILT_BANDS, jnp.cos(ang),
                                jnp.where(lane <= 2 * FILT_BANDS, -jnp.sin(ang), 0.0)))
    fr = freq_ref[...]
    h = jnp.sin(fr * (_hdot(feats, w1_ref[...]) + b1_ref[...]))
    o_ref[...] = jnp.sin(fr * (_hdot(h, w2_ref[...]) + b2_ref[...]))


def _filter_hidden(l, f_w1, f_b1, f_w2, f_b2, f_freq):
    emb, hid = f_w1.shape
    n = 2 * l
    rows = min(n, 512)
    bands = jnp.linspace(1e-4, FILT_BANDS - 1, FILT_BANDS, dtype=F32)
    bandv = jnp.zeros((1, LANES), F32).at[0, 1:1 + FILT_BANDS].set(bands)
    bandv = bandv.at[0, 1 + FILT_BANDS:1 + 2 * FILT_BANDS].set(bands)
    w1p = jnp.zeros((LANES, hid), F32).at[:emb].set(f_w1)
    c0 = lambda i: (0, 0)
    return pl.pallas_call(
        functools.partial(_filter_hidden_kernel, l=l),
        grid=(n // rows,),
        in_specs=[pl.BlockSpec((1, LANES), c0), pl.BlockSpec((LANES, hid), c0), pl.BlockSpec((1, hid), c0),
                  pl.BlockSpec((hid, hid), c0), pl.BlockSpec((1, hid), c0), pl.BlockSpec((1, hid), c0)],
        out_specs=pl.BlockSpec((rows, hid), lambda i: (i, 0)),
        out_shape=jax.ShapeDtypeStruct((n, hid), F32),
        compiler_params=_cparams("arbitrary"),
        name="filter_hidden",
    )(bandv, w1p, f_b1.reshape(1, hid), f_w2, f_b2.reshape(1, hid), f_freq.reshape(1, hid))


def _filter_rows(n0, rows, l, hid_ref, w3_ref, delta_ref):
    n, _, t01 = _circular_lag(n0, rows, l)
    h = hid_ref[pl.ds(n0, rows), :]
    hf = _bdot(h, w3_ref[0, 0])
    hb = _bdot(h, w3_ref[0, 1])
    window = jnp.exp(-t01 * delta_ref[...])
    return jnp.where(n < l, hf, jnp.where(n > l, hb, 0.0)) * window


def _filter_direct_kernel(hid_ref, w3_ref, delta_ref, ff_ref, h_ref, *, l):
    hc = _filter_rows(0, 2 * l, l, hid_ref, w3_ref, delta_ref)
    h_ref[0] = _bdot(ff_ref[...], hc) * (1.0 / (2 * l))


def _filter_two_level_kernel(hid_ref, w3_ref, delta_ref, m1f_ref, gt_ref, h_ref, hc_buf, a_buf, *, l, n2, rows):
    n = 2 * l
    n1 = n // n2

    def fill(i, carry):
        r0 = pl.multiple_of(i * rows, rows)
        hc_buf[pl.ds(r0, rows), :] = _filter_rows(r0, rows, l, hid_ref, w3_ref, delta_ref)
        return carry
    lax.fori_loop(0, n // rows, fill, 0)

    def step1(b, carry):
        col = hc_buf[pl.ds(b, n1, stride=n2), :]
        a = _bdot(m1f_ref[...], col)
        a_buf[pl.ds(b, n1, stride=2 * n2), :] = a[:n1]
        a_buf[pl.ds(n2 + b, n1, stride=2 * n2), :] = a[n1:]
        return carry
    lax.fori_loop(0, n2, step1, 0, unroll=8)

    def step2(k1, carry):
        r0 = pl.multiple_of(k1 * 2 * n2, 2 * n2)
        h_ref[0, k1] = (_bdot(gt_ref[k1], a_buf[pl.ds(r0, 2 * n2), :]) * (1.0 / n)).astype(h_ref.dtype)
        return carry
    lax.fori_loop(0, n1, step2, 0, unroll=4)


def _filter_tail_inputs(hy_w, f_w3):
    hid = f_w3.shape[0]
    w3 = f_w3.reshape(hid, HY_ORDER, N_DIR, hy_w).transpose(1, 2, 0, 3)
    max_decay = math.log(DECAY_TARGET) / FAST_DECAY_PCT
    min_decay = math.log(DECAY_TARGET) / SLOW_DECAY_PCT
    deltas = jnp.abs(jnp.linspace(min_decay, max_decay, hy_w, dtype=F32)).reshape(1, hy_w)
    return w3, deltas


def _filter_spectrum_direct(l, hy_w, filt_params, ff):
    f_w1, f_b1, f_w2, f_b2, f_w3, f_freq = filt_params
    hidden = _filter_hidden(l, f_w1, f_b1, f_w2, f_b2, f_freq)
    w3, deltas = _filter_tail_inputs(hy_w, f_w3)
    hid = f_w3.shape[0]
    n = 2 * l
    return pl.pallas_call(
        functools.partial(_filter_direct_kernel, l=l),
        grid=(HY_ORDER,),
        in_specs=[pl.BlockSpec((n, hid), lambda o: (0, 0)),
                  pl.BlockSpec((1, N_DIR, hid, hy_w), lambda o: (o, 0, 0, 0)),
                  pl.BlockSpec((1, hy_w), lambda o: (0, 0)),
                  pl.BlockSpec((2 * n, n), lambda o: (0, 0))],
        out_specs=pl.BlockSpec((1, 2 * n, hy_w), lambda o: (o, 0, 0)),
        out_shape=jax.ShapeDtypeStruct((HY_ORDER, 2 * n, hy_w), F32),
        compiler_params=_cparams("arbitrary"),
        name="filter_direct",
    )(hidden, w3, deltas, ff)


def _filter_spectrum_two_level(l, hy_w, filt_params, m1f, gt, ct):
    f_w1, f_b1, f_w2, f_b2, f_w3, f_freq = filt_params
    hidden = _filter_hidden(l, f_w1, f_b1, f_w2, f_b2, f_freq)
    w3, deltas = _filter_tail_inputs(hy_w, f_w3)
    hid = f_w3.shape[0]
    n = 2 * l
    n2 = FFT_N2
    n1 = n // n2
    return pl.pallas_call(
        functools.partial(_filter_two_level_kernel, l=l, n2=n2, rows=min(n, 512)),
        grid=(HY_ORDER, hy_w // ct),
        in_specs=[pl.BlockSpec((n, hid), lambda o, j: (0, 0)),
                  pl.BlockSpec((1, N_DIR, hid, ct), lambda o, j: (o, 0, 0, j)),
                  pl.BlockSpec((1, ct), lambda o, j: (0, j)),
                  pl.BlockSpec((2 * n1, n1), lambda o, j: (0, 0)),
                  pl.BlockSpec((n1, 2 * n2, 2 * n2), lambda o, j: (0, 0, 0))],
        out_specs=pl.BlockSpec((1, n1, 2 * n2, ct), lambda o, j: (o, 0, 0, j)),
        out_shape=jax.ShapeDtypeStruct((HY_ORDER, n1, 2 * n2, hy_w), BF16),
        scratch_shapes=[pltpu.VMEM((n, ct), F32), pltpu.VMEM((n1 * 2 * n2, ct), F32)],
        compiler_params=_cparams("arbitrary", "arbitrary"),
        name="filter_two_level",
    )(hidden, w3, deltas, m1f, gt)


def _complex_mul(x, h, half):
    xr, xi = x[:half], x[half:]
    hr, hi = h[:half], h[half:]
    return jnp.concatenate([xr * hr - xi * hi, xr * hi + xi * hr], axis=0)


def _conv_direct_kernel(z_ref, gate_ref, bias_ref, h_ref, fwd_ref, inv_ref, o_ref):
    l = z_ref.shape[1]
    z = jnp.concatenate([z_ref[0], z_ref[1]], axis=0)
    x = _bdot(fwd_ref[...], z)
    y = _bdot(inv_ref[...], _complex_mul(x, h_ref[0], 2 * l))
    bias = bias_ref[0]
    o_ref[0] = gate_ref[0] * (y[:l] + bias * z_ref[0])
    o_ref[1] = gate_ref[1] * (y[l:] + bias * z_ref[1])


def _conv_two_level_kernel(z_ref, gate_ref, bias_ref, h_ref, m1_ref, m1i_ref, gt_ref, o_ref,
                           a_buf, *, n2):
    l = z_ref.shape[1]
    n1 = 2 * l // n2
    hn = n1 // 2

    def step1(b, carry):
        za = z_ref[0, pl.ds(b, hn, stride=n2), :]
        zb = z_ref[1, pl.ds(b, hn, stride=n2), :]
        a = _bdot(m1_ref[...], jnp.concatenate([za, zb], axis=0))
        a_buf[pl.ds(b, n1, stride=2 * n2), :] = a[:n1]
        a_buf[pl.ds(n2 + b, n1, stride=2 * n2), :] = a[n1:]
        return carry
    lax.fori_loop(0, n2, step1, 0, unroll=8)

    def step2(k1, carry):
        r0 = pl.multiple_of(k1 * 2 * n2, 2 * n2)
        x = _bdot(gt_ref[k1], a_buf[pl.ds(r0, 2 * n2), :])
        y = _complex_mul(x, h_ref[0, k1].astype(F32), n2).astype(BF16)
        a_buf[pl.ds(r0, 2 * n2), :] = lax.dot_general(gt_ref[k1], y, (((0,), (0,)), ((), ())),
                                                      preferred_element_type=F32)
        return carry
    lax.fori_loop(0, n1, step2, 0, unroll=4)

    def step3(b, carry):
        br = a_buf[pl.ds(b, n1, stride=2 * n2), :]
        bi = a_buf[pl.ds(n2 + b, n1, stride=2 * n2), :]
        y = _bdot(m1i_ref[...], jnp.concatenate([br, bi], axis=0))
        o_ref[0, pl.ds(b, hn, stride=n2), :] = y[:hn]
        o_ref[1, pl.ds(b, hn, stride=n2), :] = y[hn:]
        return carry
    lax.fori_loop(0, n2, step3, 0, unroll=8)

    bias = bias_ref[0]
    for s in range(2):
        o_ref[s] = gate_ref[s] * (o_ref[s] + bias * z_ref[s])


def _long_conv_gated(u, z, z_col, gate_col, spectrum, order, bias, tables, ct):
    bsz, l, _ = u.shape
    c = spectrum.shape[-1]
    nct = c // ct
    zspec = pl.BlockSpec((2, l, ct), lambda i, j: (i, 0, z_col * nct + j))
    gspec = pl.BlockSpec((2, l, ct), lambda i, j: (i, 0, gate_col * nct + j))
    bspec = pl.BlockSpec((1, 1, ct), lambda i, j: (order, 0, j))
    ospec = pl.BlockSpec((2, l, ct), lambda i, j: (i, 0, j))
    out_shape = jax.ShapeDtypeStruct((bsz, l, c), F32)
    bias3 = bias.reshape(HY_ORDER, 1, c)
    if len(tables) == 2:
        fwd, inv = tables
        n = 2 * l
        return pl.pallas_call(
            _conv_direct_kernel,
            grid=(bsz // 2, nct),
            in_specs=[zspec, gspec, bspec,
                      pl.BlockSpec((1, 2 * n, ct), lambda i, j: (order, 0, j)),
                      pl.BlockSpec(fwd.shape, lambda i, j: (0, 0)),
                      pl.BlockSpec(inv.shape, lambda i, j: (0, 0))],
            out_specs=ospec, out_shape=out_shape,
            compiler_params=_cparams("arbitrary", "arbitrary"),
            name="long_conv_direct",
        )(z, u, bias3, spectrum, fwd, inv)
    m1, m1i, gt = tables
    n2 = FFT_N2
    n1 = 2 * l // n2
    const2 = lambda i, j: (0, 0)
    const3 = lambda i, j: (0, 0, 0)
    return pl.pallas_call(
        functools.partial(_conv_two_level_kernel, n2=n2),
        grid=(nct, bsz // 2),
        in_specs=[pl.BlockSpec((2, l, ct), lambda j, i: (i, 0, z_col * nct + j)),
                  pl.BlockSpec((2, l, ct), lambda j, i: (i, 0, gate_col * nct + j)),
                  pl.BlockSpec((1, 1, ct), lambda j, i: (order, 0, j)),
                  pl.BlockSpec((1, n1, 2 * n2, ct), lambda j, i: (order, 0, 0, j)),
                  pl.BlockSpec(m1.shape, const2), pl.BlockSpec(m1i.shape, const2),
                  pl.BlockSpec(gt.shape, const3)],
        out_specs=pl.BlockSpec((2, l, ct), lambda j, i: (i, 0, j)),
        out_shape=out_shape,
        scratch_shapes=[pltpu.VMEM((n1 * 2 * n2, ct), F32)],
        compiler_params=_cparams("arbitrary", "arbitrary"),
        name="long_conv_two_level",
    )(z, u, bias3, spectrum, m1, m1i, gt)


def _hyena(z_hy, conv_w, conv_b, filt_params, hy_bias, row_w):
    bsz, l, c3 = z_hy.shape
    c = c3 // 3
    u = _short_conv(z_hy, conv_w, conv_b, row_w, silu=False)
    table = lambda a: jnp.asarray(a, F32).astype(BF16)
    if 2 * l // FFT_N2 <= 4:
        fwd, filt = _dft_direct_tables(l)
        tables = (table(fwd), table(fwd.T))
        spectrum = _filter_spectrum_direct(l, c, filt_params, table(filt))
        ct = c
    else:
        m1, m1f, gt = _dft_two_level_tables(l, FFT_N2)
        tables = (table(m1), table(m1.T), table(gt))
        ct = LANES
        spectrum = _filter_spectrum_two_level(l, c, filt_params, table(m1f), tables[2], ct)
    z1 = _long_conv_gated(u, u, 0, 1, spectrum, 0, hy_bias, tables, ct)
    return _long_conv_gated(u, z1, 0, 2, spectrum, 1, hy_bias, tables, ct)


def _log_sigmoid(x):
    return jnp.minimum(x, 0.0) - jnp.log1p(jnp.exp(-jnp.abs(x)))


def _split3_dot(a, b, split_lhs):
    x = a if split_lhs else b
    hi = x.astype(BF16)
    rest = x - hi.astype(F32)
    mid = rest.astype(BF16)
    parts = (hi, mid, (rest - mid.astype(F32)).astype(BF16))
    if split_lhs:
        return functools.reduce(lambda u, w: u + w, [jnp.dot(p, b, preferred_element_type=F32) for p in parts])
    return functools.reduce(lambda u, w: u + w, [jnp.dot(a, p, preferred_element_type=F32) for p in parts])


def _mlstm_recur(qb, sv, m_loc, kv, g_loc, bq, btot, cn, m):
    dh = qb.shape[1]
    inter = bq + m
    mj = jnp.maximum(m_loc, inter)
    w_int = jnp.exp(inter - mj)
    w_loc = jnp.exp(m_loc - mj)
    qc = jnp.dot(qb, cn.astype(BF16), preferred_element_type=F32)
    num = w_int * qc[:, :dh] + w_loc * sv[:, :dh]
    den = w_int * qc[:, dh:] + w_loc * sv[:, dh:]
    h = num / jnp.maximum(jnp.abs(den), jnp.exp(-mj))
    m_new = jnp.maximum(btot + m, g_loc)
    cn_new = jnp.exp(btot + m - m_new) * cn + jnp.exp(g_loc - m_new) * kv
    return h, cn_new, m_new


def _mlstm_kernel(q_ref, cw_ref, cb_ref, kt_ref, v_ref, o_ref, g_ref, gt_ref, gb_ref, gbt_ref, ng_ref,
                  c0_ref, n0_ref, m0_ref, y_ref, c_ref, n_ref, m_ref,
                  hf_buf, hb_buf, cn_buf, sv_buf, kv_buf, bq_buf, ml_buf, sc_buf, qb_buf,
                  *, chunk, heads, row_w):
    l, dh = q_ref.shape[1], q_ref.shape[2]
    nc = l // chunk
    group = sv_buf.shape[1]
    pair = math.gcd(group, MLSTM_PAIR)
    scale = dh ** -0.5

    def conv_q(j, carry):
        r0 = pl.multiple_of(j * chunk, chunk)
        x = q_ref[0, pl.ds(r0, chunk), :]
        idx = lax.broadcasted_iota(jnp.int32, x.shape, 0)
        pos = (r0 + idx) & (row_w - 1)
        before = q_ref[0, pl.ds(jnp.maximum(r0 - 1, 0), 1), :]
        after = q_ref[0, pl.ds(jnp.minimum(r0 + chunk, l - 1), 1), :]
        prev = jnp.where(idx == 0, before, pltpu.roll(x, 1, 0))
        nxt = jnp.where(idx == chunk - 1, after, pltpu.roll(x, chunk - 1, 0))
        y = (jnp.where(pos == 0, 0.0, prev) * cw_ref[0:1, :] + x * cw_ref[1:2, :]
             + jnp.where(pos == row_w - 1, 0.0, nxt) * cw_ref[2:3, :] + cb_ref[...])
        qb_buf[pl.ds(r0, chunk), :] = (y * jax.nn.sigmoid(y)).astype(BF16)
        return carry
    lax.fori_loop(0, nc, conv_q, 0)
    row = lax.broadcasted_iota(jnp.int32, (chunk, chunk), 0)
    col = lax.broadcasted_iota(jnp.int32, (chunk, chunk), 1)
    lower, upper = col <= row, col >= row
    tri_l, tri_u = lower.astype(BF16), upper.astype(BF16)
    gate_row = lax.broadcasted_iota(jnp.int32, (SUBLANES, chunk), 0)
    ones = jnp.ones((chunk, dh), BF16)

    sel_row = lax.broadcasted_iota(jnp.int32, (LANES, LANES), 0)
    head = pl.program_id(1)

    def chunk_start(j, d):
        return pl.multiple_of(((nc - 1 - j) if d else j) * chunk, chunk)

    def local(gp, carry, j0):
        jobs = [(gp * pair + k, d) for k in range(pair) for d in range(N_DIR)]
        r0s = [chunk_start(j0 + g, d) for g, d in jobs]
        picks = [(sel_row == (2 * d + 1) * heads + head).astype(BF16) for d in range(N_DIR)]
        lfs = [_log_sigmoid(g_ref[0, pl.ds(r0, chunk), :] + gb_ref[...]) for r0 in r0s]
        gts = [gt_ref[0, 0, :, pl.ds(r0, chunk)] + gbt_ref[0] for r0 in r0s]
        gts = [jnp.where(gate_row % 2 == 1, _log_sigmoid(gt), gt) for gt in gts]
        lfs = [_split3_dot(lf, picks[d], split_lhs=True) for lf, (_, d) in zip(lfs, jobs)]
        brows = [_split3_dot(gt, tri_l if d else tri_u, split_lhs=True)[2 * d + 1:2 * d + 2, :]
                 for gt, (_, d) in zip(gts, jobs)]
        bqs = [_split3_dot(tri_u if d else tri_l, lf, split_lhs=False) for lf, (_, d) in zip(lfs, jobs)]
        btots = [brow[:, 0:1] if d else brow[:, chunk - 1:chunk] for brow, (_, d) in zip(brows, jobs)]
        irows = [gt[2 * d:2 * d + 1, :] for gt, (_, d) in zip(gts, jobs)]
        qbs = [qb_buf[pl.ds(r0, chunk), :] for r0 in r0s]
        kts = [kt_ref[0, :, pl.ds(r0, chunk)] * scale for r0 in r0s]
        vos = [jnp.concatenate([v_ref[0, pl.ds(r0, chunk), :].astype(BF16), ones], axis=1) for r0 in r0s]
        qks = [jnp.dot(qb, kt.astype(BF16), preferred_element_type=F32) for qb, kt in zip(qbs, kts)]
        dms = [jnp.where(upper if d else lower, bq - brow + irow, -jnp.inf)
               for bq, brow, irow, (_, d) in zip(bqs, brows, irows, jobs)]
        m_locs = [jnp.max(dm, axis=-1, keepdims=True) for dm in dms]
        ss = [(qk * jnp.exp(dm - m_loc)).astype(BF16) for qk, dm, m_loc in zip(qks, dms, m_locs)]
        gls = [btot - brow + irow for btot, brow, irow in zip(btots, brows, irows)]
        g_locs = [jnp.max(gl, axis=-1, keepdims=True) for gl in gls]
        wks = [(kt * jnp.exp(gl - g_loc)).astype(BF16) for kt, gl, g_loc in zip(kts, gls, g_locs)]
        svs = [jnp.dot(s, vo, preferred_element_type=F32) for s, vo in zip(ss, vos)]
        kvs = [jnp.dot(wk, vo, preferred_element_type=F32) for wk, vo in zip(wks, vos)]
        for (g, d), sv, kv, bq, m_loc, btot, g_loc in zip(jobs, svs, kvs, bqs, m_locs, btots, g_locs):
            sv_buf[d, g] = sv
            kv_buf[d, g] = kv
            bq_buf[d, g] = bq
            ml_buf[d, g] = jnp.broadcast_to(m_loc, (chunk, LANES))
            sc_buf[d, g, 0:1, :] = jnp.broadcast_to(btot, (1, LANES))
            sc_buf[d, g, 1:2, :] = jnp.broadcast_to(g_loc, (1, LANES))
        return carry

    def recur(g, carry, j0):
        ms = list(carry)
        for d in range(N_DIR):
            r0 = chunk_start(j0 + g, d)
            h, cn, ms[d] = _mlstm_recur(
                qb_buf[pl.ds(r0, chunk), :], sv_buf[d, g], ml_buf[d, g], kv_buf[d, g],
                sc_buf[d, g, 1:2, 0:1], bq_buf[d, g], sc_buf[d, g, 0:1, 0:1], cn_buf[d], ms[d])
            cn_buf[d] = cn
            (hb_buf if d else hf_buf)[pl.ds(r0, chunk), :] = h
        return tuple(ms)

    for d in range(N_DIR):
        n_rep = jnp.broadcast_to(n0_ref[0, 0, d:d + 1, :], (dh, dh)).T
        cn_buf[d] = jnp.concatenate([c0_ref[0, d, 0], n_rep], axis=1)

    def block(jb, carry):
        j0 = jb * group
        lax.fori_loop(0, group // pair, functools.partial(local, j0=j0), 0)
        return lax.fori_loop(0, group, functools.partial(recur, j0=j0), carry, unroll=2)

    m_fin = lax.fori_loop(0, nc // group, block, (m0_ref[0, 0, 0:1, 0:1], m0_ref[0, 0, 1:2, 0:1]))
    for d in range(N_DIR):
        c_ref[0, d, 0] = cn_buf[d, :, :dh]
        n_ref[0, 0, d:d + 1, :] = cn_buf[d, :, dh:].T[0:1, :]
        m_ref[0, 0, d:d + 1, :] = jnp.broadcast_to(m_fin[d], (1, LANES))

    def finish(j, carry):
        r0 = pl.multiple_of(j * chunk, chunk)
        hs = hf_buf[pl.ds(r0, chunk), :] + hb_buf[pl.ds(r0, chunk), :]
        hs = hs * lax.rsqrt(jnp.mean(hs * hs, axis=-1, keepdims=True) + EPS) * ng_ref[...]
        y_ref[0, pl.ds(r0, chunk), :] = jax.nn.sigmoid(o_ref[0, pl.ds(r0, chunk), :]) * hs
        return carry
    lax.fori_loop(0, nc, finish, 0)


def _mlstm(z_qk, conv_w, conv_b, row_w, kt, z_v, z_o, z_g, gate_b, norm_g, c0, n0, m0):
    bsz, l, ml = z_v.shape
    heads = gate_b.shape[-1]
    dh = ml // heads
    chunk = MLSTM_CHUNK
    assert dh == LANES and chunk == LANES and l % chunk == 0
    assert row_w & (row_w - 1) == 0 and l % row_w == 0
    group = math.gcd(l // chunk, MLSTM_GROUP)
    n_gate = N_DIR * 2
    assert z_g.shape[-1] == LANES
    g4 = z_g[..., :n_gate * heads].reshape(bsz, l, n_gate, heads)
    ght = jnp.pad(g4.transpose(0, 3, 2, 1), ((0, 0), (0, 0), (0, SUBLANES - n_gate), (0, 0)))
    gb4 = gate_b.reshape(n_gate, heads).T
    gb = jnp.pad(gate_b.reshape(1, n_gate * heads), ((0, 0), (0, LANES - n_gate * heads)))
    gbt = jnp.broadcast_to(jnp.pad(gb4, ((0, 0), (0, SUBLANES - n_gate)))[..., None], (heads, SUBLANES, LANES))
    n0h = n0.transpose(0, 2, 1, 3)
    m0h = jnp.broadcast_to(m0.transpose(0, 2, 1)[..., None], (bsz, heads, N_DIR, LANES))
    seq = pl.BlockSpec((1, l, dh), lambda b, h: (b, 0, h))
    cspec = pl.BlockSpec((1, N_DIR, 1, dh, dh), lambda b, h: (b, 0, h, 0, 0))
    sspec = pl.BlockSpec((1, 1, N_DIR, dh), lambda b, h: (b, h, 0, 0))
    mspec = pl.BlockSpec((1, 1, N_DIR, LANES), lambda b, h: (b, h, 0, 0))
    y, c, n, m = pl.pallas_call(
        functools.partial(_mlstm_kernel, chunk=chunk, heads=heads, row_w=row_w),
        grid=(bsz, heads),
        in_specs=[seq, pl.BlockSpec((3, dh), lambda b, h: (0, h)), pl.BlockSpec((1, dh), lambda b, h: (0, h)),
                  pl.BlockSpec((1, dh, l), lambda b, h: (b, h, 0)), seq, seq,
                  pl.BlockSpec((1, l, LANES), lambda b, h: (b, 0, 0)),
                  pl.BlockSpec((1, 1, SUBLANES, l), lambda b, h: (b, h, 0, 0)),
                  pl.BlockSpec((1, LANES), lambda b, h: (0, 0)),
                  pl.BlockSpec((1, SUBLANES, LANES), lambda b, h: (h, 0, 0)),
                  pl.BlockSpec((1, dh), lambda b, h: (0, h)),
                  cspec, sspec, mspec],
        out_specs=[seq, cspec, sspec, mspec],
        out_shape=[jax.ShapeDtypeStruct((bsz, l, ml), F32),
                   jax.ShapeDtypeStruct((bsz, N_DIR, heads, dh, dh), F32),
                   jax.ShapeDtypeStruct((bsz, heads, N_DIR, dh), F32),
                   jax.ShapeDtypeStruct((bsz, heads, N_DIR, LANES), F32)],
        scratch_shapes=[pltpu.VMEM((l, dh), F32), pltpu.VMEM((l, dh), F32),
                        pltpu.VMEM((N_DIR, dh, 2 * dh), F32),
                        pltpu.VMEM((N_DIR, group, chunk, 2 * dh), F32),
                        pltpu.VMEM((N_DIR, group, dh, 2 * dh), F32),
                        pltpu.VMEM((N_DIR, group, chunk, LANES), F32),
                        pltpu.VMEM((N_DIR, group, chunk, LANES), F32),
                        pltpu.VMEM((N_DIR, group, SUBLANES, LANES), F32),
                        pltpu.VMEM((l, dh), BF16)],
        compiler_params=_cparams("arbitrary", "arbitrary"),
        name="mlstm",
    )(z_qk, conv_w, conv_b.reshape(1, -1), kt, z_v, z_o, z_g, ght, gb, gbt, norm_g.reshape(1, ml), c0, n0h, m0h)
    return y, (c, n.transpose(0, 2, 1, 3), m[..., 0].transpose(0, 2, 1))


def _rows_to_tiles(tile_ref, x):
    r, d = x.shape
    s = d // LANES
    for k in range(s):
        tile_ref[pl.ds(k, r, stride=s), :] = x[:, LANES * k:LANES * (k + 1)]


def _tiles_to_rows(tile_ref, r, s, r0=0):
    return jnp.concatenate([tile_ref[pl.ds(r0 * s + k, r, stride=s), :] for k in range(s)], axis=1)


def _lane_pack(cols, lane):
    out = jnp.zeros(lane.shape, cols[0].dtype)
    for j, colv in enumerate(cols):
        out = jnp.where(lane == j, colv, out)
    return out


def _post_kernel(yhy_ref, yml_ref, x_ref, mod_ref, wo_ref, n2g_ref, rw_ref, rb_ref, cnt0_ref,
                 x1_ref, h2_ref, ti_ref, tg_ref, rk_ref, cnt_ref, carry, *, n_exp, top_k):
    @pl.when(pl.program_id(0) == 0)
    def _():
        carry[...] = cnt0_ref[0:1, :]

    hy_w = yhy_ref.shape[1]
    tm = x_ref.shape[0]
    proj = _bdot(yhy_ref[...], wo_ref[:hy_w, :]) + _bdot(yml_ref[...], wo_ref[hy_w:, :])
    x1 = x_ref[...] + mod_ref[0, 2:3, :] * proj
    x1_ref[...] = x1
    h2 = x1 * lax.rsqrt(jnp.mean(x1 * x1, axis=-1, keepdims=True) + EPS) * n2g_ref[...]
    h2 = h2 * (1.0 + mod_ref[0, 4:5, :]) + mod_ref[0, 3:4, :]
    _rows_to_tiles(h2_ref, h2)

    lane = lax.broadcasted_iota(jnp.int32, (tm, LANES), 1)
    work = jnp.where(lane < n_exp, _dot3(h2, rw_ref[...]) + rb_ref[...], -jnp.inf)
    vals, idxs, hots = [], [], []
    for _ in range(top_k):
        mx = jnp.max(work, axis=-1, keepdims=True)
        idx = jnp.min(jnp.where(work == mx, lane, LANES), axis=-1, keepdims=True)
        hot = lane == idx
        vals.append(mx)
        idxs.append(idx)
        hots.append(hot)
        work = jnp.where(hot, -jnp.inf, work)
    exps = [jnp.exp(v - vals[0]) for v in vals]
    tot = functools.reduce(lambda a, b: a + b, exps)
    ti_ref[...] = _lane_pack(idxs, lane)
    tg_ref[...] = _lane_pack([e / tot for e in exps], lane)

    hot_sum = functools.reduce(lambda a, b: a + b, [h.astype(F32) for h in hots])
    row = lax.broadcasted_iota(jnp.int32, (tm, tm), 0)
    col = lax.broadcasted_iota(jnp.int32, (tm, tm), 1)
    before = _bdot((col < row).astype(F32), hot_sum) + carry[...]
    ranks = [jnp.sum(jnp.where(h, before, 0.0), axis=-1, keepdims=True).astype(jnp.int32) for h in hots]
    rk_ref[...] = _lane_pack(ranks, lane)
    carry[...] = carry[...] + jnp.sum(hot_sum, axis=0, keepdims=True)
    cnt_ref[...] = jnp.broadcast_to(carry[...], cnt_ref.shape)


def _post(y_hy, y_ml, x, mods, w_out, norm2_g, r_w, r_b, counts, mod_map, tm):
    t, d = x.shape
    n_exp = r_w.shape[1]
    rwp = jnp.pad(r_w, ((0, 0), (0, LANES - n_exp)))
    rbp = jnp.pad(r_b, (0, LANES - n_exp)).reshape(1, LANES)
    row = lambda wd: pl.BlockSpec((tm, wd), lambda i: (i, 0))
    const = lambda shape: pl.BlockSpec(shape, lambda i: (0, 0))
    return pl.pallas_call(
        functools.partial(_post_kernel, n_exp=n_exp, top_k=TOP_K),
        grid=(t // tm,),
        in_specs=[row(y_hy.shape[1]), row(y_ml.shape[1]), row(d),
                  pl.BlockSpec((1, 6, d), mod_map),
                  const(w_out.shape), const((1, d)), const((d, LANES)), const((1, LANES)),
                  const((SUBLANES, LANES))],
        out_specs=[row(d), pl.BlockSpec((tm * d // LANES, LANES), lambda i: (i, 0)),
                   row(LANES), row(LANES), row(LANES), const((SUBLANES, LANES))],
        out_shape=[jax.ShapeDtypeStruct((t, d), F32), jax.ShapeDtypeStruct((t * d // LANES, LANES), F32),
                   jax.ShapeDtypeStruct((t, LANES), jnp.int32), jax.ShapeDtypeStruct((t, LANES), F32),
                   jax.ShapeDtypeStruct((t, LANES), jnp.int32), jax.ShapeDtypeStruct((SUBLANES, LANES), F32)],
        scratch_shapes=[pltpu.VMEM((1, LANES), F32)],
        compiler_params=_cparams("arbitrary"),
        name="post",
    )(y_hy, y_ml, x, mods, w_out.astype(BF16), norm2_g.reshape(1, d), rwp, rbp, counts)


def _dest_kernel(ti_ref, rk_ref, ps_ref, d_ref, *, top_k):
    lane = lax.broadcasted_iota(jnp.int32, ti_ref.shape, 1)
    ti, rk = ti_ref[...], rk_ref[...]
    cols = []
    for j in range(top_k):
        start = jnp.sum(jnp.where(lane == ti[:, j:j + 1], ps_ref[...], 0.0), axis=-1, keepdims=True)
        cols.append(start.astype(jnp.int32) + rk[:, j:j + 1])
    d_ref[...] = _lane_pack(cols, lane)


def _dest_rows(ti, rk, pad_start, tm):
    t = ti.shape[0]
    row = pl.BlockSpec((tm, LANES), lambda i: (i, 0))
    return pl.pallas_call(
        functools.partial(_dest_kernel, top_k=TOP_K),
        grid=(t // tm,),
        in_specs=[row, row, pl.BlockSpec((1, LANES), lambda i: (0, 0))],
        out_specs=row,
        out_shape=jax.ShapeDtypeStruct((t, LANES), jnp.int32),
        compiler_params=_cparams("arbitrary"),
        name="dest_rows",
    )(ti, rk, pad_start)


def _dispatch_kernel(first_ref, count_ref, nu_ref, dest_ref, *refs, top_k, s, rows, tiles):
    h_refs = refs[:len(tiles)]
    xs_ref, zbuf, sem, zsem = refs[len(tiles):]
    i = pl.program_id(0)
    tt = dest_ref.shape[2] // top_k
    half = zbuf.shape[0] // s
    n_blocks = xs_ref.shape[0] // (rows * s)

    def pad_rows(e, carry, wait):
        off, n = first_ref[e], count_ref[e]
        for k in range(half.bit_length()):
            bit = half >> k

            @pl.when((n & bit) != 0)
            def _():
                dst = pl.multiple_of(off * s, s)
                copy = pltpu.make_async_copy(zbuf.at[pl.ds(0, bit * s)], xs_ref.at[pl.ds(dst, bit * s)], zsem)
                copy.wait() if wait else copy.start()
            off = off + (n & bit)
        return carry

    def spare_block(b, carry, wait):
        for part in range(2):
            dst = pl.multiple_of((b * 2 + part) * half * s, half * s)
            copy = pltpu.make_async_copy(zbuf, xs_ref.at[pl.ds(dst, half * s)], zsem)
            copy.wait() if wait else copy.start()
        return carry

    @pl.when(i == 0)
    def _():
        zbuf[...] = jnp.zeros_like(zbuf)
        for wait in (False, True):
            lax.fori_loop(0, first_ref.shape[0], functools.partial(pad_rows, wait=wait), 0)
            lax.fori_loop(nu_ref[0], n_blocks, functools.partial(spare_block, wait=wait), 0)

    def scatter(h_ref):
        def issue(t, carry):
            src = pl.multiple_of(t * s, s)
            for j in range(top_k):
                dst = pl.multiple_of(dest_ref[0, 0, t * top_k + j] * s, s)
                pltpu.make_async_copy(h_ref.at[pl.ds(src, s)], xs_ref.at[pl.ds(dst, s)], sem).start(priority=j % 2)
            return carry
        lax.fori_loop(0, tt, issue, 0, unroll=2)
        for _ in range(top_k):
            pltpu.make_async_copy(h_ref, xs_ref.at[pl.ds(0, tt * s)], sem).wait()

    lo = 0
    for h_ref, n_tiles in zip(h_refs, tiles):
        @pl.when((i >= lo) & (i < lo + n_tiles))
        def _():
            scatter(h_ref)
        lo += n_tiles


def _dispatch(dest, h2ts, pad_first, pad_count, n_used, n_blocks, rows, tt):
    s = sum(h.shape[0] for h in h2ts) * TOP_K // dest.size
    tiles = tuple(h.shape[0] // (tt * s) for h in h2ts)
    starts = [sum(tiles[:k]) for k in range(len(tiles))]
    hspec = lambda lo, n: pl.BlockSpec((tt * s, LANES), lambda i, *_: (jnp.clip(i - lo, 0, n - 1), 0))
    return pl.pallas_call(
        functools.partial(_dispatch_kernel, top_k=TOP_K, s=s, rows=rows, tiles=tiles),
        grid_spec=pltpu.PrefetchScalarGridSpec(
            num_scalar_prefetch=3,
            grid=(dest.shape[0],),
            in_specs=[pl.BlockSpec((1, 1, tt * TOP_K), lambda i, *_: (i, 0, 0), memory_space=pltpu.SMEM)]
            + [hspec(lo, n) for lo, n in zip(starts, tiles)],
            out_specs=pl.BlockSpec(memory_space=pl.ANY),
            scratch_shapes=[pltpu.VMEM((rows // 2 * s, LANES), F32),
                            pltpu.SemaphoreType.DMA(()), pltpu.SemaphoreType.DMA(())]),
        out_shape=jax.ShapeDtypeStruct((n_blocks * rows * s, LANES), F32),
        compiler_params=_cparams("arbitrary"),
        name="dispatch",
    )(pad_first, pad_count, n_used, dest, *h2ts)


def _deinterleave_table():
    p = np.zeros((2 * LANES, 2 * LANES), np.float32)
    j = np.arange(LANES)
    p[2 * j, j] = 1.0
    p[2 * j + 1, LANES + j] = 1.0
    return p


def _ffn_kernel(be_ref, nu_ref, xs_ref, wgu_ref, bg_ref, bl_ref, wd_ref, bd_ref, perm_ref, ys_ref,
                wg_buf, wl_buf, wd_buf, acc_buf, *, ft):
    i = pl.program_id(0)
    f = wg_buf.shape[1]
    rows = acc_buf.shape[0]
    live = i < nu_ref[0]

    @pl.when(live & ((i == 0) | (be_ref[i] != be_ref[jnp.maximum(i - 1, 0)])))
    def _():
        for j in range(f // LANES):
            blk = wgu_ref[0, :, 2 * LANES * j:2 * LANES * (j + 1)].astype(BF16)
            split = jnp.dot(blk, perm_ref[...], preferred_element_type=F32)
            wg_buf[:, LANES * j:LANES * (j + 1)] = split[:, :LANES].astype(BF16)
            wl_buf[:, LANES * j:LANES * (j + 1)] = split[:, LANES:].astype(BF16)
        wd_buf[...] = wd_ref[0].astype(BF16)

    @pl.when(live)
    def _():
        x = _tiles_to_rows(xs_ref, rows, xs_ref.shape[0] // rows).astype(BF16)
        for j, f0 in enumerate(range(0, f, ft)):
            g = jnp.dot(x, wg_buf[:, f0:f0 + ft], preferred_element_type=F32) + bg_ref[0, :, f0:f0 + ft]
            lin = jnp.dot(x, wl_buf[:, f0:f0 + ft], preferred_element_type=F32) + bl_ref[0, :, f0:f0 + ft]
            gate = jnp.minimum(g, SWIGLU_LIMIT)
            lin = jnp.clip(lin, -SWIGLU_LIMIT, SWIGLU_LIMIT)
            act = (lin + 1.0) * gate * jax.nn.sigmoid(SWIGLU_ALPHA * gate)
            part = jnp.dot(act.astype(BF16), wd_buf[f0:f0 + ft, :], preferred_element_type=F32)
            if j == 0:
                acc_buf[...] = part + bd_ref[0]
            else:
                acc_buf[...] += part
        _rows_to_tiles(ys_ref, acc_buf[...])

    @pl.when(jnp.logical_not(live))
    def _():
        ys_ref[...] = jnp.zeros_like(ys_ref)


def _ffn(block_e, n_used, xs, w_gu, b_gu, w_d, b_d, rows):
    n_exp, d, f2 = w_gu.shape
    f = f2 // 2
    s = d // LANES
    n_rows = xs.shape[0] // s
    live = lambda i, nu: jnp.minimum(i, nu[0] - 1)
    wmap = lambda i, be, nu: (be[live(i, nu)], 0, 0)
    perm = jnp.asarray(_deinterleave_table(), BF16)
    return pl.pallas_call(
        functools.partial(_ffn_kernel, ft=min(f, 1024)),
        grid_spec=pltpu.PrefetchScalarGridSpec(
            num_scalar_prefetch=2,
            grid=(n_rows // rows,),
            in_specs=[pl.BlockSpec((rows * s, LANES), lambda i, be, nu: (live(i, nu), 0)),
                      pl.BlockSpec((1, d, f2), wmap),
                      pl.BlockSpec((1, 1, f), wmap), pl.BlockSpec((1, 1, f), wmap),
                      pl.BlockSpec((1, f, d), wmap), pl.BlockSpec((1, 1, d), wmap),
                      pl.BlockSpec(perm.shape, lambda i, be, nu: (0, 0))],
            out_specs=pl.BlockSpec((rows * s, LANES), lambda i, be, nu: (i, 0)),
            scratch_shapes=[pltpu.VMEM((d, f), BF16), pltpu.VMEM((d, f), BF16), pltpu.VMEM((f, d), BF16),
                            pltpu.VMEM((rows, d), F32)]),
        out_shape=jax.ShapeDtypeStruct(xs.shape, F32),
        compiler_params=_cparams("arbitrary"),
        name="expert_ffn",
    )(block_e, n_used, xs, w_gu, b_gu[:, 0::2].reshape(n_exp, 1, f), b_gu[:, 1::2].reshape(n_exp, 1, f),
      w_d, b_d.reshape(n_exp, 1, d), perm)


def _combine_kernel(dest_ref, next_ref, tg_ref, x1_ref, mod_ref, fg_ref, ys_ref, o_ref, ybuf, sem, *,
                    top_k, final_norm):
    i = pl.program_id(0)
    tt = x1_ref.shape[0]
    s = ybuf.shape[2] // tt
    slot = i % 2

    def gather(d_ref, to):
        def issue(t, carry):
            dst = pl.multiple_of(t * s, s)
            for j in range(top_k):
                src = pl.multiple_of(d_ref[0, 0, t * top_k + j] * s, s)
                pltpu.make_async_copy(ys_ref.at[pl.ds(src, s)], ybuf.at[to, j, pl.ds(dst, s)],
                                      sem.at[to]).start(priority=j % 2)
            return carry
        lax.fori_loop(0, tt, issue, 0, unroll=2)

    @pl.when(i == 0)
    def _():
        gather(dest_ref, 0)

    @pl.when(i + 1 < pl.num_programs(0))
    def _():
        gather(next_ref, 1 - slot)

    for j in range(top_k):
        pltpu.make_async_copy(ys_ref.at[pl.ds(0, tt * s)], ybuf.at[slot, j], sem.at[slot]).wait()
    rc = math.gcd(tt, 32)

    def rows_chunk(c, carry):
        r0 = pl.multiple_of(c * rc, rc)
        tg = tg_ref[pl.ds(r0, rc), :]
        moe = tg[:, 0:1] * _tiles_to_rows(ybuf.at[slot, 0], rc, s, r0)
        for j in range(1, top_k):
            moe = moe + tg[:, j:j + 1] * _tiles_to_rows(ybuf.at[slot, j], rc, s, r0)
        x2 = x1_ref[pl.ds(r0, rc), :] + mod_ref[0, 5:6, :] * moe
        if final_norm:
            x2 = x2 * lax.rsqrt(jnp.mean(x2 * x2, axis=-1, keepdims=True) + EPS) * fg_ref[...]
        o_ref[pl.ds(r0, rc), :] = x2
        return carry
    lax.fori_loop(0, tt // rc, rows_chunk, 0)


def _combine(dest, tg, x1, mods, final_g, ys, mod_map, tt, final_norm):
    t, d = x1.shape
    steps = t // tt
    return pl.pallas_call(
        functools.partial(_combine_kernel, top_k=TOP_K, final_norm=final_norm),
        grid=(steps,),
        in_specs=[pl.BlockSpec((1, 1, tt * TOP_K), lambda i: (i, 0, 0), memory_space=pltpu.SMEM),
                  pl.BlockSpec((1, 1, tt * TOP_K), lambda i: (jnp.minimum(i + 1, steps - 1), 0, 0),
                               memory_space=pltpu.SMEM),
                  pl.BlockSpec((tt, LANES), lambda i: (i, 0)),
                  pl.BlockSpec((tt, d), lambda i: (i, 0)),
                  pl.BlockSpec((1, 6, d), mod_map),
                  pl.BlockSpec((1, d), lambda i: (0, 0)),
                  pl.BlockSpec(memory_space=pl.ANY)],
        out_specs=pl.BlockSpec((tt, d), lambda i: (i, 0)),
        out_shape=jax.ShapeDtypeStruct((t, d), F32),
        scratch_shapes=[pltpu.VMEM((2, TOP_K, tt * d // LANES, LANES), F32), pltpu.SemaphoreType.DMA((2,))],
        compiler_params=_cparams("arbitrary"),
        name="combine",
    )(dest, dest, tg, x1, mods, final_g.reshape(1, d), ys)


def _moe_plan(counts, rows, n_blocks):
    n_exp = counts.shape[0]
    padded = (counts + rows - 1) // rows * rows
    pad_end = jnp.cumsum(padded)
    block_row = jnp.arange(n_blocks, dtype=jnp.int32) * rows
    block_e = jnp.minimum(jnp.sum(pad_end[None, :] <= block_row[:, None], axis=1), n_exp - 1).astype(jnp.int32)
    n_used = (pad_end[-1:] // rows).astype(jnp.int32)
    start = pad_end - padded
    pad_start = jnp.pad(start.astype(F32), (0, LANES - n_exp)).reshape(1, LANES)
    return pad_start, block_e, n_used, (start + counts).astype(jnp.int32), (padded - counts).astype(jnp.int32)


def _sequence_mixers(z_hy, z_qk, z_v, z_o, z_g, lw, state, row_w):
    (hy_cw, hy_cb, filt_params, hy_b, ml_cw, ml_cb, ml_gb, ml_ng) = lw
    y_hy = _hyena(z_hy, hy_cw, hy_cb, filt_params, hy_b, row_w)
    ml_w = z_v.shape[-1]
    kt = _short_conv(z_qk, ml_cw, ml_cb, row_w, silu=True, col0=ml_w, ncols=ml_w, transpose=True)
    y_ml, st = _mlstm(z_qk, ml_cw, ml_cb, row_w, kt, z_v, z_o, z_g, ml_gb, ml_ng, *state)
    return y_hy, y_ml, st


def kernel(x_prompt, x_sample, state_mlstm_C, state_mlstm_n, state_mlstm_m, c, c_ctx, ada_w, ada_b, norm1_g,
           w_in, hy_conv_w, hy_conv_b, filt_w1, filt_b1, filt_w2, filt_b2, filt_w3, filt_freq, hy_bias,
           ml_conv_w, ml_conv_b, ml_gate_b, ml_norm_g, w_out, norm2_g, router_w, router_b, moe_w_gu,
           moe_b_gu, moe_w_down, moe_b_down, final_g):
    bp, lp, d = x_prompt.shape
    bs, ls, _ = x_sample.shape
    depth = ada_w.shape[0]
    heads = ml_gate_b.shape[-1]
    hy_w = hy_bias.shape[-1]
    ml_w = ml_norm_g.shape[-1]
    dh = ml_w // heads
    n_exp = router_w.shape[-1]
    t = bp * lp + bs * ls
    ng = N_DIR * 2 * heads
    seg_widths = (3 * hy_w, 2 * ml_w, ml_w, ml_w)
    n_main = 3 * hy_w + 4 * ml_w
    n_blocks = -(-(t * TOP_K) // MOE_ROWS) + n_exp

    cond = jnp.concatenate([c_ctx[None], c, jnp.zeros((SUBLANES - 1 - bs, d), F32)], axis=0)
    zero_state = (jnp.zeros((bp, N_DIR, heads, dh, dh), F32), jnp.zeros((bp, N_DIR, heads, dh), F32),
                  jnp.zeros((bp, N_DIR, heads), F32))
    xs_paths = [x_prompt, x_sample]
    path_cfg = [(0, 0, lp), (1, 1, GRID_W)]
    new_c, new_n, new_m = [], [], []
    for l in range(depth):
        mods = _ada(cond, ada_w[l], ada_b[l]).reshape(SUBLANES, 6, d)
        w_main = w_in[l][:, :n_main].astype(BF16)
        w_gate = jnp.pad(w_in[l][:, n_main:], ((0, 0), (0, LANES - ng)))
        lw = (hy_conv_w[l], hy_conv_b[l],
              (filt_w1[l], filt_b1[l], filt_w2[l], filt_b2[l], filt_w3[l], filt_freq[l]), hy_bias[l],
              ml_conv_w[l], ml_conv_b[l], ml_gate_b[l], ml_norm_g[l])
        states = [zero_state, (state_mlstm_C[:, l], state_mlstm_n[:, l], state_mlstm_m[:, l])]
        counts = jnp.zeros((SUBLANES, LANES), F32)
        routed = []
        for x3, (mod0, mod_step, row_w), state in zip(xs_paths, path_cfg, states):
            bsz, lseq, _ = x3.shape
            tm, tt = min(ROW_TILE, lseq), min(TOK_TILE, lseq)
            xf = x3.reshape(bsz * lseq, d)
            z = _inproj(xf, mods, norm1_g[l], w_main, w_gate, seg_widths,
                        _mod_index_map(mod0, mod_step, lseq // tm), tm)
            y_hy, y_ml, st = _sequence_mixers(*[a.reshape(bsz, lseq, a.shape[1]) for a in z], lw, state, row_w)
            x1, h2t, ti, tg, rk, counts = _post(
                y_hy.reshape(bsz * lseq, hy_w), y_ml.reshape(bsz * lseq, ml_w), xf, mods, w_out[l], norm2_g[l],
                router_w[l], router_b[l], counts, _mod_index_map(mod0, mod_step, lseq // tm), tm)
            routed.append((x1, h2t, ti, tg, rk, tm, tt, _mod_index_map(mod0, mod_step, lseq // tt), st))
        new_c.append(routed[0][-1][0])
        new_n.append(routed[0][-1][1])
        new_m.append(routed[0][-1][2])

        pad_start, block_e, n_used, pad_first, pad_count = _moe_plan(
            counts[0, :n_exp].astype(jnp.int32), MOE_ROWS, n_blocks)
        tt = routed[0][6]
        assert all(r[6] == tt for r in routed)
        dests = [_dest_rows(ti, rk, pad_start, tm)[:, :TOP_K].reshape(-1, 1, tt * TOP_K)
                 for _, _, ti, _, rk, tm, _, _, _ in routed]
        xs = _dispatch(jnp.concatenate(dests, axis=0), [r[1] for r in routed], pad_first, pad_count, n_used,
                       n_blocks, MOE_ROWS, tt)
        ys = _ffn(block_e, n_used, xs, moe_w_gu[l], moe_b_gu[l], moe_w_down[l], moe_b_down[l], MOE_ROWS)
        xs_paths = [
            _combine(dest, tg, x1, mods, final_g, ys, mod_map, tt, final_norm=l == depth - 1).reshape(x3.shape)
            for dest, (x1, _, _, tg, _, _, tt, mod_map, _), x3 in zip(dests, routed, xs_paths)]
    return (xs_paths[0], xs_paths[1],
            jnp.stack(new_c, axis=1), jnp.stack(new_n, axis=1), jnp.stack(new_m, axis=1))
```

```python
import functools
import math

import numpy as np
import jax
import jax.numpy as jnp
from jax import lax
from jax.experimental import pallas as pl
from jax.experimental.pallas import tpu as pltpu

F32 = jnp.float32
BF16 = jnp.bfloat16
HIGHEST = lax.Precision.HIGHEST
EPS = 1e-6

LANES = 128
SUBLANES = 8
VMEM_LIMIT_BYTES = 56 * 1024 * 1024

GRID_W = 64
ML_HEADS = 4
N_DIR = 2
HY_ORDER = 2
FILT_BANDS = 8
DECAY_TARGET = 1e-2
FAST_DECAY_PCT = 0.3
SLOW_DECAY_PCT = 1.5
TOP_K = 4
SWIGLU_LIMIT = 7.0
SWIGLU_ALPHA = 1.702

FFT_N2 = 128
MLSTM_CHUNK = 128
MLSTM_GROUP = 8
MLSTM_PAIR = 4
MOE_ROWS = 512
ROW_TILE = 512
TOK_TILE = 256


def _cparams(*sem):
    return pltpu.CompilerParams(dimension_semantics=sem, vmem_limit_bytes=VMEM_LIMIT_BYTES)


def _lane_tile(c, cap):
    return max(t for t in range(LANES, min(c, cap) + 1, LANES) if c % t == 0)


def _bdot(a, b):
    return jnp.dot(a.astype(BF16), b.astype(BF16), preferred_element_type=F32)


def _hdot(a, b):
    return jnp.dot(a, b, precision=HIGHEST, preferred_element_type=F32)


def _dot3(a, b):
    a_hi, b_hi = a.astype(BF16), b.astype(BF16)
    a_lo = (a - a_hi.astype(F32)).astype(BF16)
    b_lo = (b - b_hi.astype(F32)).astype(BF16)
    dot = functools.partial(jnp.dot, preferred_element_type=F32)
    return dot(a_hi, b_hi) + dot(a_hi, b_lo) + dot(a_lo, b_hi)


def _ada_kernel(c_ref, w_ref, b_ref, o_ref):
    c = c_ref[...]
    o_ref[...] = _hdot(c * jax.nn.sigmoid(c), w_ref[...]) + b_ref[...]


def _ada(cond, w, b):
    r, d = cond.shape
    n = w.shape[1]
    tn = _lane_tile(n, 1024)
    return pl.pallas_call(
        _ada_kernel,
        grid=(n // tn,),
        in_specs=[pl.BlockSpec((r, d), lambda j: (0, 0)),
                  pl.BlockSpec((d, tn), lambda j: (0, j)),
                  pl.BlockSpec((1, tn), lambda j: (0, j))],
        out_specs=pl.BlockSpec((r, tn), lambda j: (0, j)),
        out_shape=jax.ShapeDtypeStruct((r, n), F32),
        compiler_params=_cparams("arbitrary"),
        name="ada",
    )(cond, w, b.reshape(1, n))


def _mod_index_map(mod0, mod_step, tiles_per_seq):
    def index_map(i):
        return (mod0 + (i // tiles_per_seq) * mod_step, 0, 0)
    return index_map


def _inproj_kernel(x_ref, mod_ref, g_ref, w_ref, wg_ref, *out_refs, offsets):
    x = x_ref[...]
    h = x * lax.rsqrt(jnp.mean(x * x, axis=-1, keepdims=True) + EPS) * g_ref[...]
    h = h * (1.0 + mod_ref[0, 1:2, :]) + mod_ref[0, 0:1, :]
    hb = h.astype(BF16)
    for o_ref, (lo, hi) in zip(out_refs[:-1], offsets):
        o_ref[...] = jnp.dot(hb, w_ref[:, lo:hi], preferred_element_type=F32)
    h_lo = (h - hb.astype(F32)).astype(BF16)
    g = jnp.dot(hb, wg_ref[...], preferred_element_type=F32)
    out_refs[-1][...] = (g[:, :LANES] + g[:, LANES:]
                         + jnp.dot(h_lo, wg_ref[:, :LANES], preferred_element_type=F32))


def _inproj(x, mods, norm_g, w_main, w_gate, seg_widths, mod_map, tm):
    t, d = x.shape
    offsets, lo = [], 0
    for wd in seg_widths:
        offsets.append((lo, lo + wd))
        lo += wd
    wg_hi = w_gate.astype(BF16)
    wg = jnp.concatenate([wg_hi, (w_gate - wg_hi.astype(F32)).astype(BF16)], axis=1)
    widths = tuple(seg_widths) + (LANES,)
    return pl.pallas_call(
        functools.partial(_inproj_kernel, offsets=tuple(offsets)),
        grid=(t // tm,),
        in_specs=[pl.BlockSpec((tm, d), lambda i: (i, 0)),
                  pl.BlockSpec((1, 6, d), mod_map),
                  pl.BlockSpec((1, d), lambda i: (0, 0)),
                  pl.BlockSpec(w_main.shape, lambda i: (0, 0)),
                  pl.BlockSpec(wg.shape, lambda i: (0, 0))],
        out_specs=[pl.BlockSpec((tm, wd), lambda i: (i, 0)) for wd in widths],
        out_shape=[jax.ShapeDtypeStruct((t, wd), F32) for wd in widths],
        compiler_params=_cparams("arbitrary"),
        name="inproj",
    )(x, mods, norm_g.reshape(1, d), w_main, wg)


def _short_conv_value(x, w_ref, b_ref, row_w):
    l = x.shape[0]
    pos = lax.broadcasted_iota(jnp.int32, x.shape, 0) % row_w
    prev = jnp.where(pos == 0, 0.0, pltpu.roll(x, 1, 0))
    nxt = jnp.where(pos == row_w - 1, 0.0, pltpu.roll(x, l - 1, 0))
    return prev * w_ref[0:1, :] + x * w_ref[1:2, :] + nxt * w_ref[2:3, :] + b_ref[...]


def _short_conv_kernel(x_ref, w_ref, b_ref, o_ref, *, row_w, silu, transpose):
    y = _short_conv_value(x_ref[0], w_ref, b_ref, row_w)
    if silu:
        y = y * jax.nn.sigmoid(y)
    o_ref[0] = y.T if transpose else y


def _short_conv(x, w, b, row_w, silu, col0=0, ncols=None, transpose=False):
    bsz, l, c = x.shape
    ncols = c if ncols is None else ncols
    ct = _lane_tile(math.gcd(ncols, col0) if col0 else ncols, LANES if transpose else 512)
    j0 = col0 // ct
    out_shape, out_block, out_map = (bsz, l, ncols), (1, l, ct), lambda i, j: (i, 0, j)
    if transpose:
        out_shape, out_block, out_map = (bsz, ncols, l), (1, ct, l), lambda i, j: (i, j, 0)
    return pl.pallas_call(
        functools.partial(_short_conv_kernel, row_w=row_w, silu=silu, transpose=transpose),
        grid=(bsz, ncols // ct),
        in_specs=[pl.BlockSpec((1, l, ct), lambda i, j: (i, 0, j0 + j)),
                  pl.BlockSpec((3, ct), lambda i, j: (0, j0 + j)),
                  pl.BlockSpec((1, ct), lambda i, j: (0, j0 + j))],
        out_specs=pl.BlockSpec(out_block, out_map),
        out_shape=jax.ShapeDtypeStruct(out_shape, F32),
        compiler_params=_cparams("arbitrary", "arbitrary"),
        name="short_conv",
    )(x, w, b.reshape(1, c))


def _dft_direct_tables(l):
    n = 2 * l
    k = np.arange(n)[:, None].astype(np.float64)
    t = np.arange(n)[None, :].astype(np.float64)
    ang = 2.0 * np.pi * ((k * t) % n) / n
    cm, sm = np.cos(ang), np.sin(ang)
    fwd = np.block([[cm[:, :l], sm[:, :l]], [-sm[:, :l], cm[:, :l]]])
    filt = np.concatenate([cm, -sm], axis=0)
    return fwd, filt


def _dft_two_level_tables(l, n2):
    n = 2 * l
    n1 = n // n2
    k1 = np.arange(n1)[:, None].astype(np.float64)
    a = np.arange(n1)[None, :].astype(np.float64)
    ang1 = 2.0 * np.pi * ((k1 * a) % n1) / n1
    c1, s1 = np.cos(ang1), np.sin(ang1)
    h = n1 // 2
    m1 = np.block([[c1[:, :h], s1[:, :h]], [-s1[:, :h], c1[:, :h]]])
    m1f = np.concatenate([c1, -s1], axis=0)
    kk = (np.arange(n1)[:, None, None] + n1 * np.arange(n2)[None, :, None]).astype(np.float64)
    b = np.arange(n2)[None, None, :].astype(np.float64)
    ang = 2.0 * np.pi * ((kk * b) % n) / n
    cg, sg = np.cos(ang), np.sin(ang)
    gt = np.concatenate([np.concatenate([cg, sg], axis=2),
                         np.concatenate([-sg, cg], axis=2)], axis=1)
    return m1, m1f, gt


def _circular_lag(n0, rows, l):
    n = n0 + lax.broadcasted_iota(jnp.int32, (rows, 1), 0)
    t = jnp.where(n < l, n, 2 * l - n).astype(F32)
    return n, t, t / float(max(l - 1, 1))


def _filter_hidden_kernel(bandv_ref, w1_ref, b1_ref, w2_ref, b2_ref, freq_ref, o_ref, *, l):
    rows = o_ref.shape[0]
    _, t, t01 = _circular_lag(pl.program_id(0) * rows, rows, l)
    lane = lax.broadcasted_iota(jnp.int32, (rows, LANES), 1)
    ang = (2.0 * math.pi / l) * t * bandv_ref[...]
    feats = jnp.where(lane == 0, t01,
                      jnp.where(lane <= FILT_BANDS, jnp.cos(ang),
                                jnp.where(lane <= 2 * FILT_BANDS, -jnp.sin(ang), 0.0)))
    fr = freq_ref[...]
    h = jnp.sin(fr * (_hdot(feats, w1_ref[...]) + b1_ref[...]))
    o_ref[...] = jnp.sin(fr * (_hdot(h, w2_ref[...]) + b2_ref[...]))


def _filter_hidden(l, f_w1, f_b1, f_w2, f_b2, f_freq):
    emb, hid = f_w1.shape
    n = 2 * l
    rows = min(n, 512)
    bands = jnp.linspace(1e-4, FILT_BANDS - 1, FILT_BANDS, dtype=F32)
    bandv = jnp.zeros((1, LANES), F32).at[0, 1:1 + FILT_BANDS].set(bands)
    bandv = bandv.at[0, 1 + FILT_BANDS:1 + 2 * FILT_BANDS].set(bands)
    w1p = jnp.zeros((LANES, hid), F32).at[:emb].set(f_w1)
    c0 = lambda i: (0, 0)
    return pl.pallas_call(
        functools.partial(_filter_hidden_kernel, l=l),
        grid=(n // rows,),
        in_specs=[pl.BlockSpec((1, LANES), c0), pl.BlockSpec((LANES, hid), c0), pl.BlockSpec((1, hid), c0),
                  pl.BlockSpec((hid, hid), c0), pl.BlockSpec((1, hid), c0), pl.BlockSpec((1, hid), c0)],
        out_specs=pl.BlockSpec((rows, hid), lambda i: (i, 0)),
        out_shape=jax.ShapeDtypeStruct((n, hid), F32),
        compiler_params=_cparams("arbitrary"),
        name="filter_hidden",
    )(bandv, w1p, f_b1.reshape(1, hid), f_w2, f_b2.reshape(1, hid), f_freq.reshape(1, hid))


def _filter_rows(n0, rows, l, hid_ref, w3_ref, delta_ref):
    n, _, t01 = _circular_lag(n0, rows, l)
    h = hid_ref[pl.ds(n0, rows), :]
    hf = _bdot(h, w3_ref[0, 0])
    hb = _bdot(h, w3_ref[0, 1])
    window = jnp.exp(-t01 * delta_ref[...])
    return jnp.where(n < l, hf, jnp.where(n > l, hb, 0.0)) * window


def _filter_direct_kernel(hid_ref, w3_ref, delta_ref, ff_ref, h_ref, *, l):
    hc = _filter_rows(0, 2 * l, l, hid_ref, w3_ref, delta_ref)
    h_ref[0] = _bdot(ff_ref[...], hc) * (1.0 / (2 * l))


def _filter_two_level_kernel(hid_ref, w3_ref, delta_ref, m1f_ref, gt_ref, h_ref, hc_buf, a_buf, *, l, n2, rows):
    n = 2 * l
    n1 = n // n2

    def fill(i, carry):
        r0 = pl.multiple_of(i * rows, rows)
        hc_buf[pl.ds(r0, rows), :] = _filter_rows(r0, rows, l, hid_ref, w3_ref, delta_ref)
        return carry
    lax.fori_loop(0, n // rows, fill, 0)

    def step1(b, carry):
        col = hc_buf[pl.ds(b, n1, stride=n2), :]
        a = _bdot(m1f_ref[...], col)
        a_buf[pl.ds(b, n1, stride=2 * n2), :] = a[:n1]
        a_buf[pl.ds(n2 + b, n1, stride=2 * n2), :] = a[n1:]
        return carry
    lax.fori_loop(0, n2, step1, 0, unroll=8)

    def step2(k1, carry):
        r0 = pl.multiple_of(k1 * 2 * n2, 2 * n2)
        h_ref[0, k1] = (_bdot(gt_ref[k1], a_buf[pl.ds(r0, 2 * n2), :]) * (1.0 / n)).astype(h_ref.dtype)
        return carry
    lax.fori_loop(0, n1, step2, 0, unroll=4)


def _filter_tail_inputs(hy_w, f_w3):
    hid = f_w3.shape[0]
    w3 = f_w3.reshape(hid, HY_ORDER, N_DIR, hy_w).transpose(1, 2, 0, 3)
    max_decay = math.log(DECAY_TARGET) / FAST_DECAY_PCT
    min_decay = math.log(DECAY_TARGET) / SLOW_DECAY_PCT
    deltas = jnp.abs(jnp.linspace(min_decay, max_decay, hy_w, dtype=F32)).reshape(1, hy_w)
    return w3, deltas


def _filter_spectrum_direct(l, hy_w, filt_params, ff):
    f_w1, f_b1, f_w2, f_b2, f_w3, f_freq = filt_params
    hidden = _filter_hidden(l, f_w1, f_b1, f_w2, f_b2, f_freq)
    w3, deltas = _filter_tail_inputs(hy_w, f_w3)
    hid = f_w3.shape[0]
    n = 2 * l
    return pl.pallas_call(
        functools.partial(_filter_direct_kernel, l=l),
        grid=(HY_ORDER,),
        in_specs=[pl.BlockSpec((n, hid), lambda o: (0, 0)),
                  pl.BlockSpec((1, N_DIR, hid, hy_w), lambda o: (o, 0, 0, 0)),
                  pl.BlockSpec((1, hy_w), lambda o: (0, 0)),
                  pl.BlockSpec((2 * n, n), lambda o: (0, 0))],
        out_specs=pl.BlockSpec((1, 2 * n, hy_w), lambda o: (o, 0, 0)),
        out_shape=jax.ShapeDtypeStruct((HY_ORDER, 2 * n, hy_w), F32),
        compiler_params=_cparams("arbitrary"),
        name="filter_direct",
    )(hidden, w3, deltas, ff)


def _filter_spectrum_two_level(l, hy_w, filt_params, m1f, gt, ct):
    f_w1, f_b1, f_w2, f_b2, f_w3, f_freq = filt_params
    hidden = _filter_hidden(l, f_w1, f_b1, f_w2, f_b2, f_freq)
    w3, deltas = _filter_tail_inputs(hy_w, f_w3)
    hid = f_w3.shape[0]
    n = 2 * l
    n2 = FFT_N2
    n1 = n // n2
    return pl.pallas_call(
        functools.partial(_filter_two_level_kernel, l=l, n2=n2, rows=min(n, 512)),
        grid=(HY_ORDER, hy_w // ct),
        in_specs=[pl.BlockSpec((n, hid), lambda o, j: (0, 0)),
                  pl.BlockSpec((1, N_DIR, hid, ct), lambda o, j: (o, 0, 0, j)),
                  pl.BlockSpec((1, ct), lambda o, j: (0, j)),
                  pl.BlockSpec((2 * n1, n1), lambda o, j: (0, 0)),
                  pl.BlockSpec((n1, 2 * n2, 2 * n2), lambda o, j: (0, 0, 0))],
        out_specs=pl.BlockSpec((1, n1, 2 * n2, ct), lambda o, j: (o, 0, 0, j)),
        out_shape=jax.ShapeDtypeStruct((HY_ORDER, n1, 2 * n2, hy_w), BF16),
        scratch_shapes=[pltpu.VMEM((n, ct), F32), pltpu.VMEM((n1 * 2 * n2, ct), F32)],
        compiler_params=_cparams("arbitrary", "arbitrary"),
        name="filter_two_level",
    )(hidden, w3, deltas, m1f, gt)


def _complex_mul(x, h, half):
    xr, xi = x[:half], x[half:]
    hr, hi = h[:half], h[half:]
    return jnp.concatenate([xr * hr - xi * hi, xr * hi + xi * hr], axis=0)


def _conv_direct_kernel(z_ref, gate_ref, zw_ref, zb_ref, gw_ref, gb_ref, bias_ref, h_ref, fwd_ref, inv_ref,
                        o_ref, *, row_w, conv_z):
    l = z_ref.shape[1]
    zs = [_short_conv_value(z_ref[s], zw_ref, zb_ref, row_w) if conv_z else z_ref[s] for s in range(2)]
    z = jnp.concatenate(zs, axis=0)
    x = _bdot(fwd_ref[...], z)
    y = _bdot(inv_ref[...], _complex_mul(x, h_ref[0], 2 * l))
    bias = bias_ref[0]
    for s in range(2):
        gate = _short_conv_value(gate_ref[s], gw_ref, gb_ref, row_w)
        o_ref[s] = gate * (y[s * l:(s + 1) * l] + bias * zs[s])


def _conv_two_level_kernel(z_ref, gate_ref, bias_ref, h_ref, m1_ref, m1i_ref, gt_ref, o_ref,
                           a_buf, *, n2):
    l = z_ref.shape[1]
    n1 = 2 * l // n2
    hn = n1 // 2

    def step1(b, carry):
        za = z_ref[0, pl.ds(b, hn, stride=n2), :]
        zb = z_ref[1, pl.ds(b, hn, stride=n2), :]
        a = _bdot(m1_ref[...], jnp.concatenate([za, zb], axis=0))
        a_buf[pl.ds(b, n1, stride=2 * n2), :] = a[:n1]
        a_buf[pl.ds(n2 + b, n1, stride=2 * n2), :] = a[n1:]
        return carry
    lax.fori_loop(0, n2, step1, 0, unroll=8)

    def step2(k1, carry):
        r0 = pl.multiple_of(k1 * 2 * n2, 2 * n2)
        x = _bdot(gt_ref[k1], a_buf[pl.ds(r0, 2 * n2), :])
        y = _complex_mul(x, h_ref[0, k1].astype(F32), n2).astype(BF16)
        a_buf[pl.ds(r0, 2 * n2), :] = lax.dot_general(gt_ref[k1], y, (((0,), (0,)), ((), ())),
                                                      preferred_element_type=F32)
        return carry
    lax.fori_loop(0, n1, step2, 0, unroll=4)

    def step3(b, carry):
        br = a_buf[pl.ds(b, n1, stride=2 * n2), :]
        bi = a_buf[pl.ds(n2 + b, n1, stride=2 * n2), :]
        y = _bdot(m1i_ref[...], jnp.concatenate([br, bi], axis=0))
        o_ref[0, pl.ds(b, hn, stride=n2), :] = y[:hn]
        o_ref[1, pl.ds(b, hn, stride=n2), :] = y[hn:]
        return carry
    lax.fori_loop(0, n2, step3, 0, unroll=8)

    bias = bias_ref[0]
    for s in range(2):
        o_ref[s] = gate_ref[s] * (o_ref[s] + bias * z_ref[s])


def _long_conv_gated(u, z, z_col, gate_col, spectrum, order, bias, tables, ct, conv=None, conv_z=False):
    bsz, l, _ = u.shape
    c = spectrum.shape[-1]
    nct = c // ct
    zspec = pl.BlockSpec((2, l, ct), lambda i, j: (i, 0, z_col * nct + j))
    gspec = pl.BlockSpec((2, l, ct), lambda i, j: (i, 0, gate_col * nct + j))
    bspec = pl.BlockSpec((1, 1, ct), lambda i, j: (order, 0, j))
    ospec = pl.BlockSpec((2, l, ct), lambda i, j: (i, 0, j))
    out_shape = jax.ShapeDtypeStruct((bsz, l, c), F32)
    bias3 = bias.reshape(HY_ORDER, 1, c)
    if len(tables) == 2:
        fwd, inv = tables
        n = 2 * l
        conv_w, conv_b, row_w = conv
        conv_b = conv_b.reshape(1, -1)
        zcol = z_col if conv_z else gate_col
        wspec = lambda col, rows: pl.BlockSpec((rows, ct), lambda i, j: (0, col * nct + j))
        return pl.pallas_call(
            functools.partial(_conv_direct_kernel, row_w=row_w, conv_z=conv_z),
            grid=(bsz // 2, nct),
            in_specs=[zspec, gspec, wspec(zcol, 3), wspec(zcol, 1), wspec(gate_col, 3), wspec(gate_col, 1), bspec,
                      pl.BlockSpec((1, 2 * n, ct), lambda i, j: (order, 0, j)),
                      pl.BlockSpec(fwd.shape, lambda i, j: (0, 0)),
                      pl.BlockSpec(inv.shape, lambda i, j: (0, 0))],
            out_specs=ospec, out_shape=out_shape,
            compiler_params=_cparams("arbitrary", "arbitrary"),
            name="long_conv_direct",
        )(z, u, conv_w, conv_b, conv_w, conv_b, bias3, spectrum, fwd, inv)
    m1, m1i, gt = tables
    n2 = FFT_N2
    n1 = 2 * l // n2
    const2 = lambda i, j: (0, 0)
    const3 = lambda i, j: (0, 0, 0)
    return pl.pallas_call(
        functools.partial(_conv_two_level_kernel, n2=n2),
        grid=(nct, bsz // 2),
        in_specs=[pl.BlockSpec((2, l, ct), lambda j, i: (i, 0, z_col * nct + j)),
                  pl.BlockSpec((2, l, ct), lambda j, i: (i, 0, gate_col * nct + j)),
                  pl.BlockSpec((1, 1, ct), lambda j, i: (order, 0, j)),
                  pl.BlockSpec((1, n1, 2 * n2, ct), lambda j, i: (order, 0, 0, j)),
                  pl.BlockSpec(m1.shape, const2), pl.BlockSpec(m1i.shape, const2),
                  pl.BlockSpec(gt.shape, const3)],
        out_specs=pl.BlockSpec((2, l, ct), lambda j, i: (i, 0, j)),
        out_shape=out_shape,
        scratch_shapes=[pltpu.VMEM((n1 * 2 * n2, ct), F32)],
        compiler_params=_cparams("arbitrary", "arbitrary"),
        name="long_conv_two_level",
    )(z, u, bias3, spectrum, m1, m1i, gt)


def _hyena(z_hy, conv_w, conv_b, filt_params, hy_bias, row_w):
    bsz, l, c3 = z_hy.shape
    c = c3 // 3
    table = lambda a: jnp.asarray(a, F32).astype(BF16)
    if 2 * l // FFT_N2 <= 4:
        fwd, filt = _dft_direct_tables(l)
        tables = (table(fwd), table(fwd.T))
        spectrum = _filter_spectrum_direct(l, c, filt_params, table(filt))
        conv = (conv_w, conv_b, row_w)
        z1 = _long_conv_gated(z_hy, z_hy, 0, 1, spectrum, 0, hy_bias, tables, c, conv, conv_z=True)
        return _long_conv_gated(z_hy, z1, 0, 2, spectrum, 1, hy_bias, tables, c, conv)
    u = _short_conv(z_hy, conv_w, conv_b, row_w, silu=False)
    m1, m1f, gt = _dft_two_level_tables(l, FFT_N2)
    tables = (table(m1), table(m1.T), table(gt))
    ct = LANES
    spectrum = _filter_spectrum_two_level(l, c, filt_params, table(m1f), tables[2], ct)
    z1 = _long_conv_gated(u, u, 0, 1, spectrum, 0, hy_bias, tables, ct)
    return _long_conv_gated(u, z1, 0, 2, spectrum, 1, hy_bias, tables, ct)


def _log_sigmoid(x):
    return jnp.minimum(x, 0.0) - jnp.log1p(jnp.exp(-jnp.abs(x)))


def _split3_dot(a, b, split_lhs):
    x = a if split_lhs else b
    hi = x.astype(BF16)
    rest = x - hi.astype(F32)
    mid = rest.astype(BF16)
    parts = (hi, mid, (rest - mid.astype(F32)).astype(BF16))
    if split_lhs:
        return functools.reduce(lambda u, w: u + w, [jnp.dot(p, b, preferred_element_type=F32) for p in parts])
    return functools.reduce(lambda u, w: u + w, [jnp.dot(a, p, preferred_element_type=F32) for p in parts])


def _mlstm_recur(qb, sv, m_loc, kv, g_loc, bq, btot, cn, m):
    dh = qb.shape[1]
    inter = bq + m
    mj = jnp.maximum(m_loc, inter)
    w_int = jnp.exp(inter - mj)
    w_loc = jnp.exp(m_loc - mj)
    qc = jnp.dot(qb, cn.astype(BF16), preferred_element_type=F32)
    num = w_int * qc[:, :dh] + w_loc * sv[:, :dh]
    den = w_int * qc[:, dh:] + w_loc * sv[:, dh:]
    h = num / jnp.maximum(jnp.abs(den), jnp.exp(-mj))
    m_new = jnp.maximum(btot + m, g_loc)
    cn_new = jnp.exp(btot + m - m_new) * cn + jnp.exp(g_loc - m_new) * kv
    return h, cn_new, m_new


def _mlstm_kernel(q_ref, cw_ref, cb_ref, kt_ref, v_ref, o_ref, g_ref, gt_ref, gb_ref, gbt_ref, ng_ref,
                  c0_ref, n0_ref, m0_ref, y_ref, c_ref, n_ref, m_ref,
                  hf_buf, hb_buf, cn_buf, sv_buf, kv_buf, bq_buf, ml_buf, sc_buf, qb_buf,
                  *, chunk, heads, row_w):
    l, dh = q_ref.shape[1], q_ref.shape[2]
    nc = l // chunk
    group = sv_buf.shape[1]
    pair = math.gcd(group, MLSTM_PAIR)
    scale = dh ** -0.5

    def conv_q(j, carry):
        r0 = pl.multiple_of(j * chunk, chunk)
        x = q_ref[0, pl.ds(r0, chunk), :]
        idx = lax.broadcasted_iota(jnp.int32, x.shape, 0)
        pos = (r0 + idx) & (row_w - 1)
        before = q_ref[0, pl.ds(jnp.maximum(r0 - 1, 0), 1), :]
        after = q_ref[0, pl.ds(jnp.minimum(r0 + chunk, l - 1), 1), :]
        prev = jnp.where(idx == 0, before, pltpu.roll(x, 1, 0))
        nxt = jnp.where(idx == chunk - 1, after, pltpu.roll(x, chunk - 1, 0))
        y = (jnp.where(pos == 0, 0.0, prev) * cw_ref[0:1, :] + x * cw_ref[1:2, :]
             + jnp.where(pos == row_w - 1, 0.0, nxt) * cw_ref[2:3, :] + cb_ref[...])
        qb_buf[pl.ds(r0, chunk), :] = (y * jax.nn.sigmoid(y)).astype(BF16)
        return carry
    lax.fori_loop(0, nc, conv_q, 0)
    row = lax.broadcasted_iota(jnp.int32, (chunk, chunk), 0)
    col = lax.broadcasted_iota(jnp.int32, (chunk, chunk), 1)
    lower, upper = col <= row, col >= row
    tri_l, tri_u = lower.astype(BF16), upper.astype(BF16)
    gate_row = lax.broadcasted_iota(jnp.int32, (SUBLANES, chunk), 0)
    ones = jnp.ones((chunk, dh), BF16)

    sel_row = lax.broadcasted_iota(jnp.int32, (LANES, LANES), 0)
    head = pl.program_id(1)

    def chunk_start(j, d):
        return pl.multiple_of(((nc - 1 - j) if d else j) * chunk, chunk)

    def local(gp, carry, j0):
        jobs = [(gp * pair + k, d) for k in range(pair) for d in range(N_DIR)]
        r0s = [chunk_start(j0 + g, d) for g, d in jobs]
        picks = [(sel_row == (2 * d + 1) * heads + head).astype(BF16) for d in range(N_DIR)]
        lfs = [_log_sigmoid(g_ref[0, pl.ds(r0, chunk), :] + gb_ref[...]) for r0 in r0s]
        gts = [gt_ref[0, 0, :, pl.ds(r0, chunk)] + gbt_ref[0] for r0 in r0s]
        gts = [jnp.where(gate_row % 2 == 1, _log_sigmoid(gt), gt) for gt in gts]
        lfs = [_split3_dot(lf, picks[d], split_lhs=True) for lf, (_, d) in zip(lfs, jobs)]
        brows = [_split3_dot(gt, tri_l if d else tri_u, split_lhs=True)[2 * d + 1:2 * d + 2, :]
                 for gt, (_, d) in zip(gts, jobs)]
        bqs = [_split3_dot(tri_u if d else tri_l, lf, split_lhs=False) for lf, (_, d) in zip(lfs, jobs)]
        btots = [brow[:, 0:1] if d else brow[:, chunk - 1:chunk] for brow, (_, d) in zip(brows, jobs)]
        irows = [gt[2 * d:2 * d + 1, :] for gt, (_, d) in zip(gts, jobs)]
        qbs = [qb_buf[pl.ds(r0, chunk), :] for r0 in r0s]
        kts = [kt_ref[0, :, pl.ds(r0, chunk)] * scale for r0 in r0s]
        vos = [jnp.concatenate([v_ref[0, pl.ds(r0, chunk), :].astype(BF16), ones], axis=1) for r0 in r0s]
        qks = [jnp.dot(qb, kt.astype(BF16), preferred_element_type=F32) for qb, kt in zip(qbs, kts)]
        dms = [jnp.where(upper if d else lower, bq - brow + irow, -jnp.inf)
               for bq, brow, irow, (_, d) in zip(bqs, brows, irows, jobs)]
        m_locs = [jnp.max(dm, axis=-1, keepdims=True) for dm in dms]
        ss = [(qk * jnp.exp(dm - m_loc)).astype(BF16) for qk, dm, m_loc in zip(qks, dms, m_locs)]
        gls = [btot - brow + irow for btot, brow, irow in zip(btots, brows, irows)]
        g_locs = [jnp.max(gl, axis=-1, keepdims=True) for gl in gls]
        wks = [(kt * jnp.exp(gl - g_loc)).astype(BF16) for kt, gl, g_loc in zip(kts, gls, g_locs)]
        svs = [jnp.dot(s, vo, preferred_element_type=F32) for s, vo in zip(ss, vos)]
        kvs = [jnp.dot(wk, vo, preferred_element_type=F32) for wk, vo in zip(wks, vos)]
        for (g, d), sv, kv, bq, m_loc, btot, g_loc in zip(jobs, svs, kvs, bqs, m_locs, btots, g_locs):
            sv_buf[d, g] = sv
            kv_buf[d, g] = kv
            bq_buf[d, g] = bq
            ml_buf[d, g] = jnp.broadcast_to(m_loc, (chunk, LANES))
            sc_buf[d, g, 0:1, :] = jnp.broadcast_to(btot, (1, LANES))
            sc_buf[d, g, 1:2, :] = jnp.broadcast_to(g_loc, (1, LANES))
        return carry

    def recur(g, carry, j0):
        ms = list(carry)
        for d in range(N_DIR):
            r0 = chunk_start(j0 + g, d)
            h, cn, ms[d] = _mlstm_recur(
                qb_buf[pl.ds(r0, chunk), :], sv_buf[d, g], ml_buf[d, g], kv_buf[d, g],
                sc_buf[d, g, 1:2, 0:1], bq_buf[d, g], sc_buf[d, g, 0:1, 0:1], cn_buf[d], ms[d])
            cn_buf[d] = cn
            (hb_buf if d else hf_buf)[pl.ds(r0, chunk), :] = h
        return tuple(ms)

    for d in range(N_DIR):
        n_rep = jnp.broadcast_to(n0_ref[0, 0, d:d + 1, :], (dh, dh)).T
        cn_buf[d] = jnp.concatenate([c0_ref[0, d, 0], n_rep], axis=1)

    def block(jb, carry):
        j0 = jb * group
        lax.fori_loop(0, group // pair, functools.partial(local, j0=j0), 0)
        return lax.fori_loop(0, group, functools.partial(recur, j0=j0), carry, unroll=2)

    m_fin = lax.fori_loop(0, nc // group, block, (m0_ref[0, 0, 0:1, 0:1], m0_ref[0, 0, 1:2, 0:1]))
    for d in range(N_DIR):
        c_ref[0, d, 0] = cn_buf[d, :, :dh]
        n_ref[0, 0, d:d + 1, :] = cn_buf[d, :, dh:].T[0:1, :]
        m_ref[0, 0, d:d + 1, :] = jnp.broadcast_to(m_fin[d], (1, LANES))

    def finish(j, carry):
        r0 = pl.multiple_of(j * chunk, chunk)
        hs = hf_buf[pl.ds(r0, chunk), :] + hb_buf[pl.ds(r0, chunk), :]
        hs = hs * lax.rsqrt(jnp.mean(hs * hs, axis=-1, keepdims=True) + EPS) * ng_ref[...]
        y_ref[0, pl.ds(r0, chunk), :] = jax.nn.sigmoid(o_ref[0, pl.ds(r0, chunk), :]) * hs
        return carry
    lax.fori_loop(0, nc, finish, 0)


def _mlstm(z_qk, conv_w, conv_b, row_w, kt, z_v, z_o, z_g, gate_b, norm_g, c0, n0, m0):
    bsz, l, ml = z_v.shape
    heads = gate_b.shape[-1]
    dh = ml // heads
    chunk = MLSTM_CHUNK
    assert dh == LANES and chunk == LANES and l % chunk == 0
    assert row_w & (row_w - 1) == 0 and l % row_w == 0
    group = math.gcd(l // chunk, MLSTM_GROUP)
    n_gate = N_DIR * 2
    assert z_g.shape[-1] == LANES
    g4 = z_g[..., :n_gate * heads].reshape(bsz, l, n_gate, heads)
    ght = jnp.pad(g4.transpose(0, 3, 2, 1), ((0, 0), (0, 0), (0, SUBLANES - n_gate), (0, 0)))
    gb4 = gate_b.reshape(n_gate, heads).T
    gb = jnp.pad(gate_b.reshape(1, n_gate * heads), ((0, 0), (0, LANES - n_gate * heads)))
    gbt = jnp.broadcast_to(jnp.pad(gb4, ((0, 0), (0, SUBLANES - n_gate)))[..., None], (heads, SUBLANES, LANES))
    n0h = n0.transpose(0, 2, 1, 3)
    m0h = jnp.broadcast_to(m0.transpose(0, 2, 1)[..., None], (bsz, heads, N_DIR, LANES))
    seq = pl.BlockSpec((1, l, dh), lambda b, h: (b, 0, h))
    cspec = pl.BlockSpec((1, N_DIR, 1, dh, dh), lambda b, h: (b, 0, h, 0, 0))
    sspec = pl.BlockSpec((1, 1, N_DIR, dh), lambda b, h: (b, h, 0, 0))
    mspec = pl.BlockSpec((1, 1, N_DIR, LANES), lambda b, h: (b, h, 0, 0))
    y, c, n, m = pl.pallas_call(
        functools.partial(_mlstm_kernel, chunk=chunk, heads=heads, row_w=row_w),
        grid=(bsz, heads),
        in_specs=[seq, pl.BlockSpec((3, dh), lambda b, h: (0, h)), pl.BlockSpec((1, dh), lambda b, h: (0, h)),
                  pl.BlockSpec((1, dh, l), lambda b, h: (b, h, 0)), seq, seq,
                  pl.BlockSpec((1, l, LANES), lambda b, h: (b, 0, 0)),
                  pl.BlockSpec((1, 1, SUBLANES, l), lambda b, h: (b, h, 0, 0)),
                  pl.BlockSpec((1, LANES), lambda b, h: (0, 0)),
                  pl.BlockSpec((1, SUBLANES, LANES), lambda b, h: (h, 0, 0)),
                  pl.BlockSpec((1, dh), lambda b, h: (0, h)),
                  cspec, sspec, mspec],
        out_specs=[seq, cspec, sspec, mspec],
        out_shape=[jax.ShapeDtypeStruct((bsz, l, ml), F32),
                   jax.ShapeDtypeStruct((bsz, N_DIR, heads, dh, dh), F32),
                   jax.ShapeDtypeStruct((bsz, heads, N_DIR, dh), F32),
                   jax.ShapeDtypeStruct((bsz, heads, N_DIR, LANES), F32)],
        scratch_shapes=[pltpu.VMEM((l, dh), F32), pltpu.VMEM((l, dh), F32),
                        pltpu.VMEM((N_DIR, dh, 2 * dh), F32),
                        pltpu.VMEM((N_DIR, group, chunk, 2 * dh), F32),
                        pltpu.VMEM((N_DIR, group, dh, 2 * dh), F32),
                        pltpu.VMEM((N_DIR, group, chunk, LANES), F32),
                        pltpu.VMEM((N_DIR, group, chunk, LANES), F32),
                        pltpu.VMEM((N_DIR, group, SUBLANES, LANES), F32),
                        pltpu.VMEM((l, dh), BF16)],
        compiler_params=_cparams("arbitrary", "arbitrary"),
        name="mlstm",
    )(z_qk, conv_w, conv_b.reshape(1, -1), kt, z_v, z_o, z_g, ght, gb, gbt, norm_g.reshape(1, ml), c0, n0h, m0h)
    return y, (c, n.transpose(0, 2, 1, 3), m[..., 0].transpose(0, 2, 1))


def _rows_to_tiles(tile_ref, x):
    r, d = x.shape
    s = d // LANES
    for k in range(s):
        tile_ref[pl.ds(k, r, stride=s), :] = x[:, LANES * k:LANES * (k + 1)]


def _tiles_to_rows(tile_ref, r, s, r0=0):
    return jnp.concatenate([tile_ref[pl.ds(r0 * s + k, r, stride=s), :] for k in range(s)], axis=1)


def _lane_pack(cols, lane):
    out = jnp.zeros(lane.shape, cols[0].dtype)
    for j, colv in enumerate(cols):
        out = jnp.where(lane == j, colv, out)
    return out


def _post_kernel(yhy_ref, yml_ref, x_ref, mod_ref, wo_ref, n2g_ref, rw_ref, rb_ref, cnt0_ref,
                 x1_ref, h2_ref, ti_ref, tg_ref, rk_ref, cnt_ref, carry, *, n_exp, top_k):
    @pl.when(pl.program_id(0) == 0)
    def _():
        carry[...] = cnt0_ref[0:1, :]

    hy_w = yhy_ref.shape[1]
    tm = x_ref.shape[0]
    proj = _bdot(yhy_ref[...], wo_ref[:hy_w, :]) + _bdot(yml_ref[...], wo_ref[hy_w:, :])
    x1 = x_ref[...] + mod_ref[0, 2:3, :] * proj
    x1_ref[...] = x1
    h2 = x1 * lax.rsqrt(jnp.mean(x1 * x1, axis=-1, keepdims=True) + EPS) * n2g_ref[...]
    h2 = h2 * (1.0 + mod_ref[0, 4:5, :]) + mod_ref[0, 3:4, :]
    _rows_to_tiles(h2_ref, h2)

    lane = lax.broadcasted_iota(jnp.int32, (tm, LANES), 1)
    work = jnp.where(lane < n_exp, _dot3(h2, rw_ref[...]) + rb_ref[...], -jnp.inf)
    vals, idxs, hots = [], [], []
    for _ in range(top_k):
        mx = jnp.max(work, axis=-1, keepdims=True)
        idx = jnp.min(jnp.where(work == mx, lane, LANES), axis=-1, keepdims=True)
        hot = lane == idx
        vals.append(mx)
        idxs.append(idx)
        hots.append(hot)
        work = jnp.where(hot, -jnp.inf, work)
    exps = [jnp.exp(v - vals[0]) for v in vals]
    tot = functools.reduce(lambda a, b: a + b, exps)
    ti_ref[...] = _lane_pack(idxs, lane)
    tg_ref[...] = _lane_pack([e / tot for e in exps], lane)

    hot_sum = functools.reduce(lambda a, b: a + b, [h.astype(F32) for h in hots])
    row = lax.broadcasted_iota(jnp.int32, (tm, tm), 0)
    col = lax.broadcasted_iota(jnp.int32, (tm, tm), 1)
    before = _bdot((col < row).astype(F32), hot_sum) + carry[...]
    ranks = [jnp.sum(jnp.where(h, before, 0.0), axis=-1, keepdims=True).astype(jnp.int32) for h in hots]
    rk_ref[...] = _lane_pack(ranks, lane)
    carry[...] = carry[...] + jnp.sum(hot_sum, axis=0, keepdims=True)
    cnt_ref[...] = jnp.broadcast_to(carry[...], cnt_ref.shape)


def _post(y_hy, y_ml, x, mods, w_out, norm2_g, r_w, r_b, counts, mod_map, tm):
    t, d = x.shape
    n_exp = r_w.shape[1]
    rwp = jnp.pad(r_w, ((0, 0), (0, LANES - n_exp)))
    rbp = jnp.pad(r_b, (0, LANES - n_exp)).reshape(1, LANES)
    row = lambda wd: pl.BlockSpec((tm, wd), lambda i: (i, 0))
    const = lambda shape: pl.BlockSpec(shape, lambda i: (0, 0))
    return pl.pallas_call(
        functools.partial(_post_kernel, n_exp=n_exp, top_k=TOP_K),
        grid=(t // tm,),
        in_specs=[row(y_hy.shape[1]), row(y_ml.shape[1]), row(d),
                  pl.BlockSpec((1, 6, d), mod_map),
                  const(w_out.shape), const((1, d)), const((d, LANES)), const((1, LANES)),
                  const((SUBLANES, LANES))],
        out_specs=[row(d), pl.BlockSpec((tm * d // LANES, LANES), lambda i: (i, 0)),
                   row(LANES), row(LANES), row(LANES), const((SUBLANES, LANES))],
        out_shape=[jax.ShapeDtypeStruct((t, d), F32), jax.ShapeDtypeStruct((t * d // LANES, LANES), F32),
                   jax.ShapeDtypeStruct((t, LANES), jnp.int32), jax.ShapeDtypeStruct((t, LANES), F32),
                   jax.ShapeDtypeStruct((t, LANES), jnp.int32), jax.ShapeDtypeStruct((SUBLANES, LANES), F32)],
        scratch_shapes=[pltpu.VMEM((1, LANES), F32)],
        compiler_params=_cparams("arbitrary"),
        name="post",
    )(y_hy, y_ml, x, mods, w_out.astype(BF16), norm2_g.reshape(1, d), rwp, rbp, counts)


def _dest_kernel(ti_ref, rk_ref, ps_ref, d_ref, *, top_k):
    lane = lax.broadcasted_iota(jnp.int32, ti_ref.shape, 1)
    ti, rk = ti_ref[...], rk_ref[...]
    cols = []
    for j in range(top_k):
        start = jnp.sum(jnp.where(lane == ti[:, j:j + 1], ps_ref[...], 0.0), axis=-1, keepdims=True)
        cols.append(start.astype(jnp.int32) + rk[:, j:j + 1])
    d_ref[...] = _lane_pack(cols, lane)


def _dest_rows(ti, rk, pad_start, tm):
    t = ti.shape[0]
    row = pl.BlockSpec((tm, LANES), lambda i: (i, 0))
    return pl.pallas_call(
        functools.partial(_dest_kernel, top_k=TOP_K),
        grid=(t // tm,),
        in_specs=[row, row, pl.BlockSpec((1, LANES), lambda i: (0, 0))],
        out_specs=row,
        out_shape=jax.ShapeDtypeStruct((t, LANES), jnp.int32),
        compiler_params=_cparams("arbitrary"),
        name="dest_rows",
    )(ti, rk, pad_start)


def _dispatch_kernel(first_ref, count_ref, nu_ref, dest_ref, *refs, top_k, s, rows, tiles):
    h_refs = refs[:len(tiles)]
    xs_ref, zbuf, sem, zsem = refs[len(tiles):]
    i = pl.program_id(0)
    tt = dest_ref.shape[2] // top_k
    half = zbuf.shape[0] // s
    n_blocks = xs_ref.shape[0] // (rows * s)

    def pad_rows(e, carry, wait):
        off, n = first_ref[e], count_ref[e]
        for k in range(half.bit_length()):
            bit = half >> k

            @pl.when((n & bit) != 0)
            def _():
                dst = pl.multiple_of(off * s, s)
                copy = pltpu.make_async_copy(zbuf.at[pl.ds(0, bit * s)], xs_ref.at[pl.ds(dst, bit * s)], zsem)
                copy.wait() if wait else copy.start()
            off = off + (n & bit)
        return carry

    def spare_block(b, carry, wait):
        for part in range(2):
            dst = pl.multiple_of((b * 2 + part) * half * s, half * s)
            copy = pltpu.make_async_copy(zbuf, xs_ref.at[pl.ds(dst, half * s)], zsem)
            copy.wait() if wait else copy.start()
        return carry

    @pl.when(i == 0)
    def _():
        zbuf[...] = jnp.zeros_like(zbuf)
        for wait in (False, True):
            lax.fori_loop(0, first_ref.shape[0], functools.partial(pad_rows, wait=wait), 0)
            lax.fori_loop(nu_ref[0], n_blocks, functools.partial(spare_block, wait=wait), 0)

    def scatter(h_ref):
        def issue(t, carry):
            src = pl.multiple_of(t * s, s)
            for j in range(top_k):
                dst = pl.multiple_of(dest_ref[0, 0, t * top_k + j] * s, s)
                pltpu.make_async_copy(h_ref.at[pl.ds(src, s)], xs_ref.at[pl.ds(dst, s)], sem).start(priority=j % 2)
            return carry
        lax.fori_loop(0, tt, issue, 0, unroll=4)
        for _ in range(top_k):
            pltpu.make_async_copy(h_ref, xs_ref.at[pl.ds(0, tt * s)], sem).wait()

    lo = 0
    for h_ref, n_tiles in zip(h_refs, tiles):
        @pl.when((i >= lo) & (i < lo + n_tiles))
        def _():
            scatter(h_ref)
        lo += n_tiles


def _dispatch(dest, h2ts, pad_first, pad_count, n_used, n_blocks, rows, tt):
    s = sum(h.shape[0] for h in h2ts) * TOP_K // dest.size
    tiles = tuple(h.shape[0] // (tt * s) for h in h2ts)
    starts = [sum(tiles[:k]) for k in range(len(tiles))]
    hspec = lambda lo, n: pl.BlockSpec((tt * s, LANES), lambda i, *_: (jnp.clip(i - lo, 0, n - 1), 0))
    return pl.pallas_call(
        functools.partial(_dispatch_kernel, top_k=TOP_K, s=s, rows=rows, tiles=tiles),
        grid_spec=pltpu.PrefetchScalarGridSpec(
            num_scalar_prefetch=3,
            grid=(dest.shape[0],),
            in_specs=[pl.BlockSpec((1, 1, tt * TOP_K), lambda i, *_: (i, 0, 0), memory_space=pltpu.SMEM)]
            + [hspec(lo, n) for lo, n in zip(starts, tiles)],
            out_specs=pl.BlockSpec(memory_space=pl.ANY),
            scratch_shapes=[pltpu.VMEM((rows // 2 * s, LANES), F32),
                            pltpu.SemaphoreType.DMA(()), pltpu.SemaphoreType.DMA(())]),
        out_shape=jax.ShapeDtypeStruct((n_blocks * rows * s, LANES), F32),
        compiler_params=_cparams("arbitrary"),
        name="dispatch",
    )(pad_first, pad_count, n_used, dest, *h2ts)


def _deinterleave_table():
    p = np.zeros((2 * LANES, 2 * LANES), np.float32)
    j = np.arange(LANES)
    p[2 * j, j] = 1.0
    p[2 * j + 1, LANES + j] = 1.0
    return p


def _ffn_kernel(be_ref, nu_ref, xs_ref, wgu_ref, bg_ref, bl_ref, wd_ref, bd_ref, perm_ref, ys_ref,
                wg_buf, wl_buf, wd_buf, acc_buf, *, ft):
    i = pl.program_id(0)
    f = wg_buf.shape[1]
    rows = acc_buf.shape[0]
    live = i < nu_ref[0]

    @pl.when(live & ((i == 0) | (be_ref[i] != be_ref[jnp.maximum(i - 1, 0)])))
    def _():
        for j in range(f // LANES):
            blk = wgu_ref[0, :, 2 * LANES * j:2 * LANES * (j + 1)].astype(BF16)
            split = jnp.dot(blk, perm_ref[...], preferred_element_type=F32)
            wg_buf[:, LANES * j:LANES * (j + 1)] = split[:, :LANES].astype(BF16)
            wl_buf[:, LANES * j:LANES * (j + 1)] = split[:, LANES:].astype(BF16)
        wd_buf[...] = wd_ref[0].astype(BF16)

    @pl.when(live)
    def _():
        x = _tiles_to_rows(xs_ref, rows, xs_ref.shape[0] // rows).astype(BF16)
        for j, f0 in enumerate(range(0, f, ft)):
            g = jnp.dot(x, wg_buf[:, f0:f0 + ft], preferred_element_type=F32) + bg_ref[0, :, f0:f0 + ft]
            lin = jnp.dot(x, wl_buf[:, f0:f0 + ft], preferred_element_type=F32) + bl_ref[0, :, f0:f0 + ft]
            gate = jnp.minimum(g, SWIGLU_LIMIT)
            lin = jnp.clip(lin, -SWIGLU_LIMIT, SWIGLU_LIMIT)
            act = (lin + 1.0) * gate * jax.nn.sigmoid(SWIGLU_ALPHA * gate)
            part = jnp.dot(act.astype(BF16), wd_buf[f0:f0 + ft, :], preferred_element_type=F32)
            if j == 0:
                acc_buf[...] = part + bd_ref[0]
            else:
                acc_buf[...] += part
        _rows_to_tiles(ys_ref, acc_buf[...])

    @pl.when(jnp.logical_not(live))
    def _():
        ys_ref[...] = jnp.zeros_like(ys_ref)


def _ffn(block_e, n_used, xs, w_gu, b_gu, w_d, b_d, rows):
    n_exp, d, f2 = w_gu.shape
    f = f2 // 2
    s = d // LANES
    n_rows = xs.shape[0] // s
    live = lambda i, nu: jnp.minimum(i, nu[0] - 1)
    wmap = lambda i, be, nu: (be[live(i, nu)], 0, 0)
    perm = jnp.asarray(_deinterleave_table(), BF16)
    return pl.pallas_call(
        functools.partial(_ffn_kernel, ft=min(f, 1024)),
        grid_spec=pltpu.PrefetchScalarGridSpec(
            num_scalar_prefetch=2,
            grid=(n_rows // rows,),
            in_specs=[pl.BlockSpec((rows * s, LANES), lambda i, be, nu: (live(i, nu), 0)),
                      pl.BlockSpec((1, d, f2), wmap),
                      pl.BlockSpec((1, 1, f), wmap), pl.BlockSpec((1, 1, f), wmap),
                      pl.BlockSpec((1, f, d), wmap), pl.BlockSpec((1, 1, d), wmap),
                      pl.BlockSpec(perm.shape, lambda i, be, nu: (0, 0))],
            out_specs=pl.BlockSpec((rows * s, LANES), lambda i, be, nu: (i, 0)),
            scratch_shapes=[pltpu.VMEM((d, f), BF16), pltpu.VMEM((d, f), BF16), pltpu.VMEM((f, d), BF16),
                            pltpu.VMEM((rows, d), F32)]),
        out_shape=jax.ShapeDtypeStruct(xs.shape, F32),
        compiler_params=_cparams("arbitrary"),
        name="expert_ffn",
    )(block_e, n_used, xs, w_gu, b_gu[:, 0::2].reshape(n_exp, 1, f), b_gu[:, 1::2].reshape(n_exp, 1, f),
      w_d, b_d.reshape(n_exp, 1, d), perm)


def _combine_kernel(dest_ref, next_ref, tg_ref, x1_ref, mod_ref, fg_ref, ys_ref, o_ref, ybuf, sem, *,
                    top_k, final_norm):
    i = pl.program_id(0)
    tt = x1_ref.shape[0]
    s = ybuf.shape[2] // tt
    slot = i % 2

    def gather(d_ref, to):
        def issue(t, carry):
            dst = pl.multiple_of(t * s, s)
            for j in range(top_k):
                src = pl.multiple_of(d_ref[0, 0, t * top_k + j] * s, s)
                pltpu.make_async_copy(ys_ref.at[pl.ds(src, s)], ybuf.at[to, j, pl.ds(dst, s)],
                                      sem.at[to]).start(priority=j % 2)
            return carry
        lax.fori_loop(0, tt, issue, 0, unroll=4)

    @pl.when(i == 0)
    def _():
        gather(dest_ref, 0)

    @pl.when(i + 1 < pl.num_programs(0))
    def _():
        gather(next_ref, 1 - slot)

    for j in range(top_k):
        pltpu.make_async_copy(ys_ref.at[pl.ds(0, tt * s)], ybuf.at[slot, j], sem.at[slot]).wait()
    tg = tg_ref[...]
    moe = tg[:, 0:1] * _tiles_to_rows(ybuf.at[slot, 0], tt, s)
    for j in range(1, top_k):
        moe = moe + tg[:, j:j + 1] * _tiles_to_rows(ybuf.at[slot, j], tt, s)
    x2 = x1_ref[...] + mod_ref[0, 5:6, :] * moe
    if final_norm:
        x2 = x2 * lax.rsqrt(jnp.mean(x2 * x2, axis=-1, keepdims=True) + EPS) * fg_ref[...]
    o_ref[...] = x2


def _combine(dest, tg, x1, mods, final_g, ys, mod_map, tt, final_norm):
    t, d = x1.shape
    steps = t // tt
    return pl.pallas_call(
        functools.partial(_combine_kernel, top_k=TOP_K, final_norm=final_norm),
        grid=(steps,),
        in_specs=[pl.BlockSpec((1, 1, tt * TOP_K), lambda i: (i, 0, 0), memory_space=pltpu.SMEM),
                  pl.BlockSpec((1, 1, tt * TOP_K), lambda i: (jnp.minimum(i + 1, steps - 1), 0, 0),
                               memory_space=pltpu.SMEM),
                  pl.BlockSpec((tt, LANES), lambda i: (i, 0)),
                  pl.BlockSpec((tt, d), lambda i: (i, 0)),
                  pl.BlockSpec((1, 6, d), mod_map),
                  pl.BlockSpec((1, d), lambda i: (0, 0)),
                  pl.BlockSpec(memory_space=pl.ANY)],
        out_specs=pl.BlockSpec((tt, d), lambda i: (i, 0)),
        out_shape=jax.ShapeDtypeStruct((t, d), F32),
        scratch_shapes=[pltpu.VMEM((2, TOP_K, tt * d // LANES, LANES), F32), pltpu.SemaphoreType.DMA((2,))],
        compiler_params=_cparams("arbitrary"),
        name="combine",
    )(dest, dest, tg, x1, mods, final_g.reshape(1, d), ys)


def _moe_plan(counts, rows, n_blocks):
    n_exp = counts.shape[0]
    padded = (counts + rows - 1) // rows * rows
    pad_end = jnp.cumsum(padded)
    block_row = jnp.arange(n_blocks, dtype=jnp.int32) * rows
    block_e = jnp.minimum(jnp.sum(pad_end[None, :] <= block_row[:, None], axis=1), n_exp - 1).astype(jnp.int32)
    n_used = (pad_end[-1:] // rows).astype(jnp.int32)
    start = pad_end - padded
    pad_start = jnp.pad(start.astype(F32), (0, LANES - n_exp)).reshape(1, LANES)
    return pad_start, block_e, n_used, (start + counts).astype(jnp.int32), (padded - counts).astype(jnp.int32)


def _sequence_mixers(z_hy, z_qk, z_v, z_o, z_g, lw, state, row_w):
    (hy_cw, hy_cb, filt_params, hy_b, ml_cw, ml_cb, ml_gb, ml_ng) = lw
    y_hy = _hyena(z_hy, hy_cw, hy_cb, filt_params, hy_b, row_w)
    ml_w = z_v.shape[-1]
    kt = _short_conv(z_qk, ml_cw, ml_cb, row_w, silu=True, col0=ml_w, ncols=ml_w, transpose=True)
    y_ml, st = _mlstm(z_qk, ml_cw, ml_cb, row_w, kt, z_v, z_o, z_g, ml_gb, ml_ng, *state)
    return y_hy, y_ml, st


def kernel(x_prompt, x_sample, state_mlstm_C, state_mlstm_n, state_mlstm_m, c, c_ctx, ada_w, ada_b, norm1_g,
           w_in, hy_conv_w, hy_conv_b, filt_w1, filt_b1, filt_w2, filt_b2, filt_w3, filt_freq, hy_bias,
           ml_conv_w, ml_conv_b, ml_gate_b, ml_norm_g, w_out, norm2_g, router_w, router_b, moe_w_gu,
           moe_b_gu, moe_w_down, moe_b_down, final_g):
    bp, lp, d = x_prompt.shape
    bs, ls, _ = x_sample.shape
    depth = ada_w.shape[0]
    heads = ml_gate_b.shape[-1]
    hy_w = hy_bias.shape[-1]
    ml_w = ml_norm_g.shape[-1]
    dh = ml_w // heads
    n_exp = router_w.shape[-1]
    t = bp * lp + bs * ls
    ng = N_DIR * 2 * heads
    seg_widths = (3 * hy_w, 2 * ml_w, ml_w, ml_w)
    n_main = 3 * hy_w + 4 * ml_w
    n_blocks = -(-(t * TOP_K) // MOE_ROWS) + n_exp

    cond = jnp.concatenate([c_ctx[None], c, jnp.zeros((SUBLANES - 1 - bs, d), F32)], axis=0)
    zero_state = (jnp.zeros((bp, N_DIR, heads, dh, dh), F32), jnp.zeros((bp, N_DIR, heads, dh), F32),
                  jnp.zeros((bp, N_DIR, heads), F32))
    xs_paths = [x_prompt, x_sample]
    path_cfg = [(0, 0, lp), (1, 1, GRID_W)]
    new_c, new_n, new_m = [], [], []
    for l in range(depth):
        mods = _ada(cond, ada_w[l], ada_b[l]).reshape(SUBLANES, 6, d)
        w_main = w_in[l][:, :n_main].astype(BF16)
        w_gate = jnp.pad(w_in[l][:, n_main:], ((0, 0), (0, LANES - ng)))
        lw = (hy_conv_w[l], hy_conv_b[l],
              (filt_w1[l], filt_b1[l], filt_w2[l], filt_b2[l], filt_w3[l], filt_freq[l]), hy_bias[l],
              ml_conv_w[l], ml_conv_b[l], ml_gate_b[l], ml_norm_g[l])
        states = [zero_state, (state_mlstm_C[:, l], state_mlstm_n[:, l], state_mlstm_m[:, l])]
        counts = jnp.zeros((SUBLANES, LANES), F32)
        routed = []
        for x3, (mod0, mod_step, row_w), state in zip(xs_paths, path_cfg, states):
            bsz, lseq, _ = x3.shape
            tm, tt = min(ROW_TILE, lseq), min(TOK_TILE, lseq)
            xf = x3.reshape(bsz * lseq, d)
            z = _inproj(xf, mods, norm1_g[l], w_main, w_gate, seg_widths,
                        _mod_index_map(mod0, mod_step, lseq // tm), tm)
            y_hy, y_ml, st = _sequence_mixers(*[a.reshape(bsz, lseq, a.shape[1]) for a in z], lw, state, row_w)
            x1, h2t, ti, tg, rk, counts = _post(
                y_hy.reshape(bsz * lseq, hy_w), y_ml.reshape(bsz * lseq, ml_w), xf, mods, w_out[l], norm2_g[l],
                router_w[l], router_b[l], counts, _mod_index_map(mod0, mod_step, lseq // tm), tm)
            routed.append((x1, h2t, ti, tg, rk, tm, tt, _mod_index_map(mod0, mod_step, lseq // tt), st))
        new_c.append(routed[0][-1][0])
        new_n.append(routed[0][-1][1])
        new_m.append(routed[0][-1][2])

        pad_start, block_e, n_used, pad_first, pad_count = _moe_plan(
            counts[0, :n_exp].astype(jnp.int32), MOE_ROWS, n_blocks)
        tt = routed[0][6]
        assert all(r[6] == tt for r in routed)
        dests = [_dest_rows(ti, rk, pad_start, tm)[:, :TOP_K].reshape(-1, 1, tt * TOP_K)
                 for _, _, ti, _, rk, tm, _, _, _ in routed]
        xs = _dispatch(jnp.concatenate(dests, axis=0), [r[1] for r in routed], pad_first, pad_count, n_used,
                       n_blocks, MOE_ROWS, tt)
        ys = _ffn(block_e, n_used, xs, moe_w_gu[l], moe_b_gu[l], moe_w_down[l], moe_b_down[l], MOE_ROWS)
        xs_paths = [
            _combine(dest, tg, x1, mods, final_g, ys, mod_map, tt, final_norm=l == depth - 1).reshape(x3.shape)
            for dest, (x1, _, _, tg, _, _, tt, mod_map, _), x3 in zip(dests, routed, xs_paths)]
    return (xs_paths[0], xs_paths[1],
            jnp.stack(new_c, axis=1), jnp.stack(new_n, axis=1), jnp.stack(new_m, axis=1))
```

```python
import functools
import math

import numpy as np
import jax
import jax.numpy as jnp
from jax import lax
from jax.experimental import pallas as pl
from jax.experimental.pallas import tpu as pltpu

F32 = jnp.float32
BF16 = jnp.bfloat16
HIGHEST = lax.Precision.HIGHEST
EPS = 1e-6

LANES = 128
SUBLANES = 8
VMEM_LIMIT_BYTES = 56 * 1024 * 1024

GRID_W = 64
ML_HEADS = 4
N_DIR = 2
HY_ORDER = 2
FILT_BANDS = 8
DECAY_TARGET = 1e-2
FAST_DECAY_PCT = 0.3
SLOW_DECAY_PCT = 1.5
TOP_K = 4
SWIGLU_LIMIT = 7.0
SWIGLU_ALPHA = 1.702

FFT_N2 = 128
MLSTM_CHUNK = 128
MLSTM_GROUP = 8
MLSTM_PAIR = 4
MOE_ROWS = 512
ROW_TILE = 512
TOK_TILE = 256


def _cparams(*sem):
    return pltpu.CompilerParams(dimension_semantics=sem, vmem_limit_bytes=VMEM_LIMIT_BYTES)


def _lane_tile(c, cap):
    return max(t for t in range(LANES, min(c, cap) + 1, LANES) if c % t == 0)


def _bdot(a, b):
    return jnp.dot(a.astype(BF16), b.astype(BF16), preferred_element_type=F32)


def _hdot(a, b):
    return jnp.dot(a, b, precision=HIGHEST, preferred_element_type=F32)


def _dot3(a, b):
    a_hi, b_hi = a.astype(BF16), b.astype(BF16)
    a_lo = (a - a_hi.astype(F32)).astype(BF16)
    b_lo = (b - b_hi.astype(F32)).astype(BF16)
    dot = functools.partial(jnp.dot, preferred_element_type=F32)
    return dot(a_hi, b_hi) + dot(a_hi, b_lo) + dot(a_lo, b_hi)


def _ada_kernel(c_ref, w_ref, b_ref, o_ref):
    c = c_ref[...]
    o_ref[...] = _hdot(c * jax.nn.sigmoid(c), w_ref[...]) + b_ref[...]


def _ada(cond, w, b):
    r, d = cond.shape
    n = w.shape[1]
    tn = _lane_tile(n, 1024)
    return pl.pallas_call(
        _ada_kernel,
        grid=(n // tn,),
        in_specs=[pl.BlockSpec((r, d), lambda j: (0, 0)),
                  pl.BlockSpec((d, tn), lambda j: (0, j)),
                  pl.BlockSpec((1, tn), lambda j: (0, j))],
        out_specs=pl.BlockSpec((r, tn), lambda j: (0, j)),
        out_shape=jax.ShapeDtypeStruct((r, n), F32),
        compiler_params=_cparams("arbitrary"),
        name="ada",
    )(cond, w, b.reshape(1, n))


def _mod_index_map(mod0, mod_step, tiles_per_seq):
    def index_map(i):
        return (mod0 + (i // tiles_per_seq) * mod_step, 0, 0)
    return index_map


def _inproj_kernel(x_ref, mod_ref, g_ref, w_ref, wg_ref, *out_refs, offsets):
    x = x_ref[...]
    h = x * lax.rsqrt(jnp.mean(x * x, axis=-1, keepdims=True) + EPS) * g_ref[...]
    h = h * (1.0 + mod_ref[0, 1:2, :]) + mod_ref[0, 0:1, :]
    hb = h.astype(BF16)
    for o_ref, (lo, hi) in zip(out_refs[:-1], offsets):
        o_ref[...] = jnp.dot(hb, w_ref[:, lo:hi], preferred_element_type=F32)
    h_lo = (h - hb.astype(F32)).astype(BF16)
    g = jnp.dot(hb, wg_ref[...], preferred_element_type=F32)
    out_refs[-1][...] = (g[:, :LANES] + g[:, LANES:]
                         + jnp.dot(h_lo, wg_ref[:, :LANES], preferred_element_type=F32))


def _inproj(x, mods, norm_g, w_main, w_gate, seg_widths, mod_map, tm):
    t, d = x.shape
    offsets, lo = [], 0
    for wd in seg_widths:
        offsets.append((lo, lo + wd))
        lo += wd
    wg_hi = w_gate.astype(BF16)
    wg = jnp.concatenate([wg_hi, (w_gate - wg_hi.astype(F32)).astype(BF16)], axis=1)
    widths = tuple(seg_widths) + (LANES,)
    return pl.pallas_call(
        functools.partial(_inproj_kernel, offsets=tuple(offsets)),
        grid=(t // tm,),
        in_specs=[pl.BlockSpec((tm, d), lambda i: (i, 0)),
                  pl.BlockSpec((1, 6, d), mod_map),
                  pl.BlockSpec((1, d), lambda i: (0, 0)),
                  pl.BlockSpec(w_main.shape, lambda i: (0, 0)),
                  pl.BlockSpec(wg.shape, lambda i: (0, 0))],
        out_specs=[pl.BlockSpec((tm, wd), lambda i: (i, 0)) for wd in widths],
        out_shape=[jax.ShapeDtypeStruct((t, wd), F32) for wd in widths],
        compiler_params=_cparams("arbitrary"),
        name="inproj",
    )(x, mods, norm_g.reshape(1, d), w_main, wg)


def _short_conv_value(x, w_ref, b_ref, row_w):
    l = x.shape[0]
    pos = lax.broadcasted_iota(jnp.int32, x.shape, 0) % row_w
    prev = jnp.where(pos == 0, 0.0, pltpu.roll(x, 1, 0))
    nxt = jnp.where(pos == row_w - 1, 0.0, pltpu.roll(x, l - 1, 0))
    return prev * w_ref[0:1, :] + x * w_ref[1:2, :] + nxt * w_ref[2:3, :] + b_ref[...]


def _short_conv_kernel(x_ref, w_ref, b_ref, o_ref, *, row_w, silu, transpose):
    y = _short_conv_value(x_ref[0], w_ref, b_ref, row_w)
    if silu:
        y = y * jax.nn.sigmoid(y)
    o_ref[0] = y.T if transpose else y


def _short_conv(x, w, b, row_w, silu, col0=0, ncols=None, transpose=False):
    bsz, l, c = x.shape
    ncols = c if ncols is None else ncols
    ct = _lane_tile(math.gcd(ncols, col0) if col0 else ncols, LANES if transpose else 512)
    j0 = col0 // ct
    out_shape, out_block, out_map = (bsz, l, ncols), (1, l, ct), lambda i, j: (i, 0, j)
    if transpose:
        out_shape, out_block, out_map = (bsz, ncols, l), (1, ct, l), lambda i, j: (i, j, 0)
    return pl.pallas_call(
        functools.partial(_short_conv_kernel, row_w=row_w, silu=silu, transpose=transpose),
        grid=(bsz, ncols // ct),
        in_specs=[pl.BlockSpec((1, l, ct), lambda i, j: (i, 0, j0 + j)),
                  pl.BlockSpec((3, ct), lambda i, j: (0, j0 + j)),
                  pl.BlockSpec((1, ct), lambda i, j: (0, j0 + j))],
        out_specs=pl.BlockSpec(out_block, out_map),
        out_shape=jax.ShapeDtypeStruct(out_shape, F32),
        compiler_params=_cparams("arbitrary", "arbitrary"),
        name="short_conv",
    )(x, w, b.reshape(1, c))


def _dft_direct_tables(l):
    n = 2 * l
    k = np.arange(n)[:, None].astype(np.float64)
    t = np.arange(n)[None, :].astype(np.float64)
    ang = 2.0 * np.pi * ((k * t) % n) / n
    cm, sm = np.cos(ang), np.sin(ang)
    fwd = np.block([[cm[:, :l], sm[:, :l]], [-sm[:, :l], cm[:, :l]]])
    filt = np.concatenate([cm, -sm], axis=0)
    return fwd, filt


def _dft_two_level_tables(l, n2):
    n = 2 * l
    n1 = n // n2
    k1 = np.arange(n1)[:, None].astype(np.float64)
    a = np.arange(n1)[None, :].astype(np.float64)
    ang1 = 2.0 * np.pi * ((k1 * a) % n1) / n1
    c1, s1 = np.cos(ang1), np.sin(ang1)
    h = n1 // 2
    m1 = np.block([[c1[:, :h], s1[:, :h]], [-s1[:, :h], c1[:, :h]]])
    m1f = np.concatenate([c1, -s1], axis=0)
    kk = (np.arange(n1)[:, None, None] + n1 * np.arange(n2)[None, :, None]).astype(np.float64)
    b = np.arange(n2)[None, None, :].astype(np.float64)
    ang = 2.0 * np.pi * ((kk * b) % n) / n
    cg, sg = np.cos(ang), np.sin(ang)
    gt = np.concatenate([np.concatenate([cg, sg], axis=2),
                         np.concatenate([-sg, cg], axis=2)], axis=1)
    return m1, m1f, gt


def _circular_lag(n0, rows, l):
    n = n0 + lax.broadcasted_iota(jnp.int32, (rows, 1), 0)
    t = jnp.where(n < l, n, 2 * l - n).astype(F32)
    return n, t, t / float(max(l - 1, 1))


def _filter_hidden_kernel(bandv_ref, w1_ref, b1_ref, w2_ref, b2_ref, freq_ref, o_ref, *, l):
    rows = o_ref.shape[0]
    _, t, t01 = _circular_lag(pl.program_id(0) * rows, rows, l)
    lane = lax.broadcasted_iota(jnp.int32, (rows, LANES), 1)
    ang = (2.0 * math.pi / l) * t * bandv_ref[...]
    feats = jnp.where(lane == 0, t01,
                      jnp.where(lane <= FILT_BANDS, jnp.cos(ang),
                                jnp.where(lane <= 2 * FILT_BANDS, -jnp.sin(ang), 0.0)))
    fr = freq_ref[...]
    h = jnp.sin(fr * (_hdot(feats, w1_ref[...]) + b1_ref[...]))
    o_ref[...] = jnp.sin(fr * (_hdot(h, w2_ref[...]) + b2_ref[...]))


def _filter_hidden(l, f_w1, f_b1, f_w2, f_b2, f_freq):
    emb, hid = f_w1.shape
    n = 2 * l
    rows = min(n, 512)
    bands = jnp.linspace(1e-4, FILT_BANDS - 1, FILT_BANDS, dtype=F32)
    bandv = jnp.zeros((1, LANES), F32).at[0, 1:1 + FILT_BANDS].set(bands)
    bandv = bandv.at[0, 1 + FILT_BANDS:1 + 2 * FILT_BANDS].set(bands)
    w1p = jnp.zeros((LANES, hid), F32).at[:emb].set(f_w1)
    c0 = lambda i: (0, 0)
    return pl.pallas_call(
        functools.partial(_filter_hidden_kernel, l=l),
        grid=(n // rows,),
        in_specs=[pl.BlockSpec((1, LANES), c0), pl.BlockSpec((LANES, hid), c0), pl.BlockSpec((1, hid), c0),
                  pl.BlockSpec((hid, hid), c0), pl.BlockSpec((1, hid), c0), pl.BlockSpec((1, hid), c0)],
        out_specs=pl.BlockSpec((rows, hid), lambda i: (i, 0)),
        out_shape=jax.ShapeDtypeStruct((n, hid), F32),
        compiler_params=_cparams("arbitrary"),
        name="filter_hidden",
    )(bandv, w1p, f_b1.reshape(1, hid), f_w2, f_b2.reshape(1, hid), f_freq.reshape(1, hid))


def _filter_rows(n0, rows, l, hid_ref, w3_ref, delta_ref):
    n, _, t01 = _circular_lag(n0, rows, l)
    h = hid_ref[pl.ds(n0, rows), :]
    hf = _bdot(h, w3_ref[0, 0])
    hb = _bdot(h, w3_ref[0, 1])
    window = jnp.exp(-t01 * delta_ref[...])
    return jnp.where(n < l, hf, jnp.where(n > l, hb, 0.0)) * window


def _filter_direct_kernel(hid_ref, w3_ref, delta_ref, ff_ref, h_ref, *, l):
    hc = _filter_rows(0, 2 * l, l, hid_ref, w3_ref, delta_ref)
    h_ref[0] = _bdot(ff_ref[...], hc) * (1.0 / (2 * l))


def _filter_two_level_kernel(hid_ref, w3_ref, delta_ref, m1f_ref, gt_ref, h_ref, hc_buf, a_buf, *, l, n2, rows):
    n = 2 * l
    n1 = n // n2

    def fill(i, carry):
        r0 = pl.multiple_of(i * rows, rows)
        hc_buf[pl.ds(r0, rows), :] = _filter_rows(r0, rows, l, hid_ref, w3_ref, delta_ref)
        return carry
    lax.fori_loop(0, n // rows, fill, 0)

    def step1(b, carry):
        col = hc_buf[pl.ds(b, n1, stride=n2), :]
        a = _bdot(m1f_ref[...], col)
        a_buf[pl.ds(b, n1, stride=2 * n2), :] = a[:n1]
        a_buf[pl.ds(n2 + b, n1, stride=2 * n2), :] = a[n1:]
        return carry
    lax.fori_loop(0, n2, step1, 0, unroll=8)

    def step2(k1, carry):
        r0 = pl.multiple_of(k1 * 2 * n2, 2 * n2)
        h_ref[0, k1] = (_bdot(gt_ref[k1], a_buf[pl.ds(r0, 2 * n2), :]) * (1.0 / n)).astype(h_ref.dtype)
        return carry
    lax.fori_loop(0, n1, step2, 0, unroll=8)


def _filter_tail_inputs(hy_w, f_w3):
    hid = f_w3.shape[0]
    w3 = f_w3.reshape(hid, HY_ORDER, N_DIR, hy_w).transpose(1, 2, 0, 3)
    max_decay = math.log(DECAY_TARGET) / FAST_DECAY_PCT
    min_decay = math.log(DECAY_TARGET) / SLOW_DECAY_PCT
    deltas = jnp.abs(jnp.linspace(min_decay, max_decay, hy_w, dtype=F32)).reshape(1, hy_w)
    return w3, deltas


def _filter_spectrum_direct(l, hy_w, filt_params, ff):
    f_w1, f_b1, f_w2, f_b2, f_w3, f_freq = filt_params
    hidden = _filter_hidden(l, f_w1, f_b1, f_w2, f_b2, f_freq)
    w3, deltas = _filter_tail_inputs(hy_w, f_w3)
    hid = f_w3.shape[0]
    n = 2 * l
    return pl.pallas_call(
        functools.partial(_filter_direct_kernel, l=l),
        grid=(HY_ORDER,),
        in_specs=[pl.BlockSpec((n, hid), lambda o: (0, 0)),
                  pl.BlockSpec((1, N_DIR, hid, hy_w), lambda o: (o, 0, 0, 0)),
                  pl.BlockSpec((1, hy_w), lambda o: (0, 0)),
                  pl.BlockSpec((2 * n, n), lambda o: (0, 0))],
        out_specs=pl.BlockSpec((1, 2 * n, hy_w), lambda o: (o, 0, 0)),
        out_shape=jax.ShapeDtypeStruct((HY_ORDER, 2 * n, hy_w), F32),
        compiler_params=_cparams("arbitrary"),
        name="filter_direct",
    )(hidden, w3, deltas, ff)


def _filter_spectrum_two_level(l, hy_w, filt_params, m1f, gt, ct):
    f_w1, f_b1, f_w2, f_b2, f_w3, f_freq = filt_params
    hidden = _filter_hidden(l, f_w1, f_b1, f_w2, f_b2, f_freq)
    w3, deltas = _filter_tail_inputs(hy_w, f_w3)
    hid = f_w3.shape[0]
    n = 2 * l
    n2 = FFT_N2
    n1 = n // n2
    return pl.pallas_call(
        functools.partial(_filter_two_level_kernel, l=l, n2=n2, rows=min(n, 512)),
        grid=(HY_ORDER, hy_w // ct),
        in_specs=[pl.BlockSpec((n, hid), lambda o, j: (0, 0)),
                  pl.BlockSpec((1, N_DIR, hid, ct), lambda o, j: (o, 0, 0, j)),
                  pl.BlockSpec((1, ct), lambda o, j: (0, j)),
                  pl.BlockSpec((2 * n1, n1), lambda o, j: (0, 0)),
                  pl.BlockSpec((n1, 2 * n2, 2 * n2), lambda o, j: (0, 0, 0))],
        out_specs=pl.BlockSpec((1, n1, 2 * n2, ct), lambda o, j: (o, 0, 0, j)),
        out_shape=jax.ShapeDtypeStruct((HY_ORDER, n1, 2 * n2, hy_w), BF16),
        scratch_shapes=[pltpu.VMEM((n, ct), F32), pltpu.VMEM((n1 * 2 * n2, ct), F32)],
        compiler_params=_cparams("arbitrary", "arbitrary"),
        name="filter_two_level",
    )(hidden, w3, deltas, m1f, gt)


def _complex_mul(x, h, half):
    xr, xi = x[:half], x[half:]
    hr, hi = h[:half], h[half:]
    return jnp.concatenate([xr * hr - xi * hi, xr * hi + xi * hr], axis=0)


def _conv_direct_kernel(z_ref, gate_ref, zw_ref, zb_ref, gw_ref, gb_ref, bias_ref, h_ref, fwd_ref, inv_ref,
                        o_ref, *, row_w, conv_z):
    l = z_ref.shape[1]
    zs = [_short_conv_value(z_ref[s], zw_ref, zb_ref, row_w) if conv_z else z_ref[s] for s in range(2)]
    z = jnp.concatenate(zs, axis=0)
    x = _bdot(fwd_ref[...], z)
    y = _bdot(inv_ref[...], _complex_mul(x, h_ref[0], 2 * l))
    bias = bias_ref[0]
    for s in range(2):
        gate = _short_conv_value(gate_ref[s], gw_ref, gb_ref, row_w)
        o_ref[s] = gate * (y[s * l:(s + 1) * l] + bias * zs[s])


def _conv_two_level_kernel(z_ref, gate_ref, bias_ref, h_ref, m1_ref, m1i_ref, gt_ref, o_ref,
                           a_buf, *, n2):
    l = z_ref.shape[1]
    n1 = 2 * l // n2
    hn = n1 // 2

    def step1(b, carry):
        za = z_ref[0, pl.ds(b, hn, stride=n2), :]
        zb = z_ref[1, pl.ds(b, hn, stride=n2), :]
        a = _bdot(m1_ref[...], jnp.concatenate([za, zb], axis=0))
        a_buf[pl.ds(b, n1, stride=2 * n2), :] = a[:n1]
        a_buf[pl.ds(n2 + b, n1, stride=2 * n2), :] = a[n1:]
        return carry
    lax.fori_loop(0, n2, step1, 0, unroll=8)

    def step2(k1, carry):
        r0 = pl.multiple_of(k1 * 2 * n2, 2 * n2)
        x = _bdot(gt_ref[k1], a_buf[pl.ds(r0, 2 * n2), :])
        y = _complex_mul(x, h_ref[0, k1].astype(F32), n2).astype(BF16)
        a_buf[pl.ds(r0, 2 * n2), :] = lax.dot_general(gt_ref[k1], y, (((0,), (0,)), ((), ())),
                                                      preferred_element_type=F32)
        return carry
    lax.fori_loop(0, n1, step2, 0, unroll=8)

    def step3(b, carry):
        br = a_buf[pl.ds(b, n1, stride=2 * n2), :]
        bi = a_buf[pl.ds(n2 + b, n1, stride=2 * n2), :]
        y = _bdot(m1i_ref[...], jnp.concatenate([br, bi], axis=0))
        o_ref[0, pl.ds(b, hn, stride=n2), :] = y[:hn]
        o_ref[1, pl.ds(b, hn, stride=n2), :] = y[hn:]
        return carry
    lax.fori_loop(0, n2, step3, 0, unroll=8)

    bias = bias_ref[0]
    for s in range(2):
        o_ref[s] = gate_ref[s] * (o_ref[s] + bias * z_ref[s])


def _long_conv_gated(u, z, z_col, gate_col, spectrum, order, bias, tables, ct, conv=None, conv_z=False):
    bsz, l, _ = u.shape
    c = spectrum.shape[-1]
    nct = c // ct
    zspec = pl.BlockSpec((2, l, ct), lambda i, j: (i, 0, z_col * nct + j))
    gspec = pl.BlockSpec((2, l, ct), lambda i, j: (i, 0, gate_col * nct + j))
    bspec = pl.BlockSpec((1, 1, ct), lambda i, j: (order, 0, j))
    ospec = pl.BlockSpec((2, l, ct), lambda i, j: (i, 0, j))
    out_shape = jax.ShapeDtypeStruct((bsz, l, c), F32)
    bias3 = bias.reshape(HY_ORDER, 1, c)
    if len(tables) == 2:
        fwd, inv = tables
        n = 2 * l
        conv_w, conv_b, row_w = conv
        conv_b = conv_b.reshape(1, -1)
        zcol = z_col if conv_z else gate_col
        wspec = lambda col, rows: pl.BlockSpec((rows, ct), lambda i, j: (0, col * nct + j))
        return pl.pallas_call(
            functools.partial(_conv_direct_kernel, row_w=row_w, conv_z=conv_z),
            grid=(bsz // 2, nct),
            in_specs=[zspec, gspec, wspec(zcol, 3), wspec(zcol, 1), wspec(gate_col, 3), wspec(gate_col, 1), bspec,
                      pl.BlockSpec((1, 2 * n, ct), lambda i, j: (order, 0, j)),
                      pl.BlockSpec(fwd.shape, lambda i, j: (0, 0)),
                      pl.BlockSpec(inv.shape, lambda i, j: (0, 0))],
            out_specs=ospec, out_shape=out_shape,
            compiler_params=_cparams("arbitrary", "arbitrary"),
            name="long_conv_direct",
        )(z, u, conv_w, conv_b, conv_w, conv_b, bias3, spectrum, fwd, inv)
    m1, m1i, gt = tables
    n2 = FFT_N2
    n1 = 2 * l // n2
    const2 = lambda i, j: (0, 0)
    const3 = lambda i, j: (0, 0, 0)
    return pl.pallas_call(
        functools.partial(_conv_two_level_kernel, n2=n2),
        grid=(nct, bsz // 2),
        in_specs=[pl.BlockSpec((2, l, ct), lambda j, i: (i, 0, z_col * nct + j)),
                  pl.BlockSpec((2, l, ct), lambda j, i: (i, 0, gate_col * nct + j)),
                  pl.BlockSpec((1, 1, ct), lambda j, i: (order, 0, j)),
                  pl.BlockSpec((1, n1, 2 * n2, ct), lambda j, i: (order, 0, 0, j)),
                  pl.BlockSpec(m1.shape, const2), pl.BlockSpec(m1i.shape, const2),
                  pl.BlockSpec(gt.shape, const3)],
        out_specs=pl.BlockSpec((2, l, ct), lambda j, i: (i, 0, j)),
        out_shape=out_shape,
        scratch_shapes=[pltpu.VMEM((n1 * 2 * n2, ct), F32)],
        compiler_params=_cparams("arbitrary", "arbitrary"),
        name="long_conv_two_level",
    )(z, u, bias3, spectrum, m1, m1i, gt)


def _hyena(z_hy, conv_w, conv_b, filt_params, hy_bias, row_w):
    bsz, l, c3 = z_hy.shape
    c = c3 // 3
    table = lambda a: jnp.asarray(a, F32).astype(BF16)
    if 2 * l // FFT_N2 <= 4:
        fwd, filt = _dft_direct_tables(l)
        tables = (table(fwd), table(fwd.T))
        spectrum = _filter_spectrum_direct(l, c, filt_params, table(filt))
        conv = (conv_w, conv_b, row_w)
        z1 = _long_conv_gated(z_hy, z_hy, 0, 1, spectrum, 0, hy_bias, tables, c, conv, conv_z=True)
        return _long_conv_gated(z_hy, z1, 0, 2, spectrum, 1, hy_bias, tables, c, conv)
    u = _short_conv(z_hy, conv_w, conv_b, row_w, silu=False)
    m1, m1f, gt = _dft_two_level_tables(l, FFT_N2)
    tables = (table(m1), table(m1.T), table(gt))
    ct = LANES
    spectrum = _filter_spectrum_two_level(l, c, filt_params, table(m1f), tables[2], ct)
    z1 = _long_conv_gated(u, u, 0, 1, spectrum, 0, hy_bias, tables, ct)
    return _long_conv_gated(u, z1, 0, 2, spectrum, 1, hy_bias, tables, ct)


def _log_sigmoid(x):
    return jnp.minimum(x, 0.0) - jnp.log1p(jnp.exp(-jnp.abs(x)))


def _split3_dot(a, b, split_lhs):
    x = a if split_lhs else b
    hi = x.astype(BF16)
    rest = x - hi.astype(F32)
    mid = rest.astype(BF16)
    parts = (hi, mid, (rest - mid.astype(F32)).astype(BF16))
    if split_lhs:
        return functools.reduce(lambda u, w: u + w, [jnp.dot(p, b, preferred_element_type=F32) for p in parts])
    return functools.reduce(lambda u, w: u + w, [jnp.dot(a, p, preferred_element_type=F32) for p in parts])


def _mlstm_recur(qb, sv, m_loc, kv, g_loc, bq, btot, cn, m):
    dh = qb.shape[1]
    inter = bq + m
    mj = jnp.maximum(m_loc, inter)
    w_int = jnp.exp(inter - mj)
    w_loc = jnp.exp(m_loc - mj)
    qc = jnp.dot(qb, cn.astype(BF16), preferred_element_type=F32)
    num = w_int * qc[:, :dh] + w_loc * sv[:, :dh]
    den = w_int * qc[:, dh:] + w_loc * sv[:, dh:]
    h = num / jnp.maximum(jnp.abs(den), jnp.exp(-mj))
    m_new = jnp.maximum(btot + m, g_loc)
    cn_new = jnp.exp(btot + m - m_new) * cn + jnp.exp(g_loc - m_new) * kv
    return h, cn_new, m_new


def _mlstm_kernel(q_ref, cw_ref, cb_ref, kt_ref, v_ref, o_ref, g_ref, gt_ref, gb_ref, gbt_ref, ng_ref,
                  c0_ref, n0_ref, m0_ref, y_ref, c_ref, n_ref, m_ref,
                  hf_buf, hb_buf, cn_buf, sv_buf, kv_buf, bq_buf, ml_buf, sc_buf, qb_buf,
                  *, chunk, heads, row_w):
    l, dh = q_ref.shape[1], q_ref.shape[2]
    nc = l // chunk
    group = sv_buf.shape[1]
    pair = math.gcd(group, MLSTM_PAIR)
    scale = dh ** -0.5

    def conv_q(j, carry):
        r0 = pl.multiple_of(j * chunk, chunk)
        x = q_ref[0, pl.ds(r0, chunk), :]
        idx = lax.broadcasted_iota(jnp.int32, x.shape, 0)
        pos = (r0 + idx) & (row_w - 1)
        before = q_ref[0, pl.ds(jnp.maximum(r0 - 1, 0), 1), :]
        after = q_ref[0, pl.ds(jnp.minimum(r0 + chunk, l - 1), 1), :]
        prev = jnp.where(idx == 0, before, pltpu.roll(x, 1, 0))
        nxt = jnp.where(idx == chunk - 1, after, pltpu.roll(x, chunk - 1, 0))
        y = (jnp.where(pos == 0, 0.0, prev) * cw_ref[0:1, :] + x * cw_ref[1:2, :]
             + jnp.where(pos == row_w - 1, 0.0, nxt) * cw_ref[2:3, :] + cb_ref[...])
        qb_buf[pl.ds(r0, chunk), :] = (y * jax.nn.sigmoid(y)).astype(BF16)
        return carry
    lax.fori_loop(0, nc, conv_q, 0, unroll=2)
    row = lax.broadcasted_iota(jnp.int32, (chunk, chunk), 0)
    col = lax.broadcasted_iota(jnp.int32, (chunk, chunk), 1)
    lower, upper = col <= row, col >= row
    tri_l, tri_u = lower.astype(BF16), upper.astype(BF16)
    gate_row = lax.broadcasted_iota(jnp.int32, (SUBLANES, chunk), 0)
    ones = jnp.ones((chunk, dh), BF16)

    sel_row = lax.broadcasted_iota(jnp.int32, (LANES, LANES), 0)
    head = pl.program_id(1)

    def chunk_start(j, d):
        return pl.multiple_of(((nc - 1 - j) if d else j) * chunk, chunk)

    def local(gp, carry, j0):
        jobs = [(gp * pair + k, d) for k in range(pair) for d in range(N_DIR)]
        r0s = [chunk_start(j0 + g, d) for g, d in jobs]
        picks = [(sel_row == (2 * d + 1) * heads + head).astype(BF16) for d in range(N_DIR)]
        lfs = [_log_sigmoid(g_ref[0, pl.ds(r0, chunk), :] + gb_ref[...]) for r0 in r0s]
        gts = [gt_ref[0, 0, :, pl.ds(r0, chunk)] + gbt_ref[0] for r0 in r0s]
        gts = [jnp.where(gate_row % 2 == 1, _log_sigmoid(gt), gt) for gt in gts]
        lfs = [_split3_dot(lf, picks[d], split_lhs=True) for lf, (_, d) in zip(lfs, jobs)]
        brows = [_split3_dot(gt, tri_l if d else tri_u, split_lhs=True)[2 * d + 1:2 * d + 2, :]
                 for gt, (_, d) in zip(gts, jobs)]
        bqs = [_split3_dot(tri_u if d else tri_l, lf, split_lhs=False) for lf, (_, d) in zip(lfs, jobs)]
        btots = [brow[:, 0:1] if d else brow[:, chunk - 1:chunk] for brow, (_, d) in zip(brows, jobs)]
        irows = [gt[2 * d:2 * d + 1, :] for gt, (_, d) in zip(gts, jobs)]
        qbs = [qb_buf[pl.ds(r0, chunk), :] for r0 in r0s]
        kts = [kt_ref[0, :, pl.ds(r0, chunk)] * scale for r0 in r0s]
        vos = [jnp.concatenate([v_ref[0, pl.ds(r0, chunk), :].astype(BF16), ones], axis=1) for r0 in r0s]
        qks = [jnp.dot(qb, kt.astype(BF16), preferred_element_type=F32) for qb, kt in zip(qbs, kts)]
        dms = [jnp.where(upper if d else lower, bq - brow + irow, -jnp.inf)
               for bq, brow, irow, (_, d) in zip(bqs, brows, irows, jobs)]
        m_locs = [jnp.max(dm, axis=-1, keepdims=True) for dm in dms]
        ss = [(qk * jnp.exp(dm - m_loc)).astype(BF16) for qk, dm, m_loc in zip(qks, dms, m_locs)]
        gls = [btot - brow + irow for btot, brow, irow in zip(btots, brows, irows)]
        g_locs = [jnp.max(gl, axis=-1, keepdims=True) for gl in gls]
        wks = [(kt * jnp.exp(gl - g_loc)).astype(BF16) for kt, gl, g_loc in zip(kts, gls, g_locs)]
        svs = [jnp.dot(s, vo, preferred_element_type=F32) for s, vo in zip(ss, vos)]
        kvs = [jnp.dot(wk, vo, preferred_element_type=F32) for wk, vo in zip(wks, vos)]
        for (g, d), sv, kv, bq, m_loc, btot, g_loc in zip(jobs, svs, kvs, bqs, m_locs, btots, g_locs):
            sv_buf[d, g] = sv
            kv_buf[d, g] = kv
            bq_buf[d, g] = bq
            ml_buf[d, g] = jnp.broadcast_to(m_loc, (chunk, LANES))
            sc_buf[d, g, 0:1, :] = jnp.broadcast_to(btot, (1, LANES))
            sc_buf[d, g, 1:2, :] = jnp.broadcast_to(g_loc, (1, LANES))
        return carry

    def recur(g, carry, j0):
        ms = list(carry)
        for d in range(N_DIR):
            r0 = chunk_start(j0 + g, d)
            h, cn, ms[d] = _mlstm_recur(
                qb_buf[pl.ds(r0, chunk), :], sv_buf[d, g], ml_buf[d, g], kv_buf[d, g],
                sc_buf[d, g, 1:2, 0:1], bq_buf[d, g], sc_buf[d, g, 0:1, 0:1], cn_buf[d], ms[d])
            cn_buf[d] = cn
            (hb_buf if d else hf_buf)[pl.ds(r0, chunk), :] = h
        return tuple(ms)

    for d in range(N_DIR):
        n_rep = jnp.broadcast_to(n0_ref[0, 0, d:d + 1, :], (dh, dh)).T
        cn_buf[d] = jnp.concatenate([c0_ref[0, d, 0], n_rep], axis=1)

    def block(jb, carry):
        j0 = jb * group
        lax.fori_loop(0, group // pair, functools.partial(local, j0=j0), 0)
        return lax.fori_loop(0, group, functools.partial(recur, j0=j0), carry, unroll=2)

    m_fin = lax.fori_loop(0, nc // group, block, (m0_ref[0, 0, 0:1, 0:1], m0_ref[0, 0, 1:2, 0:1]))
    for d in range(N_DIR):
        c_ref[0, d, 0] = cn_buf[d, :, :dh]
        n_ref[0, 0, d:d + 1, :] = cn_buf[d, :, dh:].T[0:1, :]
        m_ref[0, 0, d:d + 1, :] = jnp.broadcast_to(m_fin[d], (1, LANES))

    def finish(j, carry):
        r0 = pl.multiple_of(j * chunk, chunk)
        hs = hf_buf[pl.ds(r0, chunk), :] + hb_buf[pl.ds(r0, chunk), :]
        hs = hs * lax.rsqrt(jnp.mean(hs * hs, axis=-1, keepdims=True) + EPS) * ng_ref[...]
        y_ref[0, pl.ds(r0, chunk), :] = jax.nn.sigmoid(o_ref[0, pl.ds(r0, chunk), :]) * hs
        return carry
    lax.fori_loop(0, nc, finish, 0, unroll=2)


def _mlstm(z_qk, conv_w, conv_b, row_w, kt, z_v, z_o, z_g, gate_b, norm_g, c0, n0, m0):
    bsz, l, ml = z_v.shape
    heads = gate_b.shape[-1]
    dh = ml // heads
    chunk = MLSTM_CHUNK
    assert dh == LANES and chunk == LANES and l % chunk == 0
    assert row_w & (row_w - 1) == 0 and l % row_w == 0
    group = math.gcd(l // chunk, MLSTM_GROUP)
    n_gate = N_DIR * 2
    assert z_g.shape[-1] == LANES
    g4 = z_g[..., :n_gate * heads].reshape(bsz, l, n_gate, heads)
    ght = jnp.pad(g4.transpose(0, 3, 2, 1), ((0, 0), (0, 0), (0, SUBLANES - n_gate), (0, 0)))
    gb4 = gate_b.reshape(n_gate, heads).T
    gb = jnp.pad(gate_b.reshape(1, n_gate * heads), ((0, 0), (0, LANES - n_gate * heads)))
    gbt = jnp.broadcast_to(jnp.pad(gb4, ((0, 0), (0, SUBLANES - n_gate)))[..., None], (heads, SUBLANES, LANES))
    n0h = n0.transpose(0, 2, 1, 3)
    m0h = jnp.broadcast_to(m0.transpose(0, 2, 1)[..., None], (bsz, heads, N_DIR, LANES))
    seq = pl.BlockSpec((1, l, dh), lambda b, h: (b, 0, h))
    cspec = pl.BlockSpec((1, N_DIR, 1, dh, dh), lambda b, h: (b, 0, h, 0, 0))
    sspec = pl.BlockSpec((1, 1, N_DIR, dh), lambda b, h: (b, h, 0, 0))
    mspec = pl.BlockSpec((1, 1, N_DIR, LANES), lambda b, h: (b, h, 0, 0))
    y, c, n, m = pl.pallas_call(
        functools.partial(_mlstm_kernel, chunk=chunk, heads=heads, row_w=row_w),
        grid=(bsz, heads),
        in_specs=[seq, pl.BlockSpec((3, dh), lambda b, h: (0, h)), pl.BlockSpec((1, dh), lambda b, h: (0, h)),
                  pl.BlockSpec((1, dh, l), lambda b, h: (b, h, 0)), seq, seq,
                  pl.BlockSpec((1, l, LANES), lambda b, h: (b, 0, 0)),
                  pl.BlockSpec((1, 1, SUBLANES, l), lambda b, h: (b, h, 0, 0)),
                  pl.BlockSpec((1, LANES), lambda b, h: (0, 0)),
                  pl.BlockSpec((1, SUBLANES, LANES), lambda b, h: (h, 0, 0)),
                  pl.BlockSpec((1, dh), lambda b, h: (0, h)),
                  cspec, sspec, mspec],
        out_specs=[seq, cspec, sspec, mspec],
        out_shape=[jax.ShapeDtypeStruct((bsz, l, ml), F32),
                   jax.ShapeDtypeStruct((bsz, N_DIR, heads, dh, dh), F32),
                   jax.ShapeDtypeStruct((bsz, heads, N_DIR, dh), F32),
                   jax.ShapeDtypeStruct((bsz, heads, N_DIR, LANES), F32)],
        scratch_shapes=[pltpu.VMEM((l, dh), F32), pltpu.VMEM((l, dh), F32),
                        pltpu.VMEM((N_DIR, dh, 2 * dh), F32),
                        pltpu.VMEM((N_DIR, group, chunk, 2 * dh), F32),
                        pltpu.VMEM((N_DIR, group, dh, 2 * dh), F32),
                        pltpu.VMEM((N_DIR, group, chunk, LANES), F32),
                        pltpu.VMEM((N_DIR, group, chunk, LANES), F32),
                        pltpu.VMEM((N_DIR, group, SUBLANES, LANES), F32),
                        pltpu.VMEM((l, dh), BF16)],
        compiler_params=_cparams("arbitrary", "arbitrary"),
        name="mlstm",
    )(z_qk, conv_w, conv_b.reshape(1, -1), kt, z_v, z_o, z_g, ght, gb, gbt, norm_g.reshape(1, ml), c0, n0h, m0h)
    return y, (c, n.transpose(0, 2, 1, 3), m[..., 0].transpose(0, 2, 1))


def _rows_to_tiles(tile_ref, x):
    r, d = x.shape
    s = d // LANES
    for k in range(s):
        tile_ref[pl.ds(k, r, stride=s), :] = x[:, LANES * k:LANES * (k + 1)]


def _tiles_to_rows(tile_ref, r, s, r0=0):
    return jnp.concatenate([tile_ref[pl.ds(r0 * s + k, r, stride=s), :] for k in range(s)], axis=1)


def _lane_pack(cols, lane):
    out = jnp.zeros(lane.shape, cols[0].dtype)
    for j, colv in enumerate(cols):
        out = jnp.where(lane == j, colv, out)
    return out


def _post_kernel(yhy_ref, yml_ref, x_ref, mod_ref, wo_ref, n2g_ref, rw_ref, rb_ref, cnt0_ref,
                 x1_ref, h2_ref, ti_ref, tg_ref, rk_ref, cnt_ref, carry, *, n_exp, top_k):
    @pl.when(pl.program_id(0) == 0)
    def _():
        carry[...] = cnt0_ref[0:1, :]

    hy_w = yhy_ref.shape[1]
    tm = x_ref.shape[0]
    proj = _bdot(yhy_ref[...], wo_ref[:hy_w, :]) + _bdot(yml_ref[...], wo_ref[hy_w:, :])
    x1 = x_ref[...] + mod_ref[0, 2:3, :] * proj
    x1_ref[...] = x1
    h2 = x1 * lax.rsqrt(jnp.mean(x1 * x1, axis=-1, keepdims=True) + EPS) * n2g_ref[...]
    h2 = h2 * (1.0 + mod_ref[0, 4:5, :]) + mod_ref[0, 3:4, :]
    _rows_to_tiles(h2_ref, h2)

    lane = lax.broadcasted_iota(jnp.int32, (tm, LANES), 1)
    work = jnp.where(lane < n_exp, _dot3(h2, rw_ref[...]) + rb_ref[...], -jnp.inf)
    vals, idxs, hots = [], [], []
    for _ in range(top_k):
        mx = jnp.max(work, axis=-1, keepdims=True)
        idx = jnp.min(jnp.where(work == mx, lane, LANES), axis=-1, keepdims=True)
        hot = lane == idx
        vals.append(mx)
        idxs.append(idx)
        hots.append(hot)
        work = jnp.where(hot, -jnp.inf, work)
    exps = [jnp.exp(v - vals[0]) for v in vals]
    tot = functools.reduce(lambda a, b: a + b, exps)
    ti_ref[...] = _lane_pack(idxs, lane)
    tg_ref[...] = _lane_pack([e / tot for e in exps], lane)

    hot_sum = functools.reduce(lambda a, b: a + b, [h.astype(F32) for h in hots])
    row = lax.broadcasted_iota(jnp.int32, (tm, tm), 0)
    col = lax.broadcasted_iota(jnp.int32, (tm, tm), 1)
    before = _bdot((col < row).astype(F32), hot_sum) + carry[...]
    ranks = [jnp.sum(jnp.where(h, before, 0.0), axis=-1, keepdims=True).astype(jnp.int32) for h in hots]
    rk_ref[...] = _lane_pack(ranks, lane)
    carry[...] = carry[...] + jnp.sum(hot_sum, axis=0, keepdims=True)
    cnt_ref[...] = jnp.broadcast_to(carry[...], cnt_ref.shape)


def _post(y_hy, y_ml, x, mods, w_out, norm2_g, r_w, r_b, counts, mod_map, tm):
    t, d = x.shape
    n_exp = r_w.shape[1]
    rwp = jnp.pad(r_w, ((0, 0), (0, LANES - n_exp)))
    rbp = jnp.pad(r_b, (0, LANES - n_exp)).reshape(1, LANES)
    row = lambda wd: pl.BlockSpec((tm, wd), lambda i: (i, 0))
    const = lambda shape: pl.BlockSpec(shape, lambda i: (0, 0))
    return pl.pallas_call(
        functools.partial(_post_kernel, n_exp=n_exp, top_k=TOP_K),
        grid=(t // tm,),
        in_specs=[row(y_hy.shape[1]), row(y_ml.shape[1]), row(d),
                  pl.BlockSpec((1, 6, d), mod_map),
                  const(w_out.shape), const((1, d)), const((d, LANES)), const((1, LANES)),
                  const((SUBLANES, LANES))],
        out_specs=[row(d), pl.BlockSpec((tm * d // LANES, LANES), lambda i: (i, 0)),
                   row(LANES), row(LANES), row(LANES), const((SUBLANES, LANES))],
        out_shape=[jax.ShapeDtypeStruct((t, d), F32), jax.ShapeDtypeStruct((t * d // LANES, LANES), F32),
                   jax.ShapeDtypeStruct((t, LANES), jnp.int32), jax.ShapeDtypeStruct((t, LANES), F32),
                   jax.ShapeDtypeStruct((t, LANES), jnp.int32), jax.ShapeDtypeStruct((SUBLANES, LANES), F32)],
        scratch_shapes=[pltpu.VMEM((1, LANES), F32)],
        compiler_params=_cparams("arbitrary"),
        name="post",
    )(y_hy, y_ml, x, mods, w_out.astype(BF16), norm2_g.reshape(1, d), rwp, rbp, counts)


def _dest_kernel(ti_ref, rk_ref, ps_ref, d_ref, *, top_k):
    lane = lax.broadcasted_iota(jnp.int32, ti_ref.shape, 1)
    ti, rk = ti_ref[...], rk_ref[...]
    cols = []
    for j in range(top_k):
        start = jnp.sum(jnp.where(lane == ti[:, j:j + 1], ps_ref[...], 0.0), axis=-1, keepdims=True)
        cols.append(start.astype(jnp.int32) + rk[:, j:j + 1])
    d_ref[...] = _lane_pack(cols, lane)


def _dest_rows(ti, rk, pad_start, tm):
    t = ti.shape[0]
    row = pl.BlockSpec((tm, LANES), lambda i: (i, 0))
    return pl.pallas_call(
        functools.partial(_dest_kernel, top_k=TOP_K),
        grid=(t // tm,),
        in_specs=[row, row, pl.BlockSpec((1, LANES), lambda i: (0, 0))],
        out_specs=row,
        out_shape=jax.ShapeDtypeStruct((t, LANES), jnp.int32),
        compiler_params=_cparams("arbitrary"),
        name="dest_rows",
    )(ti, rk, pad_start)


def _dispatch_kernel(first_ref, count_ref, nu_ref, dest_ref, *refs, top_k, s, rows, tiles):
    h_refs = refs[:len(tiles)]
    xs_ref, zbuf, sem, zsem = refs[len(tiles):]
    i = pl.program_id(0)
    tt = dest_ref.shape[2] // top_k
    half = zbuf.shape[0] // s
    n_blocks = xs_ref.shape[0] // (rows * s)

    def pad_rows(e, carry, wait):
        off, n = first_ref[e], count_ref[e]
        for k in range(half.bit_length()):
            bit = half >> k

            @pl.when((n & bit) != 0)
            def _():
                dst = pl.multiple_of(off * s, s)
                copy = pltpu.make_async_copy(zbuf.at[pl.ds(0, bit * s)], xs_ref.at[pl.ds(dst, bit * s)], zsem)
                copy.wait() if wait else copy.start()
            off = off + (n & bit)
        return carry

    def spare_block(b, carry, wait):
        for part in range(2):
            dst = pl.multiple_of((b * 2 + part) * half * s, half * s)
            copy = pltpu.make_async_copy(zbuf, xs_ref.at[pl.ds(dst, half * s)], zsem)
            copy.wait() if wait else copy.start()
        return carry

    @pl.when(i == 0)
    def _():
        zbuf[...] = jnp.zeros_like(zbuf)
        for wait in (False, True):
            lax.fori_loop(0, first_ref.shape[0], functools.partial(pad_rows, wait=wait), 0)
            lax.fori_loop(nu_ref[0], n_blocks, functools.partial(spare_block, wait=wait), 0)

    def scatter(h_ref):
        def issue(t, carry):
            src = pl.multiple_of(t * s, s)
            for j in range(top_k):
                dst = pl.multiple_of(dest_ref[0, 0, t * top_k + j] * s, s)
                pltpu.make_async_copy(h_ref.at[pl.ds(src, s)], xs_ref.at[pl.ds(dst, s)], sem).start(priority=j % 2)
            return carry
        lax.fori_loop(0, tt, issue, 0, unroll=4)
        for _ in range(top_k):
            pltpu.make_async_copy(h_ref, xs_ref.at[pl.ds(0, tt * s)], sem).wait()

    lo = 0
    for h_ref, n_tiles in zip(h_refs, tiles):
        @pl.when((i >= lo) & (i < lo + n_tiles))
        def _():
            scatter(h_ref)
        lo += n_tiles


def _dispatch(dest, h2ts, pad_first, pad_count, n_used, n_blocks, rows, tt):
    s = sum(h.shape[0] for h in h2ts) * TOP_K // dest.size
    tiles = tuple(h.shape[0] // (tt * s) for h in h2ts)
    starts = [sum(tiles[:k]) for k in range(len(tiles))]
    hspec = lambda lo, n: pl.BlockSpec((tt * s, LANES), lambda i, *_: (jnp.clip(i - lo, 0, n - 1), 0))
    return pl.pallas_call(
        functools.partial(_dispatch_kernel, top_k=TOP_K, s=s, rows=rows, tiles=tiles),
        grid_spec=pltpu.PrefetchScalarGridSpec(
            num_scalar_prefetch=3,
            grid=(dest.shape[0],),
            in_specs=[pl.BlockSpec((1, 1, tt * TOP_K), lambda i, *_: (i, 0, 0), memory_space=pltpu.SMEM)]
            + [hspec(lo, n) for lo, n in zip(starts, tiles)],
            out_specs=pl.BlockSpec(memory_space=pl.ANY),
            scratch_shapes=[pltpu.VMEM((rows // 2 * s, LANES), F32),
                            pltpu.SemaphoreType.DMA(()), pltpu.SemaphoreType.DMA(())]),
        out_shape=jax.ShapeDtypeStruct((n_blocks * rows * s, LANES), F32),
        compiler_params=_cparams("arbitrary"),
        name="dispatch",
    )(pad_first, pad_count, n_used, dest, *h2ts)


def _deinterleave_table():
    p = np.zeros((2 * LANES, 2 * LANES), np.float32)
    j = np.arange(LANES)
    p[2 * j, j] = 1.0
    p[2 * j + 1, LANES + j] = 1.0
    return p


def _ffn_kernel(be_ref, nu_ref, xs_ref, wgu_ref, bg_ref, bl_ref, wd_ref, bd_ref, perm_ref, ys_ref,
                wg_buf, wl_buf, wd_buf, acc_buf, *, ft):
    i = pl.program_id(0)
    f = wg_buf.shape[1]
    rows = acc_buf.shape[0]
    live = i < nu_ref[0]

    @pl.when(live & ((i == 0) | (be_ref[i] != be_ref[jnp.maximum(i - 1, 0)])))
    def _():
        for j in range(f // LANES):
            blk = wgu_ref[0, :, 2 * LANES * j:2 * LANES * (j + 1)].astype(BF16)
            split = jnp.dot(blk, perm_ref[...], preferred_element_type=F32)
            wg_buf[:, LANES * j:LANES * (j + 1)] = split[:, :LANES].astype(BF16)
            wl_buf[:, LANES * j:LANES * (j + 1)] = split[:, LANES:].astype(BF16)
        wd_buf[...] = wd_ref[0].astype(BF16)

    @pl.when(live)
    def _():
        x = _tiles_to_rows(xs_ref, rows, xs_ref.shape[0] // rows).astype(BF16)
        for j, f0 in enumerate(range(0, f, ft)):
            g = jnp.dot(x, wg_buf[:, f0:f0 + ft], preferred_element_type=F32) + bg_ref[0, :, f0:f0 + ft]
            lin = jnp.dot(x, wl_buf[:, f0:f0 + ft], preferred_element_type=F32) + bl_ref[0, :, f0:f0 + ft]
            gate = jnp.minimum(g, SWIGLU_LIMIT)
            lin = jnp.clip(lin, -SWIGLU_LIMIT, SWIGLU_LIMIT)
            act = (lin + 1.0) * gate * jax.nn.sigmoid(SWIGLU_ALPHA * gate)
            part = jnp.dot(act.astype(BF16), wd_buf[f0:f0 + ft, :], preferred_element_type=F32)
            if j == 0:
                acc_buf[...] = part + bd_ref[0]
            else:
                acc_buf[...] += part
        _rows_to_tiles(ys_ref, acc_buf[...])

    @pl.when(jnp.logical_not(live))
    def _():
        ys_ref[...] = jnp.zeros_like(ys_ref)


def _ffn(block_e, n_used, xs, w_gu, b_gu, w_d, b_d, rows):
    n_exp, d, f2 = w_gu.shape
    f = f2 // 2
    s = d // LANES
    n_rows = xs.shape[0] // s
    live = lambda i, nu: jnp.minimum(i, nu[0] - 1)
    wmap = lambda i, be, nu: (be[live(i, nu)], 0, 0)
    perm = jnp.asarray(_deinterleave_table(), BF16)
    return pl.pallas_call(
        functools.partial(_ffn_kernel, ft=min(f, 1024)),
        grid_spec=pltpu.PrefetchScalarGridSpec(
            num_scalar_prefetch=2,
            grid=(n_rows // rows,),
            in_specs=[pl.BlockSpec((rows * s, LANES), lambda i, be, nu: (live(i, nu), 0)),
                      pl.BlockSpec((1, d, f2), wmap),
                      pl.BlockSpec((1, 1, f), wmap), pl.BlockSpec((1, 1, f), wmap),
                      pl.BlockSpec((1, f, d), wmap), pl.BlockSpec((1, 1, d), wmap),
                      pl.BlockSpec(perm.shape, lambda i, be, nu: (0, 0))],
            out_specs=pl.BlockSpec((rows * s, LANES), lambda i, be, nu: (i, 0)),
            scratch_shapes=[pltpu.VMEM((d, f), BF16), pltpu.VMEM((d, f), BF16), pltpu.VMEM((f, d), BF16),
                            pltpu.VMEM((rows, d), F32)]),
        out_shape=jax.ShapeDtypeStruct(xs.shape, F32),
        compiler_params=_cparams("arbitrary"),
        name="expert_ffn",
    )(block_e, n_used, xs, w_gu, b_gu[:, 0::2].reshape(n_exp, 1, f), b_gu[:, 1::2].reshape(n_exp, 1, f),
      w_d, b_d.reshape(n_exp, 1, d), perm)


def _combine_kernel(dest_ref, next_ref, tg_ref, x1_ref, mod_ref, fg_ref, ys_ref, o_ref, ybuf, sem, *,
                    top_k, final_norm):
    i = pl.program_id(0)
    tt = x1_ref.shape[0]
    s = ybuf.shape[2] // tt
    slot = i % 2

    def gather(d_ref, to):
        def issue(t, carry):
            dst = pl.multiple_of(t * s, s)
            for j in range(top_k):
                src = pl.multiple_of(d_ref[0, 0, t * top_k + j] * s, s)
                pltpu.make_async_copy(ys_ref.at[pl.ds(src, s)], ybuf.at[to, j, pl.ds(dst, s)],
                                      sem.at[to]).start(priority=j % 2)
            return carry
        lax.fori_loop(0, tt, issue, 0, unroll=4)

    @pl.when(i == 0)
    def _():
        gather(dest_ref, 0)

    @pl.when(i + 1 < pl.num_programs(0))
    def _():
        gather(next_ref, 1 - slot)

    for j in range(top_k):
        pltpu.make_async_copy(ys_ref.at[pl.ds(0, tt * s)], ybuf.at[slot, j], sem.at[slot]).wait()
    tg = tg_ref[...]
    moe = tg[:, 0:1] * _tiles_to_rows(ybuf.at[slot, 0], tt, s)
    for j in range(1, top_k):
        moe = moe + tg[:, j:j + 1] * _tiles_to_rows(ybuf.at[slot, j], tt, s)
    x2 = x1_ref[...] + mod_ref[0, 5:6, :] * moe
    if final_norm:
        x2 = x2 * lax.rsqrt(jnp.mean(x2 * x2, axis=-1, keepdims=True) + EPS) * fg_ref[...]
    o_ref[...] = x2


def _combine(dest, tg, x1, mods, final_g, ys, mod_map, tt, final_norm):
    t, d = x1.shape
    steps = t // tt
    return pl.pallas_call(
        functools.partial(_combine_kernel, top_k=TOP_K, final_norm=final_norm),
        grid=(steps,),
        in_specs=[pl.BlockSpec((1, 1, tt * TOP_K), lambda i: (i, 0, 0), memory_space=pltpu.SMEM),
                  pl.BlockSpec((1, 1, tt * TOP_K), lambda i: (jnp.minimum(i + 1, steps - 1), 0, 0),
                               memory_space=pltpu.SMEM),
                  pl.BlockSpec((tt, LANES), lambda i: (i, 0)),
                  pl.BlockSpec((tt, d), lambda i: (i, 0)),
                  pl.BlockSpec((1, 6, d), mod_map),
                  pl.BlockSpec((1, d), lambda i: (0, 0)),
                  pl.BlockSpec(memory_space=pl.ANY)],
        out_specs=pl.BlockSpec((tt, d), lambda i: (i, 0)),
        out_shape=jax.ShapeDtypeStruct((t, d), F32),
        scratch_shapes=[pltpu.VMEM((2, TOP_K, tt * d // LANES, LANES), F32), pltpu.SemaphoreType.DMA((2,))],
        compiler_params=_cparams("arbitrary"),
        name="combine",
    )(dest, dest, tg, x1, mods, final_g.reshape(1, d), ys)


def _moe_plan(counts, rows, n_blocks):
    n_exp = counts.shape[0]
    padded = (counts + rows - 1) // rows * rows
    pad_end = jnp.cumsum(padded)
    block_row = jnp.arange(n_blocks, dtype=jnp.int32) * rows
    block_e = jnp.minimum(jnp.sum(pad_end[None, :] <= block_row[:, None], axis=1), n_exp - 1).astype(jnp.int32)
    n_used = (pad_end[-1:] // rows).astype(jnp.int32)
    start = pad_end - padded
    pad_start = jnp.pad(start.astype(F32), (0, LANES - n_exp)).reshape(1, LANES)
    return pad_start, block_e, n_used, (start + counts).astype(jnp.int32), (padded - counts).astype(jnp.int32)


def _sequence_mixers(z_hy, z_qk, z_v, z_o, z_g, lw, state, row_w):
    (hy_cw, hy_cb, filt_params, hy_b, ml_cw, ml_cb, ml_gb, ml_ng) = lw
    y_hy = _hyena(z_hy, hy_cw, hy_cb, filt_params, hy_b, row_w)
    ml_w = z_v.shape[-1]
    kt = _short_conv(z_qk, ml_cw, ml_cb, row_w, silu=True, col0=ml_w, ncols=ml_w, transpose=True)
    y_ml, st = _mlstm(z_qk, ml_cw, ml_cb, row_w, kt, z_v, z_o, z_g, ml_gb, ml_ng, *state)
    return y_hy, y_ml, st


def kernel(x_prompt, x_sample, state_mlstm_C, state_mlstm_n, state_mlstm_m, c, c_ctx, ada_w, ada_b, norm1_g,
           w_in, hy_conv_w, hy_conv_b, filt_w1, filt_b1, filt_w2, filt_b2, filt_w3, filt_freq, hy_bias,
           ml_conv_w, ml_conv_b, ml_gate_b, ml_norm_g, w_out, norm2_g, router_w, router_b, moe_w_gu,
           moe_b_gu, moe_w_down, moe_b_down, final_g):
    bp, lp, d = x_prompt.shape
    bs, ls, _ = x_sample.shape
    depth = ada_w.shape[0]
    heads = ml_gate_b.shape[-1]
    hy_w = hy_bias.shape[-1]
    ml_w = ml_norm_g.shape[-1]
    dh = ml_w // heads
    n_exp = router_w.shape[-1]
    t = bp * lp + bs * ls
    ng = N_DIR * 2 * heads
    seg_widths = (3 * hy_w, 2 * ml_w, ml_w, ml_w)
    n_main = 3 * hy_w + 4 * ml_w
    n_blocks = -(-(t * TOP_K) // MOE_ROWS) + n_exp

    cond = jnp.concatenate([c_ctx[None], c, jnp.zeros((SUBLANES - 1 - bs, d), F32)], axis=0)
    zero_state = (jnp.zeros((bp, N_DIR, heads, dh, dh), F32), jnp.zeros((bp, N_DIR, heads, dh), F32),
                  jnp.zeros((bp, N_DIR, heads), F32))
    xs_paths = [x_prompt, x_sample]
    path_cfg = [(0, 0, lp), (1, 1, GRID_W)]
    new_c, new_n, new_m = [], [], []
    for l in range(depth):
        mods = _ada(cond, ada_w[l], ada_b[l]).reshape(SUBLANES, 6, d)
        w_main = w_in[l][:, :n_main].astype(BF16)
        w_gate = jnp.pad(w_in[l][:, n_main:], ((0, 0), (0, LANES - ng)))
        lw = (hy_conv_w[l], hy_conv_b[l],
              (filt_w1[l], filt_b1[l], filt_w2[l], filt_b2[l], filt_w3[l], filt_freq[l]), hy_bias[l],
              ml_conv_w[l], ml_conv_b[l], ml_gate_b[l], ml_norm_g[l])
        states = [zero_state, (state_mlstm_C[:, l], state_mlstm_n[:, l], state_mlstm_m[:, l])]
        counts = jnp.zeros((SUBLANES, LANES), F32)
        routed = []
        for x3, (mod0, mod_step, row_w), state in zip(xs_paths, path_cfg, states):
            bsz, lseq, _ = x3.shape
            tm, tt = min(ROW_TILE, lseq), min(TOK_TILE, lseq)
            xf = x3.reshape(bsz * lseq, d)
            z = _inproj(xf, mods, norm1_g[l], w_main, w_gate, seg_widths,
                        _mod_index_map(mod0, mod_step, lseq // tm), tm)
            y_hy, y_ml, st = _sequence_mixers(*[a.reshape(bsz, lseq, a.shape[1]) for a in z], lw, state, row_w)
            x1, h2t, ti, tg, rk, counts = _post(
                y_hy.reshape(bsz * lseq, hy_w), y_ml.reshape(bsz * lseq, ml_w), xf, mods, w_out[l], norm2_g[l],
                router_w[l], router_b[l], counts, _mod_index_map(mod0, mod_step, lseq // tm), tm)
            routed.append((x1, h2t, ti, tg, rk, tm, tt, _mod_index_map(mod0, mod_step, lseq // tt), st))
        new_c.append(routed[0][-1][0])
        new_n.append(routed[0][-1][1])
        new_m.append(routed[0][-1][2])

        pad_start, block_e, n_used, pad_first, pad_count = _moe_plan(
            counts[0, :n_exp].astype(jnp.int32), MOE_ROWS, n_blocks)
        tt = routed[0][6]
        assert all(r[6] == tt for r in routed)
        dests = [_dest_rows(ti, rk, pad_start, tm)[:, :TOP_K].reshape(-1, 1, tt * TOP_K)
                 for _, _, ti, _, rk, tm, _, _, _ in routed]
        xs = _dispatch(jnp.concatenate(dests, axis=0), [r[1] for r in routed], pad_first, pad_count, n_used,
                       n_blocks, MOE_ROWS, tt)
        ys = _ffn(block_e, n_used, xs, moe_w_gu[l], moe_b_gu[l], moe_w_down[l], moe_b_down[l], MOE_ROWS)
        xs_paths = [
            _combine(dest, tg, x1, mods, final_g, ys, mod_map, tt, final_norm=l == depth - 1).reshape(x3.shape)
            for dest, (x1, _, _, tg, _, _, tt, mod_map, _), x3 in zip(dests, routed, xs_paths)]
    return (xs_paths[0], xs_paths[1],
            jnp.stack(new_c, axis=1), jnp.stack(new_n, axis=1), jnp.stack(new_m, axis=1))
```

```python
import functools
import math

import numpy as np
import jax
import jax.numpy as jnp
from jax import lax
from jax.experimental import pallas as pl
from jax.experimental.pallas import tpu as pltpu

F32 = jnp.float32
BF16 = jnp.bfloat16
HIGHEST = lax.Precision.HIGHEST
EPS = 1e-6

LANES = 128
SUBLANES = 8
VMEM_LIMIT_BYTES = 56 * 1024 * 1024

GRID_W = 64
ML_HEADS = 4
N_DIR = 2
HY_ORDER = 2
FILT_BANDS = 8
DECAY_TARGET = 1e-2
FAST_DECAY_PCT = 0.3
SLOW_DECAY_PCT = 1.5
TOP_K = 4
SWIGLU_LIMIT = 7.0
SWIGLU_ALPHA = 1.702

FFT_N2 = 128
MLSTM_CHUNK = 128
MLSTM_GROUP = 8
MLSTM_PAIR = 4
MOE_ROWS = 512
ROW_TILE = 512
TOK_TILE = 256


def _cparams(*sem):
    return pltpu.CompilerParams(dimension_semantics=sem, vmem_limit_bytes=VMEM_LIMIT_BYTES)


def _lane_tile(c, cap):
    return max(t for t in range(LANES, min(c, cap) + 1, LANES) if c % t == 0)


def _bdot(a, b):
    return jnp.dot(a.astype(BF16), b.astype(BF16), preferred_element_type=F32)


def _hdot(a, b):
    return jnp.dot(a, b, precision=HIGHEST, preferred_element_type=F32)


def _dot3(a, b):
    a_hi, b_hi = a.astype(BF16), b.astype(BF16)
    a_lo = (a - a_hi.astype(F32)).astype(BF16)
    b_lo = (b - b_hi.astype(F32)).astype(BF16)
    dot = functools.partial(jnp.dot, preferred_element_type=F32)
    return dot(a_hi, b_hi) + dot(a_hi, b_lo) + dot(a_lo, b_hi)


def _ada_kernel(c_ref, w_ref, b_ref, o_ref):
    c = c_ref[...]
    o_ref[...] = _hdot(c * jax.nn.sigmoid(c), w_ref[...]) + b_ref[...]


def _ada(cond, w, b):
    r, d = cond.shape
    n = w.shape[1]
    tn = _lane_tile(n, 1024)
    return pl.pallas_call(
        _ada_kernel,
        grid=(n // tn,),
        in_specs=[pl.BlockSpec((r, d), lambda j: (0, 0)),
                  pl.BlockSpec((d, tn), lambda j: (0, j)),
                  pl.BlockSpec((1, tn), lambda j: (0, j))],
        out_specs=pl.BlockSpec((r, tn), lambda j: (0, j)),
        out_shape=jax.ShapeDtypeStruct((r, n), F32),
        compiler_params=_cparams("arbitrary"),
        name="ada",
    )(cond, w, b.reshape(1, n))


def _mod_index_map(mod0, mod_step, tiles_per_seq):
    def index_map(i):
        return (mod0 + (i // tiles_per_seq) * mod_step, 0, 0)
    return index_map


def _inproj_kernel(x_ref, mod_ref, g_ref, w_ref, wg_ref, *out_refs, offsets):
    x = x_ref[...]
    h = x * lax.rsqrt(jnp.mean(x * x, axis=-1, keepdims=True) + EPS) * g_ref[...]
    h = h * (1.0 + mod_ref[0, 1:2, :]) + mod_ref[0, 0:1, :]
    hb = h.astype(BF16)
    for o_ref, (lo, hi) in zip(out_refs[:-1], offsets):
        o_ref[...] = jnp.dot(hb, w_ref[:, lo:hi], preferred_element_type=F32)
    h_lo = (h - hb.astype(F32)).astype(BF16)
    g = jnp.dot(hb, wg_ref[...], preferred_element_type=F32)
    out_refs[-1][...] = (g[:, :LANES] + g[:, LANES:]
                         + jnp.dot(h_lo, wg_ref[:, :LANES], preferred_element_type=F32))


def _inproj(x, mods, norm_g, w_main, w_gate, seg_widths, mod_map, tm):
    t, d = x.shape
    offsets, lo = [], 0
    for wd in seg_widths:
        offsets.append((lo, lo + wd))
        lo += wd
    wg_hi = w_gate.astype(BF16)
    wg = jnp.concatenate([wg_hi, (w_gate - wg_hi.astype(F32)).astype(BF16)], axis=1)
    widths = tuple(seg_widths) + (LANES,)
    return pl.pallas_call(
        functools.partial(_inproj_kernel, offsets=tuple(offsets)),
        grid=(t // tm,),
        in_specs=[pl.BlockSpec((tm, d), lambda i: (i, 0)),
                  pl.BlockSpec((1, 6, d), mod_map),
                  pl.BlockSpec((1, d), lambda i: (0, 0)),
                  pl.BlockSpec(w_main.shape, lambda i: (0, 0)),
                  pl.BlockSpec(wg.shape, lambda i: (0, 0))],
        out_specs=[pl.BlockSpec((tm, wd), lambda i: (i, 0)) for wd in widths],
        out_shape=[jax.ShapeDtypeStruct((t, wd), F32) for wd in widths],
        compiler_params=_cparams("arbitrary"),
        name="inproj",
    )(x, mods, norm_g.reshape(1, d), w_main, wg)


def _short_conv_value(x, w_ref, b_ref, row_w):
    l = x.shape[0]
    pos = lax.broadcasted_iota(jnp.int32, x.shape, 0) % row_w
    prev = jnp.where(pos == 0, 0.0, pltpu.roll(x, 1, 0))
    nxt = jnp.where(pos == row_w - 1, 0.0, pltpu.roll(x, l - 1, 0))
    return prev * w_ref[0:1, :] + x * w_ref[1:2, :] + nxt * w_ref[2:3, :] + b_ref[...]


def _short_conv_kernel(x_ref, w_ref, b_ref, o_ref, *, row_w, silu, transpose):
    y = _short_conv_value(x_ref[0], w_ref, b_ref, row_w)
    if silu:
        y = y * jax.nn.sigmoid(y)
    o_ref[0] = y.T if transpose else y


def _short_conv(x, w, b, row_w, silu, col0=0, ncols=None, transpose=False):
    bsz, l, c = x.shape
    ncols = c if ncols is None else ncols
    ct = _lane_tile(math.gcd(ncols, col0) if col0 else ncols, LANES if transpose else 512)
    j0 = col0 // ct
    out_shape, out_block, out_map = (bsz, l, ncols), (1, l, ct), lambda i, j: (i, 0, j)
    if transpose:
        out_shape, out_block, out_map = (bsz, ncols, l), (1, ct, l), lambda i, j: (i, j, 0)
    return pl.pallas_call(
        functools.partial(_short_conv_kernel, row_w=row_w, silu=silu, transpose=transpose),
        grid=(bsz, ncols // ct),
        in_specs=[pl.BlockSpec((1, l, ct), lambda i, j: (i, 0, j0 + j)),
                  pl.BlockSpec((3, ct), lambda i, j: (0, j0 + j)),
                  pl.BlockSpec((1, ct), lambda i, j: (0, j0 + j))],
        out_specs=pl.BlockSpec(out_block, out_map),
        out_shape=jax.ShapeDtypeStruct(out_shape, F32),
        compiler_params=_cparams("arbitrary", "arbitrary"),
        name="short_conv",
    )(x, w, b.reshape(1, c))


def _dft_direct_tables(l):
    n = 2 * l
    k = np.arange(n)[:, None].astype(np.float64)
    t = np.arange(n)[None, :].astype(np.float64)
    ang = 2.0 * np.pi * ((k * t) % n) / n
    cm, sm = np.cos(ang), np.sin(ang)
    fwd = np.block([[cm[:, :l], sm[:, :l]], [-sm[:, :l], cm[:, :l]]])
    filt = np.concatenate([cm, -sm], axis=0)
    return fwd, filt


def _dft_two_level_tables(l, n2):
    n = 2 * l
    n1 = n // n2
    k1 = np.arange(n1)[:, None].astype(np.float64)
    a = np.arange(n1)[None, :].astype(np.float64)
    ang1 = 2.0 * np.pi * ((k1 * a) % n1) / n1
    c1, s1 = np.cos(ang1), np.sin(ang1)
    h = n1 // 2
    m1 = np.block([[c1[:, :h], s1[:, :h]], [-s1[:, :h], c1[:, :h]]])
    m1f = np.concatenate([c1, -s1], axis=0)
    kk = (np.arange(n1)[:, None, None] + n1 * np.arange(n2)[None, :, None]).astype(np.float64)
    b = np.arange(n2)[None, None, :].astype(np.float64)
    ang = 2.0 * np.pi * ((kk * b) % n) / n
    cg, sg = np.cos(ang), np.sin(ang)
    gt = np.concatenate([np.concatenate([cg, sg], axis=2),
                         np.concatenate([-sg, cg], axis=2)], axis=1)
    return m1, m1f, gt


def _circular_lag(n0, rows, l):
    n = n0 + lax.broadcasted_iota(jnp.int32, (rows, 1), 0)
    t = jnp.where(n < l, n, 2 * l - n).astype(F32)
    return n, t, t / float(max(l - 1, 1))


def _filter_hidden_kernel(bandv_ref, w1_ref, b1_ref, w2_ref, b2_ref, freq_ref, o_ref, *, l):
    rows = o_ref.shape[0]
    _, t, t01 = _circular_lag(pl.program_id(0) * rows, rows, l)
    lane = lax.broadcasted_iota(jnp.int32, (rows, LANES), 1)
    ang = (2.0 * math.pi / l) * t * bandv_ref[...]
    feats = jnp.where(lane == 0, t01,
                      jnp.where(lane <= FILT_BANDS, jnp.cos(ang),
                                jnp.where(lane <= 2 * FILT_BANDS, -jnp.sin(ang), 0.0)))
    fr = freq_ref[...]
    h = jnp.sin(fr * (_hdot(feats, w1_ref[...]) + b1_ref[...]))
    o_ref[...] = jnp.sin(fr * (_hdot(h, w2_ref[...]) + b2_ref[...]))


def _filter_hidden(l, f_w1, f_b1, f_w2, f_b2, f_freq):
    emb, hid = f_w1.shape
    n = 2 * l
    rows = min(n, 512)
    bands = jnp.linspace(1e-4, FILT_BANDS - 1, FILT_BANDS, dtype=F32)
    bandv = jnp.zeros((1, LANES), F32).at[0, 1:1 + FILT_BANDS].set(bands)
    bandv = bandv.at[0, 1 + FILT_BANDS:1 + 2 * FILT_BANDS].set(bands)
    w1p = jnp.zeros((LANES, hid), F32).at[:emb].set(f_w1)
    c0 = lambda i: (0, 0)
    return pl.pallas_call(
        functools.partial(_filter_hidden_kernel, l=l),
        grid=(n // rows,),
        in_specs=[pl.BlockSpec((1, LANES), c0), pl.BlockSpec((LANES, hid), c0), pl.BlockSpec((1, hid), c0),
                  pl.BlockSpec((hid, hid), c0), pl.BlockSpec((1, hid), c0), pl.BlockSpec((1, hid), c0)],
        out_specs=pl.BlockSpec((rows, hid), lambda i: (i, 0)),
        out_shape=jax.ShapeDtypeStruct((n, hid), F32),
        compiler_params=_cparams("arbitrary"),
        name="filter_hidden",
    )(bandv, w1p, f_b1.reshape(1, hid), f_w2, f_b2.reshape(1, hid), f_freq.reshape(1, hid))


def _filter_rows(n0, rows, l, hid_ref, w3_ref, delta_ref):
    n, _, t01 = _circular_lag(n0, rows, l)
    h = hid_ref[pl.ds(n0, rows), :]
    hf = _bdot(h, w3_ref[0, 0])
    hb = _bdot(h, w3_ref[0, 1])
    window = jnp.exp(-t01 * delta_ref[...])
    return jnp.where(n < l, hf, jnp.where(n > l, hb, 0.0)) * window


def _filter_direct_kernel(hid_ref, w3_ref, delta_ref, ff_ref, h_ref, *, l):
    hc = _filter_rows(0, 2 * l, l, hid_ref, w3_ref, delta_ref)
    h_ref[0] = _bdot(ff_ref[...], hc) * (1.0 / (2 * l))


def _filter_two_level_kernel(hid_ref, w3_ref, delta_ref, m1f_ref, gt_ref, h_ref, hc_buf, a_buf, *, l, n2, rows):
    n = 2 * l
    n1 = n // n2

    def fill(i, carry):
        r0 = pl.multiple_of(i * rows, rows)
        hc_buf[pl.ds(r0, rows), :] = _filter_rows(r0, rows, l, hid_ref, w3_ref, delta_ref)
        return carry
    lax.fori_loop(0, n // rows, fill, 0)

    def step1(b, carry):
        col = hc_buf[pl.ds(b, n1, stride=n2), :]
        a = _bdot(m1f_ref[...], col)
        a_buf[pl.ds(b, n1, stride=2 * n2), :] = a[:n1]
        a_buf[pl.ds(n2 + b, n1, stride=2 * n2), :] = a[n1:]
        return carry
    lax.fori_loop(0, n2, step1, 0, unroll=8)

    def step2(k1, carry):
        r0 = pl.multiple_of(k1 * 2 * n2, 2 * n2)
        h_ref[0, k1] = (_bdot(gt_ref[k1], a_buf[pl.ds(r0, 2 * n2), :]) * (1.0 / n)).astype(h_ref.dtype)
        return carry
    lax.fori_loop(0, n1, step2, 0, unroll=8)


def _filter_tail_inputs(hy_w, f_w3):
    hid = f_w3.shape[0]
    w3 = f_w3.reshape(hid, HY_ORDER, N_DIR, hy_w).transpose(1, 2, 0, 3)
    max_decay = math.log(DECAY_TARGET) / FAST_DECAY_PCT
    min_decay = math.log(DECAY_TARGET) / SLOW_DECAY_PCT
    deltas = jnp.abs(jnp.linspace(min_decay, max_decay, hy_w, dtype=F32)).reshape(1, hy_w)
    return w3, deltas


def _filter_spectrum_direct(l, hy_w, filt_params, ff):
    f_w1, f_b1, f_w2, f_b2, f_w3, f_freq = filt_params
    hidden = _filter_hidden(l, f_w1, f_b1, f_w2, f_b2, f_freq)
    w3, deltas = _filter_tail_inputs(hy_w, f_w3)
    hid = f_w3.shape[0]
    n = 2 * l
    return pl.pallas_call(
        functools.partial(_filter_direct_kernel, l=l),
        grid=(HY_ORDER,),
        in_specs=[pl.BlockSpec((n, hid), lambda o: (0, 0)),
                  pl.BlockSpec((1, N_DIR, hid, hy_w), lambda o: (o, 0, 0, 0)),
                  pl.BlockSpec((1, hy_w), lambda o: (0, 0)),
                  pl.BlockSpec((2 * n, n), lambda o: (0, 0))],
        out_specs=pl.BlockSpec((1, 2 * n, hy_w), lambda o: (o, 0, 0)),
        out_shape=jax.ShapeDtypeStruct((HY_ORDER, 2 * n, hy_w), F32),
        compiler_params=_cparams("arbitrary"),
        name="filter_direct",
    )(hidden, w3, deltas, ff)


def _filter_spectrum_two_level(l, hy_w, filt_params, m1f, gt, ct):
    f_w1, f_b1, f_w2, f_b2, f_w3, f_freq = filt_params
    hidden = _filter_hidden(l, f_w1, f_b1, f_w2, f_b2, f_freq)
    w3, deltas = _filter_tail_inputs(hy_w, f_w3)
    hid = f_w3.shape[0]
    n = 2 * l
    n2 = FFT_N2
    n1 = n // n2
    return pl.pallas_call(
        functools.partial(_filter_two_level_kernel, l=l, n2=n2, rows=min(n, 512)),
        grid=(HY_ORDER, hy_w // ct),
        in_specs=[pl.BlockSpec((n, hid), lambda o, j: (0, 0)),
                  pl.BlockSpec((1, N_DIR, hid, ct), lambda o, j: (o, 0, 0, j)),
                  pl.BlockSpec((1, ct), lambda o, j: (0, j)),
                  pl.BlockSpec((2 * n1, n1), lambda o, j: (0, 0)),
                  pl.BlockSpec((n1, 2 * n2, 2 * n2), lambda o, j: (0, 0, 0))],
        out_specs=pl.BlockSpec((1, n1, 2 * n2, ct), lambda o, j: (o, 0, 0, j)),
        out_shape=jax.ShapeDtypeStruct((HY_ORDER, n1, 2 * n2, hy_w), BF16),
        scratch_shapes=[pltpu.VMEM((n, ct), F32), pltpu.VMEM((n1 * 2 * n2, ct), F32)],
        compiler_params=_cparams("arbitrary", "arbitrary"),
        name="filter_two_level",
    )(hidden, w3, deltas, m1f, gt)


def _complex_mul(x, h, half):
    xr, xi = x[:half], x[half:]
    hr, hi = h[:half], h[half:]
    return jnp.concatenate([xr * hr - xi * hi, xr * hi + xi * hr], axis=0)


def _conv_direct_kernel(z_ref, gate_ref, zw_ref, zb_ref, gw_ref, gb_ref, bias_ref, h_ref, fwd_ref, inv_ref,
                        o_ref, *, row_w, conv_z):
    l = z_ref.shape[1]
    zs = [_short_conv_value(z_ref[s], zw_ref, zb_ref, row_w) if conv_z else z_ref[s] for s in range(2)]
    z = jnp.concatenate(zs, axis=0)
    x = _bdot(fwd_ref[...], z)
    y = _bdot(inv_ref[...], _complex_mul(x, h_ref[0], 2 * l))
    bias = bias_ref[0]
    for s in range(2):
        gate = _short_conv_value(gate_ref[s], gw_ref, gb_ref, row_w)
        o_ref[s] = gate * (y[s * l:(s + 1) * l] + bias * zs[s])


def _conv_two_level_kernel(z_ref, gate_ref, bias_ref, h_ref, m1_ref, m1i_ref, gt_ref, o_ref,
                           a_buf, *, n2):
    l = z_ref.shape[1]
    n1 = 2 * l // n2
    hn = n1 // 2

    def step1(b, carry):
        za = z_ref[0, pl.ds(b, hn, stride=n2), :]
        zb = z_ref[1, pl.ds(b, hn, stride=n2), :]
        a = _bdot(m1_ref[...], jnp.concatenate([za, zb], axis=0))
        a_buf[pl.ds(b, n1, stride=2 * n2), :] = a[:n1]
        a_buf[pl.ds(n2 + b, n1, stride=2 * n2), :] = a[n1:]
        return carry
    lax.fori_loop(0, n2, step1, 0, unroll=8)

    def step2(k1, carry):
        r0 = pl.multiple_of(k1 * 2 * n2, 2 * n2)
        x = _bdot(gt_ref[k1], a_buf[pl.ds(r0, 2 * n2), :])
        y = _complex_mul(x, h_ref[0, k1].astype(F32), n2).astype(BF16)
        a_buf[pl.ds(r0, 2 * n2), :] = lax.dot_general(gt_ref[k1], y, (((0,), (0,)), ((), ())),
                                                      preferred_element_type=F32)
        return carry
    lax.fori_loop(0, n1, step2, 0, unroll=8)

    def step3(b, carry):
        br = a_buf[pl.ds(b, n1, stride=2 * n2), :]
        bi = a_buf[pl.ds(n2 + b, n1, stride=2 * n2), :]
        y = _bdot(m1i_ref[...], jnp.concatenate([br, bi], axis=0))
        o_ref[0, pl.ds(b, hn, stride=n2), :] = y[:hn]
        o_ref[1, pl.ds(b, hn, stride=n2), :] = y[hn:]
        return carry
    lax.fori_loop(0, n2, step3, 0, unroll=8)

    bias = bias_ref[0]
    for s in range(2):
        o_ref[s] = gate_ref[s] * (o_ref[s] + bias * z_ref[s])


def _long_conv_gated(u, z, z_col, gate_col, spectrum, order, bias, tables, ct, conv=None, conv_z=False):
    bsz, l, _ = u.shape
    c = spectrum.shape[-1]
    nct = c // ct
    zspec = pl.BlockSpec((2, l, ct), lambda i, j: (i, 0, z_col * nct + j))
    gspec = pl.BlockSpec((2, l, ct), lambda i, j: (i, 0, gate_col * nct + j))
    bspec = pl.BlockSpec((1, 1, ct), lambda i, j: (order, 0, j))
    ospec = pl.BlockSpec((2, l, ct), lambda i, j: (i, 0, j))
    out_shape = jax.ShapeDtypeStruct((bsz, l, c), F32)
    bias3 = bias.reshape(HY_ORDER, 1, c)
    if len(tables) == 2:
        fwd, inv = tables
        n = 2 * l
        conv_w, conv_b, row_w = conv
        conv_b = conv_b.reshape(1, -1)
        zcol = z_col if conv_z else gate_col
        wspec = lambda col, rows: pl.BlockSpec((rows, ct), lambda i, j: (0, col * nct + j))
        return pl.pallas_call(
            functools.partial(_conv_direct_kernel, row_w=row_w, conv_z=conv_z),
            grid=(bsz // 2, nct),
            in_specs=[zspec, gspec, wspec(zcol, 3), wspec(zcol, 1), wspec(gate_col, 3), wspec(gate_col, 1), bspec,
                      pl.BlockSpec((1, 2 * n, ct), lambda i, j: (order, 0, j)),
                      pl.BlockSpec(fwd.shape, lambda i, j: (0, 0)),
                      pl.BlockSpec(inv.shape, lambda i, j: (0, 0))],
            out_specs=ospec, out_shape=out_shape,
            compiler_params=_cparams("arbitrary", "arbitrary"),
            name="long_conv_direct",
        )(z, u, conv_w, conv_b, conv_w, conv_b, bias3, spectrum, fwd, inv)
    m1, m1i, gt = tables
    n2 = FFT_N2
    n1 = 2 * l // n2
    const2 = lambda i, j: (0, 0)
    const3 = lambda i, j: (0, 0, 0)
    return pl.pallas_call(
        functools.partial(_conv_two_level_kernel, n2=n2),
        grid=(nct, bsz // 2),
        in_specs=[pl.BlockSpec((2, l, ct), lambda j, i: (i, 0, z_col * nct + j)),
                  pl.BlockSpec((2, l, ct), lambda j, i: (i, 0, gate_col * nct + j)),
                  pl.BlockSpec((1, 1, ct), lambda j, i: (order, 0, j)),
                  pl.BlockSpec((1, n1, 2 * n2, ct), lambda j, i: (order, 0, 0, j)),
                  pl.BlockSpec(m1.shape, const2), pl.BlockSpec(m1i.shape, const2),
                  pl.BlockSpec(gt.shape, const3)],
        out_specs=pl.BlockSpec((2, l, ct), lambda j, i: (i, 0, j)),
        out_shape=out_shape,
        scratch_shapes=[pltpu.VMEM((n1 * 2 * n2, ct), F32)],
        compiler_params=_cparams("arbitrary", "arbitrary"),
        name="long_conv_two_level",
    )(z, u, bias3, spectrum, m1, m1i, gt)


def _hyena(z_hy, conv_w, conv_b, filt_params, hy_bias, row_w):
    bsz, l, c3 = z_hy.shape
    c = c3 // 3
    table = lambda a: jnp.asarray(a, F32).astype(BF16)
    if 2 * l // FFT_N2 <= 4:
        fwd, filt = _dft_direct_tables(l)
        tables = (table(fwd), table(fwd.T))
        spectrum = _filter_spectrum_direct(l, c, filt_params, table(filt))
        conv = (conv_w, conv_b, row_w)
        z1 = _long_conv_gated(z_hy, z_hy, 0, 1, spectrum, 0, hy_bias, tables, c, conv, conv_z=True)
        return _long_conv_gated(z_hy, z1, 0, 2, spectrum, 1, hy_bias, tables, c, conv)
    u = _short_conv(z_hy, conv_w, conv_b, row_w, silu=False)
    m1, m1f, gt = _dft_two_level_tables(l, FFT_N2)
    tables = (table(m1), table(m1.T), table(gt))
    ct = LANES
    spectrum = _filter_spectrum_two_level(l, c, filt_params, table(m1f), tables[2], ct)
    z1 = _long_conv_gated(u, u, 0, 1, spectrum, 0, hy_bias, tables, ct)
    return _long_conv_gated(u, z1, 0, 2, spectrum, 1, hy_bias, tables, ct)


def _log_sigmoid(x):
    return jnp.minimum(x, 0.0) - jnp.log1p(jnp.exp(-jnp.abs(x)))


def _split3_dot(a, b, split_lhs):
    x = a if split_lhs else b
    hi = x.astype(BF16)
    rest = x - hi.astype(F32)
    mid = rest.astype(BF16)
    parts = (hi, mid, (rest - mid.astype(F32)).astype(BF16))
    if split_lhs:
        return functools.reduce(lambda u, w: u + w, [jnp.dot(p, b, preferred_element_type=F32) for p in parts])
    return functools.reduce(lambda u, w: u + w, [jnp.dot(a, p, preferred_element_type=F32) for p in parts])


def _mlstm_recur(qb, sv, m_loc, kv, g_loc, bq, btot, cn, m):
    dh = qb.shape[1]
    inter = bq + m
    mj = jnp.maximum(m_loc, inter)
    w_int = jnp.exp(inter - mj)
    w_loc = jnp.exp(m_loc - mj)
    qc = jnp.dot(qb, cn.astype(BF16), preferred_element_type=F32)
    num = w_int * qc[:, :dh] + w_loc * sv[:, :dh]
    den = w_int * qc[:, dh:] + w_loc * sv[:, dh:]
    h = num / jnp.maximum(jnp.abs(den), jnp.exp(-mj))
    m_new = jnp.maximum(btot + m, g_loc)
    cn_new = jnp.exp(btot + m - m_new) * cn + jnp.exp(g_loc - m_new) * kv
    return h, cn_new, m_new


def _mlstm_kernel(q_ref, cw_ref, cb_ref, kt_ref, v_ref, o_ref, g_ref, gt_ref, gb_ref, gbt_ref, ng_ref,
                  c0_ref, n0_ref, m0_ref, y_ref, c_ref, n_ref, m_ref,
                  hf_buf, hb_buf, cn_buf, sv_buf, kv_buf, bq_buf, ml_buf, sc_buf, qb_buf,
                  *, chunk, heads, row_w):
    l, dh = q_ref.shape[1], q_ref.shape[2]
    nc = l // chunk
    group = sv_buf.shape[1]
    pair = math.gcd(group, MLSTM_PAIR)
    scale = dh ** -0.5

    def conv_q(j, carry):
        r0 = pl.multiple_of(j * chunk, chunk)
        x = q_ref[0, pl.ds(r0, chunk), :]
        idx = lax.broadcasted_iota(jnp.int32, x.shape, 0)
        pos = (r0 + idx) & (row_w - 1)
        before = q_ref[0, pl.ds(jnp.maximum(r0 - 1, 0), 1), :]
        after = q_ref[0, pl.ds(jnp.minimum(r0 + chunk, l - 1), 1), :]
        prev = jnp.where(idx == 0, before, pltpu.roll(x, 1, 0))
        nxt = jnp.where(idx == chunk - 1, after, pltpu.roll(x, chunk - 1, 0))
        y = (jnp.where(pos == 0, 0.0, prev) * cw_ref[0:1, :] + x * cw_ref[1:2, :]
             + jnp.where(pos == row_w - 1, 0.0, nxt) * cw_ref[2:3, :] + cb_ref[...])
        qb_buf[pl.ds(r0, chunk), :] = (y * jax.nn.sigmoid(y)).astype(BF16)
        return carry
    lax.fori_loop(0, nc, conv_q, 0, unroll=2)
    row = lax.broadcasted_iota(jnp.int32, (chunk, chunk), 0)
    col = lax.broadcasted_iota(jnp.int32, (chunk, chunk), 1)
    lower, upper = col <= row, col >= row
    tri_l, tri_u = lower.astype(BF16), upper.astype(BF16)
    gate_row = lax.broadcasted_iota(jnp.int32, (SUBLANES, chunk), 0)
    ones = jnp.ones((chunk, dh), BF16)

    sel_row = lax.broadcasted_iota(jnp.int32, (LANES, LANES), 0)
    head = pl.program_id(1)

    def chunk_start(j, d):
        return pl.multiple_of(((nc - 1 - j) if d else j) * chunk, chunk)

    def local(gp, carry, j0):
        jobs = [(gp * pair + k, d) for k in range(pair) for d in range(N_DIR)]
        r0s = [chunk_start(j0 + g, d) for g, d in jobs]
        picks = [(sel_row == (2 * d + 1) * heads + head).astype(BF16) for d in range(N_DIR)]
        lfs = [_log_sigmoid(g_ref[0, pl.ds(r0, chunk), :] + gb_ref[...]) for r0 in r0s]
        gts = [gt_ref[0, 0, :, pl.ds(r0, chunk)] + gbt_ref[0] for r0 in r0s]
        gts = [jnp.where(gate_row % 2 == 1, _log_sigmoid(gt), gt) for gt in gts]
        lfs = [_split3_dot(lf, picks[d], split_lhs=True) for lf, (_, d) in zip(lfs, jobs)]
        brows = [_split3_dot(gt, tri_l if d else tri_u, split_lhs=True)[2 * d + 1:2 * d + 2, :]
                 for gt, (_, d) in zip(gts, jobs)]
        bqs = [_split3_dot(tri_u if d else tri_l, lf, split_lhs=False) for lf, (_, d) in zip(lfs, jobs)]
        btots = [brow[:, 0:1] if d else brow[:, chunk - 1:chunk] for brow, (_, d) in zip(brows, jobs)]
        irows = [gt[2 * d:2 * d + 1, :] for gt, (_, d) in zip(gts, jobs)]
        qbs = [qb_buf[pl.ds(r0, chunk), :] for r0 in r0s]
        kts = [kt_ref[0, :, pl.ds(r0, chunk)] * scale for r0 in r0s]
        vos = [jnp.concatenate([v_ref[0, pl.ds(r0, chunk), :].astype(BF16), ones], axis=1) for r0 in r0s]
        qks = [jnp.dot(qb, kt.astype(BF16), preferred_element_type=F32) for qb, kt in zip(qbs, kts)]
        dms = [jnp.where(upper if d else lower, bq - brow + irow, -jnp.inf)
               for bq, brow, irow, (_, d) in zip(bqs, brows, irows, jobs)]
        m_locs = [jnp.max(dm, axis=-1, keepdims=True) for dm in dms]
        ss = [(qk * jnp.exp(dm - m_loc)).astype(BF16) for qk, dm, m_loc in zip(qks, dms, m_locs)]
        gls = [btot - brow + irow for btot, brow, irow in zip(btots, brows, irows)]
        g_locs = [jnp.max(gl, axis=-1, keepdims=True) for gl in gls]
        wks = [(kt * jnp.exp(gl - g_loc)).astype(BF16) for kt, gl, g_loc in zip(kts, gls, g_locs)]
        svs = [jnp.dot(s, vo, preferred_element_type=F32) for s, vo in zip(ss, vos)]
        kvs = [jnp.dot(wk, vo, preferred_element_type=F32) for wk, vo in zip(wks, vos)]
        for (g, d), sv, kv, bq, m_loc, btot, g_loc in zip(jobs, svs, kvs, bqs, m_locs, btots, g_locs):
            sv_buf[d, g] = sv
            kv_buf[d, g] = kv
            bq_buf[d, g] = bq
            ml_buf[d, g] = jnp.broadcast_to(m_loc, (chunk, LANES))
            sc_buf[d, g, 0:1, :] = jnp.broadcast_to(btot, (1, LANES))
            sc_buf[d, g, 1:2, :] = jnp.broadcast_to(g_loc, (1, LANES))
        return carry

    def recur(g, carry, j0):
        ms = list(carry)
        for d in range(N_DIR):
            r0 = chunk_start(j0 + g, d)
            h, cn, ms[d] = _mlstm_recur(
                qb_buf[pl.ds(r0, chunk), :], sv_buf[d, g], ml_buf[d, g], kv_buf[d, g],
                sc_buf[d, g, 1:2, 0:1], bq_buf[d, g], sc_buf[d, g, 0:1, 0:1], cn_buf[d], ms[d])
            cn_buf[d] = cn
            (hb_buf if d else hf_buf)[pl.ds(r0, chunk), :] = h
        return tuple(ms)

    for d in range(N_DIR):
        n_rep = jnp.broadcast_to(n0_ref[0, 0, d:d + 1, :], (dh, dh)).T
        cn_buf[d] = jnp.concatenate([c0_ref[0, d, 0], n_rep], axis=1)

    def block(jb, carry):
        j0 = jb * group
        lax.fori_loop(0, group // pair, functools.partial(local, j0=j0), 0)
        return lax.fori_loop(0, group, functools.partial(recur, j0=j0), carry, unroll=2)

    m_fin = lax.fori_loop(0, nc // group, block, (m0_ref[0, 0, 0:1, 0:1], m0_ref[0, 0, 1:2, 0:1]))
    for d in range(N_DIR):
        c_ref[0, d, 0] = cn_buf[d, :, :dh]
        n_ref[0, 0, d:d + 1, :] = cn_buf[d, :, dh:].T[0:1, :]
        m_ref[0, 0, d:d + 1, :] = jnp.broadcast_to(m_fin[d], (1, LANES))

    def finish(j, carry):
        r0 = pl.multiple_of(j * chunk, chunk)
        hs = hf_buf[pl.ds(r0, chunk), :] + hb_buf[pl.ds(r0, chunk), :]
        hs = hs * lax.rsqrt(jnp.mean(hs * hs, axis=-1, keepdims=True) + EPS) * ng_ref[...]
        y_ref[0, pl.ds(r0, chunk), :] = jax.nn.sigmoid(o_ref[0, pl.ds(r0, chunk), :]) * hs
        return carry
    lax.fori_loop(0, nc, finish, 0, unroll=2)


def _mlstm(z_qk, conv_w, conv_b, row_w, kt, z_v, z_o, z_g, gate_b, norm_g, c0, n0, m0):
    bsz, l, ml = z_v.shape
    heads = gate_b.shape[-1]
    dh = ml // heads
    chunk = MLSTM_CHUNK
    assert dh == LANES and chunk == LANES and l % chunk == 0
    assert row_w & (row_w - 1) == 0 and l % row_w == 0
    group = math.gcd(l // chunk, MLSTM_GROUP)
    n_gate = N_DIR * 2
    assert z_g.shape[-1] == LANES
    g4 = z_g[..., :n_gate * heads].reshape(bsz, l, n_gate, heads)
    ght = jnp.pad(g4.transpose(0, 3, 2, 1), ((0, 0), (0, 0), (0, SUBLANES - n_gate), (0, 0)))
    gb4 = gate_b.reshape(n_gate, heads).T
    gb = jnp.pad(gate_b.reshape(1, n_gate * heads), ((0, 0), (0, LANES - n_gate * heads)))
    gbt = jnp.broadcast_to(jnp.pad(gb4, ((0, 0), (0, SUBLANES - n_gate)))[..., None], (heads, SUBLANES, LANES))
    n0h = n0.transpose(0, 2, 1, 3)
    m0h = jnp.broadcast_to(m0.transpose(0, 2, 1)[..., None], (bsz, heads, N_DIR, LANES))
    seq = pl.BlockSpec((1, l, dh), lambda b, h: (b, 0, h))
    cspec = pl.BlockSpec((1, N_DIR, 1, dh, dh), lambda b, h: (b, 0, h, 0, 0))
    sspec = pl.BlockSpec((1, 1, N_DIR, dh), lambda b, h: (b, h, 0, 0))
    mspec = pl.BlockSpec((1, 1, N_DIR, LANES), lambda b, h: (b, h, 0, 0))
    y, c, n, m = pl.pallas_call(
        functools.partial(_mlstm_kernel, chunk=chunk, heads=heads, row_w=row_w),
        grid=(bsz, heads),
        in_specs=[seq, pl.BlockSpec((3, dh), lambda b, h: (0, h)), pl.BlockSpec((1, dh), lambda b, h: (0, h)),
                  pl.BlockSpec((1, dh, l), lambda b, h: (b, h, 0)), seq, seq,
                  pl.BlockSpec((1, l, LANES), lambda b, h: (b, 0, 0)),
                  pl.BlockSpec((1, 1, SUBLANES, l), lambda b, h: (b, h, 0, 0)),
                  pl.BlockSpec((1, LANES), lambda b, h: (0, 0)),
                  pl.BlockSpec((1, SUBLANES, LANES), lambda b, h: (h, 0, 0)),
                  pl.BlockSpec((1, dh), lambda b, h: (0, h)),
                  cspec, sspec, mspec],
        out_specs=[seq, cspec, sspec, mspec],
        out_shape=[jax.ShapeDtypeStruct((bsz, l, ml), F32),
                   jax.ShapeDtypeStruct((bsz, N_DIR, heads, dh, dh), F32),
                   jax.ShapeDtypeStruct((bsz, heads, N_DIR, dh), F32),
                   jax.ShapeDtypeStruct((bsz, heads, N_DIR, LANES), F32)],
        scratch_shapes=[pltpu.VMEM((l, dh), F32), pltpu.VMEM((l, dh), F32),
                        pltpu.VMEM((N_DIR, dh, 2 * dh), F32),
                        pltpu.VMEM((N_DIR, group, chunk, 2 * dh), F32),
                        pltpu.VMEM((N_DIR, group, dh, 2 * dh), F32),
                        pltpu.VMEM((N_DIR, group, chunk, LANES), F32),
                        pltpu.VMEM((N_DIR, group, chunk, LANES), F32),
                        pltpu.VMEM((N_DIR, group, SUBLANES, LANES), F32),
                        pltpu.VMEM((l, dh), BF16)],
        compiler_params=_cparams("arbitrary", "arbitrary"),
        name="mlstm",
    )(z_qk, conv_w, conv_b.reshape(1, -1), kt, z_v, z_o, z_g, ght, gb, gbt, norm_g.reshape(1, ml), c0, n0h, m0h)
    return y, (c, n.transpose(0, 2, 1, 3), m[..., 0].transpose(0, 2, 1))


def _rows_to_tiles(tile_ref, x):
    r, d = x.shape
    s = d // LANES
    for k in range(s):
        tile_ref[pl.ds(k, r, stride=s), :] = x[:, LANES * k:LANES * (k + 1)]


def _tiles_to_rows(tile_ref, r, s, r0=0):
    return jnp.concatenate([tile_ref[pl.ds(r0 * s + k, r, stride=s), :] for k in range(s)], axis=1)


def _lane_pack(cols, lane):
    out = jnp.zeros(lane.shape, cols[0].dtype)
    for j, colv in enumerate(cols):
        out = jnp.where(lane == j, colv, out)
    return out


def _post_kernel(yhy_ref, yml_ref, x_ref, mod_ref, wo_ref, n2g_ref, rw_ref, rb_ref, cnt0_ref,
                 x1_ref, h2_ref, ti_ref, tg_ref, rk_ref, cnt_ref, carry, *, n_exp, top_k):
    @pl.when(pl.program_id(0) == 0)
    def _():
        carry[...] = cnt0_ref[0:1, :]

    hy_w = yhy_ref.shape[1]
    tm = x_ref.shape[0]
    proj = _bdot(yhy_ref[...], wo_ref[:hy_w, :]) + _bdot(yml_ref[...], wo_ref[hy_w:, :])
    x1 = x_ref[...] + mod_ref[0, 2:3, :] * proj
    x1_ref[...] = x1
    h2 = x1 * lax.rsqrt(jnp.mean(x1 * x1, axis=-1, keepdims=True) + EPS) * n2g_ref[...]
    h2 = h2 * (1.0 + mod_ref[0, 4:5, :]) + mod_ref[0, 3:4, :]
    _rows_to_tiles(h2_ref, h2)

    lane = lax.broadcasted_iota(jnp.int32, (tm, LANES), 1)
    work = jnp.where(lane < n_exp, _dot3(h2, rw_ref[...]) + rb_ref[...], -jnp.inf)
    vals, idxs, hots = [], [], []
    for _ in range(top_k):
        mx = jnp.max(work, axis=-1, keepdims=True)
        idx = jnp.min(jnp.where(work == mx, lane, LANES), axis=-1, keepdims=True)
        hot = lane == idx
        vals.append(mx)
        idxs.append(idx)
        hots.append(hot)
        work = jnp.where(hot, -jnp.inf, work)
    exps = [jnp.exp(v - vals[0]) for v in vals]
    tot = functools.reduce(lambda a, b: a + b, exps)
    ti_ref[...] = _lane_pack(idxs, lane)
    tg_ref[...] = _lane_pack([e / tot for e in exps], lane)

    hot_sum = functools.reduce(lambda a, b: a + b, [h.astype(F32) for h in hots])
    row = lax.broadcasted_iota(jnp.int32, (tm, tm), 0)
    col = lax.broadcasted_iota(jnp.int32, (tm, tm), 1)
    before = _bdot((col < row).astype(F32), hot_sum) + carry[...]
    ranks = [jnp.sum(jnp.where(h, before, 0.0), axis=-1, keepdims=True).astype(jnp.int32) for h in hots]
    rk_ref[...] = _lane_pack(ranks, lane)
    carry[...] = carry[...] + jnp.sum(hot_sum, axis=0, keepdims=True)
    cnt_ref[...] = jnp.broadcast_to(carry[...], cnt_ref.shape)


def _post(y_hy, y_ml, x, mods, w_out, norm2_g, r_w, r_b, counts, mod_map, tm):
    t, d = x.shape
    n_exp = r_w.shape[1]
    rwp = jnp.pad(r_w, ((0, 0), (0, LANES - n_exp)))
    rbp = jnp.pad(r_b, (0, LANES - n_exp)).reshape(1, LANES)
    row = lambda wd: pl.BlockSpec((tm, wd), lambda i: (i, 0))
    const = lambda shape: pl.BlockSpec(shape, lambda i: (0, 0))
    return pl.pallas_call(
        functools.partial(_post_kernel, n_exp=n_exp, top_k=TOP_K),
        grid=(t // tm,),
        in_specs=[row(y_hy.shape[1]), row(y_ml.shape[1]), row(d),
                  pl.BlockSpec((1, 6, d), mod_map),
                  const(w_out.shape), const((1, d)), const((d, LANES)), const((1, LANES)),
                  const((SUBLANES, LANES))],
        out_specs=[row(d), pl.BlockSpec((tm * d // LANES, LANES), lambda i: (i, 0)),
                   row(LANES), row(LANES), row(LANES), const((SUBLANES, LANES))],
        out_shape=[jax.ShapeDtypeStruct((t, d), F32), jax.ShapeDtypeStruct((t * d // LANES, LANES), F32),
                   jax.ShapeDtypeStruct((t, LANES), jnp.int32), jax.ShapeDtypeStruct((t, LANES), F32),
                   jax.ShapeDtypeStruct((t, LANES), jnp.int32), jax.ShapeDtypeStruct((SUBLANES, LANES), F32)],
        scratch_shapes=[pltpu.VMEM((1, LANES), F32)],
        compiler_params=_cparams("arbitrary"),
        name="post",
    )(y_hy, y_ml, x, mods, w_out.astype(BF16), norm2_g.reshape(1, d), rwp, rbp, counts)


def _dest_kernel(ti_ref, rk_ref, ps_ref, d_ref, *, top_k):
    lane = lax.broadcasted_iota(jnp.int32, ti_ref.shape, 1)
    ti, rk = ti_ref[...], rk_ref[...]
    cols = []
    for j in range(top_k):
        start = jnp.sum(jnp.where(lane == ti[:, j:j + 1], ps_ref[...], 0.0), axis=-1, keepdims=True)
        cols.append(start.astype(jnp.int32) + rk[:, j:j + 1])
    d_ref[...] = _lane_pack(cols, lane)


def _dest_rows(ti, rk, pad_start, tm):
    t = ti.shape[0]
    row = pl.BlockSpec((tm, LANES), lambda i: (i, 0))
    return pl.pallas_call(
        functools.partial(_dest_kernel, top_k=TOP_K),
        grid=(t // tm,),
        in_specs=[row, row, pl.BlockSpec((1, LANES), lambda i: (0, 0))],
        out_specs=row,
        out_shape=jax.ShapeDtypeStruct((t, LANES), jnp.int32),
        compiler_params=_cparams("arbitrary"),
        name="dest_rows",
    )(ti, rk, pad_start)


def _dispatch_kernel(first_ref, count_ref, nu_ref, dest_ref, *refs, top_k, s, rows, tiles):
    h_refs = refs[:len(tiles)]
    xs_ref, zbuf, hbuf, load_sem, sem, zsem = refs[len(tiles):]
    i = pl.program_id(0)
    steps = pl.num_programs(0)
    tt = dest_ref.shape[2] // top_k
    half = zbuf.shape[0] // s
    n_blocks = xs_ref.shape[0] // (rows * s)

    def pad_rows(e, carry, wait):
        off, n = first_ref[e], count_ref[e]
        for k in range(half.bit_length()):
            bit = half >> k

            @pl.when((n & bit) != 0)
            def _():
                dst = pl.multiple_of(off * s, s)
                copy = pltpu.make_async_copy(zbuf.at[pl.ds(0, bit * s)], xs_ref.at[pl.ds(dst, bit * s)], zsem)
                copy.wait() if wait else copy.start()
            off = off + (n & bit)
        return carry

    def spare_block(b, carry, wait):
        for part in range(2):
            dst = pl.multiple_of((b * 2 + part) * half * s, half * s)
            copy = pltpu.make_async_copy(zbuf, xs_ref.at[pl.ds(dst, half * s)], zsem)
            copy.wait() if wait else copy.start()
        return carry

    @pl.when(i == 0)
    def _():
        zbuf[...] = jnp.zeros_like(zbuf)
        for wait in (False, True):
            lax.fori_loop(0, first_ref.shape[0], functools.partial(pad_rows, wait=wait), 0)
            lax.fori_loop(nu_ref[0], n_blocks, functools.partial(spare_block, wait=wait), 0)

    def load(t, slot, start):
        if not start:
            pltpu.make_async_copy(h_refs[0].at[pl.ds(0, tt * s)], hbuf.at[slot], load_sem.at[slot]).wait()
            return
        lo = 0
        for h_ref, n_tiles in zip(h_refs, tiles):
            @pl.when((t >= lo) & (t < lo + n_tiles))
            def _():
                src = pl.multiple_of((t - lo) * tt * s, tt * s)
                pltpu.make_async_copy(h_ref.at[pl.ds(src, tt * s)], hbuf.at[slot], load_sem.at[slot]).start()
            lo += n_tiles

    def drain(slot):
        for _ in range(top_k):
            pltpu.make_async_copy(hbuf.at[slot], xs_ref.at[pl.ds(0, tt * s)], sem.at[slot]).wait()

    slot = i % 3
    ahead = (i + 1) % 3

    @pl.when(i == 0)
    def _():
        load(i, slot, True)

    @pl.when(i >= 2)
    def _():
        drain(ahead)

    @pl.when(i + 1 < steps)
    def _():
        load(i + 1, ahead, True)
    load(i, slot, False)

    def issue(t, carry):
        src = pl.multiple_of(t * s, s)
        for j in range(top_k):
            dst = pl.multiple_of(dest_ref[0, 0, t * top_k + j] * s, s)
            pltpu.make_async_copy(hbuf.at[slot, pl.ds(src, s)], xs_ref.at[pl.ds(dst, s)],
                                  sem.at[slot]).start(priority=j % 2)
        return carry
    lax.fori_loop(0, tt, issue, 0, unroll=4)

    @pl.when(i == steps - 1)
    def _():
        @pl.when(steps >= 2)
        def _():
            drain((i + 2) % 3)
        drain(slot)


def _dispatch(dest, h2ts, pad_first, pad_count, n_used, n_blocks, rows, tt):
    s = sum(h.shape[0] for h in h2ts) * TOP_K // dest.size
    tiles = tuple(h.shape[0] // (tt * s) for h in h2ts)
    return pl.pallas_call(
        functools.partial(_dispatch_kernel, top_k=TOP_K, s=s, rows=rows, tiles=tiles),
        grid_spec=pltpu.PrefetchScalarGridSpec(
            num_scalar_prefetch=3,
            grid=(dest.shape[0],),
            in_specs=[pl.BlockSpec((1, 1, tt * TOP_K), lambda i, *_: (i, 0, 0), memory_space=pltpu.SMEM)]
            + [pl.BlockSpec(memory_space=pl.ANY) for _ in h2ts],
            out_specs=pl.BlockSpec(memory_space=pl.ANY),
            scratch_shapes=[pltpu.VMEM((rows // 2 * s, LANES), F32), pltpu.VMEM((3, tt * s, LANES), F32),
                            pltpu.SemaphoreType.DMA((3,)), pltpu.SemaphoreType.DMA((3,)),
                            pltpu.SemaphoreType.DMA(())]),
        out_shape=jax.ShapeDtypeStruct((n_blocks * rows * s, LANES), F32),
        compiler_params=_cparams("arbitrary"),
        name="dispatch",
    )(pad_first, pad_count, n_used, dest, *h2ts)


def _deinterleave_table():
    p = np.zeros((2 * LANES, 2 * LANES), np.float32)
    j = np.arange(LANES)
    p[2 * j, j] = 1.0
    p[2 * j + 1, LANES + j] = 1.0
    return p


def _ffn_kernel(be_ref, nu_ref, xs_ref, wgu_ref, bg_ref, bl_ref, wd_ref, bd_ref, perm_ref, ys_ref,
                wg_buf, wl_buf, wd_buf, acc_buf, *, ft):
    i = pl.program_id(0)
    f = wg_buf.shape[1]
    rows = acc_buf.shape[0]
    live = i < nu_ref[0]

    @pl.when(live & ((i == 0) | (be_ref[i] != be_ref[jnp.maximum(i - 1, 0)])))
    def _():
        for j in range(f // LANES):
            blk = wgu_ref[0, :, 2 * LANES * j:2 * LANES * (j + 1)].astype(BF16)
            split = jnp.dot(blk, perm_ref[...], preferred_element_type=F32)
            wg_buf[:, LANES * j:LANES * (j + 1)] = split[:, :LANES].astype(BF16)
            wl_buf[:, LANES * j:LANES * (j + 1)] = split[:, LANES:].astype(BF16)
        wd_buf[...] = wd_ref[0].astype(BF16)

    @pl.when(live)
    def _():
        x = _tiles_to_rows(xs_ref, rows, xs_ref.shape[0] // rows).astype(BF16)
        for j, f0 in enumerate(range(0, f, ft)):
            g = jnp.dot(x, wg_buf[:, f0:f0 + ft], preferred_element_type=F32) + bg_ref[0, :, f0:f0 + ft]
            lin = jnp.dot(x, wl_buf[:, f0:f0 + ft], preferred_element_type=F32) + bl_ref[0, :, f0:f0 + ft]
            gate = jnp.minimum(g, SWIGLU_LIMIT)
            lin = jnp.clip(lin, -SWIGLU_LIMIT, SWIGLU_LIMIT)
            act = (lin + 1.0) * gate * jax.nn.sigmoid(SWIGLU_ALPHA * gate)
            part = jnp.dot(act.astype(BF16), wd_buf[f0:f0 + ft, :], preferred_element_type=F32)
            if j == 0:
                acc_buf[...] = part + bd_ref[0]
            else:
                acc_buf[...] += part
        _rows_to_tiles(ys_ref, acc_buf[...])

    @pl.when(jnp.logical_not(live))
    def _():
        ys_ref[...] = jnp.zeros_like(ys_ref)


def _ffn(block_e, n_used, xs, w_gu, b_gu, w_d, b_d, rows):
    n_exp, d, f2 = w_gu.shape
    f = f2 // 2
    s = d // LANES
    n_rows = xs.shape[0] // s
    live = lambda i, nu: jnp.minimum(i, nu[0] - 1)
    wmap = lambda i, be, nu: (be[live(i, nu)], 0, 0)
    perm = jnp.asarray(_deinterleave_table(), BF16)
    return pl.pallas_call(
        functools.partial(_ffn_kernel, ft=min(f, 1024)),
        grid_spec=pltpu.PrefetchScalarGridSpec(
            num_scalar_prefetch=2,
            grid=(n_rows // rows,),
            in_specs=[pl.BlockSpec((rows * s, LANES), lambda i, be, nu: (live(i, nu), 0)),
                      pl.BlockSpec((1, d, f2), wmap),
                      pl.BlockSpec((1, 1, f), wmap), pl.BlockSpec((1, 1, f), wmap),
                      pl.BlockSpec((1, f, d), wmap), pl.BlockSpec((1, 1, d), wmap),
                      pl.BlockSpec(perm.shape, lambda i, be, nu: (0, 0))],
            out_specs=pl.BlockSpec((rows * s, LANES), lambda i, be, nu: (i, 0)),
            scratch_shapes=[pltpu.VMEM((d, f), BF16), pltpu.VMEM((d, f), BF16), pltpu.VMEM((f, d), BF16),
                            pltpu.VMEM((rows, d), F32)]),
        out_shape=jax.ShapeDtypeStruct(xs.shape, F32),
        compiler_params=_cparams("arbitrary"),
        name="expert_ffn",
    )(block_e, n_used, xs, w_gu, b_gu[:, 0::2].reshape(n_exp, 1, f), b_gu[:, 1::2].reshape(n_exp, 1, f),
      w_d, b_d.reshape(n_exp, 1, d), perm)


def _combine_kernel(dest_ref, next_ref, tg_ref, x1_ref, mod_ref, fg_ref, ys_ref, o_ref, ybuf, sem, *,
                    top_k, final_norm):
    i = pl.program_id(0)
    tt = x1_ref.shape[0]
    s = ybuf.shape[2] // tt
    slot = i % 2

    def gather(d_ref, to):
        def issue(t, carry):
            dst = pl.multiple_of(t * s, s)
            for j in range(top_k):
                src = pl.multiple_of(d_ref[0, 0, t * top_k + j] * s, s)
                pltpu.make_async_copy(ys_ref.at[pl.ds(src, s)], ybuf.at[to, j, pl.ds(dst, s)],
                                      sem.at[to]).start(priority=j % 2)
            return carry
        lax.fori_loop(0, tt, issue, 0, unroll=4)

    @pl.when(i == 0)
    def _():
        gather(dest_ref, 0)

    @pl.when(i + 1 < pl.num_programs(0))
    def _():
        gather(next_ref, 1 - slot)

    for j in range(top_k):
        pltpu.make_async_copy(ys_ref.at[pl.ds(0, tt * s)], ybuf.at[slot, j], sem.at[slot]).wait()
    tg = tg_ref[...]
    moe = tg[:, 0:1] * _tiles_to_rows(ybuf.at[slot, 0], tt, s)
    for j in range(1, top_k):
        moe = moe + tg[:, j:j + 1] * _tiles_to_rows(ybuf.at[slot, j], tt, s)
    x2 = x1_ref[...] + mod_ref[0, 5:6, :] * moe
    if final_norm:
        x2 = x2 * lax.rsqrt(jnp.mean(x2 * x2, axis=-1, keepdims=True) + EPS) * fg_ref[...]
    o_ref[...] = x2


def _combine(dest, tg, x1, mods, final_g, ys, mod_map, tt, final_norm):
    t, d = x1.shape
    steps = t // tt
    return pl.pallas_call(
        functools.partial(_combine_kernel, top_k=TOP_K, final_norm=final_norm),
        grid=(steps,),
        in_specs=[pl.BlockSpec((1, 1, tt * TOP_K), lambda i: (i, 0, 0), memory_space=pltpu.SMEM),
                  pl.BlockSpec((1, 1, tt * TOP_K), lambda i: (jnp.minimum(i + 1, steps - 1), 0, 0),
                               memory_space=pltpu.SMEM),
                  pl.BlockSpec((tt, LANES), lambda i: (i, 0)),
                  pl.BlockSpec((tt, d), lambda i: (i, 0)),
                  pl.BlockSpec((1, 6, d), mod_map),
                  pl.BlockSpec((1, d), lambda i: (0, 0)),
                  pl.BlockSpec(memory_space=pl.ANY)],
        out_specs=pl.BlockSpec((tt, d), lambda i: (i, 0)),
        out_shape=jax.ShapeDtypeStruct((t, d), F32),
        scratch_shapes=[pltpu.VMEM((2, TOP_K, tt * d // LANES, LANES), F32), pltpu.SemaphoreType.DMA((2,))],
        compiler_params=_cparams("arbitrary"),
        name="combine",
    )(dest, dest, tg, x1, mods, final_g.reshape(1, d), ys)


def _moe_plan(counts, rows, n_blocks):
    n_exp = counts.shape[0]
    padded = (counts + rows - 1) // rows * rows
    pad_end = jnp.cumsum(padded)
    block_row = jnp.arange(n_blocks, dtype=jnp.int32) * rows
    block_e = jnp.minimum(jnp.sum(pad_end[None, :] <= block_row[:, None], axis=1), n_exp - 1).astype(jnp.int32)
    n_used = (pad_end[-1:] // rows).astype(jnp.int32)
    start = pad_end - padded
    pad_start = jnp.pad(start.astype(F32), (0, LANES - n_exp)).reshape(1, LANES)
    return pad_start, block_e, n_used, (start + counts).astype(jnp.int32), (padded - counts).astype(jnp.int32)


def _sequence_mixers(z_hy, z_qk, z_v, z_o, z_g, lw, state, row_w):
    (hy_cw, hy_cb, filt_params, hy_b, ml_cw, ml_cb, ml_gb, ml_ng) = lw
    y_hy = _hyena(z_hy, hy_cw, hy_cb, filt_params, hy_b, row_w)
    ml_w = z_v.shape[-1]
    kt = _short_conv(z_qk, ml_cw, ml_cb, row_w, silu=True, col0=ml_w, ncols=ml_w, transpose=True)
    y_ml, st = _mlstm(z_qk, ml_cw, ml_cb, row_w, kt, z_v, z_o, z_g, ml_gb, ml_ng, *state)
    return y_hy, y_ml, st


def kernel(x_prompt, x_sample, state_mlstm_C, state_mlstm_n, state_mlstm_m, c, c_ctx, ada_w, ada_b, norm1_g,
           w_in, hy_conv_w, hy_conv_b, filt_w1, filt_b1, filt_w2, filt_b2, filt_w3, filt_freq, hy_bias,
           ml_conv_w, ml_conv_b, ml_gate_b, ml_norm_g, w_out, norm2_g, router_w, router_b, moe_w_gu,
           moe_b_gu, moe_w_down, moe_b_down, final_g):
    bp, lp, d = x_prompt.shape
    bs, ls, _ = x_sample.shape
    depth = ada_w.shape[0]
    heads = ml_gate_b.shape[-1]
    hy_w = hy_bias.shape[-1]
    ml_w = ml_norm_g.shape[-1]
    dh = ml_w // heads
    n_exp = router_w.shape[-1]
    t = bp * lp + bs * ls
    ng = N_DIR * 2 * heads
    seg_widths = (3 * hy_w, 2 * ml_w, ml_w, ml_w)
    n_main = 3 * hy_w + 4 * ml_w
    n_blocks = -(-(t * TOP_K) // MOE_ROWS) + n_exp

    cond = jnp.concatenate([c_ctx[None], c, jnp.zeros((SUBLANES - 1 - bs, d), F32)], axis=0)
    zero_state = (jnp.zeros((bp, N_DIR, heads, dh, dh), F32), jnp.zeros((bp, N_DIR, heads, dh), F32),
                  jnp.zeros((bp, N_DIR, heads), F32))
    xs_paths = [x_prompt, x_sample]
    path_cfg = [(0, 0, lp), (1, 1, GRID_W)]
    new_c, new_n, new_m = [], [], []
    for l in range(depth):
        mods = _ada(cond, ada_w[l], ada_b[l]).reshape(SUBLANES, 6, d)
        w_main = w_in[l][:, :n_main].astype(BF16)
        w_gate = jnp.pad(w_in[l][:, n_main:], ((0, 0), (0, LANES - ng)))
        lw = (hy_conv_w[l], hy_conv_b[l],
              (filt_w1[l], filt_b1[l], filt_w2[l], filt_b2[l], filt_w3[l], filt_freq[l]), hy_bias[l],
              ml_conv_w[l], ml_conv_b[l], ml_gate_b[l], ml_norm_g[l])
        states = [zero_state, (state_mlstm_C[:, l], state_mlstm_n[:, l], state_mlstm_m[:, l])]
        counts = jnp.zeros((SUBLANES, LANES), F32)
        routed = []
        for x3, (mod0, mod_step, row_w), state in zip(xs_paths, path_cfg, states):
            bsz, lseq, _ = x3.shape
            tm, tt = min(ROW_TILE, lseq), min(TOK_TILE, lseq)
            xf = x3.reshape(bsz * lseq, d)
            z = _inproj(xf, mods, norm1_g[l], w_main, w_gate, seg_widths,
                        _mod_index_map(mod0, mod_step, lseq // tm), tm)
            y_hy, y_ml, st = _sequence_mixers(*[a.reshape(bsz, lseq, a.shape[1]) for a in z], lw, state, row_w)
            x1, h2t, ti, tg, rk, counts = _post(
                y_hy.reshape(bsz * lseq, hy_w), y_ml.reshape(bsz * lseq, ml_w), xf, mods, w_out[l], norm2_g[l],
                router_w[l], router_b[l], counts, _mod_index_map(mod0, mod_step, lseq // tm), tm)
            routed.append((x1, h2t, ti, tg, rk, tm, tt, _mod_index_map(mod0, mod_step, lseq // tt), st))
        new_c.append(routed[0][-1][0])
        new_n.append(routed[0][-1][1])
        new_m.append(routed[0][-1][2])

        pad_start, block_e, n_used, pad_first, pad_count = _moe_plan(
            counts[0, :n_exp].astype(jnp.int32), MOE_ROWS, n_blocks)
        tt = routed[0][6]
        assert all(r[6] == tt for r in routed)
        dests = [_dest_rows(ti, rk, pad_start, tm)[:, :TOP_K].reshape(-1, 1, tt * TOP_K)
                 for _, _, ti, _, rk, tm, _, _, _ in routed]
        xs = _dispatch(jnp.concatenate(dests, axis=0), [r[1] for r in routed], pad_first, pad_count, n_used,
                       n_blocks, MOE_ROWS, tt)
        ys = _ffn(block_e, n_used, xs, moe_w_gu[l], moe_b_gu[l], moe_w_down[l], moe_b_down[l], MOE_ROWS)
        xs_paths = [
            _combine(dest, tg, x1, mods, final_g, ys, mod_map, tt, final_norm=l == depth - 1).reshape(x3.shape)
            for dest, (x1, _, _, tg, _, _, tt, mod_map, _), x3 in zip(dests, routed, xs_paths)]
    return (xs_paths[0], xs_paths[1],
            jnp.stack(new_c, axis=1), jnp.stack(new_n, axis=1), jnp.stack(new_m, axis=1))
```

```python
import functools
import math

import numpy as np
import jax
import jax.numpy as jnp
from jax import lax
from jax.experimental import pallas as pl
from jax.experimental.pallas import tpu as pltpu

F32 = jnp.float32
BF16 = jnp.bfloat16
HIGHEST = lax.Precision.HIGHEST
EPS = 1e-6

LANES = 128
SUBLANES = 8
VMEM_LIMIT_BYTES = 56 * 1024 * 1024

GRID_W = 64
ML_HEADS = 4
N_DIR = 2
HY_ORDER = 2
FILT_BANDS = 8
DECAY_TARGET = 1e-2
FAST_DECAY_PCT = 0.3
SLOW_DECAY_PCT = 1.5
TOP_K = 4
SWIGLU_LIMIT = 7.0
SWIGLU_ALPHA = 1.702

FFT_N2 = 128
MLSTM_CHUNK = 128
MLSTM_GROUP = 8
MLSTM_PAIR = 4
MOE_ROWS = 512
ROW_TILE = 512
TOK_TILE = 256


def _cparams(*sem):
    return pltpu.CompilerParams(dimension_semantics=sem, vmem_limit_bytes=VMEM_LIMIT_BYTES)


def _lane_tile(c, cap):
    return max(t for t in range(LANES, min(c, cap) + 1, LANES) if c % t == 0)


def _bdot(a, b):
    return jnp.dot(a.astype(BF16), b.astype(BF16), preferred_element_type=F32)


def _hdot(a, b):
    return jnp.dot(a, b, precision=HIGHEST, preferred_element_type=F32)


def _dot3(a, b):
    a_hi, b_hi = a.astype(BF16), b.astype(BF16)
    a_lo = (a - a_hi.astype(F32)).astype(BF16)
    b_lo = (b - b_hi.astype(F32)).astype(BF16)
    dot = functools.partial(jnp.dot, preferred_element_type=F32)
    return dot(a_hi, b_hi) + dot(a_hi, b_lo) + dot(a_lo, b_hi)


def _ada_kernel(c_ref, w_ref, b_ref, o_ref):
    c = c_ref[...]
    o_ref[...] = _hdot(c * jax.nn.sigmoid(c), w_ref[...]) + b_ref[...]


def _ada(cond, w, b):
    r, d = cond.shape
    n = w.shape[1]
    tn = _lane_tile(n, 1024)
    return pl.pallas_call(
        _ada_kernel,
        grid=(n // tn,),
        in_specs=[pl.BlockSpec((r, d), lambda j: (0, 0)),
                  pl.BlockSpec((d, tn), lambda j: (0, j)),
                  pl.BlockSpec((1, tn), lambda j: (0, j))],
        out_specs=pl.BlockSpec((r, tn), lambda j: (0, j)),
        out_shape=jax.ShapeDtypeStruct((r, n), F32),
        compiler_params=_cparams("arbitrary"),
        name="ada",
    )(cond, w, b.reshape(1, n))


def _mod_index_map(mod0, mod_step, tiles_per_seq):
    def index_map(i):
        return (mod0 + (i // tiles_per_seq) * mod_step, 0, 0)
    return index_map


def _inproj_kernel(x_ref, mod_ref, g_ref, w_ref, wg_ref, *out_refs, offsets):
    x = x_ref[...]
    h = x * lax.rsqrt(jnp.mean(x * x, axis=-1, keepdims=True) + EPS) * g_ref[...]
    h = h * (1.0 + mod_ref[0, 1:2, :]) + mod_ref[0, 0:1, :]
    hb = h.astype(BF16)
    for o_ref, (lo, hi) in zip(out_refs[:-1], offsets):
        o_ref[...] = jnp.dot(hb, w_ref[:, lo:hi], preferred_element_type=F32)
    h_lo = (h - hb.astype(F32)).astype(BF16)
    g = jnp.dot(hb, wg_ref[...], preferred_element_type=F32)
    out_refs[-1][...] = (g[:, :LANES] + g[:, LANES:]
                         + jnp.dot(h_lo, wg_ref[:, :LANES], preferred_element_type=F32))


def _inproj(x, mods, norm_g, w_main, w_gate, seg_widths, mod_map, tm):
    t, d = x.shape
    offsets, lo = [], 0
    for wd in seg_widths:
        offsets.append((lo, lo + wd))
        lo += wd
    wg_hi = w_gate.astype(BF16)
    wg = jnp.concatenate([wg_hi, (w_gate - wg_hi.astype(F32)).astype(BF16)], axis=1)
    widths = tuple(seg_widths) + (LANES,)
    return pl.pallas_call(
        functools.partial(_inproj_kernel, offsets=tuple(offsets)),
        grid=(t // tm,),
        in_specs=[pl.BlockSpec((tm, d), lambda i: (i, 0)),
                  pl.BlockSpec((1, 6, d), mod_map),
                  pl.BlockSpec((1, d), lambda i: (0, 0)),
                  pl.BlockSpec(w_main.shape, lambda i: (0, 0)),
                  pl.BlockSpec(wg.shape, lambda i: (0, 0))],
        out_specs=[pl.BlockSpec((tm, wd), lambda i: (i, 0)) for wd in widths],
        out_shape=[jax.ShapeDtypeStruct((t, wd), F32) for wd in widths],
        compiler_params=_cparams("arbitrary"),
        name="inproj",
    )(x, mods, norm_g.reshape(1, d), w_main, wg)


def _short_conv_value(x, w_ref, b_ref, row_w):
    l = x.shape[0]
    pos = lax.broadcasted_iota(jnp.int32, x.shape, 0) % row_w
    prev = jnp.where(pos == 0, 0.0, pltpu.roll(x, 1, 0))
    nxt = jnp.where(pos == row_w - 1, 0.0, pltpu.roll(x, l - 1, 0))
    return prev * w_ref[0:1, :] + x * w_ref[1:2, :] + nxt * w_ref[2:3, :] + b_ref[...]


def _short_conv_kernel(x_ref, w_ref, b_ref, o_ref, *, row_w, silu, transpose):
    y = _short_conv_value(x_ref[0], w_ref, b_ref, row_w)
    if silu:
        y = y * jax.nn.sigmoid(y)
    o_ref[0] = y.T if transpose else y


def _short_conv(x, w, b, row_w, silu, col0=0, ncols=None, transpose=False):
    bsz, l, c = x.shape
    ncols = c if ncols is None else ncols
    ct = _lane_tile(math.gcd(ncols, col0) if col0 else ncols, LANES if transpose else 512)
    j0 = col0 // ct
    out_shape, out_block, out_map = (bsz, l, ncols), (1, l, ct), lambda i, j: (i, 0, j)
    if transpose:
        out_shape, out_block, out_map = (bsz, ncols, l), (1, ct, l), lambda i, j: (i, j, 0)
    return pl.pallas_call(
        functools.partial(_short_conv_kernel, row_w=row_w, silu=silu, transpose=transpose),
        grid=(bsz, ncols // ct),
        in_specs=[pl.BlockSpec((1, l, ct), lambda i, j: (i, 0, j0 + j)),
                  pl.BlockSpec((3, ct), lambda i, j: (0, j0 + j)),
                  pl.BlockSpec((1, ct), lambda i, j: (0, j0 + j))],
        out_specs=pl.BlockSpec(out_block, out_map),
        out_shape=jax.ShapeDtypeStruct(out_shape, F32),
        compiler_params=_cparams("arbitrary", "arbitrary"),
        name="short_conv",
    )(x, w, b.reshape(1, c))


def _dft_direct_tables(l):
    n = 2 * l
    k = np.arange(n)[:, None].astype(np.float64)
    t = np.arange(n)[None, :].astype(np.float64)
    ang = 2.0 * np.pi * ((k * t) % n) / n
    cm, sm = np.cos(ang), np.sin(ang)
    fwd = np.block([[cm[:, :l], sm[:, :l]], [-sm[:, :l], cm[:, :l]]])
    filt = np.concatenate([cm, -sm], axis=0)
    return fwd, filt


def _dft_two_level_tables(l, n2):
    n = 2 * l
    n1 = n // n2
    k1 = np.arange(n1)[:, None].astype(np.float64)
    a = np.arange(n1)[None, :].astype(np.float64)
    ang1 = 2.0 * np.pi * ((k1 * a) % n1) / n1
    c1, s1 = np.cos(ang1), np.sin(ang1)
    h = n1 // 2
    m1 = np.block([[c1[:, :h], s1[:, :h]], [-s1[:, :h], c1[:, :h]]])
    m1f = np.concatenate([c1, -s1], axis=0)
    kk = (np.arange(n1)[:, None, None] + n1 * np.arange(n2)[None, :, None]).astype(np.float64)
    b = np.arange(n2)[None, None, :].astype(np.float64)
    ang = 2.0 * np.pi * ((kk * b) % n) / n
    cg, sg = np.cos(ang), np.sin(ang)
    gt = np.concatenate([np.concatenate([cg, sg], axis=2),
                         np.concatenate([-sg, cg], axis=2)], axis=1)
    return m1, m1f, gt


def _circular_lag(n0, rows, l):
    n = n0 + lax.broadcasted_iota(jnp.int32, (rows, 1), 0)
    t = jnp.where(n < l, n, 2 * l - n).astype(F32)
    return n, t, t / float(max(l - 1, 1))


def _filter_hidden_kernel(bandv_ref, w1_ref, b1_ref, w2_ref, b2_ref, freq_ref, o_ref, *, l):
    rows = o_ref.shape[0]
    _, t, t01 = _circular_lag(pl.program_id(0) * rows, rows, l)
    lane = lax.broadcasted_iota(jnp.int32, (rows, LANES), 1)
    ang = (2.0 * math.pi / l) * t * bandv_ref[...]
    feats = jnp.where(lane == 0, t01,
                      jnp.where(lane <= FILT_BANDS, jnp.cos(ang),
                                jnp.where(lane <= 2 * FILT_BANDS, -jnp.sin(ang), 0.0)))
    fr = freq_ref[...]
    h = jnp.sin(fr * (_hdot(feats, w1_ref[...]) + b1_ref[...]))
    o_ref[...] = jnp.sin(fr * (_hdot(h, w2_ref[...]) + b2_ref[...]))


def _filter_hidden(l, f_w1, f_b1, f_w2, f_b2, f_freq):
    emb, hid = f_w1.shape
    n = 2 * l
    rows = min(n, 512)
    bands = jnp.linspace(1e-4, FILT_BANDS - 1, FILT_BANDS, dtype=F32)
    bandv = jnp.zeros((1, LANES), F32).at[0, 1:1 + FILT_BANDS].set(bands)
    bandv = bandv.at[0, 1 + FILT_BANDS:1 + 2 * FILT_BANDS].set(bands)
    w1p = jnp.zeros((LANES, hid), F32).at[:emb].set(f_w1)
    c0 = lambda i: (0, 0)
    return pl.pallas_call(
        functools.partial(_filter_hidden_kernel, l=l),
        grid=(n // rows,),
        in_specs=[pl.BlockSpec((1, LANES), c0), pl.BlockSpec((LANES, hid), c0), pl.BlockSpec((1, hid), c0),
                  pl.BlockSpec((hid, hid), c0), pl.BlockSpec((1, hid), c0), pl.BlockSpec((1, hid), c0)],
        out_specs=pl.BlockSpec((rows, hid), lambda i: (i, 0)),
        out_shape=jax.ShapeDtypeStruct((n, hid), F32),
        compiler_params=_cparams("arbitrary"),
        name="filter_hidden",
    )(bandv, w1p, f_b1.reshape(1, hid), f_w2, f_b2.reshape(1, hid), f_freq.reshape(1, hid))


def _filter_rows(n0, rows, l, hid_ref, w3_ref, delta_ref):
    n, _, t01 = _circular_lag(n0, rows, l)
    h = hid_ref[pl.ds(n0, rows), :]
    hf = _bdot(h, w3_ref[0, 0])
    hb = _bdot(h, w3_ref[0, 1])
    window = jnp.exp(-t01 * delta_ref[...])
    return jnp.where(n < l, hf, jnp.where(n > l, hb, 0.0)) * window


def _filter_direct_kernel(hid_ref, w3_ref, delta_ref, ff_ref, h_ref, *, l):
    hc = _filter_rows(0, 2 * l, l, hid_ref, w3_ref, delta_ref)
    h_ref[0] = _bdot(ff_ref[...], hc) * (1.0 / (2 * l))


def _filter_two_level_kernel(hid_ref, w3_ref, delta_ref, m1f_ref, gt_ref, h_ref, hc_buf, a_buf, *, l, n2, rows):
    n = 2 * l
    n1 = n // n2

    def fill(i, carry):
        r0 = pl.multiple_of(i * rows, rows)
        hc_buf[pl.ds(r0, rows), :] = _filter_rows(r0, rows, l, hid_ref, w3_ref, delta_ref)
        return carry
    lax.fori_loop(0, n // rows, fill, 0)

    def step1(b, carry):
        col = hc_buf[pl.ds(b, n1, stride=n2), :]
        a = _bdot(m1f_ref[...], col)
        a_buf[pl.ds(b, n1, stride=2 * n2), :] = a[:n1]
        a_buf[pl.ds(n2 + b, n1, stride=2 * n2), :] = a[n1:]
        return carry
    lax.fori_loop(0, n2, step1, 0, unroll=8)

    def step2(k1, carry):
        r0 = pl.multiple_of(k1 * 2 * n2, 2 * n2)
        h_ref[0, k1] = (_bdot(gt_ref[k1], a_buf[pl.ds(r0, 2 * n2), :]) * (1.0 / n)).astype(h_ref.dtype)
        return carry
    lax.fori_loop(0, n1, step2, 0, unroll=8)


def _filter_tail_inputs(hy_w, f_w3):
    hid = f_w3.shape[0]
    w3 = f_w3.reshape(hid, HY_ORDER, N_DIR, hy_w).transpose(1, 2, 0, 3)
    max_decay = math.log(DECAY_TARGET) / FAST_DECAY_PCT
    min_decay = math.log(DECAY_TARGET) / SLOW_DECAY_PCT
    deltas = jnp.abs(jnp.linspace(min_decay, max_decay, hy_w, dtype=F32)).reshape(1, hy_w)
    return w3, deltas


def _filter_spectrum_direct(l, hy_w, filt_params, ff):
    f_w1, f_b1, f_w2, f_b2, f_w3, f_freq = filt_params
    hidden = _filter_hidden(l, f_w1, f_b1, f_w2, f_b2, f_freq)
    w3, deltas = _filter_tail_inputs(hy_w, f_w3)
    hid = f_w3.shape[0]
    n = 2 * l
    return pl.pallas_call(
        functools.partial(_filter_direct_kernel, l=l),
        grid=(HY_ORDER,),
        in_specs=[pl.BlockSpec((n, hid), lambda o: (0, 0)),
                  pl.BlockSpec((1, N_DIR, hid, hy_w), lambda o: (o, 0, 0, 0)),
                  pl.BlockSpec((1, hy_w), lambda o: (0, 0)),
                  pl.BlockSpec((2 * n, n), lambda o: (0, 0))],
        out_specs=pl.BlockSpec((1, 2 * n, hy_w), lambda o: (o, 0, 0)),
        out_shape=jax.ShapeDtypeStruct((HY_ORDER, 2 * n, hy_w), F32),
        compiler_params=_cparams("arbitrary"),
        name="filter_direct",
    )(hidden, w3, deltas, ff)


def _filter_spectrum_two_level(l, hy_w, filt_params, m1f, gt, ct):
    f_w1, f_b1, f_w2, f_b2, f_w3, f_freq = filt_params
    hidden = _filter_hidden(l, f_w1, f_b1, f_w2, f_b2, f_freq)
    w3, deltas = _filter_tail_inputs(hy_w, f_w3)
    hid = f_w3.shape[0]
    n = 2 * l
    n2 = FFT_N2
    n1 = n // n2
    return pl.pallas_call(
        functools.partial(_filter_two_level_kernel, l=l, n2=n2, rows=min(n, 512)),
        grid=(HY_ORDER, hy_w // ct),
        in_specs=[pl.BlockSpec((n, hid), lambda o, j: (0, 0)),
                  pl.BlockSpec((1, N_DIR, hid, ct), lambda o, j: (o, 0, 0, j)),
                  pl.BlockSpec((1, ct), lambda o, j: (0, j)),
                  pl.BlockSpec((2 * n1, n1), lambda o, j: (0, 0)),
                  pl.BlockSpec((n1, 2 * n2, 2 * n2), lambda o, j: (0, 0, 0))],
        out_specs=pl.BlockSpec((1, n1, 2 * n2, ct), lambda o, j: (o, 0, 0, j)),
        out_shape=jax.ShapeDtypeStruct((HY_ORDER, n1, 2 * n2, hy_w), BF16),
        scratch_shapes=[pltpu.VMEM((n, ct), F32), pltpu.VMEM((n1 * 2 * n2, ct), F32)],
        compiler_params=_cparams("arbitrary", "arbitrary"),
        name="filter_two_level",
    )(hidden, w3, deltas, m1f, gt)


def _complex_mul(x, h, half):
    xr, xi = x[:half], x[half:]
    hr, hi = h[:half], h[half:]
    return jnp.concatenate([xr * hr - xi * hi, xr * hi + xi * hr], axis=0)


def _conv_direct_kernel(z_ref, gate_ref, zw_ref, zb_ref, gw_ref, gb_ref, bias_ref, h_ref, fwd_ref, inv_ref,
                        o_ref, *, row_w, conv_z):
    l = z_ref.shape[1]
    zs = [_short_conv_value(z_ref[s], zw_ref, zb_ref, row_w) if conv_z else z_ref[s] for s in range(2)]
    z = jnp.concatenate(zs, axis=0)
    x = _bdot(fwd_ref[...], z)
    y = _bdot(inv_ref[...], _complex_mul(x, h_ref[0], 2 * l))
    bias = bias_ref[0]
    for s in range(2):
        gate = _short_conv_value(gate_ref[s], gw_ref, gb_ref, row_w)
        o_ref[s] = gate * (y[s * l:(s + 1) * l] + bias * zs[s])


def _conv_two_level_kernel(z_ref, gate_ref, bias_ref, h_ref, m1_ref, m1i_ref, gt_ref, o_ref,
                           a_buf, *, n2):
    l = z_ref.shape[1]
    n1 = 2 * l // n2
    hn = n1 // 2

    def step1(b, carry):
        za = z_ref[0, pl.ds(b, hn, stride=n2), :]
        zb = z_ref[1, pl.ds(b, hn, stride=n2), :]
        a = _bdot(m1_ref[...], jnp.concatenate([za, zb], axis=0))
        a_buf[pl.ds(b, n1, stride=2 * n2), :] = a[:n1]
        a_buf[pl.ds(n2 + b, n1, stride=2 * n2), :] = a[n1:]
        return carry
    lax.fori_loop(0, n2, step1, 0, unroll=8)

    def step2(k1, carry):
        r0 = pl.multiple_of(k1 * 2 * n2, 2 * n2)
        x = _bdot(gt_ref[k1], a_buf[pl.ds(r0, 2 * n2), :])
        y = _complex_mul(x, h_ref[0, k1].astype(F32), n2).astype(BF16)
        a_buf[pl.ds(r0, 2 * n2), :] = lax.dot_general(gt_ref[k1], y, (((0,), (0,)), ((), ())),
                                                      preferred_element_type=F32)
        return carry
    lax.fori_loop(0, n1, step2, 0, unroll=8)

    def step3(b, carry):
        br = a_buf[pl.ds(b, n1, stride=2 * n2), :]
        bi = a_buf[pl.ds(n2 + b, n1, stride=2 * n2), :]
        y = _bdot(m1i_ref[...], jnp.concatenate([br, bi], axis=0))
        o_ref[0, pl.ds(b, hn, stride=n2), :] = y[:hn]
        o_ref[1, pl.ds(b, hn, stride=n2), :] = y[hn:]
        return carry
    lax.fori_loop(0, n2, step3, 0, unroll=8)

    bias = bias_ref[0]
    for s in range(2):
        o_ref[s] = gate_ref[s] * (o_ref[s] + bias * z_ref[s])


def _long_conv_gated(u, z, z_col, gate_col, spectrum, order, bias, tables, ct, conv=None, conv_z=False):
    bsz, l, _ = u.shape
    c = spectrum.shape[-1]
    nct = c // ct
    zspec = pl.BlockSpec((2, l, ct), lambda i, j: (i, 0, z_col * nct + j))
    gspec = pl.BlockSpec((2, l, ct), lambda i, j: (i, 0, gate_col * nct + j))
    bspec = pl.BlockSpec((1, 1, ct), lambda i, j: (order, 0, j))
    ospec = pl.BlockSpec((2, l, ct), lambda i, j: (i, 0, j))
    out_shape = jax.ShapeDtypeStruct((bsz, l, c), F32)
    bias3 = bias.reshape(HY_ORDER, 1, c)
    if len(tables) == 2:
        fwd, inv = tables
        n = 2 * l
        conv_w, conv_b, row_w = conv
        conv_b = conv_b.reshape(1, -1)
        zcol = z_col if conv_z else gate_col
        wspec = lambda col, rows: pl.BlockSpec((rows, ct), lambda i, j: (0, col * nct + j))
        return pl.pallas_call(
            functools.partial(_conv_direct_kernel, row_w=row_w, conv_z=conv_z),
            grid=(bsz // 2, nct),
            in_specs=[zspec, gspec, wspec(zcol, 3), wspec(zcol, 1), wspec(gate_col, 3), wspec(gate_col, 1), bspec,
                      pl.BlockSpec((1, 2 * n, ct), lambda i, j: (order, 0, j)),
                      pl.BlockSpec(fwd.shape, lambda i, j: (0, 0)),
                      pl.BlockSpec(inv.shape, lambda i, j: (0, 0))],
            out_specs=ospec, out_shape=out_shape,
            compiler_params=_cparams("arbitrary", "arbitrary"),
            name="long_conv_direct",
        )(z, u, conv_w, conv_b, conv_w, conv_b, bias3, spectrum, fwd, inv)
    m1, m1i, gt = tables
    n2 = FFT_N2
    n1 = 2 * l // n2
    const2 = lambda i, j: (0, 0)
    const3 = lambda i, j: (0, 0, 0)
    return pl.pallas_call(
        functools.partial(_conv_two_level_kernel, n2=n2),
        grid=(nct, bsz // 2),
        in_specs=[pl.BlockSpec((2, l, ct), lambda j, i: (i, 0, z_col * nct + j)),
                  pl.BlockSpec((2, l, ct), lambda j, i: (i, 0, gate_col * nct + j)),
                  pl.BlockSpec((1, 1, ct), lambda j, i: (order, 0, j)),
                  pl.BlockSpec((1, n1, 2 * n2, ct), lambda j, i: (order, 0, 0, j)),
                  pl.BlockSpec(m1.shape, const2), pl.BlockSpec(m1i.shape, const2),
                  pl.BlockSpec(gt.shape, const3)],
        out_specs=pl.BlockSpec((2, l, ct), lambda j, i: (i, 0, j)),
        out_shape=out_shape,
        scratch_shapes=[pltpu.VMEM((n1 * 2 * n2, ct), F32)],
        compiler_params=_cparams("arbitrary", "arbitrary"),
        name="long_conv_two_level",
    )(z, u, bias3, spectrum, m1, m1i, gt)


def _hyena(z_hy, conv_w, conv_b, filt_params, hy_bias, row_w):
    bsz, l, c3 = z_hy.shape
    c = c3 // 3
    table = lambda a: jnp.asarray(a, F32).astype(BF16)
    if 2 * l // FFT_N2 <= 4:
        fwd, filt = _dft_direct_tables(l)
        tables = (table(fwd), table(fwd.T))
        spectrum = _filter_spectrum_direct(l, c, filt_params, table(filt))
        conv = (conv_w, conv_b, row_w)
        z1 = _long_conv_gated(z_hy, z_hy, 0, 1, spectrum, 0, hy_bias, tables, c, conv, conv_z=True)
        return _long_conv_gated(z_hy, z1, 0, 2, spectrum, 1, hy_bias, tables, c, conv)
    u = _short_conv(z_hy, conv_w, conv_b, row_w, silu=False)
    m1, m1f, gt = _dft_two_level_tables(l, FFT_N2)
    tables = (table(m1), table(m1.T), table(gt))
    ct = LANES
    spectrum = _filter_spectrum_two_level(l, c, filt_params, table(m1f), tables[2], ct)
    z1 = _long_conv_gated(u, u, 0, 1, spectrum, 0, hy_bias, tables, ct)
    return _long_conv_gated(u, z1, 0, 2, spectrum, 1, hy_bias, tables, ct)


def _log_sigmoid(x):
    return jnp.minimum(x, 0.0) - jnp.log1p(jnp.exp(-jnp.abs(x)))


def _split3_dot(a, b, split_lhs):
    x = a if split_lhs else b
    hi = x.astype(BF16)
    rest = x - hi.astype(F32)
    mid = rest.astype(BF16)
    parts = (hi, mid, (rest - mid.astype(F32)).astype(BF16))
    if split_lhs:
        return functools.reduce(lambda u, w: u + w, [jnp.dot(p, b, preferred_element_type=F32) for p in parts])
    return functools.reduce(lambda u, w: u + w, [jnp.dot(a, p, preferred_element_type=F32) for p in parts])


def _mlstm_recur(qb, sv, m_loc, kv, g_loc, bq, btot, cn, m):
    dh = qb.shape[1]
    inter = bq + m
    mj = jnp.maximum(m_loc, inter)
    w_int = jnp.exp(inter - mj)
    w_loc = jnp.exp(m_loc - mj)
    qc = jnp.dot(qb, cn.astype(BF16), preferred_element_type=F32)
    num = w_int * qc[:, :dh] + w_loc * sv[:, :dh]
    den = w_int * qc[:, dh:] + w_loc * sv[:, dh:]
    h = num / jnp.maximum(jnp.abs(den), jnp.exp(-mj))
    m_new = jnp.maximum(btot + m, g_loc)
    cn_new = jnp.exp(btot + m - m_new) * cn + jnp.exp(g_loc - m_new) * kv
    return h, cn_new, m_new


def _mlstm_kernel(q_ref, cw_ref, cb_ref, kt_ref, v_ref, o_ref, g_ref, gt_ref, gb_ref, gbt_ref, ng_ref,
                  c0_ref, n0_ref, m0_ref, y_ref, c_ref, n_ref, m_ref,
                  hf_buf, hb_buf, cn_buf, sv_buf, kv_buf, bq_buf, ml_buf, sc_buf, qb_buf,
                  *, chunk, heads, row_w):
    l, dh = q_ref.shape[1], q_ref.shape[2]
    nc = l // chunk
    group = sv_buf.shape[1]
    pair = math.gcd(group, MLSTM_PAIR)
    scale = dh ** -0.5

    def conv_q(j, carry):
        r0 = pl.multiple_of(j * chunk, chunk)
        x = q_ref[0, pl.ds(r0, chunk), :]
        idx = lax.broadcasted_iota(jnp.int32, x.shape, 0)
        pos = (r0 + idx) & (row_w - 1)
        before = q_ref[0, pl.ds(jnp.maximum(r0 - 1, 0), 1), :]
        after = q_ref[0, pl.ds(jnp.minimum(r0 + chunk, l - 1), 1), :]
        prev = jnp.where(idx == 0, before, pltpu.roll(x, 1, 0))
        nxt = jnp.where(idx == chunk - 1, after, pltpu.roll(x, chunk - 1, 0))
        y = (jnp.where(pos == 0, 0.0, prev) * cw_ref[0:1, :] + x * cw_ref[1:2, :]
             + jnp.where(pos == row_w - 1, 0.0, nxt) * cw_ref[2:3, :] + cb_ref[...])
        qb_buf[pl.ds(r0, chunk), :] = (y * jax.nn.sigmoid(y)).astype(BF16)
        return carry
    lax.fori_loop(0, nc, conv_q, 0, unroll=2)
    row = lax.broadcasted_iota(jnp.int32, (chunk, chunk), 0)
    col = lax.broadcasted_iota(jnp.int32, (chunk, chunk), 1)
    lower, upper = col <= row, col >= row
    tri_l, tri_u = lower.astype(BF16), upper.astype(BF16)
    gate_row = lax.broadcasted_iota(jnp.int32, (SUBLANES, chunk), 0)
    ones = jnp.ones((chunk, dh), BF16)

    sel_row = lax.broadcasted_iota(jnp.int32, (LANES, LANES), 0)
    head = pl.program_id(1)

    def chunk_start(j, d):
        return pl.multiple_of(((nc - 1 - j) if d else j) * chunk, chunk)

    def local(gp, carry, j0):
        jobs = [(gp * pair + k, d) for k in range(pair) for d in range(N_DIR)]
        r0s = [chunk_start(j0 + g, d) for g, d in jobs]
        picks = [(sel_row == (2 * d + 1) * heads + head).astype(BF16) for d in range(N_DIR)]
        lfs = [_log_sigmoid(g_ref[0, pl.ds(r0, chunk), :] + gb_ref[...]) for r0 in r0s]
        gts = [gt_ref[0, 0, :, pl.ds(r0, chunk)] + gbt_ref[0] for r0 in r0s]
        gts = [jnp.where(gate_row % 2 == 1, _log_sigmoid(gt), gt) for gt in gts]
        lfs = [_split3_dot(lf, picks[d], split_lhs=True) for lf, (_, d) in zip(lfs, jobs)]
        brows = [_split3_dot(gt, tri_l if d else tri_u, split_lhs=True)[2 * d + 1:2 * d + 2, :]
                 for gt, (_, d) in zip(gts, jobs)]
        bqs = [_split3_dot(tri_u if d else tri_l, lf, split_lhs=False) for lf, (_, d) in zip(lfs, jobs)]
        btots = [brow[:, 0:1] if d else brow[:, chunk - 1:chunk] for brow, (_, d) in zip(brows, jobs)]
        irows = [gt[2 * d:2 * d + 1, :] for gt, (_, d) in zip(gts, jobs)]
        qbs = [qb_buf[pl.ds(r0, chunk), :] for r0 in r0s]
        kts = [kt_ref[0, :, pl.ds(r0, chunk)] * scale for r0 in r0s]
        vos = [jnp.concatenate([v_ref[0, pl.ds(r0, chunk), :].astype(BF16), ones], axis=1) for r0 in r0s]
        qks = [jnp.dot(qb, kt.astype(BF16), preferred_element_type=F32) for qb, kt in zip(qbs, kts)]
        dms = [jnp.where(upper if d else lower, bq - brow + irow, -jnp.inf)
               for bq, brow, irow, (_, d) in zip(bqs, brows, irows, jobs)]
        m_locs = [jnp.max(dm, axis=-1, keepdims=True) for dm in dms]
        ss = [(qk * jnp.exp(dm - m_loc)).astype(BF16) for qk, dm, m_loc in zip(qks, dms, m_locs)]
        gls = [btot - brow + irow for btot, brow, irow in zip(btots, brows, irows)]
        g_locs = [jnp.max(gl, axis=-1, keepdims=True) for gl in gls]
        wks = [(kt * jnp.exp(gl - g_loc)).astype(BF16) for kt, gl, g_loc in zip(kts, gls, g_locs)]
        svs = [jnp.dot(s, vo, preferred_element_type=F32) for s, vo in zip(ss, vos)]
        kvs = [jnp.dot(wk, vo, preferred_element_type=F32) for wk, vo in zip(wks, vos)]
        for (g, d), sv, kv, bq, m_loc, btot, g_loc in zip(jobs, svs, kvs, bqs, m_locs, btots, g_locs):
            sv_buf[d, g] = sv
            kv_buf[d, g] = kv
            bq_buf[d, g] = bq
            ml_buf[d, g] = jnp.broadcast_to(m_loc, (chunk, LANES))
            sc_buf[d, g, 0:1, :] = jnp.broadcast_to(btot, (1, LANES))
            sc_buf[d, g, 1:2, :] = jnp.broadcast_to(g_loc, (1, LANES))
        return carry

    def recur(g, carry, j0):
        ms = list(carry)
        for d in range(N_DIR):
            r0 = chunk_start(j0 + g, d)
            h, cn, ms[d] = _mlstm_recur(
                qb_buf[pl.ds(r0, chunk), :], sv_buf[d, g], ml_buf[d, g], kv_buf[d, g],
                sc_buf[d, g, 1:2, 0:1], bq_buf[d, g], sc_buf[d, g, 0:1, 0:1], cn_buf[d], ms[d])
            cn_buf[d] = cn
            (hb_buf if d else hf_buf)[pl.ds(r0, chunk), :] = h
        return tuple(ms)

    for d in range(N_DIR):
        n_rep = jnp.broadcast_to(n0_ref[0, 0, d:d + 1, :], (dh, dh)).T
        cn_buf[d] = jnp.concatenate([c0_ref[0, d, 0], n_rep], axis=1)

    def block(jb, carry):
        j0 = jb * group
        lax.fori_loop(0, group // pair, functools.partial(local, j0=j0), 0)
        return lax.fori_loop(0, group, functools.partial(recur, j0=j0), carry, unroll=2)

    m_fin = lax.fori_loop(0, nc // group, block, (m0_ref[0, 0, 0:1, 0:1], m0_ref[0, 0, 1:2, 0:1]))
    for d in range(N_DIR):
        c_ref[0, d, 0] = cn_buf[d, :, :dh]
        n_ref[0, 0, d:d + 1, :] = cn_buf[d, :, dh:].T[0:1, :]
        m_ref[0, 0, d:d + 1, :] = jnp.broadcast_to(m_fin[d], (1, LANES))

    def finish(j, carry):
        r0 = pl.multiple_of(j * chunk, chunk)
        hs = hf_buf[pl.ds(r0, chunk), :] + hb_buf[pl.ds(r0, chunk), :]
        hs = hs * lax.rsqrt(jnp.mean(hs * hs, axis=-1, keepdims=True) + EPS) * ng_ref[...]
        y_ref[0, pl.ds(r0, chunk), :] = jax.nn.sigmoid(o_ref[0, pl.ds(r0, chunk), :]) * hs
        return carry
    lax.fori_loop(0, nc, finish, 0, unroll=2)


def _mlstm(z_qk, conv_w, conv_b, row_w, kt, z_v, z_o, z_g, gate_b, norm_g, c0, n0, m0):
    bsz, l, ml = z_v.shape
    heads = gate_b.shape[-1]
    dh = ml // heads
    chunk = MLSTM_CHUNK
    assert dh == LANES and chunk == LANES and l % chunk == 0
    assert row_w & (row_w - 1) == 0 and l % row_w == 0
    group = math.gcd(l // chunk, MLSTM_GROUP)
    n_gate = N_DIR * 2
    assert z_g.shape[-1] == LANES
    g4 = z_g[..., :n_gate * heads].reshape(bsz, l, n_gate, heads)
    ght = jnp.pad(g4.transpose(0, 3, 2, 1), ((0, 0), (0, 0), (0, SUBLANES - n_gate), (0, 0)))
    gb4 = gate_b.reshape(n_gate, heads).T
    gb = jnp.pad(gate_b.reshape(1, n_gate * heads), ((0, 0), (0, LANES - n_gate * heads)))
    gbt = jnp.broadcast_to(jnp.pad(gb4, ((0, 0), (0, SUBLANES - n_gate)))[..., None], (heads, SUBLANES, LANES))
    n0h = n0.transpose(0, 2, 1, 3)
    m0h = jnp.broadcast_to(m0.transpose(0, 2, 1)[..., None], (bsz, heads, N_DIR, LANES))
    seq = pl.BlockSpec((1, l, dh), lambda b, h: (b, 0, h))
    cspec = pl.BlockSpec((1, N_DIR, 1, dh, dh), lambda b, h: (b, 0, h, 0, 0))
    sspec = pl.BlockSpec((1, 1, N_DIR, dh), lambda b, h: (b, h, 0, 0))
    mspec = pl.BlockSpec((1, 1, N_DIR, LANES), lambda b, h: (b, h, 0, 0))
    y, c, n, m = pl.pallas_call(
        functools.partial(_mlstm_kernel, chunk=chunk, heads=heads, row_w=row_w),
        grid=(bsz, heads),
        in_specs=[seq, pl.BlockSpec((3, dh), lambda b, h: (0, h)), pl.BlockSpec((1, dh), lambda b, h: (0, h)),
                  pl.BlockSpec((1, dh, l), lambda b, h: (b, h, 0)), seq, seq,
                  pl.BlockSpec((1, l, LANES), lambda b, h: (b, 0, 0)),
                  pl.BlockSpec((1, 1, SUBLANES, l), lambda b, h: (b, h, 0, 0)),
                  pl.BlockSpec((1, LANES), lambda b, h: (0, 0)),
                  pl.BlockSpec((1, SUBLANES, LANES), lambda b, h: (h, 0, 0)),
                  pl.BlockSpec((1, dh), lambda b, h: (0, h)),
                  cspec, sspec, mspec],
        out_specs=[seq, cspec, sspec, mspec],
        out_shape=[jax.ShapeDtypeStruct((bsz, l, ml), F32),
                   jax.ShapeDtypeStruct((bsz, N_DIR, heads, dh, dh), F32),
                   jax.ShapeDtypeStruct((bsz, heads, N_DIR, dh), F32),
                   jax.ShapeDtypeStruct((bsz, heads, N_DIR, LANES), F32)],
        scratch_shapes=[pltpu.VMEM((l, dh), F32), pltpu.VMEM((l, dh), F32),
                        pltpu.VMEM((N_DIR, dh, 2 * dh), F32),
                        pltpu.VMEM((N_DIR, group, chunk, 2 * dh), F32),
                        pltpu.VMEM((N_DIR, group, dh, 2 * dh), F32),
                        pltpu.VMEM((N_DIR, group, chunk, LANES), F32),
                        pltpu.VMEM((N_DIR, group, chunk, LANES), F32),
                        pltpu.VMEM((N_DIR, group, SUBLANES, LANES), F32),
                        pltpu.VMEM((l, dh), BF16)],
        compiler_params=_cparams("arbitrary", "arbitrary"),
        name="mlstm",
    )(z_qk, conv_w, conv_b.reshape(1, -1), kt, z_v, z_o, z_g, ght, gb, gbt, norm_g.reshape(1, ml), c0, n0h, m0h)
    return y, (c, n.transpose(0, 2, 1, 3), m[..., 0].transpose(0, 2, 1))


def _rows_to_tiles(tile_ref, x):
    r, d = x.shape
    s = d // LANES
    for k in range(s):
        tile_ref[pl.ds(k, r, stride=s), :] = x[:, LANES * k:LANES * (k + 1)]


def _tiles_to_rows(tile_ref, r, s, r0=0):
    return jnp.concatenate([tile_ref[pl.ds(r0 * s + k, r, stride=s), :] for k in range(s)], axis=1)


def _lane_pack(cols, lane):
    out = jnp.zeros(lane.shape, cols[0].dtype)
    for j, colv in enumerate(cols):
        out = jnp.where(lane == j, colv, out)
    return out


def _post_kernel(yhy_ref, yml_ref, x_ref, mod_ref, wo_ref, n2g_ref, rw_ref, rb_ref, cnt0_ref,
                 x1_ref, h2_ref, ti_ref, tg_ref, rk_ref, cnt_ref, carry, *, n_exp, top_k):
    @pl.when(pl.program_id(0) == 0)
    def _():
        carry[...] = cnt0_ref[0:1, :]

    hy_w = yhy_ref.shape[1]
    tm = x_ref.shape[0]
    proj = _bdot(yhy_ref[...], wo_ref[:hy_w, :]) + _bdot(yml_ref[...], wo_ref[hy_w:, :])
    x1 = x_ref[...] + mod_ref[0, 2:3, :] * proj
    x1_ref[...] = x1
    h2 = x1 * lax.rsqrt(jnp.mean(x1 * x1, axis=-1, keepdims=True) + EPS) * n2g_ref[...]
    h2 = h2 * (1.0 + mod_ref[0, 4:5, :]) + mod_ref[0, 3:4, :]
    _rows_to_tiles(h2_ref, h2)

    lane = lax.broadcasted_iota(jnp.int32, (tm, LANES), 1)
    work = jnp.where(lane < n_exp, _dot3(h2, rw_ref[...]) + rb_ref[...], -jnp.inf)
    vals, idxs, hots = [], [], []
    for _ in range(top_k):
        mx = jnp.max(work, axis=-1, keepdims=True)
        idx = jnp.min(jnp.where(work == mx, lane, LANES), axis=-1, keepdims=True)
        hot = lane == idx
        vals.append(mx)
        idxs.append(idx)
        hots.append(hot)
        work = jnp.where(hot, -jnp.inf, work)
    exps = [jnp.exp(v - vals[0]) for v in vals]
    tot = functools.reduce(lambda a, b: a + b, exps)
    ti_ref[...] = _lane_pack(idxs, lane)
    tg_ref[...] = _lane_pack([e / tot for e in exps], lane)

    hot_sum = functools.reduce(lambda a, b: a + b, [h.astype(F32) for h in hots])
    row = lax.broadcasted_iota(jnp.int32, (tm, tm), 0)
    col = lax.broadcasted_iota(jnp.int32, (tm, tm), 1)
    before = _bdot((col < row).astype(F32), hot_sum) + carry[...]
    ranks = [jnp.sum(jnp.where(h, before, 0.0), axis=-1, keepdims=True).astype(jnp.int32) for h in hots]
    rk_ref[...] = _lane_pack(ranks, lane)
    carry[...] = carry[...] + jnp.sum(hot_sum, axis=0, keepdims=True)
    cnt_ref[...] = jnp.broadcast_to(carry[...], cnt_ref.shape)


def _post(y_hy, y_ml, x, mods, w_out, norm2_g, r_w, r_b, counts, mod_map, tm):
    t, d = x.shape
    n_exp = r_w.shape[1]
    rwp = jnp.pad(r_w, ((0, 0), (0, LANES - n_exp)))
    rbp = jnp.pad(r_b, (0, LANES - n_exp)).reshape(1, LANES)
    row = lambda wd: pl.BlockSpec((tm, wd), lambda i: (i, 0))
    const = lambda shape: pl.BlockSpec(shape, lambda i: (0, 0))
    return pl.pallas_call(
        functools.partial(_post_kernel, n_exp=n_exp, top_k=TOP_K),
        grid=(t // tm,),
        in_specs=[row(y_hy.shape[1]), row(y_ml.shape[1]), row(d),
                  pl.BlockSpec((1, 6, d), mod_map),
                  const(w_out.shape), const((1, d)), const((d, LANES)), const((1, LANES)),
                  const((SUBLANES, LANES))],
        out_specs=[row(d), pl.BlockSpec((tm * d // LANES, LANES), lambda i: (i, 0)),
                   row(LANES), row(LANES), row(LANES), const((SUBLANES, LANES))],
        out_shape=[jax.ShapeDtypeStruct((t, d), F32), jax.ShapeDtypeStruct((t * d // LANES, LANES), F32),
                   jax.ShapeDtypeStruct((t, LANES), jnp.int32), jax.ShapeDtypeStruct((t, LANES), F32),
                   jax.ShapeDtypeStruct((t, LANES), jnp.int32), jax.ShapeDtypeStruct((SUBLANES, LANES), F32)],
        scratch_shapes=[pltpu.VMEM((1, LANES), F32)],
        compiler_params=_cparams("arbitrary"),
        name="post",
    )(y_hy, y_ml, x, mods, w_out.astype(BF16), norm2_g.reshape(1, d), rwp, rbp, counts)


def _dest_kernel(ti_ref, rk_ref, ps_ref, d_ref, *, top_k):
    lane = lax.broadcasted_iota(jnp.int32, ti_ref.shape, 1)
    ti, rk = ti_ref[...], rk_ref[...]
    cols = []
    for j in range(top_k):
        start = jnp.sum(jnp.where(lane == ti[:, j:j + 1], ps_ref[...], 0.0), axis=-1, keepdims=True)
        cols.append(start.astype(jnp.int32) + rk[:, j:j + 1])
    d_ref[...] = _lane_pack(cols, lane)


def _dest_rows(ti, rk, pad_start, tm):
    t = ti.shape[0]
    row = pl.BlockSpec((tm, LANES), lambda i: (i, 0))
    return pl.pallas_call(
        functools.partial(_dest_kernel, top_k=TOP_K),
        grid=(t // tm,),
        in_specs=[row, row, pl.BlockSpec((1, LANES), lambda i: (0, 0))],
        out_specs=row,
        out_shape=jax.ShapeDtypeStruct((t, LANES), jnp.int32),
        compiler_params=_cparams("arbitrary"),
        name="dest_rows",
    )(ti, rk, pad_start)


def _dispatch_kernel(first_ref, count_ref, nu_ref, dest_ref, *refs, top_k, s, rows, tiles):
    h_refs = refs[:len(tiles)]
    xs_ref, zbuf, hbuf, load_sem, sem, zsem = refs[len(tiles):]
    i = pl.program_id(0)
    steps = pl.num_programs(0)
    tt = dest_ref.shape[2] // top_k
    half = zbuf.shape[0] // s
    n_blocks = xs_ref.shape[0] // (rows * s)

    def pad_rows(e, carry, wait):
        off, n = first_ref[e], count_ref[e]
        for k in range(half.bit_length()):
            bit = half >> k

            @pl.when((n & bit) != 0)
            def _():
                dst = pl.multiple_of(off * s, s)
                copy = pltpu.make_async_copy(zbuf.at[pl.ds(0, bit * s)], xs_ref.at[pl.ds(dst, bit * s)], zsem)
                copy.wait() if wait else copy.start()
            off = off + (n & bit)
        return carry

    def spare_block(b, carry, wait):
        for part in range(2):
            dst = pl.multiple_of((b * 2 + part) * half * s, half * s)
            copy = pltpu.make_async_copy(zbuf, xs_ref.at[pl.ds(dst, half * s)], zsem)
            copy.wait() if wait else copy.start()
        return carry

    @pl.when(i == 0)
    def _():
        zbuf[...] = jnp.zeros_like(zbuf)
        for wait in (False, True):
            lax.fori_loop(0, first_ref.shape[0], functools.partial(pad_rows, wait=wait), 0)
            lax.fori_loop(nu_ref[0], n_blocks, functools.partial(spare_block, wait=wait), 0)

    def load(t, slot, start):
        if not start:
            pltpu.make_async_copy(h_refs[0].at[pl.ds(0, tt * s)], hbuf.at[slot], load_sem.at[slot]).wait()
            return
        lo = 0
        for h_ref, n_tiles in zip(h_refs, tiles):
            @pl.when((t >= lo) & (t < lo + n_tiles))
            def _():
                src = pl.multiple_of((t - lo) * tt * s, tt * s)
                pltpu.make_async_copy(h_ref.at[pl.ds(src, tt * s)], hbuf.at[slot], load_sem.at[slot]).start()
            lo += n_tiles

    def drain(slot):
        for _ in range(top_k):
            pltpu.make_async_copy(hbuf.at[slot], xs_ref.at[pl.ds(0, tt * s)], sem.at[slot]).wait()

    slot = i % 3
    ahead = (i + 1) % 3

    @pl.when(i == 0)
    def _():
        load(i, slot, True)

    @pl.when(i >= 2)
    def _():
        drain(ahead)

    @pl.when(i + 1 < steps)
    def _():
        load(i + 1, ahead, True)
    load(i, slot, False)

    def issue(t, carry):
        src = pl.multiple_of(t * s, s)
        for j in range(top_k):
            dst = pl.multiple_of(dest_ref[0, 0, t * top_k + j] * s, s)
            pltpu.make_async_copy(hbuf.at[slot, pl.ds(src, s)], xs_ref.at[pl.ds(dst, s)],
                                  sem.at[slot]).start(priority=j % 2)
        return carry
    lax.fori_loop(0, tt, issue, 0, unroll=4)

    @pl.when(i == steps - 1)
    def _():
        @pl.when(steps >= 2)
        def _():
            drain((i + 2) % 3)
        drain(slot)


def _dispatch(dest, h2ts, pad_first, pad_count, n_used, n_blocks, rows, tt):
    s = sum(h.shape[0] for h in h2ts) * TOP_K // dest.size
    tiles = tuple(h.shape[0] // (tt * s) for h in h2ts)
    return pl.pallas_call(
        functools.partial(_dispatch_kernel, top_k=TOP_K, s=s, rows=rows, tiles=tiles),
        grid_spec=pltpu.PrefetchScalarGridSpec(
            num_scalar_prefetch=3,
            grid=(dest.shape[0],),
            in_specs=[pl.BlockSpec((1, 1, tt * TOP_K), lambda i, *_: (i, 0, 0), memory_space=pltpu.SMEM)]
            + [pl.BlockSpec(memory_space=pl.ANY) for _ in h2ts],
            out_specs=pl.BlockSpec(memory_space=pl.ANY),
            scratch_shapes=[pltpu.VMEM((rows // 2 * s, LANES), F32), pltpu.VMEM((3, tt * s, LANES), F32),
                            pltpu.SemaphoreType.DMA((3,)), pltpu.SemaphoreType.DMA((3,)),
                            pltpu.SemaphoreType.DMA(())]),
        out_shape=jax.ShapeDtypeStruct((n_blocks * rows * s, LANES), F32),
        compiler_params=_cparams("arbitrary"),
        name="dispatch",
    )(pad_first, pad_count, n_used, dest, *h2ts)


def _deinterleave_table():
    p = np.zeros((2 * LANES, 2 * LANES), np.float32)
    j = np.arange(LANES)
    p[2 * j, j] = 1.0
    p[2 * j + 1, LANES + j] = 1.0
    return p


def _ffn_kernel(be_ref, nu_ref, xs_ref, wgu_ref, bg_ref, bl_ref, wd_ref, bd_ref, perm_ref, ys_ref,
                wg_buf, wl_buf, wd_buf, acc_buf, *, ft):
    i = pl.program_id(0)
    f = wg_buf.shape[1]
    rows = acc_buf.shape[0]
    live = i < nu_ref[0]

    @pl.when(live & ((i == 0) | (be_ref[i] != be_ref[jnp.maximum(i - 1, 0)])))
    def _():
        for j in range(f // LANES):
            blk = wgu_ref[0, :, 2 * LANES * j:2 * LANES * (j + 1)].astype(BF16)
            split = jnp.dot(blk, perm_ref[...], preferred_element_type=F32)
            wg_buf[:, LANES * j:LANES * (j + 1)] = split[:, :LANES].astype(BF16)
            wl_buf[:, LANES * j:LANES * (j + 1)] = split[:, LANES:].astype(BF16)
        wd_buf[...] = wd_ref[0].astype(BF16)

    @pl.when(live)
    def _():
        x = _tiles_to_rows(xs_ref, rows, xs_ref.shape[0] // rows).astype(BF16)
        for j, f0 in enumerate(range(0, f, ft)):
            g = jnp.dot(x, wg_buf[:, f0:f0 + ft], preferred_element_type=F32) + bg_ref[0, :, f0:f0 + ft]
            lin = jnp.dot(x, wl_buf[:, f0:f0 + ft], preferred_element_type=F32) + bl_ref[0, :, f0:f0 + ft]
            gate = jnp.minimum(g, SWIGLU_LIMIT)
            lin = jnp.clip(lin, -SWIGLU_LIMIT, SWIGLU_LIMIT)
            act = (lin + 1.0) * gate * jax.nn.sigmoid(SWIGLU_ALPHA * gate)
            part = jnp.dot(act.astype(BF16), wd_buf[f0:f0 + ft, :], preferred_element_type=F32)
            if j == 0:
                acc_buf[...] = part + bd_ref[0]
            else:
                acc_buf[...] += part
        _rows_to_tiles(ys_ref, acc_buf[...])

    @pl.when(jnp.logical_not(live))
    def _():
        ys_ref[...] = jnp.zeros_like(ys_ref)


def _ffn(block_e, n_used, xs, w_gu, b_gu, w_d, b_d, rows):
    n_exp, d, f2 = w_gu.shape
    f = f2 // 2
    s = d // LANES
    n_rows = xs.shape[0] // s
    live = lambda i, nu: jnp.minimum(i, nu[0] - 1)
    wmap = lambda i, be, nu: (be[live(i, nu)], 0, 0)
    perm = jnp.asarray(_deinterleave_table(), BF16)
    return pl.pallas_call(
        functools.partial(_ffn_kernel, ft=min(f, 1024)),
        grid_spec=pltpu.PrefetchScalarGridSpec(
            num_scalar_prefetch=2,
            grid=(n_rows // rows,),
            in_specs=[pl.BlockSpec((rows * s, LANES), lambda i, be, nu: (live(i, nu), 0)),
                      pl.BlockSpec((1, d, f2), wmap),
                      pl.BlockSpec((1, 1, f), wmap), pl.BlockSpec((1, 1, f), wmap),
                      pl.BlockSpec((1, f, d), wmap), pl.BlockSpec((1, 1, d), wmap),
                      pl.BlockSpec(perm.shape, lambda i, be, nu: (0, 0))],
            out_specs=pl.BlockSpec((rows * s, LANES), lambda i, be, nu: (i, 0)),
            scratch_shapes=[pltpu.VMEM((d, f), BF16), pltpu.VMEM((d, f), BF16), pltpu.VMEM((f, d), BF16),
                            pltpu.VMEM((rows, d), F32)]),
        out_shape=jax.ShapeDtypeStruct(xs.shape, F32),
        compiler_params=_cparams("arbitrary"),
        name="expert_ffn",
    )(block_e, n_used, xs, w_gu, b_gu[:, 0::2].reshape(n_exp, 1, f), b_gu[:, 1::2].reshape(n_exp, 1, f),
      w_d, b_d.reshape(n_exp, 1, d), perm)


def _combine_kernel(dest_ref, next_ref, tg_ref, x1_ref, mod_ref, fg_ref, ys_ref, o_ref, ybuf, sem, *,
                    top_k, final_norm):
    i = pl.program_id(0)
    tt = x1_ref.shape[0]
    s = ybuf.shape[2] // tt
    slot = i % 2

    def gather(d_ref, to):
        def issue(t, carry):
            dst = pl.multiple_of(t * s, s)
            for j in range(top_k):
                src = pl.multiple_of(d_ref[0, 0, t * top_k + j] * s, s)
                pltpu.make_async_copy(ys_ref.at[pl.ds(src, s)], ybuf.at[to, j, pl.ds(dst, s)],
                                      sem.at[to]).start(priority=j % 2)
            return carry
        lax.fori_loop(0, tt, issue, 0, unroll=4)

    @pl.when(i == 0)
    def _():
        gather(dest_ref, 0)

    @pl.when(i + 1 < pl.num_programs(0))
    def _():
        gather(next_ref, 1 - slot)

    for j in range(top_k):
        pltpu.make_async_copy(ys_ref.at[pl.ds(0, tt * s)], ybuf.at[slot, j], sem.at[slot]).wait()
    tg = tg_ref[...]
    moe = tg[:, 0:1] * _tiles_to_rows(ybuf.at[slot, 0], tt, s)
    for j in range(1, top_k):
        moe = moe + tg[:, j:j + 1] * _tiles_to_rows(ybuf.at[slot, j], tt, s)
    x2 = x1_ref[...] + mod_ref[0, 5:6, :] * moe
    if final_norm:
        x2 = x2 * lax.rsqrt(jnp.mean(x2 * x2, axis=-1, keepdims=True) + EPS) * fg_ref[...]
    o_ref[...] = x2


def _combine(dest, tg, x1, mods, final_g, ys, mod_map, tt, final_norm):
    t, d = x1.shape
    steps = t // tt
    return pl.pallas_call(
        functools.partial(_combine_kernel, top_k=TOP_K, final_norm=final_norm),
        grid=(steps,),
        in_specs=[pl.BlockSpec((1, 1, tt * TOP_K), lambda i: (i, 0, 0), memory_space=pltpu.SMEM),
                  pl.BlockSpec((1, 1, tt * TOP_K), lambda i: (jnp.minimum(i + 1, steps - 1), 0, 0),
                               memory_space=pltpu.SMEM),
                  pl.BlockSpec((tt, LANES), lambda i: (i, 0)),
                  pl.BlockSpec((tt, d), lambda i: (i, 0)),
                  pl.BlockSpec((1, 6, d), mod_map),
                  pl.BlockSpec((1, d), lambda i: (0, 0)),
                  pl.BlockSpec(memory_space=pl.ANY)],
        out_specs=pl.BlockSpec((tt, d), lambda i: (i, 0)),
        out_shape=jax.ShapeDtypeStruct((t, d), F32),
        scratch_shapes=[pltpu.VMEM((2, TOP_K, tt * d // LANES, LANES), F32), pltpu.SemaphoreType.DMA((2,))],
        compiler_params=_cparams("arbitrary"),
        name="combine",
    )(dest, dest, tg, x1, mods, final_g.reshape(1, d), ys)


def _moe_plan(counts, rows, n_blocks):
    n_exp = counts.shape[0]
    padded = (counts + rows - 1) // rows * rows
    pad_end = jnp.cumsum(padded)
    block_row = jnp.arange(n_blocks, dtype=jnp.int32) * rows
    block_e = jnp.minimum(jnp.sum(pad_end[None, :] <= block_row[:, None], axis=1), n_exp - 1).astype(jnp.int32)
    n_used = (pad_end[-1:] // rows).astype(jnp.int32)
    start = pad_end - padded
    pad_start = jnp.pad(start.astype(F32), (0, LANES - n_exp)).reshape(1, LANES)
    return pad_start, block_e, n_used, (start + counts).astype(jnp.int32), (padded - counts).astype(jnp.int32)


def _sequence_mixers(z_hy, z_qk, z_v, z_o, z_g, lw, state, row_w):
    (hy_cw, hy_cb, filt_params, hy_b, ml_cw, ml_cb, ml_gb, ml_ng) = lw
    y_hy = _hyena(z_hy, hy_cw, hy_cb, filt_params, hy_b, row_w)
    ml_w = z_v.shape[-1]
    kt = _short_conv(z_qk, ml_cw, ml_cb, row_w, silu=True, col0=ml_w, ncols=ml_w, transpose=True)
    y_ml, st = _mlstm(z_qk, ml_cw, ml_cb, row_w, kt, z_v, z_o, z_g, ml_gb, ml_ng, *state)
    return y_hy, y_ml, st


def kernel(x_prompt, x_sample, state_mlstm_C, state_mlstm_n, state_mlstm_m, c, c_ctx, ada_w, ada_b, norm1_g,
           w_in, hy_conv_w, hy_conv_b, filt_w1, filt_b1, filt_w2, filt_b2, filt_w3, filt_freq, hy_bias,
           ml_conv_w, ml_conv_b, ml_gate_b, ml_norm_g, w_out, norm2_g, router_w, router_b, moe_w_gu,
           moe_b_gu, moe_w_down, moe_b_down, final_g):
    bp, lp, d = x_prompt.shape
    bs, ls, _ = x_sample.shape
    depth = ada_w.shape[0]
    heads = ml_gate_b.shape[-1]
    hy_w = hy_bias.shape[-1]
    ml_w = ml_norm_g.shape[-1]
    dh = ml_w // heads
    n_exp = router_w.shape[-1]
    t = bp * lp + bs * ls
    ng = N_DIR * 2 * heads
    seg_widths = (3 * hy_w, 2 * ml_w, ml_w, ml_w)
    n_main = 3 * hy_w + 4 * ml_w
    n_blocks = -(-(t * TOP_K) // MOE_ROWS) + n_exp

    cond = jnp.concatenate([c_ctx[None], c, jnp.zeros((SUBLANES - 1 - bs, d), F32)], axis=0)
    zero_state = (jnp.zeros((bp, N_DIR, heads, dh, dh), F32), jnp.zeros((bp, N_DIR, heads, dh), F32),
                  jnp.zeros((bp, N_DIR, heads), F32))
    xs_paths = [x_prompt, x_sample]
    path_cfg = [(0, 0, lp), (1, 1, GRID_W)]
    new_c, new_n, new_m = [], [], []
    for l in range(depth):
        mods = _ada(cond, ada_w[l], ada_b[l]).reshape(SUBLANES, 6, d)
        w_main = w_in[l][:, :n_main].astype(BF16)
        w_gate = jnp.pad(w_in[l][:, n_main:], ((0, 0), (0, LANES - ng)))
        lw = (hy_conv_w[l], hy_conv_b[l],
              (filt_w1[l], filt_b1[l], filt_w2[l], filt_b2[l], filt_w3[l], filt_freq[l]), hy_bias[l],
              ml_conv_w[l], ml_conv_b[l], ml_gate_b[l], ml_norm_g[l])
        states = [zero_state, (state_mlstm_C[:, l], state_mlstm_n[:, l], state_mlstm_m[:, l])]
        counts = jnp.zeros((SUBLANES, LANES), F32)
        routed = []
        for x3, (mod0, mod_step, row_w), state in zip(xs_paths, path_cfg, states):
            bsz, lseq, _ = x3.shape
            tt = min(TOK_TILE, lseq)
            tm = math.gcd(ROW_TILE, bsz * lseq) if mod_step == 0 else min(ROW_TILE, lseq)
            xf = x3.reshape(bsz * lseq, d)
            z = _inproj(xf, mods, norm1_g[l], w_main, w_gate, seg_widths,
                        _mod_index_map(mod0, mod_step, max(lseq // tm, 1)), tm)
            y_hy, y_ml, st = _sequence_mixers(*[a.reshape(bsz, lseq, a.shape[1]) for a in z], lw, state, row_w)
            x1, h2t, ti, tg, rk, counts = _post(
                y_hy.reshape(bsz * lseq, hy_w), y_ml.reshape(bsz * lseq, ml_w), xf, mods, w_out[l], norm2_g[l],
                router_w[l], router_b[l], counts, _mod_index_map(mod0, mod_step, max(lseq // tm, 1)), tm)
            routed.append((x1, h2t, ti, tg, rk, tm, tt, _mod_index_map(mod0, mod_step, lseq // tt), st))
        new_c.append(routed[0][-1][0])
        new_n.append(routed[0][-1][1])
        new_m.append(routed[0][-1][2])

        pad_start, block_e, n_used, pad_first, pad_count = _moe_plan(
            counts[0, :n_exp].astype(jnp.int32), MOE_ROWS, n_blocks)
        tt = routed[0][6]
        assert all(r[6] == tt for r in routed)
        dests = [_dest_rows(ti, rk, pad_start, tm)[:, :TOP_K].reshape(-1, 1, tt * TOP_K)
                 for _, _, ti, _, rk, tm, _, _, _ in routed]
        xs = _dispatch(jnp.concatenate(dests, axis=0), [r[1] for r in routed], pad_first, pad_count, n_used,
                       n_blocks, MOE_ROWS, tt)
        ys = _ffn(block_e, n_used, xs, moe_w_gu[l], moe_b_gu[l], moe_w_down[l], moe_b_down[l], MOE_ROWS)
        xs_paths = [
            _combine(dest, tg, x1, mods, final_g, ys, mod_map, tt, final_norm=l == depth - 1).reshape(x3.shape)
            for dest, (x1, _, _, tg, _, _, tt, mod_map, _), x3 in zip(dests, routed, xs_paths)]
    return (xs_paths[0], xs_paths[1],
            jnp.stack(new_c, axis=1), jnp.stack(new_n, axis=1), jnp.stack(new_m, axis=1))
```
